```python
import math
import jax, jax.numpy as jnp
from jax import lax
import numpy as np


D_MODEL = 1024
BATCH = 4
SEQ = 8192
DEPTH = 1

HEAD_DIM = 64
ROPE_DIM = HEAD_DIM // 4
ROPE_THETA = 500000.0
Q_BLOCK = 128
A_HEADS = 8
A_KV_RANK = 256
A_TOPK_MAX = 256
IDX_HEADS = 8
IDX_DIM = 32
IDX_ROPE = IDX_DIM // 4
B_PATTERNS = ((128, 1), (512, 4), (2048, 16))
B_HEADS_PER_GROUP = 4
B_HEADS = B_HEADS_PER_GROUP * len(B_PATTERNS)
N_EXPERTS = 64
TOP_K = 8
N_GROUPS = 8
TOPK_GROUPS = 4
D_EXPERT = 256
D_SHARED = 256
ROUTED_SCALE = 2.5
MOE_BLOCK = 128
ALPHA = (2.0 * DEPTH) ** 0.25
BETA = (8.0 * DEPTH) ** -0.25
LN_EPS = 1e-5
RMS_EPS = 1e-6
IN_SIZES = (A_HEADS * HEAD_DIM, A_KV_RANK, ROPE_DIM, IDX_HEADS * IDX_DIM, IDX_DIM, IDX_HEADS,
            B_HEADS * HEAD_DIM, B_HEADS * HEAD_DIM, B_HEADS * HEAD_DIM, D_MODEL, D_MODEL)
IN_TOTAL = sum(IN_SIZES)
V_COL_BLOCK = 8

kernel_name = 'hybrid_dsa_dilated_moe_deepnorm'


def layer_norm(x, g, b):
    xf = x.astype(jnp.float32)
    mu = jnp.mean(xf, -1, keepdims=True)
    var = jnp.mean(jnp.square(xf - mu), -1, keepdims=True)
    y = (xf - mu) * lax.rsqrt(var + LN_EPS) * g.astype(jnp.float32) + b.astype(jnp.float32)
    return y.astype(x.dtype)


def rms_norm(x, g):
    xf = x.astype(jnp.float32)
    y = xf * lax.rsqrt(jnp.mean(jnp.square(xf), -1, keepdims=True) + RMS_EPS) * g.astype(jnp.float32)
    return y.astype(x.dtype)


def rope(x, pos):
    dim = x.shape[-1]
    inv = ROPE_THETA ** (-jnp.arange(0, dim, 2, dtype=jnp.float32) / dim)
    ang = pos.astype(jnp.float32)[:, :, None] * inv
    cos = jnp.cos(ang)[:, :, None, :]
    sin = jnp.sin(ang)[:, :, None, :]
    xf = x.astype(jnp.float32)
    x1, x2 = xf[..., :dim // 2], xf[..., dim // 2:]
    return jnp.concatenate([x1 * cos - x2 * sin, x2 * cos + x1 * sin], -1).astype(x.dtype)


def partial_rope(x, pos, rot):
    return jnp.concatenate([rope(x[..., :rot], pos), x[..., rot:]], -1)


def to_blocks(x):
    b, s = x.shape[:2]
    return jnp.moveaxis(x.reshape(b, s // Q_BLOCK, Q_BLOCK, *x.shape[2:]), 1, 0)


def from_blocks(x):
    y = jnp.moveaxis(x, 0, 1)
    return y.reshape(y.shape[0], y.shape[1] * y.shape[2], *y.shape[3:])


def dsa_mixer(q_a, ckv, k_rope, iq, ik, iw, w_uk, w_uv):
    s_len = q_a.shape[1]
    n_sel = min(A_TOPK_MAX, s_len // 4)
    scale = HEAD_DIM ** -0.5
    key_pos = jnp.arange(s_len)

    def block(args):
        qb, iqb, iwb, start = args
        tq = start + jnp.arange(Q_BLOCK)
        causal = key_pos[None, :] <= tq[:, None]
        s = jnp.einsum('bthd,bsd->bths', iqb, ik)
        score = jnp.einsum('bths,bth->bts', jax.nn.relu(s), iwb).astype(jnp.float32)
        score = jnp.where(causal[None], score, -jnp.inf)
        _, idx = lax.top_k(score, n_sel)
        valid = idx <= tq[None, :, None]
        c_sel = jax.vmap(lambda c, i: c[i])(ckv, idx)
        r_sel = jax.vmap(lambda r, i: r[i])(k_rope, idx)
        q_lat = jnp.einsum('bthd,rhd->bthr', qb[..., ROPE_DIM:], w_uk)
        logits = (jnp.einsum('bthr,btkr->bthk', q_lat, c_sel)
                  + jnp.einsum('bthd,btkd->bthk', qb[..., :ROPE_DIM], r_sel)).astype(jnp.float32) * scale
        logits = jnp.where(valid[:, :, None, :], logits, -jnp.inf)
        p = jax.nn.softmax(logits, axis=-1).astype(c_sel.dtype)
        o_lat = jnp.einsum('bthk,btkr->bthr', p, c_sel)
        return jnp.einsum('bthr,rhd->bthd', o_lat, w_uv)

    starts = jnp.arange(s_len // Q_BLOCK, dtype=jnp.int32) * Q_BLOCK
    out = lax.map(block, (to_blocks(q_a), to_blocks(iq), to_blocks(iw), starts))
    return from_blocks(out)


def dilated_mixer(qb, kb, vb):
    s_len = qb.shape[1]
    scale = HEAD_DIM ** -0.5
    groups = []
    for g, (win, dil) in enumerate(B_PATTERNS):
        hs = slice(g * B_HEADS_PER_GROUP, (g + 1) * B_HEADS_PER_GROUP)
        groups.append((hs, win, dil, kb[:, :, hs], vb[:, :, hs]))

    def block(args):
        q, start = args
        tq = start + jnp.arange(Q_BLOCK)
        outs, lses = [], []
        for hs, win, dil, kg, vg in groups:
            n_keys = win // dil + 1
            kpos = tq[:, None] - dil * jnp.arange(n_keys)[None, :]
            valid = kpos >= 0
            kpos = jnp.maximum(kpos, 0)
            ksel = jnp.take(kg, kpos, axis=1)
            vsel = jnp.take(vg, kpos, axis=1)
            l = jnp.einsum('bthd,btjhd->bthj', q[:, :, hs], ksel).astype(jnp.float32) * scale
            l = jnp.where(valid[None, :, None, :], l, -jnp.inf)
            m = jnp.max(l, -1, keepdims=True)
            e = jnp.exp(l - m)
            den = jnp.sum(e, -1)
            o = jnp.einsum('bthj,btjhd->bthd', e, vsel.astype(jnp.float32)) / den[..., None]
            outs.append(o)
            lses.append(m[..., 0] + jnp.log(den))
        wts = jax.nn.softmax(jnp.stack(lses), axis=0)
        return jnp.sum(wts[..., None] * jnp.stack(outs), axis=0).astype(q.dtype)

    starts = jnp.arange(s_len // Q_BLOCK, dtype=jnp.int32) * Q_BLOCK
    return from_blocks(lax.map(block, (to_blocks(qb), starts)))


def routed_experts(hf, top_idx, top_w, w1_e, w3_e, w2_e):
    n_tok, d = hf.shape
    n_asg = n_tok * TOP_K
    flat_e = top_idx.reshape(-1).astype(jnp.int32)
    flat_t = jnp.repeat(jnp.arange(n_tok, dtype=jnp.int32), TOP_K)
    flat_w = top_w.reshape(-1)
    order = jnp.argsort(flat_e)
    se, st, sw = flat_e[order], flat_t[order], flat_w[order]
    counts = jnp.bincount(flat_e, length=N_EXPERTS)
    padded = (counts + MOE_BLOCK - 1) // MOE_BLOCK * MOE_BLOCK
    start = jnp.cumsum(counts) - counts
    pend = jnp.cumsum(padded)
    pstart = pend - padded
    dest = pstart[se] + (jnp.arange(n_asg) - start[se])
    n_rows = n_asg + N_EXPERTS * MOE_BLOCK
    n_blk = n_rows // MOE_BLOCK
    row_t = jnp.full((n_rows,), n_tok, jnp.int32).at[dest].set(st)
    row_w = jnp.zeros((n_rows,), hf.dtype).at[dest].set(sw.astype(hf.dtype))
    blk_e = jnp.minimum(jnp.searchsorted(pend, jnp.arange(n_blk) * MOE_BLOCK, side='right'), N_EXPERTS - 1)
    x_pad = jnp.concatenate([hf, jnp.zeros((1, d), hf.dtype)], 0)

    def step(acc, inp):
        t, w, e = inp
        xb = x_pad[t]
        hb = jax.nn.silu(xb @ w1_e[e]) * (xb @ w3_e[e])
        return acc.at[t].add((hb @ w2_e[e]) * w[:, None]), None

    acc, _ = lax.scan(step, jnp.zeros((n_tok + 1, d), hf.dtype),
                      (row_t.reshape(n_blk, MOE_BLOCK), row_w.reshape(n_blk, MOE_BLOCK), blk_e))
    return acc[:n_tok]


def moe(h, w_router, router_bias, w1_e, w3_e, w2_e, ws1, ws3, ws2):
    b, s, d = h.shape
    hf = h.reshape(-1, d)
    n_tok = hf.shape[0]
    scores = jax.nn.sigmoid((hf @ w_router).astype(jnp.float32))
    biased = scores + router_bias.astype(jnp.float32)
    grp = biased.reshape(n_tok, N_GROUPS, N_EXPERTS // N_GROUPS)
    grp_score = jnp.sum(lax.top_k(grp, 2)[0], -1)
    _, gidx = lax.top_k(grp_score, TOPK_GROUPS)
    gmask = jnp.any(gidx[..., None] == jnp.arange(N_GROUPS), axis=1)
    emask = jnp.repeat(gmask, N_EXPERTS // N_GROUPS, axis=1)
    _, top_idx = lax.top_k(jnp.where(emask, biased, -jnp.inf), TOP_K)
    top_s = jnp.take_along_axis(scores, top_idx, axis=1)
    top_w = top_s / jnp.sum(top_s, -1, keepdims=True) * ROUTED_SCALE
    routed = routed_experts(hf, top_idx, top_w, w1_e, w3_e, w2_e)
    shared = (jax.nn.silu(hf @ ws1) * (hf @ ws3)) @ ws2
    return (routed + shared).reshape(b, s, d)


def hybrid_layer(x, positions, w_in, b_gate, g_kv, w_uk, w_uv, w_branch_a, w_branch_b, w_o,
                 ln1_g, ln1_b, w_router, router_bias, w1_e, w3_e, w2_e, ws1, ws3, ws2, ln2_g, ln2_b):
    b, s, d = x.shape
    proj = x @ w_in
    splits = np.cumsum(IN_SIZES)[:-1].tolist()
    aq, ckv, akr, iq, ik, iw, bq, bk, bv, ga, gb = jnp.split(proj, splits, axis=-1)
    g_a = jax.nn.sigmoid(ga + b_gate[:d])
    g_b = jax.nn.sigmoid(gb + b_gate[d:])
    aq = partial_rope(aq.reshape(b, s, A_HEADS, HEAD_DIM), positions, ROPE_DIM)
    ckv = rms_norm(ckv, g_kv)
    akr = rope(akr[:, :, None, :], positions)[:, :, 0]
    iq = partial_rope(iq.reshape(b, s, IDX_HEADS, IDX_DIM), positions, IDX_ROPE)
    ik = partial_rope(ik[:, :, None, :], positions, IDX_ROPE)[:, :, 0]
    iw = iw * (IDX_HEADS * IDX_DIM) ** -0.5
    a_out = dsa_mixer(aq, ckv, akr, iq, ik, iw, w_uk, w_uv).reshape(b, s, A_HEADS * HEAD_DIM)
    bq = partial_rope(bq.reshape(b, s, B_HEADS, HEAD_DIM), positions, ROPE_DIM)
    bk = partial_rope(bk.reshape(b, s, B_HEADS, HEAD_DIM), positions, ROPE_DIM)
    bv = bv.reshape(b, s, B_HEADS, HEAD_DIM)
    b_out = dilated_mixer(bq, bk, bv).reshape(b, s, B_HEADS_PER_GROUP * HEAD_DIM)
    mix = (g_a * (a_out @ w_branch_a) + g_b * (b_out @ w_branch_b)) @ w_o
    h = layer_norm(ALPHA * x + mix, ln1_g, ln1_b)
    ffn = moe(h, w_router, router_bias, w1_e, w3_e, w2_e, ws1, ws3, ws2)
    return layer_norm(ALPHA * h + ffn, ln2_g, ln2_b)


def setup_inputs(seed: int = 0) -> dict:
    key = jax.random.key(seed)
    ks = jax.random.split(key, 24)
    f32 = jnp.float32
    L, D = DEPTH, D_MODEL

    def nrm(k, shape, scale):
        return jax.random.normal(k, shape, f32) * scale

    x = jax.random.normal(ks[0], (BATCH, SEQ, D), f32)
    offset = jax.random.randint(ks[1], (BATCH, 1), 0, 4096, dtype=jnp.int32)
    positions = offset + jnp.arange(SEQ, dtype=jnp.int32)[None, :]
    col_scale = jnp.concatenate([jnp.full((n,), BETA if i == V_COL_BLOCK else 1.0, f32)
                                 for i, n in enumerate(IN_SIZES)])
    w_in = nrm(ks[2], (L, D, IN_TOTAL), D ** -0.5) * col_scale
    b_gate = nrm(ks[3], (L, 2 * D), 0.02)
    g_kv = 1.0 + nrm(ks[4], (L, A_KV_RANK), 0.02)
    w_uk = nrm(ks[5], (L, A_KV_RANK, A_HEADS, HEAD_DIM - ROPE_DIM), A_KV_RANK ** -0.5)
    w_uv = nrm(ks[6], (L, A_KV_RANK, A_HEADS, HEAD_DIM), BETA * A_KV_RANK ** -0.5)
    w_branch_a = nrm(ks[7], (L, A_HEADS * HEAD_DIM, D), BETA * (A_HEADS * HEAD_DIM) ** -0.5)
    w_branch_b = nrm(ks[8], (L, B_HEADS_PER_GROUP * HEAD_DIM, D), BETA * (B_HEADS_PER_GROUP * HEAD_DIM) ** -0.5)
    w_o = nrm(ks[9], (L, D, D), BETA * D ** -0.5)
    ln1_g = 1.0 + nrm(ks[10], (L, D), 0.02)
    ln1_b = nrm(ks[11], (L, D), 0.02)
    w_router = nrm(ks[12], (L, D, N_EXPERTS), D ** -0.5)
    router_bias = nrm(ks[13], (L, N_EXPERTS), 0.01)
    w1_e = nrm(ks[14], (L, N_EXPERTS, D, D_EXPERT), BETA * D ** -0.5)
    w3_e = nrm(ks[15], (L, N_EXPERTS, D, D_EXPERT), BETA * D ** -0.5)
    w2_e = nrm(ks[16], (L, N_EXPERTS, D_EXPERT, D), BETA * D_EXPERT ** -0.5)
    ws1 = nrm(ks[17], (L, D, D_SHARED), BETA * D ** -0.5)
    ws3 = nrm(ks[18], (L, D, D_SHARED), BETA * D ** -0.5)
    ws2 = nrm(ks[19], (L, D_SHARED, D), BETA * D_SHARED ** -0.5)
    ln2_g = 1.0 + nrm(ks[20], (L, D), 0.02)
    ln2_b = nrm(ks[21], (L, D), 0.02)
    return {'x': x, 'positions': positions, 'w_in': w_in, 'b_gate': b_gate, 'g_kv': g_kv,
            'w_uk': w_uk, 'w_uv': w_uv, 'w_branch_a': w_branch_a, 'w_branch_b': w_branch_b,
            'w_o': w_o, 'ln1_g': ln1_g, 'ln1_b': ln1_b, 'w_router': w_router,
            'router_bias': router_bias, 'w1_e': w1_e, 'w3_e': w3_e, 'w2_e': w2_e,
            'ws1': ws1, 'ws3': ws3, 'ws2': ws2, 'ln2_g': ln2_g, 'ln2_b': ln2_b}


def reference(x, positions, w_in, b_gate, g_kv, w_uk, w_uv, w_branch_a, w_branch_b, w_o,
              ln1_g, ln1_b, w_router, router_bias, w1_e, w3_e, w2_e, ws1, ws3, ws2, ln2_g, ln2_b):
    h = x
    for l in range(DEPTH):
        h = hybrid_layer(h, positions, w_in[l], b_gate[l], g_kv[l], w_uk[l], w_uv[l],
                         w_branch_a[l], w_branch_b[l], w_o[l], ln1_g[l], ln1_b[l],
                         w_router[l], router_bias[l], w1_e[l], w3_e[l], w2_e[l],
                         ws1[l], ws3[l], ws2[l], ln2_g[l], ln2_b[l])
    return h
```

```python
import functools

import jax
import jax.numpy as jnp
import numpy as np
from jax import lax
from jax.experimental import pallas as pl
from jax.experimental.pallas import tpu as pltpu

F32 = jnp.float32
BF16 = jnp.bfloat16

D_MODEL = 1024
HEAD_DIM = 64
ROPE_DIM = 16
ROPE_THETA = 500000.0
Q_BLOCK = 128
A_HEADS = 8
A_NOPE = HEAD_DIM - ROPE_DIM
A_KV_RANK = 256
A_TOPK_MAX = 256
IDX_HEADS = 8
IDX_DIM = 32
IDX_ROPE = 8
B_PATTERNS = ((128, 1), (512, 4), (2048, 16))
B_GROUP_HEADS = 4
B_HEADS = B_GROUP_HEADS * len(B_PATTERNS)
B_GROUP_W = B_GROUP_HEADS * HEAD_DIM
N_EXPERTS = 64
TOP_K = 8
N_GROUPS = 8
TOPK_GROUPS = 4
D_EXPERT = 256
ROUTED_SCALE = 2.5
DEPTH = 1
ALPHA = (2.0 * DEPTH) ** 0.25
LN_EPS = 1e-5
RMS_EPS = 1e-6

LANES = 128
VMEM_LIMIT = 56 * 1024 * 1024
NEG_BIG = -1e30
INT_MIN = -(2 ** 31)

_OFF_AQ = 0
_OFF_CKV = _OFF_AQ + A_HEADS * HEAD_DIM
_OFF_AKR = _OFF_CKV + A_KV_RANK
_OFF_IQ = _OFF_AKR + ROPE_DIM
_OFF_IK = _OFF_IQ + IDX_HEADS * IDX_DIM
_OFF_IW = _OFF_IK + IDX_DIM
_OFF_BQ = _OFF_IW + IDX_HEADS
_OFF_BK = _OFF_BQ + B_HEADS * HEAD_DIM
_OFF_BV = _OFF_BK + B_HEADS * HEAD_DIM
_OFF_GA = _OFF_BV + B_HEADS * HEAD_DIM
_OFF_GB = _OFF_GA + D_MODEL
_IN_TOTAL = _OFF_GB + D_MODEL

_W_QAR, _W_QAN, _W_CKV, _W_AKR = 128, A_HEADS * A_NOPE, A_KV_RANK, 128
_W_IQ, _W_IK, _W_IW = 256, 256, 128
_W_B = B_HEADS * HEAD_DIM
_C_QAR = 0
_C_QAN = _C_QAR + _W_QAR
_C_CKV = _C_QAN + _W_QAN
_C_AKR = _C_CKV + _W_CKV
_C_IQ = _C_AKR + _W_AKR
_C_IK = _C_IQ + _W_IQ
_C_IW = _C_IK + _W_IK
_C_BQ = _C_IW + _W_IW
_C_BK = _C_BQ + _W_B
_C_BV = _C_BK + _W_B
_C_GA = _C_BV + _W_B
_C_GB = _C_GA + D_MODEL
_P_TOTAL = _C_GB + D_MODEL


def _idx_lane(l):
    if l < 16:
        return l // 4, l % 4
    if l < 64:
        return (l - 16) // 12, 8 + (l - 16) % 12
    if l < 80:
        return (l - 64) // 4, 4 + (l - 64) % 4
    return (l - 80) // 12, 20 + (l - 80) % 12


def _b_lane(l):
    half, r = l // 64, l % 64
    which, rr = r // 32, r % 32
    if rr < 8:
        return which, half * 8 + rr
    return which, 16 + half * 24 + (rr - 8)


def _build_layout():
    cols = np.zeros((_P_TOTAL,), np.int32)
    keep = np.ones((_P_TOTAL,), np.float32)
    for l in range(128):
        half, h, f = l // 64, (l % 64) // 8, l % 8
        cols[_C_QAR + l] = _OFF_AQ + h * HEAD_DIM + half * 8 + f
        cols[_C_AKR + l] = _OFF_AKR + half * 8 + f
    for h in range(A_HEADS):
        for j in range(A_NOPE):
            cols[_C_QAN + h * A_NOPE + j] = _OFF_AQ + h * HEAD_DIM + ROPE_DIM + j
    cols[_C_CKV:_C_CKV + _W_CKV] = _OFF_CKV + np.arange(_W_CKV)
    for sl in range(2):
        for l in range(128):
            hh, d = _idx_lane(l)
            cols[_C_IQ + sl * 128 + l] = _OFF_IQ + (sl * 4 + hh) * IDX_DIM + d
            cols[_C_IK + sl * 128 + l] = _OFF_IK + d
    cols[_C_IW:_C_IW + IDX_HEADS] = _OFF_IW + np.arange(IDX_HEADS)
    keep[_C_IW + IDX_HEADS:_C_IW + _W_IW] = 0.0
    for p in range(B_HEADS // 2):
        for l in range(128):
            which, d = _b_lane(l)
            h = 2 * p + which
            cols[_C_BQ + p * 128 + l] = _OFF_BQ + h * HEAD_DIM + d
            cols[_C_BK + p * 128 + l] = _OFF_BK + h * HEAD_DIM + d
    cols[_C_BV:_C_BV + _W_B] = _OFF_BV + np.arange(_W_B)
    cols[_C_GA:_C_GA + D_MODEL] = _OFF_GA + np.arange(D_MODEL)
    cols[_C_GB:_C_GB + D_MODEL] = _OFF_GB + np.arange(D_MODEL)

    fa = np.array([l % 8 for l in range(128)])
    sa = np.array([-1.0 if l < 64 else 1.0 for l in range(128)], np.float32)
    fi = np.full((128,), -1)
    si = np.zeros((128,), np.float32)
    fb = np.full((128,), -1)
    sb = np.zeros((128,), np.float32)
    for l in range(128):
        if l < 16 or 64 <= l < 80:
            fi[l] = l % 4
            si[l] = -1.0 if l < 64 else 1.0
        if l % 32 < 8:
            fb[l] = l % 32
            sb[l] = -1.0 if l < 64 else 1.0

    m_ar = np.zeros((A_HEADS, 128), np.float32)
    for l in range(128):
        m_ar[(l % 64) // 8, l] = 1.0
    m_an = np.zeros((A_HEADS, _W_QAN), np.float32)
    for h in range(A_HEADS):
        m_an[h, h * A_NOPE:(h + 1) * A_NOPE] = 1.0
    m_iq = np.zeros((IDX_HEADS, _W_IQ), np.float32)
    for sl in range(2):
        for l in range(128):
            m_iq[sl * 4 + _idx_lane(l)[0], sl * 128 + l] = 1.0
    m_bq = np.zeros((B_GROUP_HEADS, B_GROUP_W), np.float32)
    for p in range(2):
        for l in range(128):
            m_bq[2 * p + _b_lane(l)[0], p * 128 + l] = 1.0
    m_bv = np.zeros((B_GROUP_HEADS, B_GROUP_W), np.float32)
    for j in range(B_GROUP_HEADS):
        m_bv[j, j * HEAD_DIM:(j + 1) * HEAD_DIM] = 1.0
    return dict(cols=cols, keep=keep, fa=fa, sa=sa, fi=fi, si=si, fb=fb, sb=sb,
                m_ar=m_ar, m_an=m_an, m_iq=m_iq, m_bq=m_bq, m_bv=m_bv)


_LAYOUT = _build_layout()


def _cparams(*sem):
    return pltpu.CompilerParams(dimension_semantics=sem, vmem_limit_bytes=VMEM_LIMIT)


def _layer_norm(v, g, b):
    mu = jnp.mean(v, axis=-1, keepdims=True)
    var = jnp.mean(jnp.square(v - mu), axis=-1, keepdims=True)
    return (v - mu) * lax.rsqrt(var + LN_EPS) * g + b


def _dot_nt(a, b):
    return lax.dot_general(a, b, (((1,), (1,)), ((), ())), preferred_element_type=F32)


def _rope_table_kernel(pos_ref, freq_ref, sign_ref, out_ref):
    pos = pos_ref[...].astype(F32)
    for k in range(3):
        ang = pos * freq_ref[k:k + 1, :]
        out_ref[:, (2 * k) * LANES:(2 * k + 1) * LANES] = jnp.cos(ang)
        out_ref[:, (2 * k + 1) * LANES:(2 * k + 2) * LANES] = jnp.sin(ang) * sign_ref[k:k + 1, :]


def _rope_tables(pos_col, freq, sign, tile):
    n = pos_col.shape[0]
    return pl.pallas_call(
        _rope_table_kernel,
        out_shape=jax.ShapeDtypeStruct((n, 6 * LANES), F32),
        grid=(n // tile,),
        in_specs=[pl.BlockSpec((tile, 1), lambda i: (i, 0)),
                  pl.BlockSpec((3, LANES), lambda i: (0, 0)),
                  pl.BlockSpec((3, LANES), lambda i: (0, 0))],
        out_specs=pl.BlockSpec((tile, 6 * LANES), lambda i: (i, 0)),
        compiler_params=_cparams("parallel"),
        name="rope_tables",
    )(pos_col, freq, sign)


def _rope_slabs(y, cos, sin):
    outs = []
    for s in range(y.shape[1] // LANES):
        ys = y[:, s * LANES:(s + 1) * LANES]
        outs.append(ys * cos + pltpu.roll(ys, 64, 1) * sin)
    return outs[0] if len(outs) == 1 else jnp.concatenate(outs, axis=1)


def _proj_kernel(x_ref, w_ref, tab_ref, gkv_ref, bg_ref,
                 qa_ref, ka_ref, iq_ref, ik_ref, iw_ref, bq_ref, bk_ref, bv_ref, g_ref):
    xb = x_ref[...].astype(BF16)

    def proj(c0, width):
        return jnp.dot(xb, w_ref[:, c0:c0 + width], preferred_element_type=F32)

    cos_a, sin_a = tab_ref[:, 0:128], tab_ref[:, 128:256]
    cos_i, sin_i = tab_ref[:, 256:384], tab_ref[:, 384:512]
    cos_b, sin_b = tab_ref[:, 512:640], tab_ref[:, 640:768]

    qa_ref[:, 0:_W_QAR] = _rope_slabs(proj(_C_QAR, _W_QAR), cos_a, sin_a).astype(BF16)
    qa_ref[:, _W_QAR:] = proj(_C_QAN, _W_QAN).astype(BF16)
    ckv = proj(_C_CKV, _W_CKV)
    ckv = ckv * lax.rsqrt(jnp.mean(jnp.square(ckv), axis=-1, keepdims=True) + RMS_EPS) * gkv_ref[...]
    ka_ref[:, 0:_W_CKV] = ckv.astype(BF16)
    ka_ref[:, _W_CKV:] = _rope_slabs(proj(_C_AKR, _W_AKR), cos_a, sin_a).astype(BF16)
    iq_ref[...] = _rope_slabs(proj(_C_IQ, _W_IQ), cos_i, sin_i).astype(BF16)
    ik_ref[...] = _rope_slabs(proj(_C_IK, _W_IK), cos_i, sin_i).astype(BF16)
    iw_ref[...] = proj(_C_IW, _W_IW) * ((IDX_HEADS * IDX_DIM) ** -0.5)
    bq_ref[...] = _rope_slabs(proj(_C_BQ, _W_B), cos_b, sin_b).astype(BF16)
    bk_ref[...] = _rope_slabs(proj(_C_BK, _W_B), cos_b, sin_b).astype(BF16)
    bv_ref[...] = proj(_C_BV, _W_B).astype(BF16)
    g_ref[...] = jax.nn.sigmoid(proj(_C_GA, 2 * D_MODEL) + bg_ref[...])


def _input_projection(xf, w_perm, tables, g_kv, b_gate, tile):
    n = xf.shape[0]
    row = lambda width: pl.BlockSpec((tile, width), lambda i: (i, 0))
    full = lambda a: pl.BlockSpec(a.shape, lambda i: (0,) * a.ndim)
    out_w = [(_W_QAR + _W_QAN, BF16), (_W_CKV + _W_AKR, BF16), (_W_IQ, BF16), (_W_IK, BF16),
             (_W_IW, F32), (_W_B, BF16), (_W_B, BF16), (_W_B, BF16), (2 * D_MODEL, F32)]
    return pl.pallas_call(
        _proj_kernel,
        out_shape=[jax.ShapeDtypeStruct((n, w), dt) for w, dt in out_w],
        grid=(n // tile,),
        in_specs=[row(D_MODEL), full(w_perm), row(6 * LANES), full(g_kv), full(b_gate)],
        out_specs=[row(w) for w, _ in out_w],
        compiler_params=_cparams("parallel"),
        name="input_projection",
    )(xf, w_perm, tables, g_kv, b_gate)


def _dilated_kernel(q_ref, kc_ref, kp_ref, vc_ref, vp_ref, mq_ref, mv_ref, o_ref, lse_ref,
                    kwin_ref, vwin_ref, *, tq):
    first = pl.program_id(2) == 0
    kwin_ref[0:Q_BLOCK, :] = kp_ref[...]
    kwin_ref[Q_BLOCK:, :] = kc_ref[...]
    vwin_ref[0:Q_BLOCK, :] = vp_ref[...]
    vwin_ref[Q_BLOCK:, :] = vc_ref[...]
    t = lax.broadcasted_iota(jnp.int32, (Q_BLOCK, 2 * Q_BLOCK), 0)
    c = lax.broadcasted_iota(jnp.int32, (Q_BLOCK, 2 * Q_BLOCK), 1)
    diff = t + Q_BLOCK - c
    band = (diff >= 0) & (diff <= Q_BLOCK)
    scale = HEAD_DIM ** -0.5
    for sb in range(tq // Q_BLOCK):
        valid = band
        if sb == 0:
            valid = band & (c >= jnp.where(first, Q_BLOCK, 0))
        bias = jnp.where(valid, 0.0, NEG_BIG).astype(F32)
        q = q_ref[sb * Q_BLOCK:(sb + 1) * Q_BLOCK, :] * scale
        kw = kwin_ref[sb * Q_BLOCK:(sb + 2) * Q_BLOCK, :]
        vw = vwin_ref[sb * Q_BLOCK:(sb + 2) * Q_BLOCK, :]
        qs = jnp.concatenate([q * mq_ref[j:j + 1, :] for j in range(B_GROUP_HEADS)], axis=0)
        s = _dot_nt(qs, kw)
        o_acc = jnp.zeros((Q_BLOCK, B_GROUP_W), F32)
        lse_acc = jnp.zeros((Q_BLOCK, B_GROUP_W), F32)
        for j in range(B_GROUP_HEADS):
            sj = s[j * Q_BLOCK:(j + 1) * Q_BLOCK, :] + bias
            m = jnp.max(sj, axis=-1, keepdims=True)
            e = jnp.exp(sj - m)
            den = jnp.sum(e, axis=-1, keepdims=True)
            pv = jnp.dot(e.astype(BF16), vw, preferred_element_type=F32)
            mv = mv_ref[j:j + 1, :]
            o_acc = o_acc + (pv / den) * mv
            lse_acc = lse_acc + (m + jnp.log(den)) * mv
        o_ref[sb * Q_BLOCK:(sb + 1) * Q_BLOCK, :] = o_acc
        lse_ref[sb * Q_BLOCK:(sb + 1) * Q_BLOCK, :] = lse_acc


def _dilated_group(bq, bk, bv, mq, mv, g, dil):
    b, s, _ = bq.shape
    sub = s // dil
    tq = min(512, sub)
    nblk = tq // Q_BLOCK
    ng = len(B_PATTERNS)
    view = lambda a: a.reshape(b, sub, dil * a.shape[2])
    cur = pl.BlockSpec((None, tq, B_GROUP_W), lambda bi, r, i: (bi, i, r * ng + g))
    prev = pl.BlockSpec((None, Q_BLOCK, B_GROUP_W),
                        lambda bi, r, i: (bi, jnp.maximum(i * nblk - 1, 0), r * ng + g))
    const = lambda a: pl.BlockSpec(a.shape, lambda bi, r, i: (0, 0))
    out = pl.BlockSpec((None, tq, B_GROUP_W), lambda bi, r, i: (bi, i, r))
    o, lse = pl.pallas_call(
        functools.partial(_dilated_kernel, tq=tq),
        out_shape=[jax.ShapeDtypeStruct((b, sub, dil * B_GROUP_W), F32)] * 2,
        grid=(b, dil, sub // tq),
        in_specs=[cur, cur, prev, cur, prev, const(mq), const(mv)],
        out_specs=[out, out],
        scratch_shapes=[pltpu.VMEM((tq + Q_BLOCK, B_GROUP_W), BF16)] * 2,
        compiler_params=_cparams("parallel", "parallel", "arbitrary"),
        name=f"dilated_attention_d{dil}",
    )(view(bq), view(bk), view(bk), view(bv), view(bv), mq, mv)
    return o.reshape(b * s, B_GROUP_W), lse.reshape(b * s, B_GROUP_W)


_TK_IDX = 512
_TK_ATT = 256


def _dsa_kernel(qa_ref, iq_ref, iw_ref, ikt_ref, ka_ref, wuk_ref, wuv_ref, mar_ref, man_ref, miq_ref,
                out_ref, key_ref, qcat_ref, m_ref, l_ref, acc_ref, *, n_sel):
    i = pl.program_id(1)
    q0 = i * Q_BLOCK
    n_keys = q0 + Q_BLOCK
    rows = A_HEADS * Q_BLOCK

    iq = iq_ref[...]
    iq_stack = jnp.concatenate([iq * miq_ref[h:h + 1, :] for h in range(IDX_HEADS)], axis=0)
    iw = iw_ref[...]
    iw_b = [jnp.broadcast_to(iw[:, h:h + 1], (Q_BLOCK, _TK_IDX)) for h in range(IDX_HEADS)]
    tq_col = q0 + lax.broadcasted_iota(jnp.int32, (Q_BLOCK, _TK_IDX), 0)
    kcol = lax.broadcasted_iota(jnp.int32, (Q_BLOCK, _TK_IDX), 1)

    def score_tile(j, carry):
        k0 = pl.multiple_of(j * _TK_IDX, _TK_IDX)
        s = _dot_nt(iq_stack, ikt_ref[pl.ds(k0, _TK_IDX), :])
        sc = jnp.zeros((Q_BLOCK, _TK_IDX), F32)
        for h in range(IDX_HEADS):
            sc = sc + jnp.maximum(s[h * Q_BLOCK:(h + 1) * Q_BLOCK, :], 0.0) * iw_b[h]
        bits = lax.bitcast_convert_type(sc + 0.0, jnp.int32)
        okey = bits ^ ((bits >> 31) & jnp.int32(0x7FFFFFFF))
        okey = jnp.where(kcol + k0 <= tq_col, okey, jnp.int32(INT_MIN))
        key_ref[:, pl.ds(k0, _TK_IDX)] = okey
        return carry

    n_idx_tiles = (n_keys + _TK_IDX - 1) // _TK_IDX
    lax.fori_loop(0, n_idx_tiles, score_tile, 0)

    ones = jnp.ones((_TK_IDX, LANES), BF16)

    def count_ge(thr):
        thr_t = jnp.concatenate([thr] * (_TK_IDX // LANES), axis=1)

        def body(j, cnt):
            k0 = pl.multiple_of(j * _TK_IDX, _TK_IDX)
            ge = jnp.where(key_ref[:, pl.ds(k0, _TK_IDX)] >= thr_t, 1.0, 0.0).astype(BF16)
            return cnt + jnp.dot(ge, ones, preferred_element_type=F32)
        return lax.fori_loop(0, n_idx_tiles, body, jnp.zeros((Q_BLOCK, LANES), F32))

    def bit_step(b, prefix):
        trial = prefix | (jnp.int32(1) << (31 - b))
        cnt = count_ge(trial ^ jnp.int32(INT_MIN))
        return jnp.where(cnt >= float(n_sel), trial, prefix)

    prefix = lax.fori_loop(0, 32, bit_step, jnp.zeros((Q_BLOCK, LANES), jnp.int32))
    thr = prefix ^ jnp.int32(INT_MIN)
    thr_t = jnp.concatenate([thr] * (_TK_ATT // LANES), axis=1)

    q_rope = qa_ref[:, 0:_W_QAR]
    q_nope = qa_ref[:, _W_QAR:]
    scale = HEAD_DIM ** -0.5
    for h in range(A_HEADS):
        q_lat = jnp.dot(q_nope * man_ref[h:h + 1, :], wuk_ref[...], preferred_element_type=F32)
        qcat_ref[h * Q_BLOCK:(h + 1) * Q_BLOCK, 0:A_KV_RANK] = q_lat.astype(BF16) * scale
        qcat_ref[h * Q_BLOCK:(h + 1) * Q_BLOCK, A_KV_RANK:] = q_rope * mar_ref[h:h + 1, :] * scale

    m_ref[...] = jnp.full((rows, LANES), NEG_BIG, F32)
    l_ref[...] = jnp.zeros((rows, LANES), F32)
    acc_ref[...] = jnp.zeros((rows, A_KV_RANK), F32)
    tq_a = q0 + lax.broadcasted_iota(jnp.int32, (Q_BLOCK, _TK_ATT), 0)
    kcol_a = lax.broadcasted_iota(jnp.int32, (Q_BLOCK, _TK_ATT), 1)

    def attn_tile(j, carry):
        k0 = pl.multiple_of(j * _TK_ATT, _TK_ATT)
        ka = ka_ref[pl.ds(k0, _TK_ATT), :]
        sel = (key_ref[:, pl.ds(k0, _TK_ATT)] >= thr_t) & (kcol_a + k0 <= tq_a)
        bias = jnp.where(sel, 0.0, NEG_BIG).astype(F32)
        s = _dot_nt(qcat_ref[...], ka)
        s = jnp.concatenate([s[h * Q_BLOCK:(h + 1) * Q_BLOCK, :] + bias for h in range(A_HEADS)], axis=0)
        m_old = m_ref[...]
        m_new = jnp.maximum(m_old, jnp.max(s, axis=-1, keepdims=True))
        alpha = jnp.exp(m_old - m_new)
        p = jnp.exp(s - jnp.concatenate([m_new] * (_TK_ATT // LANES), axis=1))
        l_ref[...] = alpha * l_ref[...] + jnp.sum(p, axis=-1, keepdims=True)
        pv = jnp.dot(p.astype(BF16), ka[:, 0:A_KV_RANK], preferred_element_type=F32)
        acc_ref[...] = acc_ref[...] * jnp.concatenate([alpha] * (A_KV_RANK // LANES), axis=1) + pv
        m_ref[...] = m_new
        return carry

    lax.fori_loop(0, (n_keys + _TK_ATT - 1) // _TK_ATT, attn_tile, 0)

    inv_l = 1.0 / l_ref[...]
    o_lat = (acc_ref[...] * jnp.concatenate([inv_l] * (A_KV_RANK // LANES), axis=1)).astype(BF16)
    out = jnp.zeros((Q_BLOCK, A_HEADS * HEAD_DIM), F32)
    for h in range(A_HEADS):
        out = out + jnp.dot(o_lat[h * Q_BLOCK:(h + 1) * Q_BLOCK, :], wuv_ref[h],
                            preferred_element_type=F32)
    out_ref[...] = out.astype(BF16)


def _dsa_mixer(qa, iq, iw, ikt, ka, wuk, wuv, m_ar, m_an, m_iq, b, s):
    n_sel = min(A_TOPK_MAX, s // 4)
    nq = s // Q_BLOCK
    rows = A_HEADS * Q_BLOCK
    blk = lambda width: pl.BlockSpec((Q_BLOCK, width), lambda bi, i: (bi * nq + i, 0))
    seq = lambda width: pl.BlockSpec((s, width), lambda bi, i: (bi, 0))
    const = lambda a: pl.BlockSpec(a.shape, lambda bi, i: (0,) * a.ndim)
    return pl.pallas_call(
        functools.partial(_dsa_kernel, n_sel=n_sel),
        out_shape=jax.ShapeDtypeStruct((b * s, A_HEADS * HEAD_DIM), BF16),
        grid=(b, nq),
        in_specs=[blk(_W_QAR + _W_QAN), blk(_W_IQ), blk(_W_IW), seq(_W_IK), seq(_W_CKV + _W_AKR),
                  const(wuk), const(wuv), const(m_ar), const(m_an), const(m_iq)],
        out_specs=blk(A_HEADS * HEAD_DIM),
        scratch_shapes=[pltpu.VMEM((Q_BLOCK, s), jnp.int32),
                        pltpu.VMEM((rows, A_KV_RANK + _W_AKR), BF16),
                        pltpu.VMEM((rows, LANES), F32),
                        pltpu.VMEM((rows, LANES), F32),
                        pltpu.VMEM((rows, A_KV_RANK), F32)],
        compiler_params=_cparams("parallel", "arbitrary"),
        name="dsa_attention",
    )(qa, iq, iw, ikt, ka, wuk, wuv, m_ar, m_an, m_iq)


def _merge_kernel(x_ref, a_ref, o1_ref, o2_ref, o3_ref, l1_ref, l2_ref, l3_ref, g_ref,
                  wa_ref, wb_ref, wo_ref, lg_ref, lb_ref, h_ref):
    lses = [l1_ref[...], l2_ref[...], l3_ref[...]]
    outs = [o1_ref[...], o2_ref[...], o3_ref[...]]
    mx = jnp.maximum(jnp.maximum(lses[0], lses[1]), lses[2])
    es = [jnp.exp(l - mx) for l in lses]
    den = es[0] + es[1] + es[2]
    b_out = (es[0] / den) * outs[0] + (es[1] / den) * outs[1] + (es[2] / den) * outs[2]
    ya = jnp.dot(a_ref[...], wa_ref[...], preferred_element_type=F32)
    yb = jnp.dot(b_out.astype(BF16), wb_ref[...], preferred_element_type=F32)
    pre = g_ref[:, 0:D_MODEL] * ya + g_ref[:, D_MODEL:] * yb
    mix = jnp.dot(pre.astype(BF16), wo_ref[...], preferred_element_type=F32)
    h_ref[...] = _layer_norm(ALPHA * x_ref[...] + mix, lg_ref[...], lb_ref[...])


def _merge(xf, a_out, b_parts, gates, wa, wb, wo, ln_g, ln_b, tile):
    n = xf.shape[0]
    row = lambda width: pl.BlockSpec((tile, width), lambda i: (i, 0))
    full = lambda a: pl.BlockSpec(a.shape, lambda i: (0,) * a.ndim)
    (o1, l1), (o2, l2), (o3, l3) = b_parts
    return pl.pallas_call(
        _merge_kernel,
        out_shape=jax.ShapeDtypeStruct((n, D_MODEL), F32),
        grid=(n // tile,),
        in_specs=[row(D_MODEL), row(A_HEADS * HEAD_DIM)] + [row(B_GROUP_W)] * 6 + [row(2 * D_MODEL),
                  full(wa), full(wb), full(wo), full(ln_g), full(ln_b)],
        out_specs=row(D_MODEL),
        compiler_params=_cparams("parallel"),
        name="merge_output_projection",
    )(xf, a_out, o1, o2, o3, l1, l2, l3, gates, wa, wb, wo, ln_g, ln_b)


def _first_max(v):
    m = jnp.max(v, axis=0, keepdims=True)
    idx = lax.broadcasted_iota(jnp.int32, v.shape, 0)
    first = jnp.min(jnp.where(v == m, idx, v.shape[0]), axis=0, keepdims=True)
    return m, idx == first


def _router_kernel(h_ref, wr_ref, rb_ref, gate_ref):
    t = h_ref.shape[0]
    gs = N_EXPERTS // N_GROUPS
    logits = lax.dot_general(wr_ref[...], h_ref[...], (((1,), (1,)), ((), ())),
                             precision=lax.Precision.HIGHEST, preferred_element_type=F32)
    scores = jax.nn.sigmoid(logits)
    biased = scores + rb_ref[...]
    gscores = []
    for g in range(N_GROUPS):
        blk = biased[g * gs:(g + 1) * gs, :]
        m1, hit = _first_max(blk)
        m2 = jnp.max(jnp.where(hit, -jnp.inf, blk), axis=0, keepdims=True)
        gscores.append(m1 + m2)
    gscore = jnp.concatenate(gscores, axis=0)
    gsel = jnp.zeros((N_GROUPS, t), F32)
    for _ in range(TOPK_GROUPS):
        _, hit = _first_max(gscore)
        gsel = jnp.where(hit, 1.0, gsel)
        gscore = jnp.where(hit, -jnp.inf, gscore)
    esel = jnp.concatenate([jnp.broadcast_to(gsel[g:g + 1, :], (gs, t)) for g in range(N_GROUPS)], axis=0)
    cand = jnp.where(esel > 0.0, biased, -jnp.inf)
    top_s = jnp.zeros((N_EXPERTS, t), F32)
    for _ in range(TOP_K):
        _, hit = _first_max(cand)
        top_s = jnp.where(hit, scores, top_s)
        cand = jnp.where(hit, -jnp.inf, cand)
    gate = top_s / jnp.sum(top_s, axis=0, keepdims=True) * ROUTED_SCALE
    gate_ref[...] = jnp.concatenate([gate, jnp.zeros((LANES - N_EXPERTS, t), F32)], axis=0).T


def _router(h, wr_t, rb_col, tile):
    n = h.shape[0]
    return pl.pallas_call(
        _router_kernel,
        out_shape=jax.ShapeDtypeStruct((n, LANES), F32),
        grid=(n // tile,),
        in_specs=[pl.BlockSpec((tile, D_MODEL), lambda i: (i, 0)),
                  pl.BlockSpec(wr_t.shape, lambda i: (0, 0)),
                  pl.BlockSpec(rb_col.shape, lambda i: (0, 0))],
        out_specs=pl.BlockSpec((tile, LANES), lambda i: (i, 0)),
        compiler_params=_cparams("parallel"),
        name="moe_router",
    )(h, wr_t, rb_col)


_EXPERTS_PER_STEP = 2


def _swiglu(hb, w1, w3, w2):
    a = jnp.dot(hb, w1, preferred_element_type=F32)
    b = jnp.dot(hb, w3, preferred_element_type=F32)
    return jnp.dot((jax.nn.silu(a) * b).astype(BF16), w2, preferred_element_type=F32)


def _experts_kernel(h_ref, gate_ref, w1_ref, w3_ref, w2_ref, s1_ref, s3_ref, s2_ref, lg_ref, lb_ref,
                    y_ref, hb_ref, acc_ref):
    step = pl.program_id(1)

    @pl.when(step == 0)
    def _():
        hb = h_ref[...].astype(BF16)
        hb_ref[...] = hb
        acc_ref[...] = _swiglu(hb, s1_ref[...], s3_ref[...], s2_ref[...])

    hb = hb_ref[...]
    gate = gate_ref[...]
    lane = lax.broadcasted_iota(jnp.int32, gate.shape, 1)
    for k in range(_EXPERTS_PER_STEP):
        e = step * _EXPERTS_PER_STEP + k
        g_col = jnp.sum(jnp.where(lane == e, gate, 0.0), axis=1, keepdims=True)
        acc_ref[...] += _swiglu(hb, w1_ref[k], w3_ref[k], w2_ref[k]) * g_col

    @pl.when(step == pl.num_programs(1) - 1)
    def _():
        y_ref[...] = _layer_norm(ALPHA * h_ref[...] + acc_ref[...], lg_ref[...], lb_ref[...])


def _experts(h, gate, w1, w3, w2, s1, s3, s2, ln_g, ln_b, tile):
    n = h.shape[0]
    ec = _EXPERTS_PER_STEP
    row = lambda width: pl.BlockSpec((tile, width), lambda i, e: (i, 0))
    full = lambda a: pl.BlockSpec(a.shape, lambda i, e: (0,) * a.ndim)
    wspec = lambda a: pl.BlockSpec((ec,) + a.shape[1:], lambda i, e: (e, 0, 0))
    return pl.pallas_call(
        _experts_kernel,
        out_shape=jax.ShapeDtypeStruct((n, D_MODEL), F32),
        grid=(n // tile, N_EXPERTS // ec),
        in_specs=[row(D_MODEL), row(LANES), wspec(w1), wspec(w3), wspec(w2),
                  full(s1), full(s3), full(s2), full(ln_g), full(ln_b)],
        out_specs=row(D_MODEL),
        scratch_shapes=[pltpu.VMEM((tile, D_MODEL), BF16), pltpu.VMEM((tile, D_MODEL), F32)],
        compiler_params=_cparams("parallel", "arbitrary"),
        name="moe_experts",
    )(h, gate, w1, w3, w2, s1, s3, s2, ln_g, ln_b)


def _rope_freqs():
    inv16 = ROPE_THETA ** (-jnp.arange(0, ROPE_DIM, 2, dtype=F32) / ROPE_DIM)
    inv8 = ROPE_THETA ** (-jnp.arange(0, IDX_ROPE, 2, dtype=F32) / IDX_ROPE)

    def row(inv, fidx):
        return jnp.where(fidx >= 0, inv[np.maximum(fidx, 0)], 0.0)
    lay = _LAYOUT
    freq = jnp.stack([row(inv16, lay["fa"]), row(inv8, lay["fi"]), row(inv16, lay["fb"])])
    sign = jnp.asarray(np.stack([lay["sa"], lay["si"], lay["sb"]]))
    return freq.astype(F32), sign


def _layer(x, positions, w_in, b_gate, g_kv, w_uk, w_uv, w_branch_a, w_branch_b, w_o, ln1_g, ln1_b,
           w_router, router_bias, w1_e, w3_e, w2_e, ws1, ws3, ws2, ln2_g, ln2_b):
    b, s, d = x.shape
    n = b * s
    lay = _LAYOUT
    tile = min(256, n)
    xf = x.reshape(n, d)

    w_perm = (w_in[:, lay["cols"]] * lay["keep"]).astype(BF16)
    wuk = jnp.transpose(w_uk, (1, 2, 0)).reshape(A_HEADS * A_NOPE, A_KV_RANK).astype(BF16)
    wuv = jnp.transpose(w_uv, (1, 0, 2))
    wuv_pad = jnp.zeros((A_HEADS, A_KV_RANK, A_HEADS, HEAD_DIM), F32)
    wuv_pad = wuv_pad.at[jnp.arange(A_HEADS), :, jnp.arange(A_HEADS), :].set(wuv)
    wuv_pad = wuv_pad.reshape(A_HEADS, A_KV_RANK, A_HEADS * HEAD_DIM).astype(BF16)
    masks = {k: jnp.asarray(lay[k], BF16) for k in ("m_ar", "m_an", "m_iq", "m_bq")}
    m_bv = jnp.asarray(lay["m_bv"], F32)

    freq, sign = _rope_freqs()
    tables = _rope_tables(positions.reshape(n, 1), freq, sign, tile)
    qa, ka, iq, ikt, iw, bq, bk, bv, gates = _input_projection(
        xf, w_perm, tables, g_kv.reshape(1, -1), b_gate.reshape(1, -1), tile)

    a_out = _dsa_mixer(qa, iq, iw, ikt, ka, wuk, wuv_pad, masks["m_ar"], masks["m_an"], masks["m_iq"], b, s)
    b3 = lambda a: a.reshape(b, s, a.shape[1])
    b_parts = [_dilated_group(b3(bq), b3(bk), b3(bv), masks["m_bq"], m_bv, g, dil)
               for g, (_, dil) in enumerate(B_PATTERNS)]

    h = _merge(xf, a_out, b_parts, gates, w_branch_a.astype(BF16), w_branch_b.astype(BF16),
               w_o.astype(BF16), ln1_g.reshape(1, -1), ln1_b.reshape(1, -1), tile)

    gate = _router(h, w_router.T, router_bias.reshape(-1, 1), tile)
    y = _experts(h, gate, w1_e.astype(BF16), w3_e.astype(BF16), w2_e.astype(BF16),
                 ws1.astype(BF16), ws3.astype(BF16), ws2.astype(BF16),
                 ln2_g.reshape(1, -1), ln2_b.reshape(1, -1), min(1024, n))
    return y.reshape(b, s, d)


def kernel(x, positions, w_in, b_gate, g_kv, w_uk, w_uv, w_branch_a, w_branch_b, w_o, ln1_g, ln1_b,
           w_router, router_bias, w1_e, w3_e, w2_e, ws1, ws3, ws2, ln2_g, ln2_b):
    h = x
    for l in range(DEPTH):
        h = _layer(h, positions, w_in[l], b_gate[l], g_kv[l], w_uk[l], w_uv[l], w_branch_a[l],
                   w_branch_b[l], w_o[l], ln1_g[l], ln1_b[l], w_router[l], router_bias[l],
                   w1_e[l], w3_e[l], w2_e[l], ws1[l], ws3[l], ws2[l], ln2_g[l], ln2_b[l])
    return h
```

```python
import functools

import jax
import jax.numpy as jnp
import numpy as np
from jax import lax
from jax.experimental import pallas as pl
from jax.experimental.pallas import tpu as pltpu

F32 = jnp.float32
BF16 = jnp.bfloat16

D_MODEL = 1024
HEAD_DIM = 64
ROPE_DIM = 16
ROPE_THETA = 500000.0
Q_BLOCK = 128
A_HEADS = 8
A_NOPE = HEAD_DIM - ROPE_DIM
A_KV_RANK = 256
A_TOPK_MAX = 256
IDX_HEADS = 8
IDX_DIM = 32
IDX_ROPE = 8
B_PATTERNS = ((128, 1), (512, 4), (2048, 16))
B_GROUP_HEADS = 4
B_HEADS = B_GROUP_HEADS * len(B_PATTERNS)
B_GROUP_W = B_GROUP_HEADS * HEAD_DIM
N_EXPERTS = 64
TOP_K = 8
N_GROUPS = 8
TOPK_GROUPS = 4
D_EXPERT = 256
ROUTED_SCALE = 2.5
DEPTH = 1
ALPHA = (2.0 * DEPTH) ** 0.25
LN_EPS = 1e-5
RMS_EPS = 1e-6

LANES = 128
VMEM_LIMIT = 56 * 1024 * 1024
NEG_BIG = -1e30
INT_MIN = -(2 ** 31)

_OFF_AQ = 0
_OFF_CKV = _OFF_AQ + A_HEADS * HEAD_DIM
_OFF_AKR = _OFF_CKV + A_KV_RANK
_OFF_IQ = _OFF_AKR + ROPE_DIM
_OFF_IK = _OFF_IQ + IDX_HEADS * IDX_DIM
_OFF_IW = _OFF_IK + IDX_DIM
_OFF_BQ = _OFF_IW + IDX_HEADS
_OFF_BK = _OFF_BQ + B_HEADS * HEAD_DIM
_OFF_BV = _OFF_BK + B_HEADS * HEAD_DIM
_OFF_GA = _OFF_BV + B_HEADS * HEAD_DIM
_OFF_GB = _OFF_GA + D_MODEL
_IN_TOTAL = _OFF_GB + D_MODEL

_W_QAR, _W_QAN, _W_CKV, _W_AKR = 128, A_HEADS * A_NOPE, A_KV_RANK, 128
_W_IQ, _W_IK, _W_IW = 256, 256, 128
_W_B = B_HEADS * HEAD_DIM
_C_QAR = 0
_C_QAN = _C_QAR + _W_QAR
_C_CKV = _C_QAN + _W_QAN
_C_AKR = _C_CKV + _W_CKV
_C_IQ = _C_AKR + _W_AKR
_C_IK = _C_IQ + _W_IQ
_C_IW = _C_IK + _W_IK
_C_BQ = _C_IW + _W_IW
_C_BK = _C_BQ + _W_B
_C_BV = _C_BK + _W_B
_C_GA = _C_BV + _W_B
_C_GB = _C_GA + D_MODEL
_P_TOTAL = _C_GB + D_MODEL


def _idx_lane(l):
    if l < 16:
        return l // 4, l % 4
    if l < 64:
        return (l - 16) // 12, 8 + (l - 16) % 12
    if l < 80:
        return (l - 64) // 4, 4 + (l - 64) % 4
    return (l - 80) // 12, 20 + (l - 80) % 12


def _b_lane(l):
    half, r = l // 64, l % 64
    which, rr = r // 32, r % 32
    if rr < 8:
        return which, half * 8 + rr
    return which, 16 + half * 24 + (rr - 8)


def _build_layout():
    cols = np.zeros((_P_TOTAL,), np.int32)
    keep = np.ones((_P_TOTAL,), np.float32)
    for l in range(128):
        half, h, f = l // 64, (l % 64) // 8, l % 8
        cols[_C_QAR + l] = _OFF_AQ + h * HEAD_DIM + half * 8 + f
        cols[_C_AKR + l] = _OFF_AKR + half * 8 + f
    for h in range(A_HEADS):
        for j in range(A_NOPE):
            cols[_C_QAN + h * A_NOPE + j] = _OFF_AQ + h * HEAD_DIM + ROPE_DIM + j
    cols[_C_CKV:_C_CKV + _W_CKV] = _OFF_CKV + np.arange(_W_CKV)
    for sl in range(2):
        for l in range(128):
            hh, d = _idx_lane(l)
            cols[_C_IQ + sl * 128 + l] = _OFF_IQ + (sl * 4 + hh) * IDX_DIM + d
            cols[_C_IK + sl * 128 + l] = _OFF_IK + d
    cols[_C_IW:_C_IW + IDX_HEADS] = _OFF_IW + np.arange(IDX_HEADS)
    keep[_C_IW + IDX_HEADS:_C_IW + _W_IW] = 0.0
    for p in range(B_HEADS // 2):
        for l in range(128):
            which, d = _b_lane(l)
            h = 2 * p + which
            cols[_C_BQ + p * 128 + l] = _OFF_BQ + h * HEAD_DIM + d
            cols[_C_BK + p * 128 + l] = _OFF_BK + h * HEAD_DIM + d
    cols[_C_BV:_C_BV + _W_B] = _OFF_BV + np.arange(_W_B)
    cols[_C_GA:_C_GA + D_MODEL] = _OFF_GA + np.arange(D_MODEL)
    cols[_C_GB:_C_GB + D_MODEL] = _OFF_GB + np.arange(D_MODEL)

    fa = np.array([l % 8 for l in range(128)])
    sa = np.array([-1.0 if l < 64 else 1.0 for l in range(128)], np.float32)
    fi = np.full((128,), -1)
    si = np.zeros((128,), np.float32)
    fb = np.full((128,), -1)
    sb = np.zeros((128,), np.float32)
    for l in range(128):
        if l < 16 or 64 <= l < 80:
            fi[l] = l % 4
            si[l] = -1.0 if l < 64 else 1.0
        if l % 32 < 8:
            fb[l] = l % 32
            sb[l] = -1.0 if l < 64 else 1.0

    m_ar = np.zeros((A_HEADS, 128), np.float32)
    for l in range(128):
        m_ar[(l % 64) // 8, l] = 1.0
    m_an = np.zeros((A_HEADS, _W_QAN), np.float32)
    for h in range(A_HEADS):
        m_an[h, h * A_NOPE:(h + 1) * A_NOPE] = 1.0
    m_iq = np.zeros((IDX_HEADS, _W_IQ), np.float32)
    for sl in range(2):
        for l in range(128):
            m_iq[sl * 4 + _idx_lane(l)[0], sl * 128 + l] = 1.0
    m_bq = np.zeros((B_GROUP_HEADS, B_GROUP_W), np.float32)
    for p in range(2):
        for l in range(128):
            m_bq[2 * p + _b_lane(l)[0], p * 128 + l] = 1.0
    m_bv = np.zeros((B_GROUP_HEADS, B_GROUP_W), np.float32)
    for j in range(B_GROUP_HEADS):
        m_bv[j, j * HEAD_DIM:(j + 1) * HEAD_DIM] = 1.0
    return dict(cols=cols, keep=keep, fa=fa, sa=sa, fi=fi, si=si, fb=fb, sb=sb,
                m_ar=m_ar, m_an=m_an, m_iq=m_iq, m_bq=m_bq, m_bv=m_bv)


_LAYOUT = _build_layout()


def _cparams(*sem):
    return pltpu.CompilerParams(dimension_semantics=sem, vmem_limit_bytes=VMEM_LIMIT)


def _layer_norm(v, g, b):
    mu = jnp.mean(v, axis=-1, keepdims=True)
    var = jnp.mean(jnp.square(v - mu), axis=-1, keepdims=True)
    return (v - mu) * lax.rsqrt(var + LN_EPS) * g + b


def _dot_nt(a, b):
    return lax.dot_general(a, b, (((1,), (1,)), ((), ())), preferred_element_type=F32)


def _rope_table_kernel(pos_ref, freq_ref, sign_ref, out_ref):
    pos = pos_ref[...].astype(F32)
    for k in range(3):
        ang = pos * freq_ref[k:k + 1, :]
        out_ref[:, (2 * k) * LANES:(2 * k + 1) * LANES] = jnp.cos(ang)
        out_ref[:, (2 * k + 1) * LANES:(2 * k + 2) * LANES] = jnp.sin(ang) * sign_ref[k:k + 1, :]


def _rope_tables(pos_col, freq, sign, tile):
    n = pos_col.shape[0]
    return pl.pallas_call(
        _rope_table_kernel,
        out_shape=jax.ShapeDtypeStruct((n, 6 * LANES), F32),
        grid=(n // tile,),
        in_specs=[pl.BlockSpec((tile, 1), lambda i: (i, 0)),
                  pl.BlockSpec((3, LANES), lambda i: (0, 0)),
                  pl.BlockSpec((3, LANES), lambda i: (0, 0))],
        out_specs=pl.BlockSpec((tile, 6 * LANES), lambda i: (i, 0)),
        compiler_params=_cparams("parallel"),
        name="rope_tables",
    )(pos_col, freq, sign)


def _rope_slabs(y, cos, sin):
    outs = []
    for s in range(y.shape[1] // LANES):
        ys = y[:, s * LANES:(s + 1) * LANES]
        outs.append(ys * cos + pltpu.roll(ys, 64, 1) * sin)
    return outs[0] if len(outs) == 1 else jnp.concatenate(outs, axis=1)


def _proj_kernel(x_ref, w_ref, tab_ref, gkv_ref, bg_ref,
                 qa_ref, ka_ref, ckvt_ref, iq_ref, ik_ref, iw_ref, bq_ref, bk_ref, bv_ref, g_ref):
    xb = x_ref[...].astype(BF16)

    def proj(c0, width):
        return jnp.dot(xb, w_ref[:, c0:c0 + width], preferred_element_type=F32)

    cos_a, sin_a = tab_ref[:, 0:128], tab_ref[:, 128:256]
    cos_i, sin_i = tab_ref[:, 256:384], tab_ref[:, 384:512]
    cos_b, sin_b = tab_ref[:, 512:640], tab_ref[:, 640:768]

    qa_ref[:, 0:_W_QAR] = _rope_slabs(proj(_C_QAR, _W_QAR), cos_a, sin_a).astype(BF16)
    qa_ref[:, _W_QAR:] = proj(_C_QAN, _W_QAN).astype(BF16)
    ckv = proj(_C_CKV, _W_CKV)
    ckv = ckv * lax.rsqrt(jnp.mean(jnp.square(ckv), axis=-1, keepdims=True) + RMS_EPS) * gkv_ref[...]
    ka_ref[:, 0:_W_CKV] = ckv.astype(BF16)
    ckvt_ref[...] = ckv.T.astype(BF16)
    ka_ref[:, _W_CKV:] = _rope_slabs(proj(_C_AKR, _W_AKR), cos_a, sin_a).astype(BF16)
    iq_ref[...] = _rope_slabs(proj(_C_IQ, _W_IQ), cos_i, sin_i).astype(BF16)
    ik_ref[...] = _rope_slabs(proj(_C_IK, _W_IK), cos_i, sin_i).astype(BF16)
    iw_ref[...] = proj(_C_IW, _W_IW) * ((IDX_HEADS * IDX_DIM) ** -0.5)
    bq_ref[...] = _rope_slabs(proj(_C_BQ, _W_B), cos_b, sin_b).astype(BF16)
    bk_ref[...] = _rope_slabs(proj(_C_BK, _W_B), cos_b, sin_b).astype(BF16)
    bv_ref[...] = proj(_C_BV, _W_B).astype(BF16)
    g_ref[...] = jax.nn.sigmoid(proj(_C_GA, 2 * D_MODEL) + bg_ref[...])


def _input_projection(xf, w_perm, tables, g_kv, b_gate, tile):
    n = xf.shape[0]
    row = lambda width: pl.BlockSpec((tile, width), lambda i: (i, 0))
    full = lambda a: pl.BlockSpec(a.shape, lambda i: (0,) * a.ndim)
    out_w = [(_W_QAR + _W_QAN, BF16), (_W_CKV + _W_AKR, BF16), None, (_W_IQ, BF16), (_W_IK, BF16),
             (_W_IW, F32), (_W_B, BF16), (_W_B, BF16), (_W_B, BF16), (2 * D_MODEL, F32)]
    shapes = [jax.ShapeDtypeStruct((A_KV_RANK, n), BF16) if o is None else jax.ShapeDtypeStruct((n, o[0]), o[1])
              for o in out_w]
    specs = [pl.BlockSpec((A_KV_RANK, tile), lambda i: (0, i)) if o is None else row(o[0]) for o in out_w]
    return pl.pallas_call(
        _proj_kernel,
        out_shape=shapes,
        grid=(n // tile,),
        in_specs=[row(D_MODEL), full(w_perm), row(6 * LANES), full(g_kv), full(b_gate)],
        out_specs=specs,
        compiler_params=_cparams("parallel"),
        name="input_projection",
    )(xf, w_perm, tables, g_kv, b_gate)


def _dilated_kernel(q_ref, kc_ref, kp_ref, vc_ref, vp_ref, mq_ref, mv_ref, o_ref, lse_ref,
                    kwin_ref, vwin_ref, *, tq):
    first = pl.program_id(2) == 0
    kwin_ref[0:Q_BLOCK, :] = kp_ref[...]
    kwin_ref[Q_BLOCK:, :] = kc_ref[...]
    vwin_ref[0:Q_BLOCK, :] = vp_ref[...]
    vwin_ref[Q_BLOCK:, :] = vc_ref[...]
    t = lax.broadcasted_iota(jnp.int32, (Q_BLOCK, 2 * Q_BLOCK), 0)
    c = lax.broadcasted_iota(jnp.int32, (Q_BLOCK, 2 * Q_BLOCK), 1)
    diff = t + Q_BLOCK - c
    band = (diff >= 0) & (diff <= Q_BLOCK)
    scale = HEAD_DIM ** -0.5
    for sb in range(tq // Q_BLOCK):
        valid = band
        if sb == 0:
            valid = band & (c >= jnp.where(first, Q_BLOCK, 0))
        bias = jnp.where(valid, 0.0, NEG_BIG).astype(F32)
        q = q_ref[sb * Q_BLOCK:(sb + 1) * Q_BLOCK, :] * scale
        kw = kwin_ref[sb * Q_BLOCK:(sb + 2) * Q_BLOCK, :]
        vw = vwin_ref[sb * Q_BLOCK:(sb + 2) * Q_BLOCK, :]
        qs = jnp.concatenate([q * mq_ref[j:j + 1, :] for j in range(B_GROUP_HEADS)], axis=0)
        s = _dot_nt(qs, kw)
        o_acc = jnp.zeros((Q_BLOCK, B_GROUP_W), F32)
        lse_acc = jnp.zeros((Q_BLOCK, B_GROUP_W), F32)
        for j in range(B_GROUP_HEADS):
            sj = s[j * Q_BLOCK:(j + 1) * Q_BLOCK, :] + bias
            m = jnp.max(sj, axis=-1, keepdims=True)
            e = jnp.exp(sj - m)
            den = jnp.sum(e, axis=-1, keepdims=True)
            pv = jnp.dot(e.astype(BF16), vw, preferred_element_type=F32)
            mv = mv_ref[j:j + 1, :]
            o_acc = o_acc + (pv / den) * mv
            lse_acc = lse_acc + (m + jnp.log(den)) * mv
        o_ref[sb * Q_BLOCK:(sb + 1) * Q_BLOCK, :] = o_acc
        lse_ref[sb * Q_BLOCK:(sb + 1) * Q_BLOCK, :] = lse_acc


def _dilated_group(bq, bk, bv, mq, mv, g, dil):
    b, s, _ = bq.shape
    sub = s // dil
    tq = min(512, sub)
    nblk = tq // Q_BLOCK
    ng = len(B_PATTERNS)
    view = lambda a: a.reshape(b, sub, dil * a.shape[2])
    cur = pl.BlockSpec((None, tq, B_GROUP_W), lambda bi, r, i: (bi, i, r * ng + g))
    prev = pl.BlockSpec((None, Q_BLOCK, B_GROUP_W),
                        lambda bi, r, i: (bi, jnp.maximum(i * nblk - 1, 0), r * ng + g))
    const = lambda a: pl.BlockSpec(a.shape, lambda bi, r, i: (0, 0))
    out = pl.BlockSpec((None, tq, B_GROUP_W), lambda bi, r, i: (bi, i, r))
    o, lse = pl.pallas_call(
        functools.partial(_dilated_kernel, tq=tq),
        out_shape=[jax.ShapeDtypeStruct((b, sub, dil * B_GROUP_W), F32)] * 2,
        grid=(b, dil, sub // tq),
        in_specs=[cur, cur, prev, cur, prev, const(mq), const(mv)],
        out_specs=[out, out],
        scratch_shapes=[pltpu.VMEM((tq + Q_BLOCK, B_GROUP_W), BF16)] * 2,
        compiler_params=_cparams("parallel", "parallel", "arbitrary"),
        name=f"dilated_attention_d{dil}",
    )(view(bq), view(bk), view(bk), view(bv), view(bv), mq, mv)
    return o.reshape(b * s, B_GROUP_W), lse.reshape(b * s, B_GROUP_W)


_TK = 512


def _fold_rows(x, op):
    parts = [x[r:r + 8, :] for r in range(0, x.shape[0], 8)]
    while len(parts) > 1:
        parts = [op(parts[k], parts[k + 1]) for k in range(0, len(parts) - 1, 2)] + parts[len(parts) & ~1:]
    return parts[0]


def _dsa_kernel(qa_ref, iq_ref, iw_ref, ikt_ref, ka_ref, ckvt_ref, wuk_ref, wuv_ref, mar_ref, man_ref, miq_ref,
                out_ref, key_ref, iqs_ref, qcat_ref, acc_ref, *, n_sel):
    i = pl.program_id(1)
    q0 = i * Q_BLOCK
    n_keys = q0 + Q_BLOCK
    rows = A_HEADS * Q_BLOCK

    n_tiles = (n_keys + _TK - 1) // _TK
    tq_lane = q0 + lax.broadcasted_iota(jnp.int32, (_TK, Q_BLOCK), 1)
    krow = lax.broadcasted_iota(jnp.int32, (_TK, Q_BLOCK), 0)

    iq = iq_ref[...]
    for h in range(IDX_HEADS):
        iqs_ref[h * Q_BLOCK:(h + 1) * Q_BLOCK, :] = iq * miq_ref[h:h + 1, :]
    iw_t = iw_ref[...].T

    def score_tile(j, carry):
        k0 = pl.multiple_of(j * _TK, _TK)
        s = _dot_nt(ikt_ref[pl.ds(k0, _TK), :], iqs_ref[...])
        sc = jnp.zeros((_TK, Q_BLOCK), F32)
        for h in range(IDX_HEADS):
            sc = sc + jnp.maximum(s[:, h * Q_BLOCK:(h + 1) * Q_BLOCK], 0.0) * iw_t[h:h + 1, :]
        bits = lax.bitcast_convert_type(sc + 0.0, jnp.int32)
        okey = bits ^ ((bits >> 31) & jnp.int32(0x7FFFFFFF))
        key_ref[pl.ds(k0, _TK), :] = jnp.where(krow + k0 <= tq_lane, okey, jnp.int32(INT_MIN))
        return carry

    lax.fori_loop(0, n_tiles, score_tile, 0)

    def count_ge(thr):
        def body(j, cnt):
            k0 = pl.multiple_of(j * _TK, _TK)
            ge = jnp.where(key_ref[pl.ds(k0, _TK), :] >= thr, 1.0, 0.0)
            return cnt + _fold_rows(ge, jnp.add)
        cnt8 = lax.fori_loop(0, n_tiles, body, jnp.zeros((8, Q_BLOCK), F32))
        return jnp.sum(cnt8, axis=0, keepdims=True)

    def bit_step(b, prefix):
        trial = prefix | (jnp.int32(1) << (31 - b))
        cnt = count_ge(trial ^ jnp.int32(INT_MIN))
        return jnp.where(cnt >= float(n_sel), trial, prefix)

    prefix = lax.fori_loop(0, 32, bit_step, jnp.zeros((1, Q_BLOCK), jnp.int32))
    thr = prefix ^ jnp.int32(INT_MIN)

    q_rope = qa_ref[:, 0:_W_QAR]
    q_nope = qa_ref[:, _W_QAR:]
    scale = HEAD_DIM ** -0.5
    for h in range(A_HEADS):
        q_lat = jnp.dot(q_nope * man_ref[h:h + 1, :], wuk_ref[...], preferred_element_type=F32)
        qcat_ref[h * Q_BLOCK:(h + 1) * Q_BLOCK, 0:A_KV_RANK] = q_lat.astype(BF16) * scale
        qcat_ref[h * Q_BLOCK:(h + 1) * Q_BLOCK, A_KV_RANK:] = q_rope * mar_ref[h:h + 1, :] * scale

    acc_ref[...] = jnp.zeros((A_KV_RANK, rows), F32)

    def attn_tile(j, carry):
        m_old, l_old = carry
        k0 = pl.multiple_of(j * _TK, _TK)
        sel = (key_ref[pl.ds(k0, _TK), :] >= thr) & (krow + k0 <= tq_lane)
        bias = jnp.where(sel, 0.0, NEG_BIG).astype(F32)
        s = _dot_nt(ka_ref[pl.ds(k0, _TK), :], qcat_ref[...])
        s = jnp.concatenate([s[:, h * Q_BLOCK:(h + 1) * Q_BLOCK] + bias for h in range(A_HEADS)], axis=1)
        m_new = jnp.maximum(m_old, jnp.max(_fold_rows(s, jnp.maximum), axis=0, keepdims=True))
        alpha = jnp.exp(m_old - m_new)
        p = jnp.exp(s - m_new)
        l_new = alpha * l_old + jnp.sum(_fold_rows(p, jnp.add), axis=0, keepdims=True)
        pv = jnp.dot(ckvt_ref[:, pl.ds(k0, _TK)], p.astype(BF16), preferred_element_type=F32)
        acc_ref[...] = acc_ref[...] * alpha + pv
        return m_new, l_new

    _, l_fin = lax.fori_loop(0, n_tiles, attn_tile,
                             (jnp.full((1, rows), NEG_BIG, F32), jnp.zeros((1, rows), F32)))

    o_lat = (acc_ref[...] * (1.0 / l_fin)).astype(BF16)
    out_t = jnp.zeros((A_HEADS * HEAD_DIM, Q_BLOCK), F32)
    for h in range(A_HEADS):
        out_t = out_t + jnp.dot(wuv_ref[h], o_lat[:, h * Q_BLOCK:(h + 1) * Q_BLOCK],
                                preferred_element_type=F32)
    out_ref[...] = out_t.T.astype(BF16)


def _dsa_mixer(qa, iq, iw, ikt, ka, ckvt, wuk, wuv_t, m_ar, m_an, m_iq, b, s):
    n_sel = min(A_TOPK_MAX, s // 4)
    nq = s // Q_BLOCK
    rows = A_HEADS * Q_BLOCK
    blk = lambda width: pl.BlockSpec((Q_BLOCK, width), lambda bi, i: (bi * nq + i, 0))
    seq = lambda width: pl.BlockSpec((s, width), lambda bi, i: (bi, 0))
    const = lambda a: pl.BlockSpec(a.shape, lambda bi, i: (0,) * a.ndim)
    return pl.pallas_call(
        functools.partial(_dsa_kernel, n_sel=n_sel),
        out_shape=jax.ShapeDtypeStruct((b * s, A_HEADS * HEAD_DIM), BF16),
        grid=(b, nq),
        in_specs=[blk(_W_QAR + _W_QAN), blk(_W_IQ), blk(_W_IW), seq(_W_IK), seq(_W_CKV + _W_AKR),
                  pl.BlockSpec((A_KV_RANK, s), lambda bi, i: (0, bi)),
                  const(wuk), const(wuv_t), const(m_ar), const(m_an), const(m_iq)],
        out_specs=blk(A_HEADS * HEAD_DIM),
        scratch_shapes=[pltpu.VMEM((s, Q_BLOCK), jnp.int32),
                        pltpu.VMEM((rows, _W_IQ), BF16),
                        pltpu.VMEM((rows, A_KV_RANK + _W_AKR), BF16),
                        pltpu.VMEM((A_KV_RANK, rows), F32)],
        compiler_params=_cparams("parallel", "arbitrary"),
        name="dsa_attention",
    )(qa, iq, iw, ikt, ka, ckvt, wuk, wuv_t, m_ar, m_an, m_iq)


def _merge_kernel(x_ref, a_ref, o1_ref, o2_ref, o3_ref, l1_ref, l2_ref, l3_ref, g_ref,
                  wa_ref, wb_ref, wo_ref, lg_ref, lb_ref, h_ref):
    lses = [l1_ref[...], l2_ref[...], l3_ref[...]]
    outs = [o1_ref[...], o2_ref[...], o3_ref[...]]
    mx = jnp.maximum(jnp.maximum(lses[0], lses[1]), lses[2])
    es = [jnp.exp(l - mx) for l in lses]
    den = es[0] + es[1] + es[2]
    b_out = (es[0] / den) * outs[0] + (es[1] / den) * outs[1] + (es[2] / den) * outs[2]
    ya = jnp.dot(a_ref[...], wa_ref[...], preferred_element_type=F32)
    yb = jnp.dot(b_out.astype(BF16), wb_ref[...], preferred_element_type=F32)
    pre = g_ref[:, 0:D_MODEL] * ya + g_ref[:, D_MODEL:] * yb
    mix = jnp.dot(pre.astype(BF16), wo_ref[...], preferred_element_type=F32)
    h_ref[...] = _layer_norm(ALPHA * x_ref[...] + mix, lg_ref[...], lb_ref[...])


def _merge(xf, a_out, b_parts, gates, wa, wb, wo, ln_g, ln_b, tile):
    n = xf.shape[0]
    row = lambda width: pl.BlockSpec((tile, width), lambda i: (i, 0))
    full = lambda a: pl.BlockSpec(a.shape, lambda i: (0,) * a.ndim)
    (o1, l1), (o2, l2), (o3, l3) = b_parts
    return pl.pallas_call(
        _merge_kernel,
        out_shape=jax.ShapeDtypeStruct((n, D_MODEL), F32),
        grid=(n // tile,),
        in_specs=[row(D_MODEL), row(A_HEADS * HEAD_DIM)] + [row(B_GROUP_W)] * 6 + [row(2 * D_MODEL),
                  full(wa), full(wb), full(wo), full(ln_g), full(ln_b)],
        out_specs=row(D_MODEL),
        compiler_params=_cparams("parallel"),
        name="merge_output_projection",
    )(xf, a_out, o1, o2, o3, l1, l2, l3, gates, wa, wb, wo, ln_g, ln_b)


def _first_max(v):
    m = jnp.max(v, axis=0, keepdims=True)
    idx = lax.broadcasted_iota(jnp.int32, v.shape, 0)
    first = jnp.min(jnp.where(v == m, idx, v.shape[0]), axis=0, keepdims=True)
    return m, idx == first


def _router_kernel(h_ref, wr_ref, rb_ref, gate_ref):
    t = h_ref.shape[0]
    gs = N_EXPERTS // N_GROUPS
    logits = lax.dot_general(wr_ref[...], h_ref[...], (((1,), (1,)), ((), ())),
                             precision=lax.Precision.HIGHEST, preferred_element_type=F32)
    scores = jax.nn.sigmoid(logits)
    biased = scores + rb_ref[...]
    gscores = []
    for g in range(N_GROUPS):
        blk = biased[g * gs:(g + 1) * gs, :]
        m1, hit = _first_max(blk)
        m2 = jnp.max(jnp.where(hit, -jnp.inf, blk), axis=0, keepdims=True)
        gscores.append(m1 + m2)
    gscore = jnp.concatenate(gscores, axis=0)
    gsel = jnp.zeros((N_GROUPS, t), F32)
    for _ in range(TOPK_GROUPS):
        _, hit = _first_max(gscore)
        gsel = jnp.where(hit, 1.0, gsel)
        gscore = jnp.where(hit, -jnp.inf, gscore)
    esel = jnp.concatenate([jnp.broadcast_to(gsel[g:g + 1, :], (gs, t)) for g in range(N_GROUPS)], axis=0)
    cand = jnp.where(esel > 0.0, biased, -jnp.inf)
    top_s = jnp.zeros((N_EXPERTS, t), F32)
    for _ in range(TOP_K):
        _, hit = _first_max(cand)
        top_s = jnp.where(hit, scores, top_s)
        cand = jnp.where(hit, -jnp.inf, cand)
    gate = top_s / jnp.sum(top_s, axis=0, keepdims=True) * ROUTED_SCALE
    gate_ref[...] = jnp.concatenate([gate, jnp.zeros((LANES - N_EXPERTS, t), F32)], axis=0).T


def _router(h, wr_t, rb_col, tile):
    n = h.shape[0]
    return pl.pallas_call(
        _router_kernel,
        out_shape=jax.ShapeDtypeStruct((n, LANES), F32),
        grid=(n // tile,),
        in_specs=[pl.BlockSpec((tile, D_MODEL), lambda i: (i, 0)),
                  pl.BlockSpec(wr_t.shape, lambda i: (0, 0)),
                  pl.BlockSpec(rb_col.shape, lambda i: (0, 0))],
        out_specs=pl.BlockSpec((tile, LANES), lambda i: (i, 0)),
        compiler_params=_cparams("parallel"),
        name="moe_router",
    )(h, wr_t, rb_col)


_EXPERTS_PER_STEP = 2


def _swiglu(hb, w1, w3, w2):
    a = jnp.dot(hb, w1, preferred_element_type=F32)
    b = jnp.dot(hb, w3, preferred_element_type=F32)
    return jnp.dot((jax.nn.silu(a) * b).astype(BF16), w2, preferred_element_type=F32)


def _experts_kernel(h_ref, gate_ref, w1_ref, w3_ref, w2_ref, s1_ref, s3_ref, s2_ref, lg_ref, lb_ref,
                    y_ref, hb_ref, acc_ref):
    step = pl.program_id(1)

    @pl.when(step == 0)
    def _():
        hb = h_ref[...].astype(BF16)
        hb_ref[...] = hb
        acc_ref[...] = _swiglu(hb, s1_ref[...], s3_ref[...], s2_ref[...])

    hb = hb_ref[...]
    gate = gate_ref[...]
    lane = lax.broadcasted_iota(jnp.int32, gate.shape, 1)
    for k in range(_EXPERTS_PER_STEP):
        e = step * _EXPERTS_PER_STEP + k
        g_col = jnp.sum(jnp.where(lane == e, gate, 0.0), axis=1, keepdims=True)
        acc_ref[...] += _swiglu(hb, w1_ref[k], w3_ref[k], w2_ref[k]) * g_col

    @pl.when(step == pl.num_programs(1) - 1)
    def _():
        y_ref[...] = _layer_norm(ALPHA * h_ref[...] + acc_ref[...], lg_ref[...], lb_ref[...])


def _experts(h, gate, w1, w3, w2, s1, s3, s2, ln_g, ln_b, tile):
    n = h.shape[0]
    ec = _EXPERTS_PER_STEP
    row = lambda width: pl.BlockSpec((tile, width), lambda i, e: (i, 0))
    full = lambda a: pl.BlockSpec(a.shape, lambda i, e: (0,) * a.ndim)
    wspec = lambda a: pl.BlockSpec((ec,) + a.shape[1:], lambda i, e: (e, 0, 0))
    return pl.pallas_call(
        _experts_kernel,
        out_shape=jax.ShapeDtypeStruct((n, D_MODEL), F32),
        grid=(n // tile, N_EXPERTS // ec),
        in_specs=[row(D_MODEL), row(LANES), wspec(w1), wspec(w3), wspec(w2),
                  full(s1), full(s3), full(s2), full(ln_g), full(ln_b)],
        out_specs=row(D_MODEL),
        scratch_shapes=[pltpu.VMEM((tile, D_MODEL), BF16), pltpu.VMEM((tile, D_MODEL), F32)],
        compiler_params=_cparams("parallel", "arbitrary"),
        name="moe_experts",
    )(h, gate, w1, w3, w2, s1, s3, s2, ln_g, ln_b)


def _rope_freqs():
    inv16 = ROPE_THETA ** (-jnp.arange(0, ROPE_DIM, 2, dtype=F32) / ROPE_DIM)
    inv8 = ROPE_THETA ** (-jnp.arange(0, IDX_ROPE, 2, dtype=F32) / IDX_ROPE)

    def row(inv, fidx):
        return jnp.where(fidx >= 0, inv[np.maximum(fidx, 0)], 0.0)
    lay = _LAYOUT
    freq = jnp.stack([row(inv16, lay["fa"]), row(inv8, lay["fi"]), row(inv16, lay["fb"])])
    sign = jnp.asarray(np.stack([lay["sa"], lay["si"], lay["sb"]]))
    return freq.astype(F32), sign


def _layer(x, positions, w_in, b_gate, g_kv, w_uk, w_uv, w_branch_a, w_branch_b, w_o, ln1_g, ln1_b,
           w_router, router_bias, w1_e, w3_e, w2_e, ws1, ws3, ws2, ln2_g, ln2_b):
    b, s, d = x.shape
    n = b * s
    lay = _LAYOUT
    tile = min(256, n)
    xf = x.reshape(n, d)

    w_perm = (w_in[:, lay["cols"]] * lay["keep"]).astype(BF16)
    wuk = jnp.transpose(w_uk, (1, 2, 0)).reshape(A_HEADS * A_NOPE, A_KV_RANK).astype(BF16)
    wuv_t = jnp.zeros((A_HEADS, A_HEADS, HEAD_DIM, A_KV_RANK), F32)
    wuv_t = wuv_t.at[jnp.arange(A_HEADS), jnp.arange(A_HEADS)].set(jnp.transpose(w_uv, (1, 2, 0)))
    wuv_t = wuv_t.reshape(A_HEADS, A_HEADS * HEAD_DIM, A_KV_RANK).astype(BF16)
    masks = {k: jnp.asarray(lay[k], BF16) for k in ("m_ar", "m_an", "m_iq", "m_bq")}
    m_bv = jnp.asarray(lay["m_bv"], F32)

    freq, sign = _rope_freqs()
    tables = _rope_tables(positions.reshape(n, 1), freq, sign, tile)
    qa, ka, ckvt, iq, ikt, iw, bq, bk, bv, gates = _input_projection(
        xf, w_perm, tables, g_kv.reshape(1, -1), b_gate.reshape(1, -1), tile)

    a_out = _dsa_mixer(qa, iq, iw, ikt, ka, ckvt, wuk, wuv_t, masks["m_ar"], masks["m_an"], masks["m_iq"], b, s)
    b3 = lambda a: a.reshape(b, s, a.shape[1])
    b_parts = [_dilated_group(b3(bq), b3(bk), b3(bv), masks["m_bq"], m_bv, g, dil)
               for g, (_, dil) in enumerate(B_PATTERNS)]

    h = _merge(xf, a_out, b_parts, gates, w_branch_a.astype(BF16), w_branch_b.astype(BF16),
               w_o.astype(BF16), ln1_g.reshape(1, -1), ln1_b.reshape(1, -1), tile)

    gate = _router(h, w_router.T, router_bias.reshape(-1, 1), tile)
    y = _experts(h, gate, w1_e.astype(BF16), w3_e.astype(BF16), w2_e.astype(BF16),
                 ws1.astype(BF16), ws3.astype(BF16), ws2.astype(BF16),
                 ln2_g.reshape(1, -1), ln2_b.reshape(1, -1), min(1024, n))
    return y.reshape(b, s, d)


def kernel(x, positions, w_in, b_gate, g_kv, w_uk, w_uv, w_branch_a, w_branch_b, w_o, ln1_g, ln1_b,
           w_router, router_bias, w1_e, w3_e, w2_e, ws1, ws3, ws2, ln2_g, ln2_b):
    h = x
    for l in range(DEPTH):
        h = _layer(h, positions, w_in[l], b_gate[l], g_kv[l], w_uk[l], w_uv[l], w_branch_a[l],
                   w_branch_b[l], w_o[l], ln1_g[l], ln1_b[l], w_router[l], router_bias[l],
                   w1_e[l], w3_e[l], w2_e[l], ws1[l], ws3[l], ws2[l], ln2_g[l], ln2_b[l])
    return h
```

```python
import functools

import jax
import jax.numpy as jnp
import numpy as np
from jax import lax
from jax.experimental import pallas as pl
from jax.experimental.pallas import tpu as pltpu

F32 = jnp.float32
BF16 = jnp.bfloat16

D_MODEL = 1024
HEAD_DIM = 64
ROPE_DIM = 16
ROPE_THETA = 500000.0
Q_BLOCK = 128
A_HEADS = 8
A_NOPE = HEAD_DIM - ROPE_DIM
A_KV_RANK = 256
A_TOPK_MAX = 256
IDX_HEADS = 8
IDX_DIM = 32
IDX_ROPE = 8
B_PATTERNS = ((128, 1), (512, 4), (2048, 16))
B_GROUP_HEADS = 4
B_HEADS = B_GROUP_HEADS * len(B_PATTERNS)
B_GROUP_W = B_GROUP_HEADS * HEAD_DIM
N_EXPERTS = 64
TOP_K = 8
N_GROUPS = 8
TOPK_GROUPS = 4
D_EXPERT = 256
ROUTED_SCALE = 2.5
DEPTH = 1
ALPHA = (2.0 * DEPTH) ** 0.25
LN_EPS = 1e-5
RMS_EPS = 1e-6

LANES = 128
VMEM_LIMIT = 56 * 1024 * 1024
NEG_BIG = -1e30
INT_MIN = -(2 ** 31)

_OFF_AQ = 0
_OFF_CKV = _OFF_AQ + A_HEADS * HEAD_DIM
_OFF_AKR = _OFF_CKV + A_KV_RANK
_OFF_IQ = _OFF_AKR + ROPE_DIM
_OFF_IK = _OFF_IQ + IDX_HEADS * IDX_DIM
_OFF_IW = _OFF_IK + IDX_DIM
_OFF_BQ = _OFF_IW + IDX_HEADS
_OFF_BK = _OFF_BQ + B_HEADS * HEAD_DIM
_OFF_BV = _OFF_BK + B_HEADS * HEAD_DIM
_OFF_GA = _OFF_BV + B_HEADS * HEAD_DIM
_OFF_GB = _OFF_GA + D_MODEL
_IN_TOTAL = _OFF_GB + D_MODEL

_W_QAR, _W_QAN, _W_CKV, _W_AKR = 128, A_HEADS * A_NOPE, A_KV_RANK, 128
_W_IQ, _W_IK, _W_IW = 256, 256, 128
_W_B = B_HEADS * HEAD_DIM
_C_QAR = 0
_C_QAN = _C_QAR + _W_QAR
_C_CKV = _C_QAN + _W_QAN
_C_AKR = _C_CKV + _W_CKV
_C_IQ = _C_AKR + _W_AKR
_C_IK = _C_IQ + _W_IQ
_C_IW = _C_IK + _W_IK
_C_BQ = _C_IW + _W_IW
_C_BK = _C_BQ + _W_B
_C_BV = _C_BK + _W_B
_C_GA = _C_BV + _W_B
_C_GB = _C_GA + D_MODEL
_P_TOTAL = _C_GB + D_MODEL


def _idx_lane(l):
    if l < 16:
        return l // 4, l % 4
    if l < 64:
        return (l - 16) // 12, 8 + (l - 16) % 12
    if l < 80:
        return (l - 64) // 4, 4 + (l - 64) % 4
    return (l - 80) // 12, 20 + (l - 80) % 12


def _b_lane(l):
    half, r = l // 64, l % 64
    which, rr = r // 32, r % 32
    if rr < 8:
        return which, half * 8 + rr
    return which, 16 + half * 24 + (rr - 8)


def _build_layout():
    cols = np.zeros((_P_TOTAL,), np.int32)
    keep = np.ones((_P_TOTAL,), np.float32)
    for l in range(128):
        half, h, f = l // 64, (l % 64) // 8, l % 8
        cols[_C_QAR + l] = _OFF_AQ + h * HEAD_DIM + half * 8 + f
        cols[_C_AKR + l] = _OFF_AKR + half * 8 + f
    for h in range(A_HEADS):
        for j in range(A_NOPE):
            cols[_C_QAN + h * A_NOPE + j] = _OFF_AQ + h * HEAD_DIM + ROPE_DIM + j
    cols[_C_CKV:_C_CKV + _W_CKV] = _OFF_CKV + np.arange(_W_CKV)
    for sl in range(2):
        for l in range(128):
            hh, d = _idx_lane(l)
            cols[_C_IQ + sl * 128 + l] = _OFF_IQ + (sl * 4 + hh) * IDX_DIM + d
            cols[_C_IK + sl * 128 + l] = _OFF_IK + d
    cols[_C_IW:_C_IW + IDX_HEADS] = _OFF_IW + np.arange(IDX_HEADS)
    keep[_C_IW + IDX_HEADS:_C_IW + _W_IW] = 0.0
    for p in range(B_HEADS // 2):
        for l in range(128):
            which, d = _b_lane(l)
            h = 2 * p + which
            cols[_C_BQ + p * 128 + l] = _OFF_BQ + h * HEAD_DIM + d
            cols[_C_BK + p * 128 + l] = _OFF_BK + h * HEAD_DIM + d
    cols[_C_BV:_C_BV + _W_B] = _OFF_BV + np.arange(_W_B)
    cols[_C_GA:_C_GA + D_MODEL] = _OFF_GA + np.arange(D_MODEL)
    cols[_C_GB:_C_GB + D_MODEL] = _OFF_GB + np.arange(D_MODEL)

    fa = np.array([l % 8 for l in range(128)])
    sa = np.array([-1.0 if l < 64 else 1.0 for l in range(128)], np.float32)
    fi = np.full((128,), -1)
    si = np.zeros((128,), np.float32)
    fb = np.full((128,), -1)
    sb = np.zeros((128,), np.float32)
    for l in range(128):
        if l < 16 or 64 <= l < 80:
            fi[l] = l % 4
            si[l] = -1.0 if l < 64 else 1.0
        if l % 32 < 8:
            fb[l] = l % 32
            sb[l] = -1.0 if l < 64 else 1.0

    m_ar = np.zeros((A_HEADS, 128), np.float32)
    for l in range(128):
        m_ar[(l % 64) // 8, l] = 1.0
    m_an = np.zeros((A_HEADS, _W_QAN), np.float32)
    for h in range(A_HEADS):
        m_an[h, h * A_NOPE:(h + 1) * A_NOPE] = 1.0
    m_iq = np.zeros((IDX_HEADS, _W_IQ), np.float32)
    for sl in range(2):
        for l in range(128):
            m_iq[sl * 4 + _idx_lane(l)[0], sl * 128 + l] = 1.0
    m_bq = np.zeros((B_GROUP_HEADS, B_GROUP_W), np.float32)
    for p in range(2):
        for l in range(128):
            m_bq[2 * p + _b_lane(l)[0], p * 128 + l] = 1.0
    m_bv = np.zeros((B_GROUP_HEADS, B_GROUP_W), np.float32)
    for j in range(B_GROUP_HEADS):
        m_bv[j, j * HEAD_DIM:(j + 1) * HEAD_DIM] = 1.0
    return dict(cols=cols, keep=keep, fa=fa, sa=sa, fi=fi, si=si, fb=fb, sb=sb,
                m_ar=m_ar, m_an=m_an, m_iq=m_iq, m_bq=m_bq, m_bv=m_bv)


_LAYOUT = _build_layout()


def _cparams(*sem):
    return pltpu.CompilerParams(dimension_semantics=sem, vmem_limit_bytes=VMEM_LIMIT)


def _layer_norm(v, g, b):
    mu = jnp.mean(v, axis=-1, keepdims=True)
    var = jnp.mean(jnp.square(v - mu), axis=-1, keepdims=True)
    return (v - mu) * lax.rsqrt(var + LN_EPS) * g + b


def _dot_nt(a, b):
    return lax.dot_general(a, b, (((1,), (1,)), ((), ())), preferred_element_type=F32)


def _rope_table_kernel(pos_ref, freq_ref, sign_ref, out_ref):
    pos = pos_ref[...].astype(F32)
    for k in range(3):
        ang = pos * freq_ref[k:k + 1, :]
        out_ref[:, (2 * k) * LANES:(2 * k + 1) * LANES] = jnp.cos(ang)
        out_ref[:, (2 * k + 1) * LANES:(2 * k + 2) * LANES] = jnp.sin(ang) * sign_ref[k:k + 1, :]


def _rope_tables(pos_col, freq, sign, tile):
    n = pos_col.shape[0]
    return pl.pallas_call(
        _rope_table_kernel,
        out_shape=jax.ShapeDtypeStruct((n, 6 * LANES), F32),
        grid=(n // tile,),
        in_specs=[pl.BlockSpec((tile, 1), lambda i: (i, 0)),
                  pl.BlockSpec((3, LANES), lambda i: (0, 0)),
                  pl.BlockSpec((3, LANES), lambda i: (0, 0))],
        out_specs=pl.BlockSpec((tile, 6 * LANES), lambda i: (i, 0)),
        compiler_params=_cparams("parallel"),
        name="rope_tables",
    )(pos_col, freq, sign)


def _rope_slabs(y, cos, sin):
    outs = []
    for s in range(y.shape[1] // LANES):
        ys = y[:, s * LANES:(s + 1) * LANES]
        outs.append(ys * cos + pltpu.roll(ys, 64, 1) * sin)
    return outs[0] if len(outs) == 1 else jnp.concatenate(outs, axis=1)


def _proj_kernel(x_ref, w_ref, tab_ref, gkv_ref, bg_ref,
                 qa_ref, ka_ref, ckvt_ref, iq_ref, ik_ref, iw_ref, bq_ref, bk_ref, bv_ref, g_ref):
    xb = x_ref[...].astype(BF16)

    def proj(c0, width):
        return jnp.dot(xb, w_ref[:, c0:c0 + width], preferred_element_type=F32)

    cos_a, sin_a = tab_ref[:, 0:128], tab_ref[:, 128:256]
    cos_i, sin_i = tab_ref[:, 256:384], tab_ref[:, 384:512]
    cos_b, sin_b = tab_ref[:, 512:640], tab_ref[:, 640:768]

    qa_ref[:, 0:_W_QAR] = _rope_slabs(proj(_C_QAR, _W_QAR), cos_a, sin_a).astype(BF16)
    qa_ref[:, _W_QAR:] = proj(_C_QAN, _W_QAN).astype(BF16)
    ckv = proj(_C_CKV, _W_CKV)
    ckv = ckv * lax.rsqrt(jnp.mean(jnp.square(ckv), axis=-1, keepdims=True) + RMS_EPS) * gkv_ref[...]
    ka_ref[:, 0:_W_CKV] = ckv.astype(BF16)
    ckvt_ref[...] = ckv.T.astype(BF16)
    ka_ref[:, _W_CKV:] = _rope_slabs(proj(_C_AKR, _W_AKR), cos_a, sin_a).astype(BF16)
    iq_ref[...] = _rope_slabs(proj(_C_IQ, _W_IQ), cos_i, sin_i).astype(BF16)
    ik_ref[...] = _rope_slabs(proj(_C_IK, _W_IK), cos_i, sin_i).astype(BF16)
    iw_ref[...] = proj(_C_IW, _W_IW) * ((IDX_HEADS * IDX_DIM) ** -0.5)
    bq_ref[...] = _rope_slabs(proj(_C_BQ, _W_B), cos_b, sin_b).astype(BF16)
    bk_ref[...] = _rope_slabs(proj(_C_BK, _W_B), cos_b, sin_b).astype(BF16)
    bv_ref[...] = proj(_C_BV, _W_B).astype(BF16)
    g_ref[...] = jax.nn.sigmoid(proj(_C_GA, 2 * D_MODEL) + bg_ref[...])


def _input_projection(xf, w_perm, tables, g_kv, b_gate, tile):
    n = xf.shape[0]
    row = lambda width: pl.BlockSpec((tile, width), lambda i: (i, 0))
    full = lambda a: pl.BlockSpec(a.shape, lambda i: (0,) * a.ndim)
    out_w = [(_W_QAR + _W_QAN, BF16), (_W_CKV + _W_AKR, BF16), None, (_W_IQ, BF16), (_W_IK, BF16),
             (_W_IW, F32), (_W_B, BF16), (_W_B, BF16), (_W_B, BF16), (2 * D_MODEL, F32)]
    shapes = [jax.ShapeDtypeStruct((A_KV_RANK, n), BF16) if o is None else jax.ShapeDtypeStruct((n, o[0]), o[1])
              for o in out_w]
    specs = [pl.BlockSpec((A_KV_RANK, tile), lambda i: (0, i)) if o is None else row(o[0]) for o in out_w]
    return pl.pallas_call(
        _proj_kernel,
        out_shape=shapes,
        grid=(n // tile,),
        in_specs=[row(D_MODEL), full(w_perm), row(6 * LANES), full(g_kv), full(b_gate)],
        out_specs=specs,
        compiler_params=_cparams("parallel"),
        name="input_projection",
    )(xf, w_perm, tables, g_kv, b_gate)


def _dilated_kernel(q_ref, kc_ref, kp_ref, vc_ref, vp_ref, mq_ref, mv_ref, o_ref, lse_ref,
                    kwin_ref, vwin_ref, *, tq):
    first = pl.program_id(2) == 0
    kwin_ref[0:Q_BLOCK, :] = kp_ref[...]
    kwin_ref[Q_BLOCK:, :] = kc_ref[...]
    vwin_ref[0:Q_BLOCK, :] = vp_ref[...]
    vwin_ref[Q_BLOCK:, :] = vc_ref[...]
    t = lax.broadcasted_iota(jnp.int32, (Q_BLOCK, 2 * Q_BLOCK), 0)
    c = lax.broadcasted_iota(jnp.int32, (Q_BLOCK, 2 * Q_BLOCK), 1)
    diff = t + Q_BLOCK - c
    band = (diff >= 0) & (diff <= Q_BLOCK)
    scale = HEAD_DIM ** -0.5
    for sb in range(tq // Q_BLOCK):
        valid = band
        if sb == 0:
            valid = band & (c >= jnp.where(first, Q_BLOCK, 0))
        bias = jnp.where(valid, 0.0, NEG_BIG).astype(F32)
        q = q_ref[sb * Q_BLOCK:(sb + 1) * Q_BLOCK, :] * scale
        kw = kwin_ref[sb * Q_BLOCK:(sb + 2) * Q_BLOCK, :]
        vw = vwin_ref[sb * Q_BLOCK:(sb + 2) * Q_BLOCK, :]
        qs = jnp.concatenate([q * mq_ref[j:j + 1, :] for j in range(B_GROUP_HEADS)], axis=0)
        s = _dot_nt(qs, kw)
        o_acc = jnp.zeros((Q_BLOCK, B_GROUP_W), F32)
        lse_acc = jnp.zeros((Q_BLOCK, B_GROUP_W), F32)
        for j in range(B_GROUP_HEADS):
            sj = s[j * Q_BLOCK:(j + 1) * Q_BLOCK, :] + bias
            m = jnp.max(sj, axis=-1, keepdims=True)
            e = jnp.exp(sj - m)
            den = jnp.sum(e, axis=-1, keepdims=True)
            pv = jnp.dot(e.astype(BF16), vw, preferred_element_type=F32)
            mv = mv_ref[j:j + 1, :]
            o_acc = o_acc + (pv / den) * mv
            lse_acc = lse_acc + (m + jnp.log(den)) * mv
        o_ref[sb * Q_BLOCK:(sb + 1) * Q_BLOCK, :] = o_acc
        lse_ref[sb * Q_BLOCK:(sb + 1) * Q_BLOCK, :] = lse_acc


def _dilated_group(bq, bk, bv, mq, mv, g, dil):
    b, s, _ = bq.shape
    sub = s // dil
    tq = min(512, sub)
    nblk = tq // Q_BLOCK
    ng = len(B_PATTERNS)
    view = lambda a: a.reshape(b, sub, dil * a.shape[2])
    cur = pl.BlockSpec((None, tq, B_GROUP_W), lambda bi, r, i: (bi, i, r * ng + g))
    prev = pl.BlockSpec((None, Q_BLOCK, B_GROUP_W),
                        lambda bi, r, i: (bi, jnp.maximum(i * nblk - 1, 0), r * ng + g))
    const = lambda a: pl.BlockSpec(a.shape, lambda bi, r, i: (0, 0))
    out = pl.BlockSpec((None, tq, B_GROUP_W), lambda bi, r, i: (bi, i, r))
    o, lse = pl.pallas_call(
        functools.partial(_dilated_kernel, tq=tq),
        out_shape=[jax.ShapeDtypeStruct((b, sub, dil * B_GROUP_W), F32)] * 2,
        grid=(b, dil, sub // tq),
        in_specs=[cur, cur, prev, cur, prev, const(mq), const(mv)],
        out_specs=[out, out],
        scratch_shapes=[pltpu.VMEM((tq + Q_BLOCK, B_GROUP_W), BF16)] * 2,
        compiler_params=_cparams("parallel", "parallel", "arbitrary"),
        name=f"dilated_attention_d{dil}",
    )(view(bq), view(bk), view(bk), view(bv), view(bv), mq, mv)
    return o.reshape(b * s, B_GROUP_W), lse.reshape(b * s, B_GROUP_W)


_TK = 512


def _fold_rows(x, op):
    parts = [x[r:r + 8, :] for r in range(0, x.shape[0], 8)]
    while len(parts) > 1:
        parts = [op(parts[k], parts[k + 1]) for k in range(0, len(parts) - 1, 2)] + parts[len(parts) & ~1:]
    return parts[0]


def _skewed_tiles(n_tiles, produce, consume, buf_a, buf_b, carry):
    buf_a[...] = produce(0)

    def pair(t, c):
        j = 2 * t
        buf_b[...] = produce(j + 1)
        c = consume(j, buf_a, c)
        buf_a[...] = produce(j + 2)
        return consume(j + 1, buf_b, c)

    n_pairs = (n_tiles - 1) // 2
    carry = lax.fori_loop(0, n_pairs, pair, carry)
    j = 2 * n_pairs

    def last_two(c):
        buf_b[...] = produce(j + 1)
        return consume(j + 1, buf_b, consume(j, buf_a, c))

    return lax.cond(n_tiles - j == 2, last_two, lambda c: consume(j, buf_a, c), carry)


def _dsa_kernel(qa_ref, iq_ref, iw_ref, ikt_ref, ka_ref, ckvt_ref, wuk_ref, wuv_ref, mar_ref, man_ref, miq_ref,
                out_ref, key_ref, iqs_ref, qcat_ref, acc_ref, sa_ref, sb_ref, p_ref, *, n_sel):
    i = pl.program_id(1)
    q0 = i * Q_BLOCK
    n_keys = q0 + Q_BLOCK
    rows = A_HEADS * Q_BLOCK

    n_tiles = (n_keys + _TK - 1) // _TK
    tq_lane = q0 + lax.broadcasted_iota(jnp.int32, (_TK, Q_BLOCK), 1)
    krow = lax.broadcasted_iota(jnp.int32, (_TK, Q_BLOCK), 0)

    iq = iq_ref[...]
    for h in range(IDX_HEADS):
        iqs_ref[:, h * Q_BLOCK:(h + 1) * Q_BLOCK] = (iq * miq_ref[h:h + 1, :]).astype(F32).T.astype(BF16)
    iw_t = iw_ref[...].T

    def score_matmul(j):
        k0 = pl.multiple_of(j * _TK, _TK)
        return jnp.dot(ikt_ref[pl.ds(k0, _TK), :], iqs_ref[...], preferred_element_type=F32)

    def score_keys(j, s_ref, carry):
        k0 = pl.multiple_of(j * _TK, _TK)
        sc = jnp.zeros((_TK, Q_BLOCK), F32)
        for h in range(IDX_HEADS):
            sc = sc + jnp.maximum(s_ref[:, h * Q_BLOCK:(h + 1) * Q_BLOCK], 0.0) * iw_t[h:h + 1, :]
        bits = lax.bitcast_convert_type(sc + 0.0, jnp.int32)
        okey = bits ^ ((bits >> 31) & jnp.int32(0x7FFFFFFF))
        key_ref[pl.ds(k0, _TK), :] = jnp.where(krow + k0 <= tq_lane, okey, jnp.int32(INT_MIN))
        return carry

    _skewed_tiles(n_tiles, score_matmul, score_keys, sa_ref, sb_ref, 0)

    def count_ge(thr):
        def body(j, cnt):
            k0 = pl.multiple_of(j * _TK, _TK)
            ge = jnp.where(key_ref[pl.ds(k0, _TK), :] >= thr, 1.0, 0.0)
            return cnt + _fold_rows(ge, jnp.add)
        cnt8 = lax.fori_loop(0, n_tiles, body, jnp.zeros((8, Q_BLOCK), F32))
        return jnp.sum(cnt8, axis=0, keepdims=True)

    def bit_step(b, prefix):
        trial = prefix | (jnp.int32(1) << (31 - b))
        cnt = count_ge(trial ^ jnp.int32(INT_MIN))
        return jnp.where(cnt >= float(n_sel), trial, prefix)

    prefix = lax.fori_loop(0, 32, bit_step, jnp.zeros((1, Q_BLOCK), jnp.int32))
    thr = prefix ^ jnp.int32(INT_MIN)

    q_rope = qa_ref[:, 0:_W_QAR]
    q_nope = qa_ref[:, _W_QAR:]
    scale = HEAD_DIM ** -0.5
    c_rope = A_KV_RANK
    for h in range(A_HEADS):
        q_lat = jnp.dot(q_nope * man_ref[h:h + 1, :], wuk_ref[...], preferred_element_type=F32)
        cols = slice(h * Q_BLOCK, (h + 1) * Q_BLOCK)
        qcat_ref[0:c_rope, cols] = (q_lat.astype(BF16) * scale).astype(F32).T.astype(BF16)
        qcat_ref[c_rope:, cols] = (q_rope * mar_ref[h:h + 1, :] * scale).astype(F32).T.astype(BF16)

    acc_ref[...] = jnp.zeros((A_KV_RANK, rows), F32)

    def logit_matmul(j):
        k0 = pl.multiple_of(j * _TK, _TK)
        return jnp.dot(ka_ref[pl.ds(k0, _TK), :], qcat_ref[...], preferred_element_type=F32)

    def softmax_pv(j, s_ref, carry):
        m_old, l_old = carry
        k0 = pl.multiple_of(j * _TK, _TK)
        sel = (key_ref[pl.ds(k0, _TK), :] >= thr) & (krow + k0 <= tq_lane)
        bias = jnp.where(sel, 0.0, NEG_BIG).astype(F32)
        m_parts, l_parts, a_parts = [], [], []
        for h in range(A_HEADS):
            cols = slice(h * Q_BLOCK, (h + 1) * Q_BLOCK)
            s = s_ref[:, cols] + bias
            m_h = jnp.maximum(m_old[:, cols], jnp.max(_fold_rows(s, jnp.maximum), axis=0, keepdims=True))
            a_h = jnp.exp(m_old[:, cols] - m_h)
            p = jnp.exp(s - m_h)
            p_ref[:, cols] = p.astype(BF16)
            l_parts.append(a_h * l_old[:, cols] + jnp.sum(_fold_rows(p, jnp.add), axis=0, keepdims=True))
            m_parts.append(m_h)
            a_parts.append(a_h)
        alpha = jnp.concatenate(a_parts, axis=1)
        pv = jnp.dot(ckvt_ref[:, pl.ds(k0, _TK)], p_ref[...], preferred_element_type=F32)
        acc_ref[...] = acc_ref[...] * alpha + pv
        return jnp.concatenate(m_parts, axis=1), jnp.concatenate(l_parts, axis=1)

    _, l_fin = _skewed_tiles(n_tiles, logit_matmul, softmax_pv, sa_ref, sb_ref,
                             (jnp.full((1, rows), NEG_BIG, F32), jnp.zeros((1, rows), F32)))

    o_lat = (acc_ref[...] * (1.0 / l_fin)).astype(BF16)
    out_t = jnp.zeros((A_HEADS * HEAD_DIM, Q_BLOCK), F32)
    for h in range(A_HEADS):
        out_t = out_t + jnp.dot(wuv_ref[h], o_lat[:, h * Q_BLOCK:(h + 1) * Q_BLOCK],
                                preferred_element_type=F32)
    out_ref[...] = out_t.T.astype(BF16)


def _dsa_mixer(qa, iq, iw, ikt, ka, ckvt, wuk, wuv_t, m_ar, m_an, m_iq, b, s):
    n_sel = min(A_TOPK_MAX, s // 4)
    nq = s // Q_BLOCK
    rows = A_HEADS * Q_BLOCK
    blk = lambda width: pl.BlockSpec((Q_BLOCK, width), lambda bi, i: (bi * nq + i, 0))
    seq = lambda width: pl.BlockSpec((s, width), lambda bi, i: (bi, 0))
    const = lambda a: pl.BlockSpec(a.shape, lambda bi, i: (0,) * a.ndim)
    return pl.pallas_call(
        functools.partial(_dsa_kernel, n_sel=n_sel),
        out_shape=jax.ShapeDtypeStruct((b * s, A_HEADS * HEAD_DIM), BF16),
        grid=(b, nq),
        in_specs=[blk(_W_QAR + _W_QAN), blk(_W_IQ), blk(_W_IW), seq(_W_IK), seq(_W_CKV + _W_AKR),
                  pl.BlockSpec((A_KV_RANK, s), lambda bi, i: (0, bi)),
                  const(wuk), const(wuv_t), const(m_ar), const(m_an), const(m_iq)],
        out_specs=blk(A_HEADS * HEAD_DIM),
        scratch_shapes=[pltpu.VMEM((s, Q_BLOCK), jnp.int32),
                        pltpu.VMEM((_W_IQ, rows), BF16),
                        pltpu.VMEM((A_KV_RANK + _W_AKR, rows), BF16),
                        pltpu.VMEM((A_KV_RANK, rows), F32),
                        pltpu.VMEM((_TK, rows), F32),
                        pltpu.VMEM((_TK, rows), F32),
                        pltpu.VMEM((_TK, rows), BF16)],
        compiler_params=_cparams("parallel", "arbitrary"),
        name="dsa_attention",
    )(qa, iq, iw, ikt, ka, ckvt, wuk, wuv_t, m_ar, m_an, m_iq)


def _merge_kernel(x_ref, a_ref, o1_ref, o2_ref, o3_ref, l1_ref, l2_ref, l3_ref, g_ref,
                  wa_ref, wb_ref, wo_ref, lg_ref, lb_ref, h_ref):
    lses = [l1_ref[...], l2_ref[...], l3_ref[...]]
    outs = [o1_ref[...], o2_ref[...], o3_ref[...]]
    mx = jnp.maximum(jnp.maximum(lses[0], lses[1]), lses[2])
    es = [jnp.exp(l - mx) for l in lses]
    den = es[0] + es[1] + es[2]
    b_out = (es[0] / den) * outs[0] + (es[1] / den) * outs[1] + (es[2] / den) * outs[2]
    ya = jnp.dot(a_ref[...], wa_ref[...], preferred_element_type=F32)
    yb = jnp.dot(b_out.astype(BF16), wb_ref[...], preferred_element_type=F32)
    pre = g_ref[:, 0:D_MODEL] * ya + g_ref[:, D_MODEL:] * yb
    mix = jnp.dot(pre.astype(BF16), wo_ref[...], preferred_element_type=F32)
    h_ref[...] = _layer_norm(ALPHA * x_ref[...] + mix, lg_ref[...], lb_ref[...])


def _merge(xf, a_out, b_parts, gates, wa, wb, wo, ln_g, ln_b, tile):
    n = xf.shape[0]
    row = lambda width: pl.BlockSpec((tile, width), lambda i: (i, 0))
    full = lambda a: pl.BlockSpec(a.shape, lambda i: (0,) * a.ndim)
    (o1, l1), (o2, l2), (o3, l3) = b_parts
    return pl.pallas_call(
        _merge_kernel,
        out_shape=jax.ShapeDtypeStruct((n, D_MODEL), F32),
        grid=(n // tile,),
        in_specs=[row(D_MODEL), row(A_HEADS * HEAD_DIM)] + [row(B_GROUP_W)] * 6 + [row(2 * D_MODEL),
                  full(wa), full(wb), full(wo), full(ln_g), full(ln_b)],
        out_specs=row(D_MODEL),
        compiler_params=_cparams("parallel"),
        name="merge_output_projection",
    )(xf, a_out, o1, o2, o3, l1, l2, l3, gates, wa, wb, wo, ln_g, ln_b)


def _first_max(v):
    m = jnp.max(v, axis=0, keepdims=True)
    idx = lax.broadcasted_iota(jnp.int32, v.shape, 0)
    first = jnp.min(jnp.where(v == m, idx, v.shape[0]), axis=0, keepdims=True)
    return m, idx == first


def _router_kernel(h_ref, wr_ref, rb_ref, gate_ref):
    t = h_ref.shape[0]
    gs = N_EXPERTS // N_GROUPS
    logits = lax.dot_general(wr_ref[...], h_ref[...], (((1,), (1,)), ((), ())),
                             precision=lax.Precision.HIGHEST, preferred_element_type=F32)
    scores = jax.nn.sigmoid(logits)
    biased = scores + rb_ref[...]
    gscores = []
    for g in range(N_GROUPS):
        blk = biased[g * gs:(g + 1) * gs, :]
        m1, hit = _first_max(blk)
        m2 = jnp.max(jnp.where(hit, -jnp.inf, blk), axis=0, keepdims=True)
        gscores.append(m1 + m2)
    gscore = jnp.concatenate(gscores, axis=0)
    gsel = jnp.zeros((N_GROUPS, t), F32)
    for _ in range(TOPK_GROUPS):
        _, hit = _first_max(gscore)
        gsel = jnp.where(hit, 1.0, gsel)
        gscore = jnp.where(hit, -jnp.inf, gscore)
    esel = jnp.concatenate([jnp.broadcast_to(gsel[g:g + 1, :], (gs, t)) for g in range(N_GROUPS)], axis=0)
    cand = jnp.where(esel > 0.0, biased, -jnp.inf)
    top_s = jnp.zeros((N_EXPERTS, t), F32)
    for _ in range(TOP_K):
        _, hit = _first_max(cand)
        top_s = jnp.where(hit, scores, top_s)
        cand = jnp.where(hit, -jnp.inf, cand)
    gate = top_s / jnp.sum(top_s, axis=0, keepdims=True) * ROUTED_SCALE
    gate_ref[...] = jnp.concatenate([gate, jnp.zeros((LANES - N_EXPERTS, t), F32)], axis=0).T


def _router(h, wr_t, rb_col, tile):
    n = h.shape[0]
    return pl.pallas_call(
        _router_kernel,
        out_shape=jax.ShapeDtypeStruct((n, LANES), F32),
        grid=(n // tile,),
        in_specs=[pl.BlockSpec((tile, D_MODEL), lambda i: (i, 0)),
                  pl.BlockSpec(wr_t.shape, lambda i: (0, 0)),
                  pl.BlockSpec(rb_col.shape, lambda i: (0, 0))],
        out_specs=pl.BlockSpec((tile, LANES), lambda i: (i, 0)),
        compiler_params=_cparams("parallel"),
        name="moe_router",
    )(h, wr_t, rb_col)


_EXPERTS_PER_STEP = 2


def _swiglu(hb, w1, w3, w2):
    a = jnp.dot(hb, w1, preferred_element_type=F32)
    b = jnp.dot(hb, w3, preferred_element_type=F32)
    return jnp.dot((jax.nn.silu(a) * b).astype(BF16), w2, preferred_element_type=F32)


def _experts_kernel(h_ref, gate_ref, w1_ref, w3_ref, w2_ref, s1_ref, s3_ref, s2_ref, lg_ref, lb_ref,
                    y_ref, hb_ref, acc_ref):
    step = pl.program_id(1)

    @pl.when(step == 0)
    def _():
        hb = h_ref[...].astype(BF16)
        hb_ref[...] = hb
        acc_ref[...] = _swiglu(hb, s1_ref[...], s3_ref[...], s2_ref[...])

    hb = hb_ref[...]
    gate = gate_ref[...]
    lane = lax.broadcasted_iota(jnp.int32, gate.shape, 1)
    for k in range(_EXPERTS_PER_STEP):
        e = step * _EXPERTS_PER_STEP + k
        g_col = jnp.sum(jnp.where(lane == e, gate, 0.0), axis=1, keepdims=True)
        acc_ref[...] += _swiglu(hb, w1_ref[k], w3_ref[k], w2_ref[k]) * g_col

    @pl.when(step == pl.num_programs(1) - 1)
    def _():
        y_ref[...] = _layer_norm(ALPHA * h_ref[...] + acc_ref[...], lg_ref[...], lb_ref[...])


def _experts(h, gate, w1, w3, w2, s1, s3, s2, ln_g, ln_b, tile):
    n = h.shape[0]
    ec = _EXPERTS_PER_STEP
    row = lambda width: pl.BlockSpec((tile, width), lambda i, e: (i, 0))
    full = lambda a: pl.BlockSpec(a.shape, lambda i, e: (0,) * a.ndim)
    wspec = lambda a: pl.BlockSpec((ec,) + a.shape[1:], lambda i, e: (e, 0, 0))
    return pl.pallas_call(
        _experts_kernel,
        out_shape=jax.ShapeDtypeStruct((n, D_MODEL), F32),
        grid=(n // tile, N_EXPERTS // ec),
        in_specs=[row(D_MODEL), row(LANES), wspec(w1), wspec(w3), wspec(w2),
                  full(s1), full(s3), full(s2), full(ln_g), full(ln_b)],
        out_specs=row(D_MODEL),
        scratch_shapes=[pltpu.VMEM((tile, D_MODEL), BF16), pltpu.VMEM((tile, D_MODEL), F32)],
        compiler_params=_cparams("parallel", "arbitrary"),
        name="moe_experts",
    )(h, gate, w1, w3, w2, s1, s3, s2, ln_g, ln_b)


def _rope_freqs():
    inv16 = ROPE_THETA ** (-jnp.arange(0, ROPE_DIM, 2, dtype=F32) / ROPE_DIM)
    inv8 = ROPE_THETA ** (-jnp.arange(0, IDX_ROPE, 2, dtype=F32) / IDX_ROPE)

    def row(inv, fidx):
        return jnp.where(fidx >= 0, inv[np.maximum(fidx, 0)], 0.0)
    lay = _LAYOUT
    freq = jnp.stack([row(inv16, lay["fa"]), row(inv8, lay["fi"]), row(inv16, lay["fb"])])
    sign = jnp.asarray(np.stack([lay["sa"], lay["si"], lay["sb"]]))
    return freq.astype(F32), sign


def _layer(x, positions, w_in, b_gate, g_kv, w_uk, w_uv, w_branch_a, w_branch_b, w_o, ln1_g, ln1_b,
           w_router, router_bias, w1_e, w3_e, w2_e, ws1, ws3, ws2, ln2_g, ln2_b):
    b, s, d = x.shape
    n = b * s
    lay = _LAYOUT
    tile = min(256, n)
    xf = x.reshape(n, d)

    w_perm = (w_in[:, lay["cols"]] * lay["keep"]).astype(BF16)
    wuk = jnp.transpose(w_uk, (1, 2, 0)).reshape(A_HEADS * A_NOPE, A_KV_RANK).astype(BF16)
    wuv_t = jnp.zeros((A_HEADS, A_HEADS, HEAD_DIM, A_KV_RANK), F32)
    wuv_t = wuv_t.at[jnp.arange(A_HEADS), jnp.arange(A_HEADS)].set(jnp.transpose(w_uv, (1, 2, 0)))
    wuv_t = wuv_t.reshape(A_HEADS, A_HEADS * HEAD_DIM, A_KV_RANK).astype(BF16)
    masks = {k: jnp.asarray(lay[k], BF16) for k in ("m_ar", "m_an", "m_iq", "m_bq")}
    m_bv = jnp.asarray(lay["m_bv"], F32)

    freq, sign = _rope_freqs()
    tables = _rope_tables(positions.reshape(n, 1), freq, sign, tile)
    qa, ka, ckvt, iq, ikt, iw, bq, bk, bv, gates = _input_projection(
        xf, w_perm, tables, g_kv.reshape(1, -1), b_gate.reshape(1, -1), tile)

    a_out = _dsa_mixer(qa, iq, iw, ikt, ka, ckvt, wuk, wuv_t, masks["m_ar"], masks["m_an"], masks["m_iq"], b, s)
    b3 = lambda a: a.reshape(b, s, a.shape[1])
    b_parts = [_dilated_group(b3(bq), b3(bk), b3(bv), masks["m_bq"], m_bv, g, dil)
               for g, (_, dil) in enumerate(B_PATTERNS)]

    h = _merge(xf, a_out, b_parts, gates, w_branch_a.astype(BF16), w_branch_b.astype(BF16),
               w_o.astype(BF16), ln1_g.reshape(1, -1), ln1_b.reshape(1, -1), tile)

    gate = _router(h, w_router.T, router_bias.reshape(-1, 1), tile)
    y = _experts(h, gate, w1_e.astype(BF16), w3_e.astype(BF16), w2_e.astype(BF16),
                 ws1.astype(BF16), ws3.astype(BF16), ws2.astype(BF16),
                 ln2_g.reshape(1, -1), ln2_b.reshape(1, -1), min(1024, n))
    return y.reshape(b, s, d)


def kernel(x, positions, w_in, b_gate, g_kv, w_uk, w_uv, w_branch_a, w_branch_b, w_o, ln1_g, ln1_b,
           w_router, router_bias, w1_e, w3_e, w2_e, ws1, ws3, ws2, ln2_g, ln2_b):
    h = x
    for l in range(DEPTH):
        h = _layer(h, positions, w_in[l], b_gate[l], g_kv[l], w_uk[l], w_uv[l], w_branch_a[l],
                   w_branch_b[l], w_o[l], ln1_g[l], ln1_b[l], w_router[l], router_bias[l],
                   w1_e[l], w3_e[l], w2_e[l], ws1[l], ws3[l], ws2[l], ln2_g[l], ln2_b[l])
    return h
```

```python
import functools

import jax
import jax.numpy as jnp
import numpy as np
from jax import lax
from jax.experimental import pallas as pl
from jax.experimental.pallas import tpu as pltpu

F32 = jnp.float32
BF16 = jnp.bfloat16

D_MODEL = 1024
HEAD_DIM = 64
ROPE_DIM = 16
ROPE_THETA = 500000.0
Q_BLOCK = 128
A_HEADS = 8
A_NOPE = HEAD_DIM - ROPE_DIM
A_KV_RANK = 256
A_TOPK_MAX = 256
IDX_HEADS = 8
IDX_DIM = 32
IDX_ROPE = 8
B_PATTERNS = ((128, 1), (512, 4), (2048, 16))
B_GROUP_HEADS = 4
B_HEADS = B_GROUP_HEADS * len(B_PATTERNS)
B_GROUP_W = B_GROUP_HEADS * HEAD_DIM
N_EXPERTS = 64
TOP_K = 8
N_GROUPS = 8
TOPK_GROUPS = 4
D_EXPERT = 256
ROUTED_SCALE = 2.5
DEPTH = 1
ALPHA = (2.0 * DEPTH) ** 0.25
LN_EPS = 1e-5
RMS_EPS = 1e-6

LANES = 128
VMEM_LIMIT = 56 * 1024 * 1024
NEG_BIG = -1e30
INT_MIN = -(2 ** 31)

_OFF_AQ = 0
_OFF_CKV = _OFF_AQ + A_HEADS * HEAD_DIM
_OFF_AKR = _OFF_CKV + A_KV_RANK
_OFF_IQ = _OFF_AKR + ROPE_DIM
_OFF_IK = _OFF_IQ + IDX_HEADS * IDX_DIM
_OFF_IW = _OFF_IK + IDX_DIM
_OFF_BQ = _OFF_IW + IDX_HEADS
_OFF_BK = _OFF_BQ + B_HEADS * HEAD_DIM
_OFF_BV = _OFF_BK + B_HEADS * HEAD_DIM
_OFF_GA = _OFF_BV + B_HEADS * HEAD_DIM
_OFF_GB = _OFF_GA + D_MODEL
_IN_TOTAL = _OFF_GB + D_MODEL

_W_QAR, _W_QAN, _W_CKV, _W_AKR = 128, A_HEADS * A_NOPE, A_KV_RANK, 128
_W_IQ, _W_IK, _W_IW = 256, 256, 128
_W_B = B_HEADS * HEAD_DIM
_C_QAR = 0
_C_QAN = _C_QAR + _W_QAR
_C_CKV = _C_QAN + _W_QAN
_C_AKR = _C_CKV + _W_CKV
_C_IQ = _C_AKR + _W_AKR
_C_IK = _C_IQ + _W_IQ
_C_IW = _C_IK + _W_IK
_C_BQ = _C_IW + _W_IW
_C_BK = _C_BQ + _W_B
_C_BV = _C_BK + _W_B
_C_GA = _C_BV + _W_B
_C_GB = _C_GA + D_MODEL
_P_TOTAL = _C_GB + D_MODEL


def _idx_lane(l):
    if l < 16:
        return l // 4, l % 4
    if l < 64:
        return (l - 16) // 12, 8 + (l - 16) % 12
    if l < 80:
        return (l - 64) // 4, 4 + (l - 64) % 4
    return (l - 80) // 12, 20 + (l - 80) % 12


def _b_lane(l):
    half, r = l // 64, l % 64
    which, rr = r // 32, r % 32
    if rr < 8:
        return which, half * 8 + rr
    return which, 16 + half * 24 + (rr - 8)


def _build_layout():
    cols = np.zeros((_P_TOTAL,), np.int32)
    keep = np.ones((_P_TOTAL,), np.float32)
    for l in range(128):
        half, h, f = l // 64, (l % 64) // 8, l % 8
        cols[_C_QAR + l] = _OFF_AQ + h * HEAD_DIM + half * 8 + f
        cols[_C_AKR + l] = _OFF_AKR + half * 8 + f
    for h in range(A_HEADS):
        for j in range(A_NOPE):
            cols[_C_QAN + h * A_NOPE + j] = _OFF_AQ + h * HEAD_DIM + ROPE_DIM + j
    cols[_C_CKV:_C_CKV + _W_CKV] = _OFF_CKV + np.arange(_W_CKV)
    for sl in range(2):
        for l in range(128):
            hh, d = _idx_lane(l)
            cols[_C_IQ + sl * 128 + l] = _OFF_IQ + (sl * 4 + hh) * IDX_DIM + d
            cols[_C_IK + sl * 128 + l] = _OFF_IK + d
    cols[_C_IW:_C_IW + IDX_HEADS] = _OFF_IW + np.arange(IDX_HEADS)
    keep[_C_IW + IDX_HEADS:_C_IW + _W_IW] = 0.0
    for p in range(B_HEADS // 2):
        for l in range(128):
            which, d = _b_lane(l)
            h = 2 * p + which
            cols[_C_BQ + p * 128 + l] = _OFF_BQ + h * HEAD_DIM + d
            cols[_C_BK + p * 128 + l] = _OFF_BK + h * HEAD_DIM + d
    cols[_C_BV:_C_BV + _W_B] = _OFF_BV + np.arange(_W_B)
    cols[_C_GA:_C_GA + D_MODEL] = _OFF_GA + np.arange(D_MODEL)
    cols[_C_GB:_C_GB + D_MODEL] = _OFF_GB + np.arange(D_MODEL)

    fa = np.array([l % 8 for l in range(128)])
    sa = np.array([-1.0 if l < 64 else 1.0 for l in range(128)], np.float32)
    fi = np.full((128,), -1)
    si = np.zeros((128,), np.float32)
    fb = np.full((128,), -1)
    sb = np.zeros((128,), np.float32)
    for l in range(128):
        if l < 16 or 64 <= l < 80:
            fi[l] = l % 4
            si[l] = -1.0 if l < 64 else 1.0
        if l % 32 < 8:
            fb[l] = l % 32
            sb[l] = -1.0 if l < 64 else 1.0

    m_ar = np.zeros((A_HEADS, 128), np.float32)
    for l in range(128):
        m_ar[(l % 64) // 8, l] = 1.0
    m_an = np.zeros((A_HEADS, _W_QAN), np.float32)
    for h in range(A_HEADS):
        m_an[h, h * A_NOPE:(h + 1) * A_NOPE] = 1.0
    m_iq = np.zeros((IDX_HEADS, _W_IQ), np.float32)
    for sl in range(2):
        for l in range(128):
            m_iq[sl * 4 + _idx_lane(l)[0], sl * 128 + l] = 1.0
    m_bq = np.zeros((B_GROUP_HEADS, B_GROUP_W), np.float32)
    for p in range(2):
        for l in range(128):
            m_bq[2 * p + _b_lane(l)[0], p * 128 + l] = 1.0
    m_bv = np.zeros((B_GROUP_HEADS, B_GROUP_W), np.float32)
    for j in range(B_GROUP_HEADS):
        m_bv[j, j * HEAD_DIM:(j + 1) * HEAD_DIM] = 1.0
    return dict(cols=cols, keep=keep, fa=fa, sa=sa, fi=fi, si=si, fb=fb, sb=sb,
                m_ar=m_ar, m_an=m_an, m_iq=m_iq, m_bq=m_bq, m_bv=m_bv)


_LAYOUT = _build_layout()


def _cparams(*sem):
    return pltpu.CompilerParams(dimension_semantics=sem, vmem_limit_bytes=VMEM_LIMIT)


def _layer_norm(v, g, b):
    mu = jnp.mean(v, axis=-1, keepdims=True)
    var = jnp.mean(jnp.square(v - mu), axis=-1, keepdims=True)
    return (v - mu) * lax.rsqrt(var + LN_EPS) * g + b


def _dot_nt(a, b):
    return lax.dot_general(a, b, (((1,), (1,)), ((), ())), preferred_element_type=F32)


def _rope_table_kernel(pos_ref, freq_ref, sign_ref, out_ref):
    pos = pos_ref[...].astype(F32)
    for k in range(3):
        ang = pos * freq_ref[k:k + 1, :]
        out_ref[:, (2 * k) * LANES:(2 * k + 1) * LANES] = jnp.cos(ang)
        out_ref[:, (2 * k + 1) * LANES:(2 * k + 2) * LANES] = jnp.sin(ang) * sign_ref[k:k + 1, :]


def _rope_tables(pos_col, freq, sign, tile):
    n = pos_col.shape[0]
    return pl.pallas_call(
        _rope_table_kernel,
        out_shape=jax.ShapeDtypeStruct((n, 6 * LANES), F32),
        grid=(n // tile,),
        in_specs=[pl.BlockSpec((tile, 1), lambda i: (i, 0)),
                  pl.BlockSpec((3, LANES), lambda i: (0, 0)),
                  pl.BlockSpec((3, LANES), lambda i: (0, 0))],
        out_specs=pl.BlockSpec((tile, 6 * LANES), lambda i: (i, 0)),
        compiler_params=_cparams("parallel"),
        name="rope_tables",
    )(pos_col, freq, sign)


def _rope_slabs(y, cos, sin):
    outs = []
    for s in range(y.shape[1] // LANES):
        ys = y[:, s * LANES:(s + 1) * LANES]
        outs.append(ys * cos + pltpu.roll(ys, 64, 1) * sin)
    return outs[0] if len(outs) == 1 else jnp.concatenate(outs, axis=1)


def _proj_kernel(x_ref, w_ref, tab_ref, gkv_ref, bg_ref,
                 qa_ref, ka_ref, ckvt_ref, iq_ref, ik_ref, iw_ref, bq_ref, bk_ref, bv_ref, g_ref):
    xb = x_ref[...].astype(BF16)

    def proj(c0, width):
        return jnp.dot(xb, w_ref[:, c0:c0 + width], preferred_element_type=F32)

    cos_a, sin_a = tab_ref[:, 0:128], tab_ref[:, 128:256]
    cos_i, sin_i = tab_ref[:, 256:384], tab_ref[:, 384:512]
    cos_b, sin_b = tab_ref[:, 512:640], tab_ref[:, 640:768]

    qa_ref[:, 0:_W_QAR] = _rope_slabs(proj(_C_QAR, _W_QAR), cos_a, sin_a).astype(BF16)
    qa_ref[:, _W_QAR:] = proj(_C_QAN, _W_QAN).astype(BF16)
    ckv = proj(_C_CKV, _W_CKV)
    ckv = ckv * lax.rsqrt(jnp.mean(jnp.square(ckv), axis=-1, keepdims=True) + RMS_EPS) * gkv_ref[...]
    ka_ref[:, 0:_W_CKV] = ckv.astype(BF16)
    ckvt_ref[...] = ckv.T.astype(BF16)
    ka_ref[:, _W_CKV:] = _rope_slabs(proj(_C_AKR, _W_AKR), cos_a, sin_a).astype(BF16)
    iq_ref[...] = _rope_slabs(proj(_C_IQ, _W_IQ), cos_i, sin_i).astype(BF16)
    ik_ref[...] = _rope_slabs(proj(_C_IK, _W_IK), cos_i, sin_i).astype(BF16)
    iw_ref[...] = proj(_C_IW, _W_IW) * ((IDX_HEADS * IDX_DIM) ** -0.5)
    bq_ref[...] = _rope_slabs(proj(_C_BQ, _W_B), cos_b, sin_b).astype(BF16)
    bk_ref[...] = _rope_slabs(proj(_C_BK, _W_B), cos_b, sin_b).astype(BF16)
    bv_ref[...] = proj(_C_BV, _W_B).astype(BF16)
    g_ref[...] = jax.nn.sigmoid(proj(_C_GA, 2 * D_MODEL) + bg_ref[...])


def _input_projection(xf, w_perm, tables, g_kv, b_gate, tile):
    n = xf.shape[0]
    row = lambda width: pl.BlockSpec((tile, width), lambda i: (i, 0))
    full = lambda a: pl.BlockSpec(a.shape, lambda i: (0,) * a.ndim)
    out_w = [(_W_QAR + _W_QAN, BF16), (_W_CKV + _W_AKR, BF16), None, (_W_IQ, BF16), (_W_IK, BF16),
             (_W_IW, F32), (_W_B, BF16), (_W_B, BF16), (_W_B, BF16), (2 * D_MODEL, F32)]
    shapes = [jax.ShapeDtypeStruct((A_KV_RANK, n), BF16) if o is None else jax.ShapeDtypeStruct((n, o[0]), o[1])
              for o in out_w]
    specs = [pl.BlockSpec((A_KV_RANK, tile), lambda i: (0, i)) if o is None else row(o[0]) for o in out_w]
    return pl.pallas_call(
        _proj_kernel,
        out_shape=shapes,
        grid=(n // tile,),
        in_specs=[row(D_MODEL), full(w_perm), row(6 * LANES), full(g_kv), full(b_gate)],
        out_specs=specs,
        compiler_params=_cparams("parallel"),
        name="input_projection",
    )(xf, w_perm, tables, g_kv, b_gate)


def _dilated_kernel(q_ref, kc_ref, kp_ref, vc_ref, vp_ref, mq_ref, mv_ref, o_ref, lse_ref,
                    kwin_ref, vwin_ref, *, tq):
    first = pl.program_id(2) == 0
    kwin_ref[0:Q_BLOCK, :] = kp_ref[...]
    kwin_ref[Q_BLOCK:, :] = kc_ref[...]
    vwin_ref[0:Q_BLOCK, :] = vp_ref[...]
    vwin_ref[Q_BLOCK:, :] = vc_ref[...]
    t = lax.broadcasted_iota(jnp.int32, (Q_BLOCK, 2 * Q_BLOCK), 0)
    c = lax.broadcasted_iota(jnp.int32, (Q_BLOCK, 2 * Q_BLOCK), 1)
    diff = t + Q_BLOCK - c
    band = (diff >= 0) & (diff <= Q_BLOCK)
    scale = HEAD_DIM ** -0.5
    for sb in range(tq // Q_BLOCK):
        valid = band
        if sb == 0:
            valid = band & (c >= jnp.where(first, Q_BLOCK, 0))
        bias = jnp.where(valid, 0.0, NEG_BIG).astype(F32)
        q = q_ref[sb * Q_BLOCK:(sb + 1) * Q_BLOCK, :] * scale
        kw = kwin_ref[sb * Q_BLOCK:(sb + 2) * Q_BLOCK, :]
        vw = vwin_ref[sb * Q_BLOCK:(sb + 2) * Q_BLOCK, :]
        qs = jnp.concatenate([q * mq_ref[j:j + 1, :] for j in range(B_GROUP_HEADS)], axis=0)
        s = _dot_nt(qs, kw)
        o_acc = jnp.zeros((Q_BLOCK, B_GROUP_W), F32)
        lse_acc = jnp.zeros((Q_BLOCK, B_GROUP_W), F32)
        for j in range(B_GROUP_HEADS):
            sj = s[j * Q_BLOCK:(j + 1) * Q_BLOCK, :] + bias
            m = jnp.max(sj, axis=-1, keepdims=True)
            e = jnp.exp(sj - m)
            den = jnp.sum(e, axis=-1, keepdims=True)
            pv = jnp.dot(e.astype(BF16), vw, preferred_element_type=F32)
            mv = mv_ref[j:j + 1, :]
            o_acc = o_acc + (pv / den) * mv
            lse_acc = lse_acc + (m + jnp.log(den)) * mv
        o_ref[sb * Q_BLOCK:(sb + 1) * Q_BLOCK, :] = o_acc
        lse_ref[sb * Q_BLOCK:(sb + 1) * Q_BLOCK, :] = lse_acc


def _dilated_group(bq, bk, bv, mq, mv, g, dil):
    b, s, _ = bq.shape
    sub = s // dil
    tq = min(512, sub)
    nblk = tq // Q_BLOCK
    ng = len(B_PATTERNS)
    view = lambda a: a.reshape(b, sub, dil * a.shape[2])
    cur = pl.BlockSpec((None, tq, B_GROUP_W), lambda bi, r, i: (bi, i, r * ng + g))
    prev = pl.BlockSpec((None, Q_BLOCK, B_GROUP_W),
                        lambda bi, r, i: (bi, jnp.maximum(i * nblk - 1, 0), r * ng + g))
    const = lambda a: pl.BlockSpec(a.shape, lambda bi, r, i: (0, 0))
    out = pl.BlockSpec((None, tq, B_GROUP_W), lambda bi, r, i: (bi, i, r))
    o, lse = pl.pallas_call(
        functools.partial(_dilated_kernel, tq=tq),
        out_shape=[jax.ShapeDtypeStruct((b, sub, dil * B_GROUP_W), F32)] * 2,
        grid=(b, dil, sub // tq),
        in_specs=[cur, cur, prev, cur, prev, const(mq), const(mv)],
        out_specs=[out, out],
        scratch_shapes=[pltpu.VMEM((tq + Q_BLOCK, B_GROUP_W), BF16)] * 2,
        compiler_params=_cparams("parallel", "parallel", "arbitrary"),
        name=f"dilated_attention_d{dil}",
    )(view(bq), view(bk), view(bk), view(bv), view(bv), mq, mv)
    return o.reshape(b * s, B_GROUP_W), lse.reshape(b * s, B_GROUP_W)


_TK = 512


def _fold_rows(x, op):
    parts = [x[r:r + 8, :] for r in range(0, x.shape[0], 8)]
    while len(parts) > 1:
        parts = [op(parts[k], parts[k + 1]) for k in range(0, len(parts) - 1, 2)] + parts[len(parts) & ~1:]
    return parts[0]


def _skewed_tiles(n_tiles, produce, consume, buf_a, buf_b, carry):
    produce(0, buf_a)

    def pair(t, c):
        j = 2 * t
        produce(j + 1, buf_b)
        c = consume(j, buf_a, c)
        produce(j + 2, buf_a)
        return consume(j + 1, buf_b, c)

    n_pairs = (n_tiles - 1) // 2
    carry = lax.fori_loop(0, n_pairs, pair, carry)
    j = 2 * n_pairs

    def last_two(c):
        produce(j + 1, buf_b)
        return consume(j + 1, buf_b, consume(j, buf_a, c))

    return lax.cond(n_tiles - j == 2, last_two, lambda c: consume(j, buf_a, c), carry)


def _dsa_kernel(qa_ref, iq_ref, iw_ref, ikt_ref, ka_ref, ckvt_ref, wuk_ref, wuv_ref, mar_ref, man_ref, miq_ref,
                out_ref, key_ref, iqs_ref, qcat_ref, acc_ref, sa_ref, sb_ref, ma_ref, mb_ref, p_ref, *, n_sel):
    i = pl.program_id(1)
    q0 = i * Q_BLOCK
    n_keys = q0 + Q_BLOCK
    rows = A_HEADS * Q_BLOCK

    n_tiles = (n_keys + _TK - 1) // _TK
    tq_lane = q0 + lax.broadcasted_iota(jnp.int32, (_TK, Q_BLOCK), 1)
    krow = lax.broadcasted_iota(jnp.int32, (_TK, Q_BLOCK), 0)

    iq = iq_ref[...]
    for h in range(IDX_HEADS):
        iqs_ref[:, h * Q_BLOCK:(h + 1) * Q_BLOCK] = (iq * miq_ref[h:h + 1, :]).astype(F32).T.astype(BF16)
    iw_t = iw_ref[...].T

    def score_matmul(j, buf):
        k0 = pl.multiple_of(j * _TK, _TK)
        s = jnp.dot(ikt_ref[pl.ds(k0, _TK), :], iqs_ref[...], preferred_element_type=F32)
        for h in range(IDX_HEADS):
            buf[0][h] = s[:, h * Q_BLOCK:(h + 1) * Q_BLOCK]

    def score_keys(j, buf, carry):
        k0 = pl.multiple_of(j * _TK, _TK)
        sc = jnp.zeros((_TK, Q_BLOCK), F32)
        for h in range(IDX_HEADS):
            sc = sc + jnp.maximum(buf[0][h], 0.0) * iw_t[h:h + 1, :]
        bits = lax.bitcast_convert_type(sc + 0.0, jnp.int32)
        okey = bits ^ ((bits >> 31) & jnp.int32(0x7FFFFFFF))
        key_ref[pl.ds(k0, _TK), :] = jnp.where(krow + k0 <= tq_lane, okey, jnp.int32(INT_MIN))
        return carry

    _skewed_tiles(n_tiles, score_matmul, score_keys, (sa_ref, ma_ref), (sb_ref, mb_ref), 0)

    def count_ge(thr):
        def body(j, cnt):
            k0 = pl.multiple_of(j * _TK, _TK)
            ge = jnp.where(key_ref[pl.ds(k0, _TK), :] >= thr, 1.0, 0.0)
            return cnt + _fold_rows(ge, jnp.add)
        cnt8 = lax.fori_loop(0, n_tiles, body, jnp.zeros((8, Q_BLOCK), F32))
        return jnp.sum(cnt8, axis=0, keepdims=True)

    def bit_step(b, prefix):
        trial = prefix | (jnp.int32(1) << (31 - b))
        cnt = count_ge(trial ^ jnp.int32(INT_MIN))
        return jnp.where(cnt >= float(n_sel), trial, prefix)

    prefix = lax.fori_loop(0, 32, bit_step, jnp.zeros((1, Q_BLOCK), jnp.int32))
    thr = prefix ^ jnp.int32(INT_MIN)

    q_rope = qa_ref[:, 0:_W_QAR]
    q_nope = qa_ref[:, _W_QAR:]
    scale = HEAD_DIM ** -0.5
    c_rope = A_KV_RANK
    for h in range(A_HEADS):
        q_lat = jnp.dot(q_nope * man_ref[h:h + 1, :], wuk_ref[...], preferred_element_type=F32)
        cols = slice(h * Q_BLOCK, (h + 1) * Q_BLOCK)
        qcat_ref[0:c_rope, cols] = (q_lat.astype(BF16) * scale).astype(F32).T.astype(BF16)
        qcat_ref[c_rope:, cols] = (q_rope * mar_ref[h:h + 1, :] * scale).astype(F32).T.astype(BF16)

    acc_ref[...] = jnp.zeros(acc_ref.shape, F32)

    def logit_matmul(j, buf):
        s_buf, mx_buf = buf
        k0 = pl.multiple_of(j * _TK, _TK)
        sel = (key_ref[pl.ds(k0, _TK), :] >= thr) & (krow + k0 <= tq_lane)
        bias = jnp.where(sel, 0.0, NEG_BIG).astype(F32)
        s = jnp.dot(ka_ref[pl.ds(k0, _TK), :], qcat_ref[...], preferred_element_type=F32)
        for h in range(A_HEADS):
            sh = s[:, h * Q_BLOCK:(h + 1) * Q_BLOCK] + bias
            s_buf[h] = sh
            mx_buf[:, h * Q_BLOCK:(h + 1) * Q_BLOCK] = _fold_rows(sh, jnp.maximum)

    def softmax_pv(j, buf, carry):
        s_buf, mx_buf = buf
        m_old, l_old = carry
        k0 = pl.multiple_of(j * _TK, _TK)
        ckv_t = ckvt_ref[:, pl.ds(k0, _TK)]
        m_parts, l_parts = [], []
        for c in range(A_HEADS // 2):
            a_parts = []
            for h in (2 * c, 2 * c + 1):
                cols = slice(h * Q_BLOCK, (h + 1) * Q_BLOCK)
                m_h = jnp.maximum(m_old[:, cols], jnp.max(mx_buf[:, cols], axis=0, keepdims=True))
                a_h = jnp.exp(m_old[:, cols] - m_h)
                p = jnp.exp(s_buf[h] - m_h)
                p_ref[c, :, (h % 2) * Q_BLOCK:(h % 2 + 1) * Q_BLOCK] = p.astype(BF16)
                l_parts.append(a_h * l_old[:, cols] + jnp.sum(_fold_rows(p, jnp.add), axis=0, keepdims=True))
                m_parts.append(m_h)
                a_parts.append(a_h)
            pv = jnp.dot(ckv_t, p_ref[c], preferred_element_type=F32)
            acc_ref[c] = acc_ref[c] * jnp.concatenate(a_parts, axis=1) + pv
        return jnp.concatenate(m_parts, axis=1), jnp.concatenate(l_parts, axis=1)

    _, l_fin = _skewed_tiles(n_tiles, logit_matmul, softmax_pv, (sa_ref, ma_ref), (sb_ref, mb_ref),
                             (jnp.full((1, rows), NEG_BIG, F32), jnp.zeros((1, rows), F32)))

    inv_l = 1.0 / l_fin
    out_t = jnp.zeros((A_HEADS * HEAD_DIM, Q_BLOCK), F32)
    for h in range(A_HEADS):
        lanes = slice((h % 2) * Q_BLOCK, (h % 2 + 1) * Q_BLOCK)
        o_lat = (acc_ref[h // 2][:, lanes] * inv_l[:, h * Q_BLOCK:(h + 1) * Q_BLOCK]).astype(BF16)
        out_t = out_t + jnp.dot(wuv_ref[h], o_lat, preferred_element_type=F32)
    out_ref[...] = out_t.T.astype(BF16)


def _dsa_mixer(qa, iq, iw, ikt, ka, ckvt, wuk, wuv_t, m_ar, m_an, m_iq, b, s):
    n_sel = min(A_TOPK_MAX, s // 4)
    nq = s // Q_BLOCK
    rows = A_HEADS * Q_BLOCK
    blk = lambda width: pl.BlockSpec((Q_BLOCK, width), lambda bi, i: (bi * nq + i, 0))
    seq = lambda width: pl.BlockSpec((s, width), lambda bi, i: (bi, 0))
    const = lambda a: pl.BlockSpec(a.shape, lambda bi, i: (0,) * a.ndim)
    return pl.pallas_call(
        functools.partial(_dsa_kernel, n_sel=n_sel),
        out_shape=jax.ShapeDtypeStruct((b * s, A_HEADS * HEAD_DIM), BF16),
        grid=(b, nq),
        in_specs=[blk(_W_QAR + _W_QAN), blk(_W_IQ), blk(_W_IW), seq(_W_IK), seq(_W_CKV + _W_AKR),
                  pl.BlockSpec((A_KV_RANK, s), lambda bi, i: (0, bi)),
                  const(wuk), const(wuv_t), const(m_ar), const(m_an), const(m_iq)],
        out_specs=blk(A_HEADS * HEAD_DIM),
        scratch_shapes=[pltpu.VMEM((s, Q_BLOCK), jnp.int32),
                        pltpu.VMEM((_W_IQ, rows), BF16),
                        pltpu.VMEM((A_KV_RANK + _W_AKR, rows), BF16),
                        pltpu.VMEM((A_HEADS // 2, A_KV_RANK, 2 * Q_BLOCK), F32),
                        pltpu.VMEM((A_HEADS, _TK, Q_BLOCK), F32),
                        pltpu.VMEM((A_HEADS, _TK, Q_BLOCK), F32),
                        pltpu.VMEM((8, rows), F32),
                        pltpu.VMEM((8, rows), F32),
                        pltpu.VMEM((A_HEADS // 2, _TK, 2 * Q_BLOCK), BF16)],
        compiler_params=_cparams("parallel", "arbitrary"),
        name="dsa_attention",
    )(qa, iq, iw, ikt, ka, ckvt, wuk, wuv_t, m_ar, m_an, m_iq)


def _merge_kernel(x_ref, a_ref, o1_ref, o2_ref, o3_ref, l1_ref, l2_ref, l3_ref, g_ref,
                  wa_ref, wb_ref, wo_ref, lg_ref, lb_ref, h_ref):
    lses = [l1_ref[...], l2_ref[...], l3_ref[...]]
    outs = [o1_ref[...], o2_ref[...], o3_ref[...]]
    mx = jnp.maximum(jnp.maximum(lses[0], lses[1]), lses[2])
    es = [jnp.exp(l - mx) for l in lses]
    den = es[0] + es[1] + es[2]
    b_out = (es[0] / den) * outs[0] + (es[1] / den) * outs[1] + (es[2] / den) * outs[2]
    ya = jnp.dot(a_ref[...], wa_ref[...], preferred_element_type=F32)
    yb = jnp.dot(b_out.astype(BF16), wb_ref[...], preferred_element_type=F32)
    pre = g_ref[:, 0:D_MODEL] * ya + g_ref[:, D_MODEL:] * yb
    mix = jnp.dot(pre.astype(BF16), wo_ref[...], preferred_element_type=F32)
    h_ref[...] = _layer_norm(ALPHA * x_ref[...] + mix, lg_ref[...], lb_ref[...])


def _merge(xf, a_out, b_parts, gates, wa, wb, wo, ln_g, ln_b, tile):
    n = xf.shape[0]
    row = lambda width: pl.BlockSpec((tile, width), lambda i: (i, 0))
    full = lambda a: pl.BlockSpec(a.shape, lambda i: (0,) * a.ndim)
    (o1, l1), (o2, l2), (o3, l3) = b_parts
    return pl.pallas_call(
        _merge_kernel,
        out_shape=jax.ShapeDtypeStruct((n, D_MODEL), F32),
        grid=(n // tile,),
        in_specs=[row(D_MODEL), row(A_HEADS * HEAD_DIM)] + [row(B_GROUP_W)] * 6 + [row(2 * D_MODEL),
                  full(wa), full(wb), full(wo), full(ln_g), full(ln_b)],
        out_specs=row(D_MODEL),
        compiler_params=_cparams("parallel"),
        name="merge_output_projection",
    )(xf, a_out, o1, o2, o3, l1, l2, l3, gates, wa, wb, wo, ln_g, ln_b)


def _first_max(v):
    m = jnp.max(v, axis=0, keepdims=True)
    idx = lax.broadcasted_iota(jnp.int32, v.shape, 0)
    first = jnp.min(jnp.where(v == m, idx, v.shape[0]), axis=0, keepdims=True)
    return m, idx == first


def _router_kernel(h_ref, wr_ref, rb_ref, gate_ref):
    t = h_ref.shape[0]
    gs = N_EXPERTS // N_GROUPS
    logits = lax.dot_general(wr_ref[...], h_ref[...], (((1,), (1,)), ((), ())),
                             precision=lax.Precision.HIGHEST, preferred_element_type=F32)
    scores = jax.nn.sigmoid(logits)
    biased = scores + rb_ref[...]
    gscores = []
    for g in range(N_GROUPS):
        blk = biased[g * gs:(g + 1) * gs, :]
        m1, hit = _first_max(blk)
        m2 = jnp.max(jnp.where(hit, -jnp.inf, blk), axis=0, keepdims=True)
        gscores.append(m1 + m2)
    gscore = jnp.concatenate(gscores, axis=0)
    gsel = jnp.zeros((N_GROUPS, t), F32)
    for _ in range(TOPK_GROUPS):
        _, hit = _first_max(gscore)
        gsel = jnp.where(hit, 1.0, gsel)
        gscore = jnp.where(hit, -jnp.inf, gscore)
    esel = jnp.concatenate([jnp.broadcast_to(gsel[g:g + 1, :], (gs, t)) for g in range(N_GROUPS)], axis=0)
    cand = jnp.where(esel > 0.0, biased, -jnp.inf)
    top_s = jnp.zeros((N_EXPERTS, t), F32)
    for _ in range(TOP_K):
        _, hit = _first_max(cand)
        top_s = jnp.where(hit, scores, top_s)
        cand = jnp.where(hit, -jnp.inf, cand)
    gate = top_s / jnp.sum(top_s, axis=0, keepdims=True) * ROUTED_SCALE
    gate_ref[...] = jnp.concatenate([gate, jnp.zeros((LANES - N_EXPERTS, t), F32)], axis=0).T


def _router(h, wr_t, rb_col, tile):
    n = h.shape[0]
    return pl.pallas_call(
        _router_kernel,
        out_shape=jax.ShapeDtypeStruct((n, LANES), F32),
        grid=(n // tile,),
        in_specs=[pl.BlockSpec((tile, D_MODEL), lambda i: (i, 0)),
                  pl.BlockSpec(wr_t.shape, lambda i: (0, 0)),
                  pl.BlockSpec(rb_col.shape, lambda i: (0, 0))],
        out_specs=pl.BlockSpec((tile, LANES), lambda i: (i, 0)),
        compiler_params=_cparams("parallel"),
        name="moe_router",
    )(h, wr_t, rb_col)


_EXPERTS_PER_STEP = 2


def _swiglu(hb, w1, w3, w2):
    a = jnp.dot(hb, w1, preferred_element_type=F32)
    b = jnp.dot(hb, w3, preferred_element_type=F32)
    return jnp.dot((jax.nn.silu(a) * b).astype(BF16), w2, preferred_element_type=F32)


def _experts_kernel(h_ref, gate_ref, w1_ref, w3_ref, w2_ref, s1_ref, s3_ref, s2_ref, lg_ref, lb_ref,
                    y_ref, hb_ref, acc_ref):
    step = pl.program_id(1)

    @pl.when(step == 0)
    def _():
        hb = h_ref[...].astype(BF16)
        hb_ref[...] = hb
        acc_ref[...] = _swiglu(hb, s1_ref[...], s3_ref[...], s2_ref[...])

    hb = hb_ref[...]
    gate = gate_ref[...]
    lane = lax.broadcasted_iota(jnp.int32, gate.shape, 1)
    for k in range(_EXPERTS_PER_STEP):
        e = step * _EXPERTS_PER_STEP + k
        g_col = jnp.sum(jnp.where(lane == e, gate, 0.0), axis=1, keepdims=True)
        acc_ref[...] += _swiglu(hb, w1_ref[k], w3_ref[k], w2_ref[k]) * g_col

    @pl.when(step == pl.num_programs(1) - 1)
    def _():
        y_ref[...] = _layer_norm(ALPHA * h_ref[...] + acc_ref[...], lg_ref[...], lb_ref[...])


def _experts(h, gate, w1, w3, w2, s1, s3, s2, ln_g, ln_b, tile):
    n = h.shape[0]
    ec = _EXPERTS_PER_STEP
    row = lambda width: pl.BlockSpec((tile, width), lambda i, e: (i, 0))
    full = lambda a: pl.BlockSpec(a.shape, lambda i, e: (0,) * a.ndim)
    wspec = lambda a: pl.BlockSpec((ec,) + a.shape[1:], lambda i, e: (e, 0, 0))
    return pl.pallas_call(
        _experts_kernel,
        out_shape=jax.ShapeDtypeStruct((n, D_MODEL), F32),
        grid=(n // tile, N_EXPERTS // ec),
        in_specs=[row(D_MODEL), row(LANES), wspec(w1), wspec(w3), wspec(w2),
                  full(s1), full(s3), full(s2), full(ln_g), full(ln_b)],
        out_specs=row(D_MODEL),
        scratch_shapes=[pltpu.VMEM((tile, D_MODEL), BF16), pltpu.VMEM((tile, D_MODEL), F32)],
        compiler_params=_cparams("parallel", "arbitrary"),
        name="moe_experts",
    )(h, gate, w1, w3, w2, s1, s3, s2, ln_g, ln_b)


def _rope_freqs():
    inv16 = ROPE_THETA ** (-jnp.arange(0, ROPE_DIM, 2, dtype=F32) / ROPE_DIM)
    inv8 = ROPE_THETA ** (-jnp.arange(0, IDX_ROPE, 2, dtype=F32) / IDX_ROPE)

    def row(inv, fidx):
        return jnp.where(fidx >= 0, inv[np.maximum(fidx, 0)], 0.0)
    lay = _LAYOUT
    freq = jnp.stack([row(inv16, lay["fa"]), row(inv8, lay["fi"]), row(inv16, lay["fb"])])
    sign = jnp.asarray(np.stack([lay["sa"], lay["si"], lay["sb"]]))
    return freq.astype(F32), sign


def _layer(x, positions, w_in, b_gate, g_kv, w_uk, w_uv, w_branch_a, w_branch_b, w_o, ln1_g, ln1_b,
           w_router, router_bias, w1_e, w3_e, w2_e, ws1, ws3, ws2, ln2_g, ln2_b):
    b, s, d = x.shape
    n = b * s
    lay = _LAYOUT
    tile = min(256, n)
    xf = x.reshape(n, d)

    w_perm = (w_in[:, lay["cols"]] * lay["keep"]).astype(BF16)
    wuk = jnp.transpose(w_uk, (1, 2, 0)).reshape(A_HEADS * A_NOPE, A_KV_RANK).astype(BF16)
    wuv_t = jnp.zeros((A_HEADS, A_HEADS, HEAD_DIM, A_KV_RANK), F32)
    wuv_t = wuv_t.at[jnp.arange(A_HEADS), jnp.arange(A_HEADS)].set(jnp.transpose(w_uv, (1, 2, 0)))
    wuv_t = wuv_t.reshape(A_HEADS, A_HEADS * HEAD_DIM, A_KV_RANK).astype(BF16)
    masks = {k: jnp.asarray(lay[k], BF16) for k in ("m_ar", "m_an", "m_iq", "m_bq")}
    m_bv = jnp.asarray(lay["m_bv"], F32)

    freq, sign = _rope_freqs()
    tables = _rope_tables(positions.reshape(n, 1), freq, sign, tile)
    qa, ka, ckvt, iq, ikt, iw, bq, bk, bv, gates = _input_projection(
        xf, w_perm, tables, g_kv.reshape(1, -1), b_gate.reshape(1, -1), tile)

    a_out = _dsa_mixer(qa, iq, iw, ikt, ka, ckvt, wuk, wuv_t, masks["m_ar"], masks["m_an"], masks["m_iq"], b, s)
    b3 = lambda a: a.reshape(b, s, a.shape[1])
    b_parts = [_dilated_group(b3(bq), b3(bk), b3(bv), masks["m_bq"], m_bv, g, dil)
               for g, (_, dil) in enumerate(B_PATTERNS)]

    h = _merge(xf, a_out, b_parts, gates, w_branch_a.astype(BF16), w_branch_b.astype(BF16),
               w_o.astype(BF16), ln1_g.reshape(1, -1), ln1_b.reshape(1, -1), tile)

    gate = _router(h, w_router.T, router_bias.reshape(-1, 1), tile)
    y = _experts(h, gate, w1_e.astype(BF16), w3_e.astype(BF16), w2_e.astype(BF16),
                 ws1.astype(BF16), ws3.astype(BF16), ws2.astype(BF16),
                 ln2_g.reshape(1, -1), ln2_b.reshape(1, -1), min(1024, n))
    return y.reshape(b, s, d)


def kernel(x, positions, w_in, b_gate, g_kv, w_uk, w_uv, w_branch_a, w_branch_b, w_o, ln1_g, ln1_b,
           w_router, router_bias, w1_e, w3_e, w2_e, ws1, ws3, ws2, ln2_g, ln2_b):
    h = x
    for l in range(DEPTH):
        h = _layer(h, positions, w_in[l], b_gate[l], g_kv[l], w_uk[l], w_uv[l], w_branch_a[l],
                   w_branch_b[l], w_o[l], ln1_g[l], ln1_b[l], w_router[l], router_bias[l],
                   w1_e[l], w3_e[l], w2_e[l], ws1[l], ws3[l], ws2[l], ln2_g[l], ln2_b[l])
    return h
```

```python
import functools

import jax
import jax.numpy as jnp
import numpy as np
from jax import lax
from jax.experimental import pallas as pl
from jax.experimental.pallas import tpu as pltpu

F32 = jnp.float32
BF16 = jnp.bfloat16

D_MODEL = 1024
HEAD_DIM = 64
ROPE_DIM = 16
ROPE_THETA = 500000.0
Q_BLOCK = 128
A_HEADS = 8
A_NOPE = HEAD_DIM - ROPE_DIM
A_KV_RANK = 256
A_TOPK_MAX = 256
IDX_HEADS = 8
IDX_DIM = 32
IDX_ROPE = 8
B_PATTERNS = ((128, 1), (512, 4), (2048, 16))
B_GROUP_HEADS = 4
B_HEADS = B_GROUP_HEADS * len(B_PATTERNS)
B_GROUP_W = B_GROUP_HEADS * HEAD_DIM
N_EXPERTS = 64
TOP_K = 8
N_GROUPS = 8
TOPK_GROUPS = 4
D_EXPERT = 256
ROUTED_SCALE = 2.5
DEPTH = 1
ALPHA = (2.0 * DEPTH) ** 0.25
LN_EPS = 1e-5
RMS_EPS = 1e-6

LANES = 128
VMEM_LIMIT = 56 * 1024 * 1024
NEG_BIG = -1e30
INT_MIN = -(2 ** 31)

_OFF_AQ = 0
_OFF_CKV = _OFF_AQ + A_HEADS * HEAD_DIM
_OFF_AKR = _OFF_CKV + A_KV_RANK
_OFF_IQ = _OFF_AKR + ROPE_DIM
_OFF_IK = _OFF_IQ + IDX_HEADS * IDX_DIM
_OFF_IW = _OFF_IK + IDX_DIM
_OFF_BQ = _OFF_IW + IDX_HEADS
_OFF_BK = _OFF_BQ + B_HEADS * HEAD_DIM
_OFF_BV = _OFF_BK + B_HEADS * HEAD_DIM
_OFF_GA = _OFF_BV + B_HEADS * HEAD_DIM
_OFF_GB = _OFF_GA + D_MODEL
_IN_TOTAL = _OFF_GB + D_MODEL

_W_QAR, _W_QAN, _W_CKV, _W_AKR = 128, A_HEADS * A_NOPE, A_KV_RANK, 128
_W_IQ, _W_IK, _W_IW = 256, 256, 128
_W_B = B_HEADS * HEAD_DIM
_C_QAR = 0
_C_QAN = _C_QAR + _W_QAR
_C_CKV = _C_QAN + _W_QAN
_C_AKR = _C_CKV + _W_CKV
_C_IQ = _C_AKR + _W_AKR
_C_IK = _C_IQ + _W_IQ
_C_IW = _C_IK + _W_IK
_C_BQ = _C_IW + _W_IW
_C_BK = _C_BQ + _W_B
_C_BV = _C_BK + _W_B
_C_GA = _C_BV + _W_B
_C_GB = _C_GA + D_MODEL
_P_TOTAL = _C_GB + D_MODEL


def _idx_lane(l):
    if l < 16:
        return l // 4, l % 4
    if l < 64:
        return (l - 16) // 12, 8 + (l - 16) % 12
    if l < 80:
        return (l - 64) // 4, 4 + (l - 64) % 4
    return (l - 80) // 12, 20 + (l - 80) % 12


def _b_lane(l):
    half, r = l // 64, l % 64
    which, rr = r // 32, r % 32
    if rr < 8:
        return which, half * 8 + rr
    return which, 16 + half * 24 + (rr - 8)


def _build_layout():
    cols = np.zeros((_P_TOTAL,), np.int32)
    keep = np.ones((_P_TOTAL,), np.float32)
    for l in range(128):
        half, h, f = l // 64, (l % 64) // 8, l % 8
        cols[_C_QAR + l] = _OFF_AQ + h * HEAD_DIM + half * 8 + f
        cols[_C_AKR + l] = _OFF_AKR + half * 8 + f
    for h in range(A_HEADS):
        for j in range(A_NOPE):
            cols[_C_QAN + h * A_NOPE + j] = _OFF_AQ + h * HEAD_DIM + ROPE_DIM + j
    cols[_C_CKV:_C_CKV + _W_CKV] = _OFF_CKV + np.arange(_W_CKV)
    for sl in range(2):
        for l in range(128):
            hh, d = _idx_lane(l)
            cols[_C_IQ + sl * 128 + l] = _OFF_IQ + (sl * 4 + hh) * IDX_DIM + d
            cols[_C_IK + sl * 128 + l] = _OFF_IK + d
    cols[_C_IW:_C_IW + IDX_HEADS] = _OFF_IW + np.arange(IDX_HEADS)
    keep[_C_IW + IDX_HEADS:_C_IW + _W_IW] = 0.0
    for p in range(B_HEADS // 2):
        for l in range(128):
            which, d = _b_lane(l)
            h = 2 * p + which
            cols[_C_BQ + p * 128 + l] = _OFF_BQ + h * HEAD_DIM + d
            cols[_C_BK + p * 128 + l] = _OFF_BK + h * HEAD_DIM + d
    cols[_C_BV:_C_BV + _W_B] = _OFF_BV + np.arange(_W_B)
    cols[_C_GA:_C_GA + D_MODEL] = _OFF_GA + np.arange(D_MODEL)
    cols[_C_GB:_C_GB + D_MODEL] = _OFF_GB + np.arange(D_MODEL)

    fa = np.array([l % 8 for l in range(128)])
    sa = np.array([-1.0 if l < 64 else 1.0 for l in range(128)], np.float32)
    fi = np.full((128,), -1)
    si = np.zeros((128,), np.float32)
    fb = np.full((128,), -1)
    sb = np.zeros((128,), np.float32)
    for l in range(128):
        if l < 16 or 64 <= l < 80:
            fi[l] = l % 4
            si[l] = -1.0 if l < 64 else 1.0
        if l % 32 < 8:
            fb[l] = l % 32
            sb[l] = -1.0 if l < 64 else 1.0

    m_ar = np.zeros((A_HEADS, 128), np.float32)
    for l in range(128):
        m_ar[(l % 64) // 8, l] = 1.0
    m_an = np.zeros((A_HEADS, _W_QAN), np.float32)
    for h in range(A_HEADS):
        m_an[h, h * A_NOPE:(h + 1) * A_NOPE] = 1.0
    m_iq = np.zeros((IDX_HEADS, _W_IQ), np.float32)
    for sl in range(2):
        for l in range(128):
            m_iq[sl * 4 + _idx_lane(l)[0], sl * 128 + l] = 1.0
    m_bq = np.zeros((B_GROUP_HEADS, B_GROUP_W), np.float32)
    for p in range(2):
        for l in range(128):
            m_bq[2 * p + _b_lane(l)[0], p * 128 + l] = 1.0
    m_bv = np.zeros((B_GROUP_HEADS, B_GROUP_W), np.float32)
    for j in range(B_GROUP_HEADS):
        m_bv[j, j * HEAD_DIM:(j + 1) * HEAD_DIM] = 1.0
    return dict(cols=cols, keep=keep, fa=fa, sa=sa, fi=fi, si=si, fb=fb, sb=sb,
                m_ar=m_ar, m_an=m_an, m_iq=m_iq, m_bq=m_bq, m_bv=m_bv)


_LAYOUT = _build_layout()


def _cparams(*sem):
    return pltpu.CompilerParams(dimension_semantics=sem, vmem_limit_bytes=VMEM_LIMIT)


def _layer_norm(v, g, b):
    mu = jnp.mean(v, axis=-1, keepdims=True)
    var = jnp.mean(jnp.square(v - mu), axis=-1, keepdims=True)
    return (v - mu) * lax.rsqrt(var + LN_EPS) * g + b


def _dot_nt(a, b):
    return lax.dot_general(a, b, (((1,), (1,)), ((), ())), preferred_element_type=F32)


def _rope_table_kernel(pos_ref, freq_ref, sign_ref, out_ref):
    pos = pos_ref[...].astype(F32)
    for k in range(3):
        ang = pos * freq_ref[k:k + 1, :]
        out_ref[:, (2 * k) * LANES:(2 * k + 1) * LANES] = jnp.cos(ang)
        out_ref[:, (2 * k + 1) * LANES:(2 * k + 2) * LANES] = jnp.sin(ang) * sign_ref[k:k + 1, :]


def _rope_tables(pos_col, freq, sign, tile):
    n = pos_col.shape[0]
    return pl.pallas_call(
        _rope_table_kernel,
        out_shape=jax.ShapeDtypeStruct((n, 6 * LANES), F32),
        grid=(n // tile,),
        in_specs=[pl.BlockSpec((tile, 1), lambda i: (i, 0)),
                  pl.BlockSpec((3, LANES), lambda i: (0, 0)),
                  pl.BlockSpec((3, LANES), lambda i: (0, 0))],
        out_specs=pl.BlockSpec((tile, 6 * LANES), lambda i: (i, 0)),
        compiler_params=_cparams("parallel"),
        name="rope_tables",
    )(pos_col, freq, sign)


def _rope_slabs(y, cos, sin):
    outs = []
    for s in range(y.shape[1] // LANES):
        ys = y[:, s * LANES:(s + 1) * LANES]
        outs.append(ys * cos + pltpu.roll(ys, 64, 1) * sin)
    return outs[0] if len(outs) == 1 else jnp.concatenate(outs, axis=1)


def _store_residue_major(out_ref, y, scr_ref, dil):
    if dil == 1:
        out_ref[...] = y.astype(out_ref.dtype)
        return
    rows, width = y.shape[0] // dil, y.shape[1]
    for c in range(width // LANES):
        scr_ref[c] = y[:, c * LANES:(c + 1) * LANES]
    for r in range(dil):
        for c in range(width // LANES):
            lanes = slice(r * width + c * LANES, r * width + (c + 1) * LANES)
            out_ref[:, lanes] = scr_ref[c, pl.ds(r, rows, stride=dil), :].astype(out_ref.dtype)


def _load_token_major(ref, scr_ref, dil):
    if dil == 1:
        return ref[...]
    rows, width = ref.shape[0], ref.shape[1] // dil
    for r in range(dil):
        for c in range(width // LANES):
            lanes = slice(r * width + c * LANES, r * width + (c + 1) * LANES)
            scr_ref[c, pl.ds(r, rows, stride=dil), :] = ref[:, lanes]
    return jnp.concatenate([scr_ref[c] for c in range(width // LANES)], axis=1)


def _proj_kernel(x_ref, w_ref, tab_ref, gkv_ref, bg_ref,
                 qa_ref, ka_ref, ckvt_ref, iq_ref, ik_ref, iw_ref, g_ref, *rest):
    b_refs, scr_ref = rest[:-1], rest[-1]
    xb = x_ref[...].astype(BF16)

    def proj(c0, width):
        return jnp.dot(xb, w_ref[:, c0:c0 + width], preferred_element_type=F32)

    cos_a, sin_a = tab_ref[:, 0:128], tab_ref[:, 128:256]
    cos_i, sin_i = tab_ref[:, 256:384], tab_ref[:, 384:512]
    cos_b, sin_b = tab_ref[:, 512:640], tab_ref[:, 640:768]

    qa_ref[:, 0:_W_QAR] = _rope_slabs(proj(_C_QAR, _W_QAR), cos_a, sin_a).astype(BF16)
    qa_ref[:, _W_QAR:] = proj(_C_QAN, _W_QAN).astype(BF16)
    ckv = proj(_C_CKV, _W_CKV)
    ckv = ckv * lax.rsqrt(jnp.mean(jnp.square(ckv), axis=-1, keepdims=True) + RMS_EPS) * gkv_ref[...]
    ka_ref[:, 0:_W_CKV] = ckv.astype(BF16)
    ckvt_ref[...] = ckv.T.astype(BF16)
    ka_ref[:, _W_CKV:] = _rope_slabs(proj(_C_AKR, _W_AKR), cos_a, sin_a).astype(BF16)
    iq_ref[...] = _rope_slabs(proj(_C_IQ, _W_IQ), cos_i, sin_i).astype(BF16)
    ik_ref[...] = _rope_slabs(proj(_C_IK, _W_IK), cos_i, sin_i).astype(BF16)
    iw_ref[...] = proj(_C_IW, _W_IW) * ((IDX_HEADS * IDX_DIM) ** -0.5)
    g_ref[...] = jax.nn.sigmoid(proj(_C_GA, 2 * D_MODEL) + bg_ref[...])
    ng = len(B_PATTERNS)
    for kind, c0 in enumerate((_C_BQ, _C_BK, _C_BV)):
        for g, (_, dil) in enumerate(B_PATTERNS):
            y = proj(c0 + g * B_GROUP_W, B_GROUP_W)
            if kind < 2:
                y = _rope_slabs(y, cos_b, sin_b)
            _store_residue_major(b_refs[kind * ng + g], y, scr_ref, dil)


def _input_projection(xf, w_perm, tables, g_kv, b_gate, tile):
    n = xf.shape[0]
    row = lambda width: pl.BlockSpec((tile, width), lambda i: (i, 0))
    full = lambda a: pl.BlockSpec(a.shape, lambda i: (0,) * a.ndim)
    out_w = [(_W_QAR + _W_QAN, BF16), (_W_CKV + _W_AKR, BF16), None, (_W_IQ, BF16), (_W_IK, BF16),
             (_W_IW, F32), (2 * D_MODEL, F32)]
    shapes = [jax.ShapeDtypeStruct((A_KV_RANK, n), BF16) if o is None else jax.ShapeDtypeStruct((n, o[0]), o[1])
              for o in out_w]
    specs = [pl.BlockSpec((A_KV_RANK, tile), lambda i: (0, i)) if o is None else row(o[0]) for o in out_w]
    for _ in range(3):
        for _, dil in B_PATTERNS:
            shapes.append(jax.ShapeDtypeStruct((n // dil, dil * B_GROUP_W), BF16))
            specs.append(pl.BlockSpec((tile // dil, dil * B_GROUP_W), lambda i: (i, 0)))
    return pl.pallas_call(
        _proj_kernel,
        out_shape=shapes,
        grid=(n // tile,),
        in_specs=[row(D_MODEL), full(w_perm), row(6 * LANES), full(g_kv), full(b_gate)],
        out_specs=specs,
        scratch_shapes=[pltpu.VMEM((B_GROUP_W // LANES, tile, LANES), F32)],
        compiler_params=_cparams("parallel"),
        name="input_projection",
    )(xf, w_perm, tables, g_kv, b_gate)


def _dilated_kernel(q_ref, kc_ref, kp_ref, vc_ref, vp_ref, mq_ref, mv_ref, o_ref, lse_ref,
                    kwin_ref, vwin_ref, *, tq):
    first = pl.program_id(2) == 0
    kwin_ref[0:Q_BLOCK, :] = kp_ref[...]
    kwin_ref[Q_BLOCK:, :] = kc_ref[...]
    vwin_ref[0:Q_BLOCK, :] = vp_ref[...]
    vwin_ref[Q_BLOCK:, :] = vc_ref[...]
    t = lax.broadcasted_iota(jnp.int32, (Q_BLOCK, 2 * Q_BLOCK), 0)
    c = lax.broadcasted_iota(jnp.int32, (Q_BLOCK, 2 * Q_BLOCK), 1)
    diff = t + Q_BLOCK - c
    band = (diff >= 0) & (diff <= Q_BLOCK)
    scale = HEAD_DIM ** -0.5
    for sb in range(tq // Q_BLOCK):
        valid = band
        if sb == 0:
            valid = band & (c >= jnp.where(first, Q_BLOCK, 0))
        bias = jnp.where(valid, 0.0, NEG_BIG).astype(F32)
        q = q_ref[sb * Q_BLOCK:(sb + 1) * Q_BLOCK, :] * scale
        kw = kwin_ref[sb * Q_BLOCK:(sb + 2) * Q_BLOCK, :]
        vw = vwin_ref[sb * Q_BLOCK:(sb + 2) * Q_BLOCK, :]
        qs = jnp.concatenate([q * mq_ref[j:j + 1, :] for j in range(B_GROUP_HEADS)], axis=0)
        s = _dot_nt(qs, kw)
        o_acc = jnp.zeros((Q_BLOCK, B_GROUP_W), F32)
        lse_acc = jnp.zeros((Q_BLOCK, B_GROUP_W), F32)
        for j in range(B_GROUP_HEADS):
            sj = s[j * Q_BLOCK:(j + 1) * Q_BLOCK, :] + bias
            m = jnp.max(sj, axis=-1, keepdims=True)
            e = jnp.exp(sj - m)
            den = jnp.sum(e, axis=-1, keepdims=True)
            pv = jnp.dot(e.astype(BF16), vw, preferred_element_type=F32)
            mv = mv_ref[j:j + 1, :]
            o_acc = o_acc + (pv / den) * mv
            lse_acc = lse_acc + (m + jnp.log(den)) * mv
        o_ref[sb * Q_BLOCK:(sb + 1) * Q_BLOCK, :] = o_acc
        lse_ref[sb * Q_BLOCK:(sb + 1) * Q_BLOCK, :] = lse_acc


def _dilated_group(bq, bk, bv, mq, mv, b, dil):
    sub = bq.shape[0] // b
    tq = min(512, sub)
    nblk = tq // Q_BLOCK
    view = lambda a: a.reshape(b, sub, dil * B_GROUP_W)
    cur = pl.BlockSpec((None, tq, B_GROUP_W), lambda bi, r, i: (bi, i, r))
    prev = pl.BlockSpec((None, Q_BLOCK, B_GROUP_W),
                        lambda bi, r, i: (bi, jnp.maximum(i * nblk - 1, 0), r))
    const = lambda a: pl.BlockSpec(a.shape, lambda bi, r, i: (0, 0))
    out = cur
    o, lse = pl.pallas_call(
        functools.partial(_dilated_kernel, tq=tq),
        out_shape=[jax.ShapeDtypeStruct((b, sub, dil * B_GROUP_W), F32)] * 2,
        grid=(b, dil, sub // tq),
        in_specs=[cur, cur, prev, cur, prev, const(mq), const(mv)],
        out_specs=[out, out],
        scratch_shapes=[pltpu.VMEM((tq + Q_BLOCK, B_GROUP_W), BF16)] * 2,
        compiler_params=_cparams("parallel", "parallel", "arbitrary"),
        name=f"dilated_attention_d{dil}",
    )(view(bq), view(bk), view(bk), view(bv), view(bv), mq, mv)
    return o.reshape(b * sub, dil * B_GROUP_W), lse.reshape(b * sub, dil * B_GROUP_W)


_TK = 512


def _fold_rows(x, op):
    parts = [x[r:r + 8, :] for r in range(0, x.shape[0], 8)]
    while len(parts) > 1:
        parts = [op(parts[k], parts[k + 1]) for k in range(0, len(parts) - 1, 2)] + parts[len(parts) & ~1:]
    return parts[0]


def _skewed_tiles(n_tiles, produce, consume, buf_a, buf_b, carry):
    produce(0, buf_a)

    def pair(t, c):
        j = 2 * t
        produce(j + 1, buf_b)
        c = consume(j, buf_a, c)
        produce(j + 2, buf_a)
        return consume(j + 1, buf_b, c)

    n_pairs = (n_tiles - 1) // 2
    carry = lax.fori_loop(0, n_pairs, pair, carry)
    j = 2 * n_pairs

    def last_two(c):
        produce(j + 1, buf_b)
        return consume(j + 1, buf_b, consume(j, buf_a, c))

    return lax.cond(n_tiles - j == 2, last_two, lambda c: consume(j, buf_a, c), carry)


def _dsa_kernel(qa_ref, iq_ref, iw_ref, ikt_ref, ka_ref, ckvt_ref, wuk_ref, wuv_ref, mar_ref, man_ref, miq_ref,
                out_ref, key_ref, iqs_ref, qcat_ref, acc_ref, sa_ref, sb_ref, ma_ref, mb_ref, p_ref, *, n_sel):
    i = pl.program_id(1)
    q0 = i * Q_BLOCK
    n_keys = q0 + Q_BLOCK
    rows = A_HEADS * Q_BLOCK

    n_tiles = (n_keys + _TK - 1) // _TK
    tq_lane = q0 + lax.broadcasted_iota(jnp.int32, (_TK, Q_BLOCK), 1)
    krow = lax.broadcasted_iota(jnp.int32, (_TK, Q_BLOCK), 0)

    iq = iq_ref[...]
    for h in range(IDX_HEADS):
        iqs_ref[:, h * Q_BLOCK:(h + 1) * Q_BLOCK] = (iq * miq_ref[h:h + 1, :]).astype(F32).T.astype(BF16)
    iw_t = iw_ref[...].T

    def score_matmul(j, buf):
        k0 = pl.multiple_of(j * _TK, _TK)
        s = jnp.dot(ikt_ref[pl.ds(k0, _TK), :], iqs_ref[...], preferred_element_type=F32)
        for h in range(IDX_HEADS):
            buf[0][h] = s[:, h * Q_BLOCK:(h + 1) * Q_BLOCK]

    def score_keys(j, buf, carry):
        k0 = pl.multiple_of(j * _TK, _TK)
        sc = jnp.zeros((_TK, Q_BLOCK), F32)
        for h in range(IDX_HEADS):
            sc = sc + jnp.maximum(buf[0][h], 0.0) * iw_t[h:h + 1, :]
        bits = lax.bitcast_convert_type(sc + 0.0, jnp.int32)
        okey = bits ^ ((bits >> 31) & jnp.int32(0x7FFFFFFF))
        key_ref[pl.ds(k0, _TK), :] = jnp.where(krow + k0 <= tq_lane, okey, jnp.int32(INT_MIN))
        return carry

    _skewed_tiles(n_tiles, score_matmul, score_keys, (sa_ref, ma_ref), (sb_ref, mb_ref), 0)

    def count_keys(pred):
        def body(j, cnt):
            k0 = pl.multiple_of(j * _TK, _TK)
            hit = jnp.where(pred(key_ref[pl.ds(k0, _TK), :], k0), 1.0, 0.0)
            return cnt + _fold_rows(hit, jnp.add)
        cnt8 = lax.fori_loop(0, n_tiles, body, jnp.zeros((8, Q_BLOCK), F32))
        return jnp.sum(cnt8, axis=0, keepdims=True)

    def bit_step(b, prefix):
        trial = prefix | (jnp.int32(1) << (31 - b))
        t = trial ^ jnp.int32(INT_MIN)
        cnt = count_keys(lambda keys, k0: keys >= t)
        return jnp.where(cnt >= float(n_sel), trial, prefix)

    prefix = lax.fori_loop(0, 32, bit_step, jnp.zeros((1, Q_BLOCK), jnp.int32))
    thr = prefix ^ jnp.int32(INT_MIN)

    surplus = (count_keys(lambda keys, k0: keys >= thr) > float(n_sel)) & (thr != jnp.int32(INT_MIN))

    @pl.when(jnp.max(jnp.where(surplus, 1.0, 0.0)) > 0.0)
    def _():
        need = float(n_sel) - count_keys(lambda keys, k0: keys > thr)

        def index_bit(b, bound):
            trial = bound | (jnp.int32(1) << (30 - b))
            below = count_keys(lambda keys, k0: (keys == thr) & (krow + k0 < trial))
            return jnp.where(below < need, trial, bound)

        last = lax.fori_loop(0, 31, index_bit, jnp.zeros((1, Q_BLOCK), jnp.int32))

        def demote(j, carry):
            k0 = pl.multiple_of(j * _TK, _TK)
            keys = key_ref[pl.ds(k0, _TK), :]
            key_ref[pl.ds(k0, _TK), :] = jnp.where((keys == thr) & (krow + k0 > last), jnp.int32(INT_MIN), keys)
            return carry

        lax.fori_loop(0, n_tiles, demote, 0)

    q_rope = qa_ref[:, 0:_W_QAR]
    q_nope = qa_ref[:, _W_QAR:]
    scale = HEAD_DIM ** -0.5
    c_rope = A_KV_RANK
    for h in range(A_HEADS):
        q_lat = jnp.dot(q_nope * man_ref[h:h + 1, :], wuk_ref[...], preferred_element_type=F32)
        cols = slice(h * Q_BLOCK, (h + 1) * Q_BLOCK)
        qcat_ref[0:c_rope, cols] = (q_lat.astype(BF16) * scale).astype(F32).T.astype(BF16)
        qcat_ref[c_rope:, cols] = (q_rope * mar_ref[h:h + 1, :] * scale).astype(F32).T.astype(BF16)

    acc_ref[...] = jnp.zeros(acc_ref.shape, F32)

    def logit_matmul(j, buf):
        s_buf, mx_buf = buf
        k0 = pl.multiple_of(j * _TK, _TK)
        sel = (key_ref[pl.ds(k0, _TK), :] >= thr) & (krow + k0 <= tq_lane)
        bias = jnp.where(sel, 0.0, NEG_BIG).astype(F32)
        s = jnp.dot(ka_ref[pl.ds(k0, _TK), :], qcat_ref[...], preferred_element_type=F32)
        for h in range(A_HEADS):
            sh = s[:, h * Q_BLOCK:(h + 1) * Q_BLOCK] + bias
            s_buf[h] = sh
            mx_buf[:, h * Q_BLOCK:(h + 1) * Q_BLOCK] = _fold_rows(sh, jnp.maximum)

    def softmax_pv(j, buf, carry):
        s_buf, mx_buf = buf
        m_old, l_old = carry
        k0 = pl.multiple_of(j * _TK, _TK)
        ckv_t = ckvt_ref[:, pl.ds(k0, _TK)]
        m_parts, l_parts = [], []
        for c in range(A_HEADS // 2):
            a_parts = []
            for h in (2 * c, 2 * c + 1):
                cols = slice(h * Q_BLOCK, (h + 1) * Q_BLOCK)
                m_h = jnp.maximum(m_old[:, cols], jnp.max(mx_buf[:, cols], axis=0, keepdims=True))
                a_h = jnp.exp(m_old[:, cols] - m_h)
                p = jnp.exp(s_buf[h] - m_h)
                p_ref[c, :, (h % 2) * Q_BLOCK:(h % 2 + 1) * Q_BLOCK] = p.astype(BF16)
                l_parts.append(a_h * l_old[:, cols] + jnp.sum(_fold_rows(p, jnp.add), axis=0, keepdims=True))
                m_parts.append(m_h)
                a_parts.append(a_h)
            pv = jnp.dot(ckv_t, p_ref[c], preferred_element_type=F32)
            acc_ref[c] = acc_ref[c] * jnp.concatenate(a_parts, axis=1) + pv
        return jnp.concatenate(m_parts, axis=1), jnp.concatenate(l_parts, axis=1)

    _, l_fin = _skewed_tiles(n_tiles, logit_matmul, softmax_pv, (sa_ref, ma_ref), (sb_ref, mb_ref),
                             (jnp.full((1, rows), NEG_BIG, F32), jnp.zeros((1, rows), F32)))

    inv_l = 1.0 / l_fin
    out_t = jnp.zeros((A_HEADS * HEAD_DIM, Q_BLOCK), F32)
    for h in range(A_HEADS):
        lanes = slice((h % 2) * Q_BLOCK, (h % 2 + 1) * Q_BLOCK)
        o_lat = (acc_ref[h // 2][:, lanes] * inv_l[:, h * Q_BLOCK:(h + 1) * Q_BLOCK]).astype(BF16)
        out_t = out_t + jnp.dot(wuv_ref[h], o_lat, preferred_element_type=F32)
    out_ref[...] = out_t.T.astype(BF16)


def _dsa_mixer(qa, iq, iw, ikt, ka, ckvt, wuk, wuv_t, m_ar, m_an, m_iq, b, s):
    n_sel = min(A_TOPK_MAX, s // 4)
    nq = s // Q_BLOCK
    rows = A_HEADS * Q_BLOCK
    blk = lambda width: pl.BlockSpec((Q_BLOCK, width), lambda bi, i: (bi * nq + i, 0))
    seq = lambda width: pl.BlockSpec((s, width), lambda bi, i: (bi, 0))
    const = lambda a: pl.BlockSpec(a.shape, lambda bi, i: (0,) * a.ndim)
    return pl.pallas_call(
        functools.partial(_dsa_kernel, n_sel=n_sel),
        out_shape=jax.ShapeDtypeStruct((b * s, A_HEADS * HEAD_DIM), BF16),
        grid=(b, nq),
        in_specs=[blk(_W_QAR + _W_QAN), blk(_W_IQ), blk(_W_IW), seq(_W_IK), seq(_W_CKV + _W_AKR),
                  pl.BlockSpec((A_KV_RANK, s), lambda bi, i: (0, bi)),
                  const(wuk), const(wuv_t), const(m_ar), const(m_an), const(m_iq)],
        out_specs=blk(A_HEADS * HEAD_DIM),
        scratch_shapes=[pltpu.VMEM((s, Q_BLOCK), jnp.int32),
                        pltpu.VMEM((_W_IQ, rows), BF16),
                        pltpu.VMEM((A_KV_RANK + _W_AKR, rows), BF16),
                        pltpu.VMEM((A_HEADS // 2, A_KV_RANK, 2 * Q_BLOCK), F32),
                        pltpu.VMEM((A_HEADS, _TK, Q_BLOCK), F32),
                        pltpu.VMEM((A_HEADS, _TK, Q_BLOCK), F32),
                        pltpu.VMEM((8, rows), F32),
                        pltpu.VMEM((8, rows), F32),
                        pltpu.VMEM((A_HEADS // 2, _TK, 2 * Q_BLOCK), BF16)],
        compiler_params=_cparams("parallel", "arbitrary"),
        name="dsa_attention",
    )(qa, iq, iw, ikt, ka, ckvt, wuk, wuv_t, m_ar, m_an, m_iq)


def _merge_kernel(x_ref, a_ref, o1_ref, o2_ref, o3_ref, l1_ref, l2_ref, l3_ref, g_ref,
                  wa_ref, wb_ref, wo_ref, lg_ref, lb_ref, h_ref, *scr):
    dils = [dil for _, dil in B_PATTERNS]
    lses = [_load_token_major(r, scr[2 * g], dils[g]) for g, r in enumerate((l1_ref, l2_ref, l3_ref))]
    outs = [_load_token_major(r, scr[2 * g + 1], dils[g]) for g, r in enumerate((o1_ref, o2_ref, o3_ref))]
    mx = jnp.maximum(jnp.maximum(lses[0], lses[1]), lses[2])
    es = [jnp.exp(l - mx) for l in lses]
    den = es[0] + es[1] + es[2]
    b_out = (es[0] / den) * outs[0] + (es[1] / den) * outs[1] + (es[2] / den) * outs[2]
    ya = jnp.dot(a_ref[...], wa_ref[...], preferred_element_type=F32)
    yb = jnp.dot(b_out.astype(BF16), wb_ref[...], preferred_element_type=F32)
    pre = g_ref[:, 0:D_MODEL] * ya + g_ref[:, D_MODEL:] * yb
    mix = jnp.dot(pre.astype(BF16), wo_ref[...], preferred_element_type=F32)
    h_ref[...] = _layer_norm(ALPHA * x_ref[...] + mix, lg_ref[...], lb_ref[...])


def _merge(xf, a_out, b_parts, gates, wa, wb, wo, ln_g, ln_b, tile):
    n = xf.shape[0]
    row = lambda width: pl.BlockSpec((tile, width), lambda i: (i, 0))
    full = lambda a: pl.BlockSpec(a.shape, lambda i: (0,) * a.ndim)
    (o1, l1), (o2, l2), (o3, l3) = b_parts
    grp = [pl.BlockSpec((tile // dil, dil * B_GROUP_W), lambda i: (i, 0)) for _, dil in B_PATTERNS]
    return pl.pallas_call(
        _merge_kernel,
        out_shape=jax.ShapeDtypeStruct((n, D_MODEL), F32),
        grid=(n // tile,),
        in_specs=[row(D_MODEL), row(A_HEADS * HEAD_DIM)] + grp + grp + [row(2 * D_MODEL),
                  full(wa), full(wb), full(wo), full(ln_g), full(ln_b)],
        out_specs=row(D_MODEL),
        scratch_shapes=[pltpu.VMEM((B_GROUP_W // LANES, tile, LANES), F32)] * (2 * len(B_PATTERNS)),
        compiler_params=_cparams("parallel"),
        name="merge_output_projection",
    )(xf, a_out, o1, o2, o3, l1, l2, l3, gates, wa, wb, wo, ln_g, ln_b)


def _first_max(v):
    m = jnp.max(v, axis=0, keepdims=True)
    idx = lax.broadcasted_iota(jnp.int32, v.shape, 0)
    first = jnp.min(jnp.where(v == m, idx, v.shape[0]), axis=0, keepdims=True)
    return m, idx == first


def _router_kernel(h_ref, wr_ref, rb_ref, gate_ref):
    t = h_ref.shape[0]
    gs = N_EXPERTS // N_GROUPS
    logits = lax.dot_general(wr_ref[...], h_ref[...], (((1,), (1,)), ((), ())),
                             precision=lax.Precision.HIGHEST, preferred_element_type=F32)
    scores = jax.nn.sigmoid(logits)
    biased = scores + rb_ref[...]
    gscores = []
    for g in range(N_GROUPS):
        blk = biased[g * gs:(g + 1) * gs, :]
        m1, hit = _first_max(blk)
        m2 = jnp.max(jnp.where(hit, -jnp.inf, blk), axis=0, keepdims=True)
        gscores.append(m1 + m2)
    gscore = jnp.concatenate(gscores, axis=0)
    gsel = jnp.zeros((N_GROUPS, t), F32)
    for _ in range(TOPK_GROUPS):
        _, hit = _first_max(gscore)
        gsel = jnp.where(hit, 1.0, gsel)
        gscore = jnp.where(hit, -jnp.inf, gscore)
    esel = jnp.concatenate([jnp.broadcast_to(gsel[g:g + 1, :], (gs, t)) for g in range(N_GROUPS)], axis=0)
    cand = jnp.where(esel > 0.0, biased, -jnp.inf)
    top_s = jnp.zeros((N_EXPERTS, t), F32)
    for _ in range(TOP_K):
        _, hit = _first_max(cand)
        top_s = jnp.where(hit, scores, top_s)
        cand = jnp.where(hit, -jnp.inf, cand)
    gate = top_s / jnp.sum(top_s, axis=0, keepdims=True) * ROUTED_SCALE
    gate_ref[...] = jnp.concatenate([gate, jnp.zeros((LANES - N_EXPERTS, t), F32)], axis=0).T


def _router(h, wr_t, rb_col, tile):
    n = h.shape[0]
    return pl.pallas_call(
        _router_kernel,
        out_shape=jax.ShapeDtypeStruct((n, LANES), F32),
        grid=(n // tile,),
        in_specs=[pl.BlockSpec((tile, D_MODEL), lambda i: (i, 0)),
                  pl.BlockSpec(wr_t.shape, lambda i: (0, 0)),
                  pl.BlockSpec(rb_col.shape, lambda i: (0, 0))],
        out_specs=pl.BlockSpec((tile, LANES), lambda i: (i, 0)),
        compiler_params=_cparams("parallel"),
        name="moe_router",
    )(h, wr_t, rb_col)


_EXPERTS_PER_STEP = 2


def _swiglu(hb, w1, w3, w2):
    a = jnp.dot(hb, w1, preferred_element_type=F32)
    b = jnp.dot(hb, w3, preferred_element_type=F32)
    return jnp.dot((jax.nn.silu(a) * b).astype(BF16), w2, preferred_element_type=F32)


def _experts_kernel(h_ref, gate_ref, w1_ref, w3_ref, w2_ref, s1_ref, s3_ref, s2_ref, lg_ref, lb_ref,
                    y_ref, hb_ref, acc_ref):
    step = pl.program_id(1)

    @pl.when(step == 0)
    def _():
        hb = h_ref[...].astype(BF16)
        hb_ref[...] = hb
        acc_ref[...] = _swiglu(hb, s1_ref[...], s3_ref[...], s2_ref[...])

    hb = hb_ref[...]
    gate = gate_ref[...]
    lane = lax.broadcasted_iota(jnp.int32, gate.shape, 1)
    for k in range(_EXPERTS_PER_STEP):
        e = step * _EXPERTS_PER_STEP + k
        g_col = jnp.sum(jnp.where(lane == e, gate, 0.0), axis=1, keepdims=True)
        acc_ref[...] += _swiglu(hb, w1_ref[k], w3_ref[k], w2_ref[k]) * g_col

    @pl.when(step == pl.num_programs(1) - 1)
    def _():
        y_ref[...] = _layer_norm(ALPHA * h_ref[...] + acc_ref[...], lg_ref[...], lb_ref[...])


def _experts(h, gate, w1, w3, w2, s1, s3, s2, ln_g, ln_b, tile):
    n = h.shape[0]
    ec = _EXPERTS_PER_STEP
    row = lambda width: pl.BlockSpec((tile, width), lambda i, e: (i, 0))
    full = lambda a: pl.BlockSpec(a.shape, lambda i, e: (0,) * a.ndim)
    wspec = lambda a: pl.BlockSpec((ec,) + a.shape[1:], lambda i, e: (e, 0, 0))
    return pl.pallas_call(
        _experts_kernel,
        out_shape=jax.ShapeDtypeStruct((n, D_MODEL), F32),
        grid=(n // tile, N_EXPERTS // ec),
        in_specs=[row(D_MODEL), row(LANES), wspec(w1), wspec(w3), wspec(w2),
                  full(s1), full(s3), full(s2), full(ln_g), full(ln_b)],
        out_specs=row(D_MODEL),
        scratch_shapes=[pltpu.VMEM((tile, D_MODEL), BF16), pltpu.VMEM((tile, D_MODEL), F32)],
        compiler_params=_cparams("parallel", "arbitrary"),
        name="moe_experts",
    )(h, gate, w1, w3, w2, s1, s3, s2, ln_g, ln_b)


def _rope_freqs():
    inv16 = ROPE_THETA ** (-jnp.arange(0, ROPE_DIM, 2, dtype=F32) / ROPE_DIM)
    inv8 = ROPE_THETA ** (-jnp.arange(0, IDX_ROPE, 2, dtype=F32) / IDX_ROPE)

    def row(inv, fidx):
        return jnp.where(fidx >= 0, inv[np.maximum(fidx, 0)], 0.0)
    lay = _LAYOUT
    freq = jnp.stack([row(inv16, lay["fa"]), row(inv8, lay["fi"]), row(inv16, lay["fb"])])
    sign = jnp.asarray(np.stack([lay["sa"], lay["si"], lay["sb"]]))
    return freq.astype(F32), sign


def _layer(x, positions, w_in, b_gate, g_kv, w_uk, w_uv, w_branch_a, w_branch_b, w_o, ln1_g, ln1_b,
           w_router, router_bias, w1_e, w3_e, w2_e, ws1, ws3, ws2, ln2_g, ln2_b):
    b, s, d = x.shape
    n = b * s
    lay = _LAYOUT
    tile = min(256, n)
    xf = x.reshape(n, d)

    w_perm = (w_in[:, lay["cols"]] * lay["keep"]).astype(BF16)
    wuk = jnp.transpose(w_uk, (1, 2, 0)).reshape(A_HEADS * A_NOPE, A_KV_RANK).astype(BF16)
    wuv_t = jnp.zeros((A_HEADS, A_HEADS, HEAD_DIM, A_KV_RANK), F32)
    wuv_t = wuv_t.at[jnp.arange(A_HEADS), jnp.arange(A_HEADS)].set(jnp.transpose(w_uv, (1, 2, 0)))
    wuv_t = wuv_t.reshape(A_HEADS, A_HEADS * HEAD_DIM, A_KV_RANK).astype(BF16)
    masks = {k: jnp.asarray(lay[k], BF16) for k in ("m_ar", "m_an", "m_iq", "m_bq")}
    m_bv = jnp.asarray(lay["m_bv"], F32)

    freq, sign = _rope_freqs()
    tables = _rope_tables(positions.reshape(n, 1), freq, sign, tile)
    qa, ka, ckvt, iq, ikt, iw, gates, *bqkv = _input_projection(
        xf, w_perm, tables, g_kv.reshape(1, -1), b_gate.reshape(1, -1), tile)

    a_out = _dsa_mixer(qa, iq, iw, ikt, ka, ckvt, wuk, wuv_t, masks["m_ar"], masks["m_an"], masks["m_iq"], b, s)
    ng = len(B_PATTERNS)
    b_parts = [_dilated_group(bqkv[g], bqkv[ng + g], bqkv[2 * ng + g], masks["m_bq"], m_bv, b, dil)
               for g, (_, dil) in enumerate(B_PATTERNS)]

    h = _merge(xf, a_out, b_parts, gates, w_branch_a.astype(BF16), w_branch_b.astype(BF16),
               w_o.astype(BF16), ln1_g.reshape(1, -1), ln1_b.reshape(1, -1), tile)

    gate = _router(h, w_router.T, router_bias.reshape(-1, 1), tile)
    y = _experts(h, gate, w1_e.astype(BF16), w3_e.astype(BF16), w2_e.astype(BF16),
                 ws1.astype(BF16), ws3.astype(BF16), ws2.astype(BF16),
                 ln2_g.reshape(1, -1), ln2_b.reshape(1, -1), min(1024, n))
    return y.reshape(b, s, d)


def kernel(x, positions, w_in, b_gate, g_kv, w_uk, w_uv, w_branch_a, w_branch_b, w_o, ln1_g, ln1_b,
           w_router, router_bias, w1_e, w3_e, w2_e, ws1, ws3, ws2, ln2_g, ln2_b):
    h = x
    for l in range(DEPTH):
        h = _layer(h, positions, w_in[l], b_gate[l], g_kv[l], w_uk[l], w_uv[l], w_branch_a[l],
                   w_branch_b[l], w_o[l], ln1_g[l], ln1_b[l], w_router[l], router_bias[l],
                   w1_e[l], w3_e[l], w2_e[l], ws1[l], ws3[l], ws2[l], ln2_g[l], ln2_b[l])
    return h
```

```python
import functools

import jax
import jax.numpy as jnp
import numpy as np
from jax import lax
from jax.experimental import pallas as pl
from jax.experimental.pallas import tpu as pltpu

F32 = jnp.float32
BF16 = jnp.bfloat16

D_MODEL = 1024
HEAD_DIM = 64
ROPE_DIM = 16
ROPE_THETA = 500000.0
Q_BLOCK = 128
A_HEADS = 8
A_NOPE = HEAD_DIM - ROPE_DIM
A_KV_RANK = 256
A_TOPK_MAX = 256
IDX_HEADS = 8
IDX_DIM = 32
IDX_ROPE = 8
B_PATTERNS = ((128, 1), (512, 4), (2048, 16))
B_GROUP_HEADS = 4
B_HEADS = B_GROUP_HEADS * len(B_PATTERNS)
B_GROUP_W = B_GROUP_HEADS * HEAD_DIM
N_EXPERTS = 64
TOP_K = 8
N_GROUPS = 8
TOPK_GROUPS = 4
D_EXPERT = 256
ROUTED_SCALE = 2.5
DEPTH = 1
ALPHA = (2.0 * DEPTH) ** 0.25
LN_EPS = 1e-5
RMS_EPS = 1e-6

LANES = 128
VMEM_LIMIT = 56 * 1024 * 1024
NEG_BIG = -1e30
INT_MIN = -(2 ** 31)

_OFF_AQ = 0
_OFF_CKV = _OFF_AQ + A_HEADS * HEAD_DIM
_OFF_AKR = _OFF_CKV + A_KV_RANK
_OFF_IQ = _OFF_AKR + ROPE_DIM
_OFF_IK = _OFF_IQ + IDX_HEADS * IDX_DIM
_OFF_IW = _OFF_IK + IDX_DIM
_OFF_BQ = _OFF_IW + IDX_HEADS
_OFF_BK = _OFF_BQ + B_HEADS * HEAD_DIM
_OFF_BV = _OFF_BK + B_HEADS * HEAD_DIM
_OFF_GA = _OFF_BV + B_HEADS * HEAD_DIM
_OFF_GB = _OFF_GA + D_MODEL
_IN_TOTAL = _OFF_GB + D_MODEL

_W_QAR, _W_QAN, _W_CKV, _W_AKR = 128, A_HEADS * A_NOPE, A_KV_RANK, 128
_W_IQ, _W_IK, _W_IW = 256, 256, 128
_W_B = B_HEADS * HEAD_DIM
_C_QAR = 0
_C_QAN = _C_QAR + _W_QAR
_C_CKV = _C_QAN + _W_QAN
_C_AKR = _C_CKV + _W_CKV
_C_IQ = _C_AKR + _W_AKR
_C_IK = _C_IQ + _W_IQ
_C_IW = _C_IK + _W_IK
_C_BQ = _C_IW + _W_IW
_C_BK = _C_BQ + _W_B
_C_BV = _C_BK + _W_B
_C_GA = _C_BV + _W_B
_C_GB = _C_GA + D_MODEL
_P_TOTAL = _C_GB + D_MODEL


def _idx_lane(l):
    if l < 16:
        return l // 4, l % 4
    if l < 64:
        return (l - 16) // 12, 8 + (l - 16) % 12
    if l < 80:
        return (l - 64) // 4, 4 + (l - 64) % 4
    return (l - 80) // 12, 20 + (l - 80) % 12


def _b_lane(l):
    half, r = l // 64, l % 64
    which, rr = r // 32, r % 32
    if rr < 8:
        return which, half * 8 + rr
    return which, 16 + half * 24 + (rr - 8)


def _build_layout():
    cols = np.zeros((_P_TOTAL,), np.int32)
    keep = np.ones((_P_TOTAL,), np.float32)
    for l in range(128):
        half, h, f = l // 64, (l % 64) // 8, l % 8
        cols[_C_QAR + l] = _OFF_AQ + h * HEAD_DIM + half * 8 + f
        cols[_C_AKR + l] = _OFF_AKR + half * 8 + f
    for h in range(A_HEADS):
        for j in range(A_NOPE):
            cols[_C_QAN + h * A_NOPE + j] = _OFF_AQ + h * HEAD_DIM + ROPE_DIM + j
    cols[_C_CKV:_C_CKV + _W_CKV] = _OFF_CKV + np.arange(_W_CKV)
    for sl in range(2):
        for l in range(128):
            hh, d = _idx_lane(l)
            cols[_C_IQ + sl * 128 + l] = _OFF_IQ + (sl * 4 + hh) * IDX_DIM + d
            cols[_C_IK + sl * 128 + l] = _OFF_IK + d
    cols[_C_IW:_C_IW + IDX_HEADS] = _OFF_IW + np.arange(IDX_HEADS)
    keep[_C_IW + IDX_HEADS:_C_IW + _W_IW] = 0.0
    for p in range(B_HEADS // 2):
        for l in range(128):
            which, d = _b_lane(l)
            h = 2 * p + which
            cols[_C_BQ + p * 128 + l] = _OFF_BQ + h * HEAD_DIM + d
            cols[_C_BK + p * 128 + l] = _OFF_BK + h * HEAD_DIM + d
    cols[_C_BV:_C_BV + _W_B] = _OFF_BV + np.arange(_W_B)
    cols[_C_GA:_C_GA + D_MODEL] = _OFF_GA + np.arange(D_MODEL)
    cols[_C_GB:_C_GB + D_MODEL] = _OFF_GB + np.arange(D_MODEL)

    fa = np.array([l % 8 for l in range(128)])
    sa = np.array([-1.0 if l < 64 else 1.0 for l in range(128)], np.float32)
    fi = np.full((128,), -1)
    si = np.zeros((128,), np.float32)
    fb = np.full((128,), -1)
    sb = np.zeros((128,), np.float32)
    for l in range(128):
        if l < 16 or 64 <= l < 80:
            fi[l] = l % 4
            si[l] = -1.0 if l < 64 else 1.0
        if l % 32 < 8:
            fb[l] = l % 32
            sb[l] = -1.0 if l < 64 else 1.0

    m_ar = np.zeros((A_HEADS, 128), np.float32)
    for l in range(128):
        m_ar[(l % 64) // 8, l] = 1.0
    m_an = np.zeros((A_HEADS, _W_QAN), np.float32)
    for h in range(A_HEADS):
        m_an[h, h * A_NOPE:(h + 1) * A_NOPE] = 1.0
    m_iq = np.zeros((IDX_HEADS, _W_IQ), np.float32)
    for sl in range(2):
        for l in range(128):
            m_iq[sl * 4 + _idx_lane(l)[0], sl * 128 + l] = 1.0
    m_bq = np.zeros((B_GROUP_HEADS, B_GROUP_W), np.float32)
    for p in range(2):
        for l in range(128):
            m_bq[2 * p + _b_lane(l)[0], p * 128 + l] = 1.0
    m_bv = np.zeros((B_GROUP_HEADS, B_GROUP_W), np.float32)
    for j in range(B_GROUP_HEADS):
        m_bv[j, j * HEAD_DIM:(j + 1) * HEAD_DIM] = 1.0
    return dict(cols=cols, keep=keep, fa=fa, sa=sa, fi=fi, si=si, fb=fb, sb=sb,
                m_ar=m_ar, m_an=m_an, m_iq=m_iq, m_bq=m_bq, m_bv=m_bv)


_LAYOUT = _build_layout()


def _cparams(*sem):
    return pltpu.CompilerParams(dimension_semantics=sem, vmem_limit_bytes=VMEM_LIMIT)


def _layer_norm(v, g, b):
    mu = jnp.mean(v, axis=-1, keepdims=True)
    var = jnp.mean(jnp.square(v - mu), axis=-1, keepdims=True)
    return (v - mu) * lax.rsqrt(var + LN_EPS) * g + b


def _dot_nt(a, b):
    return lax.dot_general(a, b, (((1,), (1,)), ((), ())), preferred_element_type=F32)


def _rope_table_kernel(pos_ref, freq_ref, sign_ref, out_ref):
    pos = pos_ref[...].astype(F32)
    for k in range(3):
        ang = pos * freq_ref[k:k + 1, :]
        out_ref[:, (2 * k) * LANES:(2 * k + 1) * LANES] = jnp.cos(ang)
        out_ref[:, (2 * k + 1) * LANES:(2 * k + 2) * LANES] = jnp.sin(ang) * sign_ref[k:k + 1, :]


def _rope_tables(pos_col, freq, sign, tile):
    n = pos_col.shape[0]
    return pl.pallas_call(
        _rope_table_kernel,
        out_shape=jax.ShapeDtypeStruct((n, 6 * LANES), F32),
        grid=(n // tile,),
        in_specs=[pl.BlockSpec((tile, 1), lambda i: (i, 0)),
                  pl.BlockSpec((3, LANES), lambda i: (0, 0)),
                  pl.BlockSpec((3, LANES), lambda i: (0, 0))],
        out_specs=pl.BlockSpec((tile, 6 * LANES), lambda i: (i, 0)),
        compiler_params=_cparams("parallel"),
        name="rope_tables",
    )(pos_col, freq, sign)


def _rope_slabs(y, cos, sin):
    outs = []
    for s in range(y.shape[1] // LANES):
        ys = y[:, s * LANES:(s + 1) * LANES]
        outs.append(ys * cos + pltpu.roll(ys, 64, 1) * sin)
    return outs[0] if len(outs) == 1 else jnp.concatenate(outs, axis=1)


def _store_residue_major(out_ref, y, scr_ref, dil):
    if dil == 1:
        out_ref[...] = y.astype(out_ref.dtype)
        return
    rows, width = y.shape[0] // dil, y.shape[1]
    for c in range(width // LANES):
        scr_ref[c] = y[:, c * LANES:(c + 1) * LANES]
    for r in range(dil):
        for c in range(width // LANES):
            lanes = slice(r * width + c * LANES, r * width + (c + 1) * LANES)
            out_ref[:, lanes] = scr_ref[c, pl.ds(r, rows, stride=dil), :].astype(out_ref.dtype)


def _load_token_major(ref, scr_ref, dil):
    if dil == 1:
        return ref[...]
    rows, width = ref.shape[0], ref.shape[1] // dil
    for r in range(dil):
        for c in range(width // LANES):
            lanes = slice(r * width + c * LANES, r * width + (c + 1) * LANES)
            scr_ref[c, pl.ds(r, rows, stride=dil), :] = ref[:, lanes]
    return jnp.concatenate([scr_ref[c] for c in range(width // LANES)], axis=1)


def _proj_kernel(x_ref, w_ref, tab_ref, gkv_ref,
                 qa_ref, ka_ref, ckvt_ref, iq_ref, ik_ref, iw_ref, *rest):
    b_refs, scr_ref = rest[:-1], rest[-1]
    xb = x_ref[...].astype(BF16)

    def proj(c0, width):
        return jnp.dot(xb, w_ref[:, c0:c0 + width], preferred_element_type=F32)

    cos_a, sin_a = tab_ref[:, 0:128], tab_ref[:, 128:256]
    cos_i, sin_i = tab_ref[:, 256:384], tab_ref[:, 384:512]
    cos_b, sin_b = tab_ref[:, 512:640], tab_ref[:, 640:768]

    qa_ref[:, 0:_W_QAR] = _rope_slabs(proj(_C_QAR, _W_QAR), cos_a, sin_a).astype(BF16)
    qa_ref[:, _W_QAR:] = proj(_C_QAN, _W_QAN).astype(BF16)
    ckv = proj(_C_CKV, _W_CKV)
    ckv = ckv * lax.rsqrt(jnp.mean(jnp.square(ckv), axis=-1, keepdims=True) + RMS_EPS) * gkv_ref[...]
    ka_ref[:, 0:_W_CKV] = ckv.astype(BF16)
    ckvt_ref[...] = ckv.T.astype(BF16)
    ka_ref[:, _W_CKV:] = _rope_slabs(proj(_C_AKR, _W_AKR), cos_a, sin_a).astype(BF16)
    iq_ref[...] = _rope_slabs(proj(_C_IQ, _W_IQ), cos_i, sin_i).astype(BF16)
    ik_ref[...] = _rope_slabs(proj(_C_IK, _W_IK), cos_i, sin_i).astype(BF16)
    iw_ref[...] = proj(_C_IW, _W_IW) * ((IDX_HEADS * IDX_DIM) ** -0.5)
    ng = len(B_PATTERNS)
    for kind, c0 in enumerate((_C_BQ, _C_BK, _C_BV)):
        for g, (_, dil) in enumerate(B_PATTERNS):
            y = proj(c0 + g * B_GROUP_W, B_GROUP_W)
            if kind < 2:
                y = _rope_slabs(y, cos_b, sin_b)
            _store_residue_major(b_refs[kind * ng + g], y, scr_ref, dil)


def _input_projection(xf, w_perm, tables, g_kv, tile):
    n = xf.shape[0]
    row = lambda width: pl.BlockSpec((tile, width), lambda i: (i, 0))
    full = lambda a: pl.BlockSpec(a.shape, lambda i: (0,) * a.ndim)
    out_w = [(_W_QAR + _W_QAN, BF16), (_W_CKV + _W_AKR, BF16), None, (_W_IQ, BF16), (_W_IK, BF16),
             (_W_IW, F32)]
    shapes = [jax.ShapeDtypeStruct((A_KV_RANK, n), BF16) if o is None else jax.ShapeDtypeStruct((n, o[0]), o[1])
              for o in out_w]
    specs = [pl.BlockSpec((A_KV_RANK, tile), lambda i: (0, i)) if o is None else row(o[0]) for o in out_w]
    for _ in range(3):
        for _, dil in B_PATTERNS:
            shapes.append(jax.ShapeDtypeStruct((n // dil, dil * B_GROUP_W), BF16))
            specs.append(pl.BlockSpec((tile // dil, dil * B_GROUP_W), lambda i: (i, 0)))
    return pl.pallas_call(
        _proj_kernel,
        out_shape=shapes,
        grid=(n // tile,),
        in_specs=[row(D_MODEL), full(w_perm), row(6 * LANES), full(g_kv)],
        out_specs=specs,
        scratch_shapes=[pltpu.VMEM((B_GROUP_W // LANES, tile, LANES), F32)],
        compiler_params=_cparams("parallel"),
        name="input_projection",
    )(xf, w_perm, tables, g_kv)


def _dilated_kernel(q_ref, kc_ref, kp_ref, vc_ref, vp_ref, mq_ref, mv_ref, o_ref, lse_ref,
                    kwin_ref, vwin_ref, *, tq):
    first = pl.program_id(2) == 0
    kwin_ref[0:Q_BLOCK, :] = kp_ref[...]
    kwin_ref[Q_BLOCK:, :] = kc_ref[...]
    vwin_ref[0:Q_BLOCK, :] = vp_ref[...]
    vwin_ref[Q_BLOCK:, :] = vc_ref[...]
    t = lax.broadcasted_iota(jnp.int32, (Q_BLOCK, 2 * Q_BLOCK), 0)
    c = lax.broadcasted_iota(jnp.int32, (Q_BLOCK, 2 * Q_BLOCK), 1)
    diff = t + Q_BLOCK - c
    band = (diff >= 0) & (diff <= Q_BLOCK)
    scale = HEAD_DIM ** -0.5
    for sb in range(tq // Q_BLOCK):
        valid = band
        if sb == 0:
            valid = band & (c >= jnp.where(first, Q_BLOCK, 0))
        bias = jnp.where(valid, 0.0, NEG_BIG).astype(F32)
        q = q_ref[sb * Q_BLOCK:(sb + 1) * Q_BLOCK, :] * scale
        kw = kwin_ref[sb * Q_BLOCK:(sb + 2) * Q_BLOCK, :]
        vw = vwin_ref[sb * Q_BLOCK:(sb + 2) * Q_BLOCK, :]
        qs = jnp.concatenate([q * mq_ref[j:j + 1, :] for j in range(B_GROUP_HEADS)], axis=0)
        s = _dot_nt(qs, kw)
        o_acc = jnp.zeros((Q_BLOCK, B_GROUP_W), F32)
        lse_acc = jnp.zeros((Q_BLOCK, B_GROUP_W), F32)
        for j in range(B_GROUP_HEADS):
            sj = s[j * Q_BLOCK:(j + 1) * Q_BLOCK, :] + bias
            m = jnp.max(sj, axis=-1, keepdims=True)
            e = jnp.exp(sj - m)
            den = jnp.sum(e, axis=-1, keepdims=True)
            pv = jnp.dot(e.astype(BF16), vw, preferred_element_type=F32)
            mv = mv_ref[j:j + 1, :]
            o_acc = o_acc + (pv / den) * mv
            lse_acc = lse_acc + (m + jnp.log(den)) * mv
        o_ref[sb * Q_BLOCK:(sb + 1) * Q_BLOCK, :] = o_acc
        lse_ref[sb * Q_BLOCK:(sb + 1) * Q_BLOCK, :] = lse_acc


def _dilated_group(bq, bk, bv, mq, mv, b, dil):
    sub = bq.shape[0] // b
    tq = min(512, sub)
    nblk = tq // Q_BLOCK
    view = lambda a: a.reshape(b, sub, dil * B_GROUP_W)
    cur = pl.BlockSpec((None, tq, B_GROUP_W), lambda bi, r, i: (bi, i, r))
    prev = pl.BlockSpec((None, Q_BLOCK, B_GROUP_W),
                        lambda bi, r, i: (bi, jnp.maximum(i * nblk - 1, 0), r))
    const = lambda a: pl.BlockSpec(a.shape, lambda bi, r, i: (0, 0))
    out = cur
    o, lse = pl.pallas_call(
        functools.partial(_dilated_kernel, tq=tq),
        out_shape=[jax.ShapeDtypeStruct((b, sub, dil * B_GROUP_W), F32)] * 2,
        grid=(b, dil, sub // tq),
        in_specs=[cur, cur, prev, cur, prev, const(mq), const(mv)],
        out_specs=[out, out],
        scratch_shapes=[pltpu.VMEM((tq + Q_BLOCK, B_GROUP_W), BF16)] * 2,
        compiler_params=_cparams("parallel", "parallel", "arbitrary"),
        name=f"dilated_attention_d{dil}",
    )(view(bq), view(bk), view(bk), view(bv), view(bv), mq, mv)
    return o.reshape(b * sub, dil * B_GROUP_W), lse.reshape(b * sub, dil * B_GROUP_W)


_TK = 512


def _fold_rows(x, op):
    parts = [x[r:r + 8, :] for r in range(0, x.shape[0], 8)]
    while len(parts) > 1:
        parts = [op(parts[k], parts[k + 1]) for k in range(0, len(parts) - 1, 2)] + parts[len(parts) & ~1:]
    return parts[0]


def _skewed_tiles(n_tiles, produce, consume, buf_a, buf_b, carry):
    produce(0, buf_a)

    def pair(t, c):
        j = 2 * t
        produce(j + 1, buf_b)
        c = consume(j, buf_a, c)
        produce(j + 2, buf_a)
        return consume(j + 1, buf_b, c)

    n_pairs = (n_tiles - 1) // 2
    carry = lax.fori_loop(0, n_pairs, pair, carry)
    j = 2 * n_pairs

    def last_two(c):
        produce(j + 1, buf_b)
        return consume(j + 1, buf_b, consume(j, buf_a, c))

    return lax.cond(n_tiles - j == 2, last_two, lambda c: consume(j, buf_a, c), carry)


def _dsa_kernel(qa_ref, iq_ref, iw_ref, ikt_ref, ka_ref, ckvt_ref, wuk_ref, wuv_ref, mar_ref, man_ref, miq_ref,
                out_ref, key_ref, tie_ref, iqs_ref, qcat_ref, acc_ref, sa_ref, sb_ref, ma_ref, mb_ref, p_ref,
                *, n_sel):
    i = pl.program_id(1)
    q0 = i * Q_BLOCK
    n_keys = q0 + Q_BLOCK
    rows = A_HEADS * Q_BLOCK

    n_tiles = (n_keys + _TK - 1) // _TK
    tq_lane = q0 + lax.broadcasted_iota(jnp.int32, (_TK, Q_BLOCK), 1)
    krow = lax.broadcasted_iota(jnp.int32, (_TK, Q_BLOCK), 0)

    iq = iq_ref[...]
    for h in range(IDX_HEADS):
        iqs_ref[:, h * Q_BLOCK:(h + 1) * Q_BLOCK] = (iq * miq_ref[h:h + 1, :]).astype(F32).T.astype(BF16)
    iw_t = iw_ref[...].T

    def score_matmul(j, buf):
        k0 = pl.multiple_of(j * _TK, _TK)
        s = jnp.dot(ikt_ref[pl.ds(k0, _TK), :], iqs_ref[...], preferred_element_type=F32)
        for h in range(IDX_HEADS):
            buf[0][h] = s[:, h * Q_BLOCK:(h + 1) * Q_BLOCK]

    def score_keys(j, buf, carry):
        k0 = pl.multiple_of(j * _TK, _TK)
        sc = jnp.zeros((_TK, Q_BLOCK), F32)
        for h in range(IDX_HEADS):
            sc = sc + jnp.maximum(buf[0][h], 0.0) * iw_t[h:h + 1, :]
        bits = lax.bitcast_convert_type(sc + 0.0, jnp.int32)
        okey = bits ^ ((bits >> 31) & jnp.int32(0x7FFFFFFF))
        key_ref[pl.ds(k0, _TK), :] = jnp.where(krow + k0 <= tq_lane, okey, jnp.int32(INT_MIN))
        return carry

    _skewed_tiles(n_tiles, score_matmul, score_keys, (sa_ref, ma_ref), (sb_ref, mb_ref), 0)

    def count_keys(pred):
        def body(j, cnt):
            k0 = pl.multiple_of(j * _TK, _TK)
            hit = jnp.where(pred(key_ref[pl.ds(k0, _TK), :], k0), 1.0, 0.0)
            return cnt + _fold_rows(hit, jnp.add)
        cnt8 = lax.fori_loop(0, n_tiles, body, jnp.zeros((8, Q_BLOCK), F32))
        return jnp.sum(cnt8, axis=0, keepdims=True)

    def bit_step(b, prefix):
        trial = prefix | (jnp.int32(1) << (31 - b))
        t = trial ^ jnp.int32(INT_MIN)
        cnt = count_keys(lambda keys, k0: keys >= t)
        return jnp.where(cnt >= float(n_sel), trial, prefix)

    prefix = lax.fori_loop(0, 32, bit_step, jnp.zeros((1, Q_BLOCK), jnp.int32))
    thr = prefix ^ jnp.int32(INT_MIN)

    surplus = (count_keys(lambda keys, k0: keys >= thr) > float(n_sel)) & (thr != jnp.int32(INT_MIN))

    @pl.when(jnp.max(jnp.where(surplus, 1.0, 0.0)) > 0.0)
    def _():
        need = float(n_sel) - count_keys(lambda keys, k0: keys > thr)
        index_bits = (key_ref.shape[0] - 1).bit_length()
        not_tied = jnp.int32(1 << index_bits)

        def tie_positions(j, carry):
            k0 = pl.multiple_of(j * _TK, _TK)
            tied = key_ref[pl.ds(k0, _TK), :] == thr
            tie_ref[pl.ds(k0, _TK), :] = jnp.where(tied, krow + k0, not_tied)
            return carry

        lax.fori_loop(0, n_tiles, tie_positions, 0)

        def count_ties_below(bound):
            def body(j, cnt):
                k0 = pl.multiple_of(j * _TK, _TK)
                hit = jnp.where(tie_ref[pl.ds(k0, _TK), :] < bound, 1.0, 0.0)
                return cnt + _fold_rows(hit, jnp.add)
            cnt8 = lax.fori_loop(0, n_tiles, body, jnp.zeros((8, Q_BLOCK), F32))
            return jnp.sum(cnt8, axis=0, keepdims=True)

        def index_bit(b, bound):
            trial = bound | (jnp.int32(1) << (index_bits - 1 - b))
            return jnp.where(count_ties_below(trial) < need, trial, bound)

        last = lax.fori_loop(0, index_bits, index_bit, jnp.zeros((1, Q_BLOCK), jnp.int32))

        def demote(j, carry):
            k0 = pl.multiple_of(j * _TK, _TK)
            pos = tie_ref[pl.ds(k0, _TK), :]
            drop = (pos > last) & (pos < not_tied)
            key_ref[pl.ds(k0, _TK), :] = jnp.where(drop, jnp.int32(INT_MIN), key_ref[pl.ds(k0, _TK), :])
            return carry

        lax.fori_loop(0, n_tiles, demote, 0)

    q_rope = qa_ref[:, 0:_W_QAR]
    q_nope = qa_ref[:, _W_QAR:]
    scale = HEAD_DIM ** -0.5
    c_rope = A_KV_RANK
    for h in range(A_HEADS):
        q_lat = jnp.dot(q_nope * man_ref[h:h + 1, :], wuk_ref[...], preferred_element_type=F32)
        cols = slice(h * Q_BLOCK, (h + 1) * Q_BLOCK)
        qcat_ref[0:c_rope, cols] = (q_lat.astype(BF16) * scale).astype(F32).T.astype(BF16)
        qcat_ref[c_rope:, cols] = (q_rope * mar_ref[h:h + 1, :] * scale).astype(F32).T.astype(BF16)

    acc_ref[...] = jnp.zeros(acc_ref.shape, F32)

    def logit_matmul(j, buf):
        s_buf, mx_buf = buf
        k0 = pl.multiple_of(j * _TK, _TK)
        sel = (key_ref[pl.ds(k0, _TK), :] >= thr) & (krow + k0 <= tq_lane)
        bias = jnp.where(sel, 0.0, NEG_BIG).astype(F32)
        s = jnp.dot(ka_ref[pl.ds(k0, _TK), :], qcat_ref[...], preferred_element_type=F32)
        for h in range(A_HEADS):
            sh = s[:, h * Q_BLOCK:(h + 1) * Q_BLOCK] + bias
            s_buf[h] = sh
            mx_buf[:, h * Q_BLOCK:(h + 1) * Q_BLOCK] = _fold_rows(sh, jnp.maximum)

    def softmax_pv(j, buf, carry):
        s_buf, mx_buf = buf
        m_old, l_old = carry
        k0 = pl.multiple_of(j * _TK, _TK)
        ckv_t = ckvt_ref[:, pl.ds(k0, _TK)]
        m_parts, l_parts = [], []
        for c in range(A_HEADS // 2):
            a_parts = []
            for h in (2 * c, 2 * c + 1):
                cols = slice(h * Q_BLOCK, (h + 1) * Q_BLOCK)
                m_h = jnp.maximum(m_old[:, cols], jnp.max(mx_buf[:, cols], axis=0, keepdims=True))
                a_h = jnp.exp(m_old[:, cols] - m_h)
                p = jnp.exp(s_buf[h] - m_h)
                p_ref[c, :, (h % 2) * Q_BLOCK:(h % 2 + 1) * Q_BLOCK] = p.astype(BF16)
                l_parts.append(a_h * l_old[:, cols] + jnp.sum(_fold_rows(p, jnp.add), axis=0, keepdims=True))
                m_parts.append(m_h)
                a_parts.append(a_h)
            pv = jnp.dot(ckv_t, p_ref[c], preferred_element_type=F32)
            acc_ref[c] = acc_ref[c] * jnp.concatenate(a_parts, axis=1) + pv
        return jnp.concatenate(m_parts, axis=1), jnp.concatenate(l_parts, axis=1)

    _, l_fin = _skewed_tiles(n_tiles, logit_matmul, softmax_pv, (sa_ref, ma_ref), (sb_ref, mb_ref),
                             (jnp.full((1, rows), NEG_BIG, F32), jnp.zeros((1, rows), F32)))

    inv_l = 1.0 / l_fin
    out_t = jnp.zeros((A_HEADS * HEAD_DIM, Q_BLOCK), F32)
    for h in range(A_HEADS):
        lanes = slice((h % 2) * Q_BLOCK, (h % 2 + 1) * Q_BLOCK)
        o_lat = (acc_ref[h // 2][:, lanes] * inv_l[:, h * Q_BLOCK:(h + 1) * Q_BLOCK]).astype(BF16)
        out_t = out_t + jnp.dot(wuv_ref[h], o_lat, preferred_element_type=F32)
    out_ref[...] = out_t.T.astype(BF16)


def _dsa_mixer(qa, iq, iw, ikt, ka, ckvt, wuk, wuv_t, m_ar, m_an, m_iq, b, s):
    n_sel = min(A_TOPK_MAX, s // 4)
    nq = s // Q_BLOCK
    rows = A_HEADS * Q_BLOCK
    blk = lambda width: pl.BlockSpec((Q_BLOCK, width), lambda bi, i: (bi * nq + i, 0))
    seq = lambda width: pl.BlockSpec((s, width), lambda bi, i: (bi, 0))
    const = lambda a: pl.BlockSpec(a.shape, lambda bi, i: (0,) * a.ndim)
    return pl.pallas_call(
        functools.partial(_dsa_kernel, n_sel=n_sel),
        out_shape=jax.ShapeDtypeStruct((b * s, A_HEADS * HEAD_DIM), BF16),
        grid=(b, nq),
        in_specs=[blk(_W_QAR + _W_QAN), blk(_W_IQ), blk(_W_IW), seq(_W_IK), seq(_W_CKV + _W_AKR),
                  pl.BlockSpec((A_KV_RANK, s), lambda bi, i: (0, bi)),
                  const(wuk), const(wuv_t), const(m_ar), const(m_an), const(m_iq)],
        out_specs=blk(A_HEADS * HEAD_DIM),
        scratch_shapes=[pltpu.VMEM((s, Q_BLOCK), jnp.int32),
                        pltpu.VMEM((s, Q_BLOCK), jnp.int32),
                        pltpu.VMEM((_W_IQ, rows), BF16),
                        pltpu.VMEM((A_KV_RANK + _W_AKR, rows), BF16),
                        pltpu.VMEM((A_HEADS // 2, A_KV_RANK, 2 * Q_BLOCK), F32),
                        pltpu.VMEM((A_HEADS, _TK, Q_BLOCK), F32),
                        pltpu.VMEM((A_HEADS, _TK, Q_BLOCK), F32),
                        pltpu.VMEM((8, rows), F32),
                        pltpu.VMEM((8, rows), F32),
                        pltpu.VMEM((A_HEADS // 2, _TK, 2 * Q_BLOCK), BF16)],
        compiler_params=_cparams("parallel", "arbitrary"),
        name="dsa_attention",
    )(qa, iq, iw, ikt, ka, ckvt, wuk, wuv_t, m_ar, m_an, m_iq)


def _merge_kernel(x_ref, a_ref, o1_ref, o2_ref, o3_ref, l1_ref, l2_ref, l3_ref, wg_ref, bg_ref,
                  wa_ref, wb_ref, wo_ref, lg_ref, lb_ref, h_ref, *scr):
    dils = [dil for _, dil in B_PATTERNS]
    lses = [_load_token_major(r, scr[2 * g], dils[g]) for g, r in enumerate((l1_ref, l2_ref, l3_ref))]
    outs = [_load_token_major(r, scr[2 * g + 1], dils[g]) for g, r in enumerate((o1_ref, o2_ref, o3_ref))]
    mx = jnp.maximum(jnp.maximum(lses[0], lses[1]), lses[2])
    es = [jnp.exp(l - mx) for l in lses]
    den = es[0] + es[1] + es[2]
    b_out = (es[0] / den) * outs[0] + (es[1] / den) * outs[1] + (es[2] / den) * outs[2]
    ya = jnp.dot(a_ref[...], wa_ref[...], preferred_element_type=F32)
    yb = jnp.dot(b_out.astype(BF16), wb_ref[...], preferred_element_type=F32)
    x = x_ref[...]
    gates = jax.nn.sigmoid(jnp.dot(x.astype(BF16), wg_ref[...], preferred_element_type=F32) + bg_ref[...])
    pre = gates[:, 0:D_MODEL] * ya + gates[:, D_MODEL:] * yb
    mix = jnp.dot(pre.astype(BF16), wo_ref[...], preferred_element_type=F32)
    h_ref[...] = _layer_norm(ALPHA * x + mix, lg_ref[...], lb_ref[...])


def _merge(xf, a_out, b_parts, w_gate, b_gate, wa, wb, wo, ln_g, ln_b, tile):
    n = xf.shape[0]
    row = lambda width: pl.BlockSpec((tile, width), lambda i: (i, 0))
    full = lambda a: pl.BlockSpec(a.shape, lambda i: (0,) * a.ndim)
    (o1, l1), (o2, l2), (o3, l3) = b_parts
    grp = [pl.BlockSpec((tile // dil, dil * B_GROUP_W), lambda i: (i, 0)) for _, dil in B_PATTERNS]
    return pl.pallas_call(
        _merge_kernel,
        out_shape=jax.ShapeDtypeStruct((n, D_MODEL), F32),
        grid=(n // tile,),
        in_specs=[row(D_MODEL), row(A_HEADS * HEAD_DIM)] + grp + grp + [full(w_gate), full(b_gate),
                  full(wa), full(wb), full(wo), full(ln_g), full(ln_b)],
        out_specs=row(D_MODEL),
        scratch_shapes=[pltpu.VMEM((B_GROUP_W // LANES, tile, LANES), F32)] * (2 * len(B_PATTERNS)),
        compiler_params=_cparams("parallel"),
        name="merge_output_projection",
    )(xf, a_out, o1, o2, o3, l1, l2, l3, w_gate, b_gate, wa, wb, wo, ln_g, ln_b)


def _first_max(v):
    m = jnp.max(v, axis=0, keepdims=True)
    idx = lax.broadcasted_iota(jnp.int32, v.shape, 0)
    first = jnp.min(jnp.where(v == m, idx, v.shape[0]), axis=0, keepdims=True)
    return m, idx == first


def _router_kernel(h_ref, wr_ref, rb_ref, gate_ref):
    t = h_ref.shape[0]
    gs = N_EXPERTS // N_GROUPS
    logits = lax.dot_general(wr_ref[...], h_ref[...], (((1,), (1,)), ((), ())),
                             precision=lax.Precision.HIGHEST, preferred_element_type=F32)
    scores = jax.nn.sigmoid(logits)
    biased = scores + rb_ref[...]
    gscores = []
    for g in range(N_GROUPS):
        blk = biased[g * gs:(g + 1) * gs, :]
        m1, hit = _first_max(blk)
        m2 = jnp.max(jnp.where(hit, -jnp.inf, blk), axis=0, keepdims=True)
        gscores.append(m1 + m2)
    gscore = jnp.concatenate(gscores, axis=0)
    gsel = jnp.zeros((N_GROUPS, t), F32)
    for _ in range(TOPK_GROUPS):
        _, hit = _first_max(gscore)
        gsel = jnp.where(hit, 1.0, gsel)
        gscore = jnp.where(hit, -jnp.inf, gscore)
    esel = jnp.concatenate([jnp.broadcast_to(gsel[g:g + 1, :], (gs, t)) for g in range(N_GROUPS)], axis=0)
    cand = jnp.where(esel > 0.0, biased, -jnp.inf)
    top_s = jnp.zeros((N_EXPERTS, t), F32)
    for _ in range(TOP_K):
        _, hit = _first_max(cand)
        top_s = jnp.where(hit, scores, top_s)
        cand = jnp.where(hit, -jnp.inf, cand)
    gate = top_s / jnp.sum(top_s, axis=0, keepdims=True) * ROUTED_SCALE
    gate_ref[...] = jnp.concatenate([gate, jnp.zeros((LANES - N_EXPERTS, t), F32)], axis=0).T


def _router(h, wr_t, rb_col, tile):
    n = h.shape[0]
    return pl.pallas_call(
        _router_kernel,
        out_shape=jax.ShapeDtypeStruct((n, LANES), F32),
        grid=(n // tile,),
        in_specs=[pl.BlockSpec((tile, D_MODEL), lambda i: (i, 0)),
                  pl.BlockSpec(wr_t.shape, lambda i: (0, 0)),
                  pl.BlockSpec(rb_col.shape, lambda i: (0, 0))],
        out_specs=pl.BlockSpec((tile, LANES), lambda i: (i, 0)),
        compiler_params=_cparams("parallel"),
        name="moe_router",
    )(h, wr_t, rb_col)


_EXPERTS_PER_STEP = 2


def _swiglu(hb, w1, w3, w2):
    a = jnp.dot(hb, w1, preferred_element_type=F32)
    b = jnp.dot(hb, w3, preferred_element_type=F32)
    return jnp.dot((jax.nn.silu(a) * b).astype(BF16), w2, preferred_element_type=F32)


def _experts_kernel(h_ref, gate_ref, w1_ref, w3_ref, w2_ref, s1_ref, s3_ref, s2_ref, lg_ref, lb_ref,
                    y_ref, hb_ref, acc_ref):
    step = pl.program_id(1)

    @pl.when(step == 0)
    def _():
        hb = h_ref[...].astype(BF16)
        hb_ref[...] = hb
        acc_ref[...] = _swiglu(hb, s1_ref[...], s3_ref[...], s2_ref[...])

    hb = hb_ref[...]
    gate = gate_ref[...]
    lane = lax.broadcasted_iota(jnp.int32, gate.shape, 1)
    for k in range(_EXPERTS_PER_STEP):
        e = step * _EXPERTS_PER_STEP + k
        g_col = jnp.sum(jnp.where(lane == e, gate, 0.0), axis=1, keepdims=True)
        acc_ref[...] += _swiglu(hb, w1_ref[k], w3_ref[k], w2_ref[k]) * g_col

    @pl.when(step == pl.num_programs(1) - 1)
    def _():
        y_ref[...] = _layer_norm(ALPHA * h_ref[...] + acc_ref[...], lg_ref[...], lb_ref[...])


def _experts(h, gate, w1, w3, w2, s1, s3, s2, ln_g, ln_b, tile):
    n = h.shape[0]
    ec = _EXPERTS_PER_STEP
    row = lambda width: pl.BlockSpec((tile, width), lambda i, e: (i, 0))
    full = lambda a: pl.BlockSpec(a.shape, lambda i, e: (0,) * a.ndim)
    wspec = lambda a: pl.BlockSpec((ec,) + a.shape[1:], lambda i, e: (e, 0, 0))
    return pl.pallas_call(
        _experts_kernel,
        out_shape=jax.ShapeDtypeStruct((n, D_MODEL), F32),
        grid=(n // tile, N_EXPERTS // ec),
        in_specs=[row(D_MODEL), row(LANES), wspec(w1), wspec(w3), wspec(w2),
                  full(s1), full(s3), full(s2), full(ln_g), full(ln_b)],
        out_specs=row(D_MODEL),
        scratch_shapes=[pltpu.VMEM((tile, D_MODEL), BF16), pltpu.VMEM((tile, D_MODEL), F32)],
        compiler_params=_cparams("parallel", "arbitrary"),
        name="moe_experts",
    )(h, gate, w1, w3, w2, s1, s3, s2, ln_g, ln_b)


def _rope_freqs():
    inv16 = ROPE_THETA ** (-jnp.arange(0, ROPE_DIM, 2, dtype=F32) / ROPE_DIM)
    inv8 = ROPE_THETA ** (-jnp.arange(0, IDX_ROPE, 2, dtype=F32) / IDX_ROPE)

    def row(inv, fidx):
        return jnp.where(fidx >= 0, inv[np.maximum(fidx, 0)], 0.0)
    lay = _LAYOUT
    freq = jnp.stack([row(inv16, lay["fa"]), row(inv8, lay["fi"]), row(inv16, lay["fb"])])
    sign = jnp.asarray(np.stack([lay["sa"], lay["si"], lay["sb"]]))
    return freq.astype(F32), sign


def _layer(x, positions, w_in, b_gate, g_kv, w_uk, w_uv, w_branch_a, w_branch_b, w_o, ln1_g, ln1_b,
           w_router, router_bias, w1_e, w3_e, w2_e, ws1, ws3, ws2, ln2_g, ln2_b):
    b, s, d = x.shape
    n = b * s
    lay = _LAYOUT
    tile = min(256, n)
    xf = x.reshape(n, d)

    n_proj = _C_GA
    w_perm = (w_in[:, lay["cols"][:n_proj]] * lay["keep"][:n_proj]).astype(BF16)
    w_gate = w_in[:, _OFF_GA:_OFF_GA + 2 * D_MODEL].astype(BF16)
    wuk =jnp.transpose(w_uk, (1, 2, 0)).reshape(A_HEADS * A_NOPE, A_KV_RANK).astype(BF16)
    wuv_t = jnp.zeros((A_HEADS, A_HEADS, HEAD_DIM, A_KV_RANK), F32)
    wuv_t = wuv_t.at[jnp.arange(A_HEADS), jnp.arange(A_HEADS)].set(jnp.transpose(w_uv, (1, 2, 0)))
    wuv_t = wuv_t.reshape(A_HEADS, A_HEADS * HEAD_DIM, A_KV_RANK).astype(BF16)
    masks = {k: jnp.asarray(lay[k], BF16) for k in ("m_ar", "m_an", "m_iq", "m_bq")}
    m_bv = jnp.asarray(lay["m_bv"], F32)

    freq, sign = _rope_freqs()
    tables = _rope_tables(positions.reshape(n, 1), freq, sign, tile)
    qa, ka, ckvt, iq, ikt, iw, *bqkv = _input_projection(xf, w_perm, tables, g_kv.reshape(1, -1), tile)

    a_out = _dsa_mixer(qa, iq, iw, ikt, ka, ckvt, wuk, wuv_t, masks["m_ar"], masks["m_an"], masks["m_iq"], b, s)
    ng = len(B_PATTERNS)
    b_parts = [_dilated_group(bqkv[g], bqkv[ng + g], bqkv[2 * ng + g], masks["m_bq"], m_bv, b, dil)
               for g, (_, dil) in enumerate(B_PATTERNS)]

    h = _merge(xf, a_out, b_parts, w_gate, b_gate.reshape(1, -1), w_branch_a.astype(BF16),
               w_branch_b.astype(BF16), w_o.astype(BF16), ln1_g.reshape(1, -1), ln1_b.reshape(1, -1), tile)

    gate = _router(h, w_router.T, router_bias.reshape(-1, 1), min(1024, n))
    y = _experts(h, gate, w1_e.astype(BF16), w3_e.astype(BF16), w2_e.astype(BF16),
                 ws1.astype(BF16), ws3.astype(BF16), ws2.astype(BF16),
                 ln2_g.reshape(1, -1), ln2_b.reshape(1, -1), min(1024, n))
    return y.reshape(b, s, d)


def kernel(x, positions, w_in, b_gate, g_kv, w_uk, w_uv, w_branch_a, w_branch_b, w_o, ln1_g, ln1_b,
           w_router, router_bias, w1_e, w3_e, w2_e, ws1, ws3, ws2, ln2_g, ln2_b):
    h = x
    for l in range(DEPTH):
        h = _layer(h, positions, w_in[l], b_gate[l], g_kv[l], w_uk[l], w_uv[l], w_branch_a[l],
                   w_branch_b[l], w_o[l], ln1_g[l], ln1_b[l], w_router[l], router_bias[l],
                   w1_e[l], w3_e[l], w2_e[l], ws1[l], ws3[l], ws2[l], ln2_g[l], ln2_b[l])
    return h
```

```python
import functools

import jax
import jax.numpy as jnp
import numpy as np
from jax import lax
from jax.experimental import pallas as pl
from jax.experimental.pallas import tpu as pltpu

F32 = jnp.float32
BF16 = jnp.bfloat16

D_MODEL = 1024
HEAD_DIM = 64
ROPE_DIM = 16
ROPE_THETA = 500000.0
Q_BLOCK = 128
A_HEADS = 8
A_NOPE = HEAD_DIM - ROPE_DIM
A_KV_RANK = 256
A_TOPK_MAX = 256
IDX_HEADS = 8
IDX_DIM = 32
IDX_ROPE = 8
B_PATTERNS = ((128, 1), (512, 4), (2048, 16))
B_GROUP_HEADS = 4
B_HEADS = B_GROUP_HEADS * len(B_PATTERNS)
B_GROUP_W = B_GROUP_HEADS * HEAD_DIM
N_EXPERTS = 64
TOP_K = 8
N_GROUPS = 8
TOPK_GROUPS = 4
D_EXPERT = 256
ROUTED_SCALE = 2.5
DEPTH = 1
ALPHA = (2.0 * DEPTH) ** 0.25
LN_EPS = 1e-5
RMS_EPS = 1e-6

LANES = 128
VMEM_LIMIT = 56 * 1024 * 1024
NEG_BIG = -1e30
INT_MIN = -(2 ** 31)

_OFF_AQ = 0
_OFF_CKV = _OFF_AQ + A_HEADS * HEAD_DIM
_OFF_AKR = _OFF_CKV + A_KV_RANK
_OFF_IQ = _OFF_AKR + ROPE_DIM
_OFF_IK = _OFF_IQ + IDX_HEADS * IDX_DIM
_OFF_IW = _OFF_IK + IDX_DIM
_OFF_BQ = _OFF_IW + IDX_HEADS
_OFF_BK = _OFF_BQ + B_HEADS * HEAD_DIM
_OFF_BV = _OFF_BK + B_HEADS * HEAD_DIM
_OFF_GA = _OFF_BV + B_HEADS * HEAD_DIM
_OFF_GB = _OFF_GA + D_MODEL
_IN_TOTAL = _OFF_GB + D_MODEL

_W_QAR, _W_QAN, _W_CKV, _W_AKR = 128, A_HEADS * A_NOPE, A_KV_RANK, 128
_W_IQ, _W_IK, _W_IW = 256, 256, 128
_W_B = B_HEADS * HEAD_DIM
_C_QAR = 0
_C_QAN = _C_QAR + _W_QAR
_C_CKV = _C_QAN + _W_QAN
_C_AKR = _C_CKV + _W_CKV
_C_IQ = _C_AKR + _W_AKR
_C_IK = _C_IQ + _W_IQ
_C_IW = _C_IK + _W_IK
_C_BQ = _C_IW + _W_IW
_C_BK = _C_BQ + _W_B
_C_BV = _C_BK + _W_B
_C_GA = _C_BV + _W_B
_C_GB = _C_GA + D_MODEL
_P_TOTAL = _C_GB + D_MODEL


def _idx_lane(l):
    if l < 16:
        return l // 4, l % 4
    if l < 64:
        return (l - 16) // 12, 8 + (l - 16) % 12
    if l < 80:
        return (l - 64) // 4, 4 + (l - 64) % 4
    return (l - 80) // 12, 20 + (l - 80) % 12


def _b_lane(l):
    half, r = l // 64, l % 64
    which, rr = r // 32, r % 32
    if rr < 8:
        return which, half * 8 + rr
    return which, 16 + half * 24 + (rr - 8)


def _build_layout():
    cols = np.zeros((_P_TOTAL,), np.int32)
    keep = np.ones((_P_TOTAL,), np.float32)
    for l in range(128):
        half, h, f = l // 64, (l % 64) // 8, l % 8
        cols[_C_QAR + l] = _OFF_AQ + h * HEAD_DIM + half * 8 + f
        cols[_C_AKR + l] = _OFF_AKR + half * 8 + f
    for h in range(A_HEADS):
        for j in range(A_NOPE):
            cols[_C_QAN + h * A_NOPE + j] = _OFF_AQ + h * HEAD_DIM + ROPE_DIM + j
    cols[_C_CKV:_C_CKV + _W_CKV] = _OFF_CKV + np.arange(_W_CKV)
    for sl in range(2):
        for l in range(128):
            hh, d = _idx_lane(l)
            cols[_C_IQ + sl * 128 + l] = _OFF_IQ + (sl * 4 + hh) * IDX_DIM + d
            cols[_C_IK + sl * 128 + l] = _OFF_IK + d
    cols[_C_IW:_C_IW + IDX_HEADS] = _OFF_IW + np.arange(IDX_HEADS)
    keep[_C_IW + IDX_HEADS:_C_IW + _W_IW] = 0.0
    for p in range(B_HEADS // 2):
        for l in range(128):
            which, d = _b_lane(l)
            h = 2 * p + which
            cols[_C_BQ + p * 128 + l] = _OFF_BQ + h * HEAD_DIM + d
            cols[_C_BK + p * 128 + l] = _OFF_BK + h * HEAD_DIM + d
    cols[_C_BV:_C_BV + _W_B] = _OFF_BV + np.arange(_W_B)
    cols[_C_GA:_C_GA + D_MODEL] = _OFF_GA + np.arange(D_MODEL)
    cols[_C_GB:_C_GB + D_MODEL] = _OFF_GB + np.arange(D_MODEL)

    fa = np.array([l % 8 for l in range(128)])
    sa = np.array([-1.0 if l < 64 else 1.0 for l in range(128)], np.float32)
    fi = np.full((128,), -1)
    si = np.zeros((128,), np.float32)
    fb = np.full((128,), -1)
    sb = np.zeros((128,), np.float32)
    for l in range(128):
        if l < 16 or 64 <= l < 80:
            fi[l] = l % 4
            si[l] = -1.0 if l < 64 else 1.0
        if l % 32 < 8:
            fb[l] = l % 32
            sb[l] = -1.0 if l < 64 else 1.0

    m_ar = np.zeros((A_HEADS, 128), np.float32)
    for l in range(128):
        m_ar[(l % 64) // 8, l] = 1.0
    m_an = np.zeros((A_HEADS, _W_QAN), np.float32)
    for h in range(A_HEADS):
        m_an[h, h * A_NOPE:(h + 1) * A_NOPE] = 1.0
    m_iq = np.zeros((IDX_HEADS, _W_IQ), np.float32)
    for sl in range(2):
        for l in range(128):
            m_iq[sl * 4 + _idx_lane(l)[0], sl * 128 + l] = 1.0
    m_bq = np.zeros((B_GROUP_HEADS, B_GROUP_W), np.float32)
    for p in range(2):
        for l in range(128):
            m_bq[2 * p + _b_lane(l)[0], p * 128 + l] = 1.0
    m_bv = np.zeros((B_GROUP_HEADS, B_GROUP_W), np.float32)
    for j in range(B_GROUP_HEADS):
        m_bv[j, j * HEAD_DIM:(j + 1) * HEAD_DIM] = 1.0
    return dict(cols=cols, keep=keep, fa=fa, sa=sa, fi=fi, si=si, fb=fb, sb=sb,
                m_ar=m_ar, m_an=m_an, m_iq=m_iq, m_bq=m_bq, m_bv=m_bv)


_LAYOUT = _build_layout()


def _cparams(*sem):
    return pltpu.CompilerParams(dimension_semantics=sem, vmem_limit_bytes=VMEM_LIMIT)


def _layer_norm(v, g, b):
    mu = jnp.mean(v, axis=-1, keepdims=True)
    var = jnp.mean(jnp.square(v - mu), axis=-1, keepdims=True)
    return (v - mu) * lax.rsqrt(var + LN_EPS) * g + b


def _dot_nt(a, b):
    return lax.dot_general(a, b, (((1,), (1,)), ((), ())), preferred_element_type=F32)


def _rope_table_kernel(pos_ref, freq_ref, sign_ref, out_ref):
    pos = pos_ref[...].astype(F32)
    for k in range(3):
        ang = pos * freq_ref[k:k + 1, :]
        out_ref[:, (2 * k) * LANES:(2 * k + 1) * LANES] = jnp.cos(ang)
        out_ref[:, (2 * k + 1) * LANES:(2 * k + 2) * LANES] = jnp.sin(ang) * sign_ref[k:k + 1, :]


def _rope_tables(pos_col, freq, sign, tile):
    n = pos_col.shape[0]
    return pl.pallas_call(
        _rope_table_kernel,
        out_shape=jax.ShapeDtypeStruct((n, 6 * LANES), F32),
        grid=(n // tile,),
        in_specs=[pl.BlockSpec((tile, 1), lambda i: (i, 0)),
                  pl.BlockSpec((3, LANES), lambda i: (0, 0)),
                  pl.BlockSpec((3, LANES), lambda i: (0, 0))],
        out_specs=pl.BlockSpec((tile, 6 * LANES), lambda i: (i, 0)),
        compiler_params=_cparams("parallel"),
        name="rope_tables",
    )(pos_col, freq, sign)


def _rope_slabs(y, cos, sin):
    outs = []
    for s in range(y.shape[1] // LANES):
        ys = y[:, s * LANES:(s + 1) * LANES]
        outs.append(ys * cos + pltpu.roll(ys, 64, 1) * sin)
    return outs[0] if len(outs) == 1 else jnp.concatenate(outs, axis=1)


def _store_residue_major(out_ref, y, scr_ref, dil):
    if dil == 1:
        out_ref[...] = y.astype(out_ref.dtype)
        return
    rows, width = y.shape[0] // dil, y.shape[1]
    for c in range(width // LANES):
        scr_ref[c] = y[:, c * LANES:(c + 1) * LANES]
    for r in range(dil):
        for c in range(width // LANES):
            lanes = slice(r * width + c * LANES, r * width + (c + 1) * LANES)
            out_ref[:, lanes] = scr_ref[c, pl.ds(r, rows, stride=dil), :].astype(out_ref.dtype)


def _load_token_major(ref, scr_ref, dil):
    if dil == 1:
        return ref[...]
    rows, width = ref.shape[0], ref.shape[1] // dil
    for r in range(dil):
        for c in range(width // LANES):
            lanes = slice(r * width + c * LANES, r * width + (c + 1) * LANES)
            scr_ref[c, pl.ds(r, rows, stride=dil), :] = ref[:, lanes]
    return jnp.concatenate([scr_ref[c] for c in range(width // LANES)], axis=1)


def _proj_kernel(x_ref, w_ref, tab_ref, gkv_ref,
                 qa_ref, ka_ref, ckvt_ref, iq_ref, ik_ref, iw_ref, *rest):
    b_refs, scr_ref = rest[:-1], rest[-1]
    xb = x_ref[...].astype(BF16)

    def proj(c0, width):
        return jnp.dot(xb, w_ref[:, c0:c0 + width], preferred_element_type=F32)

    cos_a, sin_a = tab_ref[:, 0:128], tab_ref[:, 128:256]
    cos_i, sin_i = tab_ref[:, 256:384], tab_ref[:, 384:512]
    cos_b, sin_b = tab_ref[:, 512:640], tab_ref[:, 640:768]

    qa_ref[:, 0:_W_QAR] = _rope_slabs(proj(_C_QAR, _W_QAR), cos_a, sin_a).astype(BF16)
    qa_ref[:, _W_QAR:] = proj(_C_QAN, _W_QAN).astype(BF16)
    ckv = proj(_C_CKV, _W_CKV)
    ckv = ckv * lax.rsqrt(jnp.mean(jnp.square(ckv), axis=-1, keepdims=True) + RMS_EPS) * gkv_ref[...]
    ka_ref[:, 0:_W_CKV] = ckv.astype(BF16)
    ckvt_ref[...] = ckv.T.astype(BF16)
    ka_ref[:, _W_CKV:] = _rope_slabs(proj(_C_AKR, _W_AKR), cos_a, sin_a).astype(BF16)
    iq_ref[...] = _rope_slabs(proj(_C_IQ, _W_IQ), cos_i, sin_i).astype(BF16)
    ik_ref[...] = _rope_slabs(proj(_C_IK, _W_IK), cos_i, sin_i).astype(BF16)
    iw_ref[...] = proj(_C_IW, _W_IW) * ((IDX_HEADS * IDX_DIM) ** -0.5)
    ng = len(B_PATTERNS)
    for kind, c0 in enumerate((_C_BQ, _C_BK, _C_BV)):
        for g, (_, dil) in enumerate(B_PATTERNS):
            y = proj(c0 + g * B_GROUP_W, B_GROUP_W)
            if kind < 2:
                y = _rope_slabs(y, cos_b, sin_b)
            _store_residue_major(b_refs[kind * ng + g], y, scr_ref, dil)


def _input_projection(xf, w_perm, tables, g_kv, tile):
    n = xf.shape[0]
    row = lambda width: pl.BlockSpec((tile, width), lambda i: (i, 0))
    full = lambda a: pl.BlockSpec(a.shape, lambda i: (0,) * a.ndim)
    out_w = [(_W_QAR + _W_QAN, BF16), (_W_CKV + _W_AKR, BF16), None, (_W_IQ, BF16), (_W_IK, BF16),
             (_W_IW, F32)]
    shapes = [jax.ShapeDtypeStruct((A_KV_RANK, n), BF16) if o is None else jax.ShapeDtypeStruct((n, o[0]), o[1])
              for o in out_w]
    specs = [pl.BlockSpec((A_KV_RANK, tile), lambda i: (0, i)) if o is None else row(o[0]) for o in out_w]
    for _ in range(3):
        for _, dil in B_PATTERNS:
            shapes.append(jax.ShapeDtypeStruct((n // dil, dil * B_GROUP_W), BF16))
            specs.append(pl.BlockSpec((tile // dil, dil * B_GROUP_W), lambda i: (i, 0)))
    return pl.pallas_call(
        _proj_kernel,
        out_shape=shapes,
        grid=(n // tile,),
        in_specs=[row(D_MODEL), full(w_perm), row(6 * LANES), full(g_kv)],
        out_specs=specs,
        scratch_shapes=[pltpu.VMEM((B_GROUP_W // LANES, tile, LANES), F32)],
        compiler_params=_cparams("parallel"),
        name="input_projection",
    )(xf, w_perm, tables, g_kv)


def _dilated_kernel(q_ref, kc_ref, kp_ref, vc_ref, vp_ref, mq_ref, mv_ref, o_ref, lse_ref,
                    kwin_ref, vwin_ref, *, tq):
    first = pl.program_id(2) == 0
    kwin_ref[0:Q_BLOCK, :] = kp_ref[...]
    kwin_ref[Q_BLOCK:, :] = kc_ref[...]
    vwin_ref[0:Q_BLOCK, :] = vp_ref[...]
    vwin_ref[Q_BLOCK:, :] = vc_ref[...]
    t = lax.broadcasted_iota(jnp.int32, (Q_BLOCK, 2 * Q_BLOCK), 0)
    c = lax.broadcasted_iota(jnp.int32, (Q_BLOCK, 2 * Q_BLOCK), 1)
    diff = t + Q_BLOCK - c
    band = (diff >= 0) & (diff <= Q_BLOCK)
    scale = HEAD_DIM ** -0.5
    for sb in range(tq // Q_BLOCK):
        valid = band
        if sb == 0:
            valid = band & (c >= jnp.where(first, Q_BLOCK, 0))
        bias = jnp.where(valid, 0.0, NEG_BIG).astype(F32)
        q = q_ref[sb * Q_BLOCK:(sb + 1) * Q_BLOCK, :] * scale
        kw = kwin_ref[sb * Q_BLOCK:(sb + 2) * Q_BLOCK, :]
        vw = vwin_ref[sb * Q_BLOCK:(sb + 2) * Q_BLOCK, :]
        qs = jnp.concatenate([q * mq_ref[j:j + 1, :] for j in range(B_GROUP_HEADS)], axis=0)
        s = _dot_nt(qs, kw)
        o_acc = jnp.zeros((Q_BLOCK, B_GROUP_W), F32)
        lse_acc = jnp.zeros((Q_BLOCK, B_GROUP_W), F32)
        for j in range(B_GROUP_HEADS):
            sj = s[j * Q_BLOCK:(j + 1) * Q_BLOCK, :] + bias
            m = jnp.max(sj, axis=-1, keepdims=True)
            e = jnp.exp(sj - m)
            den = jnp.sum(e, axis=-1, keepdims=True)
            pv = jnp.dot(e.astype(BF16), vw, preferred_element_type=F32)
            mv = mv_ref[j:j + 1, :]
            o_acc = o_acc + (pv / den) * mv
            lse_acc = lse_acc + (m + jnp.log(den)) * mv
        o_ref[sb * Q_BLOCK:(sb + 1) * Q_BLOCK, :] = o_acc
        lse_ref[sb * Q_BLOCK:(sb + 1) * Q_BLOCK, :] = lse_acc


def _dilated_group(bq, bk, bv, mq, mv, b, dil):
    sub = bq.shape[0] // b
    tq = min(512, sub)
    nblk = tq // Q_BLOCK
    view = lambda a: a.reshape(b, sub, dil * B_GROUP_W)
    cur = pl.BlockSpec((None, tq, B_GROUP_W), lambda bi, r, i: (bi, i, r))
    prev = pl.BlockSpec((None, Q_BLOCK, B_GROUP_W),
                        lambda bi, r, i: (bi, jnp.maximum(i * nblk - 1, 0), r))
    const = lambda a: pl.BlockSpec(a.shape, lambda bi, r, i: (0, 0))
    out = cur
    o, lse = pl.pallas_call(
        functools.partial(_dilated_kernel, tq=tq),
        out_shape=[jax.ShapeDtypeStruct((b, sub, dil * B_GROUP_W), F32)] * 2,
        grid=(b, dil, sub // tq),
        in_specs=[cur, cur, prev, cur, prev, const(mq), const(mv)],
        out_specs=[out, out],
        scratch_shapes=[pltpu.VMEM((tq + Q_BLOCK, B_GROUP_W), BF16)] * 2,
        compiler_params=_cparams("parallel", "parallel", "arbitrary"),
        name=f"dilated_attention_d{dil}",
    )(view(bq), view(bk), view(bk), view(bv), view(bv), mq, mv)
    return o.reshape(b * sub, dil * B_GROUP_W), lse.reshape(b * sub, dil * B_GROUP_W)


_TK = 512


def _fold_rows(x, op, slab=8):
    parts = [x[r:r + slab, :] for r in range(0, x.shape[0], slab)]
    while len(parts) > 1:
        parts = [op(parts[k], parts[k + 1]) for k in range(0, len(parts) - 1, 2)] + parts[len(parts) & ~1:]
    return parts[0]


def _skewed_tiles(n_tiles, produce, consume, buf_a, buf_b, carry):
    produce(0, buf_a)

    def pair(t, c):
        j = 2 * t
        produce(j + 1, buf_b)
        c = consume(j, buf_a, c)
        produce(j + 2, buf_a)
        return consume(j + 1, buf_b, c)

    n_pairs = (n_tiles - 1) // 2
    carry = lax.fori_loop(0, n_pairs, pair, carry)
    j = 2 * n_pairs

    def last_two(c):
        produce(j + 1, buf_b)
        return consume(j + 1, buf_b, consume(j, buf_a, c))

    return lax.cond(n_tiles - j == 2, last_two, lambda c: consume(j, buf_a, c), carry)


def _dsa_kernel(qa_ref, iq_ref, iw_ref, ikt_ref, ka_ref, ckvt_ref, wuk_ref, wuv_ref, mar_ref, man_ref, miq_ref,
                out_ref, key_ref, tie_ref, top_ref, iqs_ref, qcat_ref, acc_ref, sa_ref, sb_ref, ma_ref, mb_ref,
                p_ref, *, n_sel):
    i = pl.program_id(1)
    q0 = i * Q_BLOCK
    n_keys = q0 + Q_BLOCK
    rows = A_HEADS * Q_BLOCK

    n_tiles = (n_keys + _TK - 1) // _TK
    tq_lane = q0 + lax.broadcasted_iota(jnp.int32, (_TK, Q_BLOCK), 1)
    krow = lax.broadcasted_iota(jnp.int32, (_TK, Q_BLOCK), 0)

    iq = iq_ref[...]
    for h in range(IDX_HEADS):
        iqs_ref[:, h * Q_BLOCK:(h + 1) * Q_BLOCK] = (iq * miq_ref[h:h + 1, :]).astype(F32).T.astype(BF16)
    iw_t = iw_ref[...].T

    def score_matmul(j, buf):
        k0 = pl.multiple_of(j * _TK, _TK)
        s = jnp.dot(ikt_ref[pl.ds(k0, _TK), :], iqs_ref[...], preferred_element_type=F32)
        for h in range(IDX_HEADS):
            buf[0][h] = s[:, h * Q_BLOCK:(h + 1) * Q_BLOCK]

    def score_keys(j, buf, carry):
        k0 = pl.multiple_of(j * _TK, _TK)
        sc = jnp.zeros((_TK, Q_BLOCK), F32)
        for h in range(IDX_HEADS):
            sc = sc + jnp.maximum(buf[0][h], 0.0) * iw_t[h:h + 1, :]
        bits = lax.bitcast_convert_type(jnp.where(sc == 0.0, 0.0, sc), jnp.int32)
        okey = bits ^ ((bits >> 31) & jnp.int32(0x7FFFFFFF))
        causal = krow + k0 <= tq_lane
        key_ref[pl.ds(k0, _TK), :] = jnp.where(causal, okey, jnp.int32(INT_MIN))
        top = lax.bitcast_convert_type(bits & jnp.int32(-65536), F32)
        top_ref[pl.ds(k0, _TK), :] = jnp.where(causal, top, -jnp.inf).astype(BF16)
        return carry

    _skewed_tiles(n_tiles, score_matmul, score_keys, (sa_ref, ma_ref), (sb_ref, mb_ref), 0)

    def count_keys(pred):
        def body(j, cnt):
            k0 = pl.multiple_of(j * _TK, _TK)
            hit = jnp.where(pred(key_ref[pl.ds(k0, _TK), :], k0), 1.0, 0.0)
            return cnt + _fold_rows(hit, jnp.add)
        cnt8 = lax.fori_loop(0, n_tiles, body, jnp.zeros((8, Q_BLOCK), F32))
        return jnp.sum(cnt8, axis=0, keepdims=True)

    def count_top_ge(t):
        one, zero = jnp.ones((), BF16), jnp.zeros((), BF16)
        t_b = jnp.broadcast_to(t, (_TK, Q_BLOCK))

        def body(j, cnt):
            k0 = pl.multiple_of(j * _TK, _TK)
            hit = jnp.where(top_ref[pl.ds(k0, _TK), :] >= t_b, one, zero)
            return cnt + _fold_rows(hit, jnp.add, slab=16).astype(F32)
        cnt16 = lax.fori_loop(0, n_tiles, body, jnp.zeros((16, Q_BLOCK), F32))
        return jnp.sum(cnt16, axis=0, keepdims=True)

    def top_bit_step(b, prefix):
        trial = prefix | (jnp.int32(1) << (15 - b))
        k = trial - 32768
        k = jnp.where((k > 0) & (k < 0x80), 0x80, k)
        fbits = jnp.where(k < 0, (k ^ jnp.int32(0x7FFF)) & jnp.int32(0xFFFF), k) << 16
        t = lax.bitcast_convert_type(fbits, F32).astype(BF16)
        return jnp.where(count_top_ge(t) >= float(n_sel), trial, prefix)

    def bit_step(b, prefix):
        trial = prefix | (jnp.int32(1) << (15 - b))
        t = trial ^ jnp.int32(INT_MIN)
        cnt = count_keys(lambda keys, k0: keys >= t)
        return jnp.where(cnt >= float(n_sel), trial, prefix)

    prefix = lax.fori_loop(0, 16, top_bit_step, jnp.zeros((1, Q_BLOCK), jnp.int32)) << 16
    prefix = lax.fori_loop(0, 16, bit_step, prefix)
    thr = prefix ^ jnp.int32(INT_MIN)

    surplus = (count_keys(lambda keys, k0: keys >= thr) > float(n_sel)) & (thr != jnp.int32(INT_MIN))

    @pl.when(jnp.max(jnp.where(surplus, 1.0, 0.0)) > 0.0)
    def _():
        need = float(n_sel) - count_keys(lambda keys, k0: keys > thr)
        index_bits = (key_ref.shape[0] - 1).bit_length()
        not_tied = jnp.int32(1 << index_bits)

        def tie_positions(j, carry):
            k0 = pl.multiple_of(j * _TK, _TK)
            tied = key_ref[pl.ds(k0, _TK), :] == thr
            tie_ref[pl.ds(k0, _TK), :] = jnp.where(tied, krow + k0, not_tied)
            return carry

        lax.fori_loop(0, n_tiles, tie_positions, 0)

        def count_ties_below(bound):
            def body(j, cnt):
                k0 = pl.multiple_of(j * _TK, _TK)
                hit = jnp.where(tie_ref[pl.ds(k0, _TK), :] < bound, 1.0, 0.0)
                return cnt + _fold_rows(hit, jnp.add)
            cnt8 = lax.fori_loop(0, n_tiles, body, jnp.zeros((8, Q_BLOCK), F32))
            return jnp.sum(cnt8, axis=0, keepdims=True)

        def index_bit(b, bound):
            trial = bound | (jnp.int32(1) << (index_bits - 1 - b))
            return jnp.where(count_ties_below(trial) < need, trial, bound)

        last = lax.fori_loop(0, index_bits, index_bit, jnp.zeros((1, Q_BLOCK), jnp.int32))

        def demote(j, carry):
            k0 = pl.multiple_of(j * _TK, _TK)
            pos = tie_ref[pl.ds(k0, _TK), :]
            drop = (pos > last) & (pos < not_tied)
            key_ref[pl.ds(k0, _TK), :] = jnp.where(drop, jnp.int32(INT_MIN), key_ref[pl.ds(k0, _TK), :])
            return carry

        lax.fori_loop(0, n_tiles, demote, 0)

    q_rope = qa_ref[:, 0:_W_QAR]
    q_nope = qa_ref[:, _W_QAR:]
    scale = HEAD_DIM ** -0.5
    c_rope = A_KV_RANK
    for h in range(A_HEADS):
        q_lat = jnp.dot(q_nope * man_ref[h:h + 1, :], wuk_ref[...], preferred_element_type=F32)
        cols = slice(h * Q_BLOCK, (h + 1) * Q_BLOCK)
        qcat_ref[0:c_rope, cols] = (q_lat.astype(BF16) * scale).astype(F32).T.astype(BF16)
        qcat_ref[c_rope:, cols] = (q_rope * mar_ref[h:h + 1, :] * scale).astype(F32).T.astype(BF16)

    acc_ref[...] = jnp.zeros(acc_ref.shape, F32)

    def logit_matmul(j, buf):
        s_buf, mx_buf = buf
        k0 = pl.multiple_of(j * _TK, _TK)
        sel = (key_ref[pl.ds(k0, _TK), :] >= thr) & (krow + k0 <= tq_lane)
        bias = jnp.where(sel, 0.0, NEG_BIG).astype(F32)
        s = jnp.dot(ka_ref[pl.ds(k0, _TK), :], qcat_ref[...], preferred_element_type=F32)
        for h in range(A_HEADS):
            sh = s[:, h * Q_BLOCK:(h + 1) * Q_BLOCK] + bias
            s_buf[h] = sh
            mx_buf[:, h * Q_BLOCK:(h + 1) * Q_BLOCK] = _fold_rows(sh, jnp.maximum)

    def softmax_pv(j, buf, carry):
        s_buf, mx_buf = buf
        m_old, l_old = carry
        k0 = pl.multiple_of(j * _TK, _TK)
        ckv_t = ckvt_ref[:, pl.ds(k0, _TK)]
        m_parts, l_parts = [], []
        for c in range(A_HEADS // 2):
            a_parts = []
            for h in (2 * c, 2 * c + 1):
                cols = slice(h * Q_BLOCK, (h + 1) * Q_BLOCK)
                m_h = jnp.maximum(m_old[:, cols], jnp.max(mx_buf[:, cols], axis=0, keepdims=True))
                a_h = jnp.exp(m_old[:, cols] - m_h)
                p = jnp.exp(s_buf[h] - m_h)
                p_ref[c, :, (h % 2) * Q_BLOCK:(h % 2 + 1) * Q_BLOCK] = p.astype(BF16)
                l_parts.append(a_h * l_old[:, cols] + jnp.sum(_fold_rows(p, jnp.add), axis=0, keepdims=True))
                m_parts.append(m_h)
                a_parts.append(a_h)
            pv = jnp.dot(ckv_t, p_ref[c], preferred_element_type=F32)
            acc_ref[c] = acc_ref[c] * jnp.concatenate(a_parts, axis=1) + pv
        return jnp.concatenate(m_parts, axis=1), jnp.concatenate(l_parts, axis=1)

    _, l_fin = _skewed_tiles(n_tiles, logit_matmul, softmax_pv, (sa_ref, ma_ref), (sb_ref, mb_ref),
                             (jnp.full((1, rows), NEG_BIG, F32), jnp.zeros((1, rows), F32)))

    inv_l = 1.0 / l_fin
    out_t = jnp.zeros((A_HEADS * HEAD_DIM, Q_BLOCK), F32)
    for h in range(A_HEADS):
        lanes = slice((h % 2) * Q_BLOCK, (h % 2 + 1) * Q_BLOCK)
        o_lat = (acc_ref[h // 2][:, lanes] * inv_l[:, h * Q_BLOCK:(h + 1) * Q_BLOCK]).astype(BF16)
        out_t = out_t + jnp.dot(wuv_ref[h], o_lat, preferred_element_type=F32)
    out_ref[...] = out_t.T.astype(BF16)


def _dsa_mixer(qa, iq, iw, ikt, ka, ckvt, wuk, wuv_t, m_ar, m_an, m_iq, b, s):
    n_sel = min(A_TOPK_MAX, s // 4)
    nq = s // Q_BLOCK
    rows = A_HEADS * Q_BLOCK
    blk = lambda width: pl.BlockSpec((Q_BLOCK, width), lambda bi, i: (bi * nq + i, 0))
    seq = lambda width: pl.BlockSpec((s, width), lambda bi, i: (bi, 0))
    const = lambda a: pl.BlockSpec(a.shape, lambda bi, i: (0,) * a.ndim)
    return pl.pallas_call(
        functools.partial(_dsa_kernel, n_sel=n_sel),
        out_shape=jax.ShapeDtypeStruct((b * s, A_HEADS * HEAD_DIM), BF16),
        grid=(b, nq),
        in_specs=[blk(_W_QAR + _W_QAN), blk(_W_IQ), blk(_W_IW), seq(_W_IK), seq(_W_CKV + _W_AKR),
                  pl.BlockSpec((A_KV_RANK, s), lambda bi, i: (0, bi)),
                  const(wuk), const(wuv_t), const(m_ar), const(m_an), const(m_iq)],
        out_specs=blk(A_HEADS * HEAD_DIM),
        scratch_shapes=[pltpu.VMEM((s, Q_BLOCK), jnp.int32),
                        pltpu.VMEM((s, Q_BLOCK), jnp.int32),
                        pltpu.VMEM((s, Q_BLOCK), BF16),
                        pltpu.VMEM((_W_IQ, rows), BF16),
                        pltpu.VMEM((A_KV_RANK + _W_AKR, rows), BF16),
                        pltpu.VMEM((A_HEADS // 2, A_KV_RANK, 2 * Q_BLOCK), F32),
                        pltpu.VMEM((A_HEADS, _TK, Q_BLOCK), F32),
                        pltpu.VMEM((A_HEADS, _TK, Q_BLOCK), F32),
                        pltpu.VMEM((8, rows), F32),
                        pltpu.VMEM((8, rows), F32),
                        pltpu.VMEM((A_HEADS // 2, _TK, 2 * Q_BLOCK), BF16)],
        compiler_params=_cparams("parallel", "arbitrary"),
        name="dsa_attention",
    )(qa, iq, iw, ikt, ka, ckvt, wuk, wuv_t, m_ar, m_an, m_iq)


def _merge_kernel(x_ref, a_ref, o1_ref, o2_ref, o3_ref, l1_ref, l2_ref, l3_ref, wg_ref, bg_ref,
                  wa_ref, wb_ref, wo_ref, lg_ref, lb_ref, h_ref, *scr):
    dils = [dil for _, dil in B_PATTERNS]
    lses = [_load_token_major(r, scr[2 * g], dils[g]) for g, r in enumerate((l1_ref, l2_ref, l3_ref))]
    outs = [_load_token_major(r, scr[2 * g + 1], dils[g]) for g, r in enumerate((o1_ref, o2_ref, o3_ref))]
    mx = jnp.maximum(jnp.maximum(lses[0], lses[1]), lses[2])
    es = [jnp.exp(l - mx) for l in lses]
    den = es[0] + es[1] + es[2]
    b_out = (es[0] / den) * outs[0] + (es[1] / den) * outs[1] + (es[2] / den) * outs[2]
    ya = jnp.dot(a_ref[...], wa_ref[...], preferred_element_type=F32)
    yb = jnp.dot(b_out.astype(BF16), wb_ref[...], preferred_element_type=F32)
    x = x_ref[...]
    gates = jax.nn.sigmoid(jnp.dot(x.astype(BF16), wg_ref[...], preferred_element_type=F32) + bg_ref[...])
    pre = gates[:, 0:D_MODEL] * ya + gates[:, D_MODEL:] * yb
    mix = jnp.dot(pre.astype(BF16), wo_ref[...], preferred_element_type=F32)
    h_ref[...] = _layer_norm(ALPHA * x + mix, lg_ref[...], lb_ref[...])


def _merge(xf, a_out, b_parts, w_gate, b_gate, wa, wb, wo, ln_g, ln_b, tile):
    n = xf.shape[0]
    row = lambda width: pl.BlockSpec((tile, width), lambda i: (i, 0))
    full = lambda a: pl.BlockSpec(a.shape, lambda i: (0,) * a.ndim)
    (o1, l1), (o2, l2), (o3, l3) = b_parts
    grp = [pl.BlockSpec((tile // dil, dil * B_GROUP_W), lambda i: (i, 0)) for _, dil in B_PATTERNS]
    return pl.pallas_call(
        _merge_kernel,
        out_shape=jax.ShapeDtypeStruct((n, D_MODEL), F32),
        grid=(n // tile,),
        in_specs=[row(D_MODEL), row(A_HEADS * HEAD_DIM)] + grp + grp + [full(w_gate), full(b_gate),
                  full(wa), full(wb), full(wo), full(ln_g), full(ln_b)],
        out_specs=row(D_MODEL),
        scratch_shapes=[pltpu.VMEM((B_GROUP_W // LANES, tile, LANES), F32)] * (2 * len(B_PATTERNS)),
        compiler_params=_cparams("parallel"),
        name="merge_output_projection",
    )(xf, a_out, o1, o2, o3, l1, l2, l3, w_gate, b_gate, wa, wb, wo, ln_g, ln_b)


def _first_max(v):
    m = jnp.max(v, axis=0, keepdims=True)
    idx = lax.broadcasted_iota(jnp.int32, v.shape, 0)
    first = jnp.min(jnp.where(v == m, idx, v.shape[0]), axis=0, keepdims=True)
    return m, idx == first


def _router_kernel(h_ref, wr_ref, rb_ref, gate_ref):
    t = h_ref.shape[0]
    gs = N_EXPERTS // N_GROUPS
    logits = lax.dot_general(wr_ref[...], h_ref[...], (((1,), (1,)), ((), ())),
                             precision=lax.Precision.HIGHEST, preferred_element_type=F32)
    scores = jax.nn.sigmoid(logits)
    biased = scores + rb_ref[...]
    gscores = []
    for g in range(N_GROUPS):
        blk = biased[g * gs:(g + 1) * gs, :]
        m1, hit = _first_max(blk)
        m2 = jnp.max(jnp.where(hit, -jnp.inf, blk), axis=0, keepdims=True)
        gscores.append(m1 + m2)
    gscore = jnp.concatenate(gscores, axis=0)
    gsel = jnp.zeros((N_GROUPS, t), F32)
    for _ in range(TOPK_GROUPS):
        _, hit = _first_max(gscore)
        gsel = jnp.where(hit, 1.0, gsel)
        gscore = jnp.where(hit, -jnp.inf, gscore)
    esel = jnp.concatenate([jnp.broadcast_to(gsel[g:g + 1, :], (gs, t)) for g in range(N_GROUPS)], axis=0)
    cand = jnp.where(esel > 0.0, biased, -jnp.inf)
    top_s = jnp.zeros((N_EXPERTS, t), F32)
    for _ in range(TOP_K):
        _, hit = _first_max(cand)
        top_s = jnp.where(hit, scores, top_s)
        cand = jnp.where(hit, -jnp.inf, cand)
    gate = top_s / jnp.sum(top_s, axis=0, keepdims=True) * ROUTED_SCALE
    gate_ref[...] = jnp.concatenate([gate, jnp.zeros((LANES - N_EXPERTS, t), F32)], axis=0).T


def _router(h, wr_t, rb_col, tile):
    n = h.shape[0]
    return pl.pallas_call(
        _router_kernel,
        out_shape=jax.ShapeDtypeStruct((n, LANES), F32),
        grid=(n // tile,),
        in_specs=[pl.BlockSpec((tile, D_MODEL), lambda i: (i, 0)),
                  pl.BlockSpec(wr_t.shape, lambda i: (0, 0)),
                  pl.BlockSpec(rb_col.shape, lambda i: (0, 0))],
        out_specs=pl.BlockSpec((tile, LANES), lambda i: (i, 0)),
        compiler_params=_cparams("parallel"),
        name="moe_router",
    )(h, wr_t, rb_col)


_EXPERTS_PER_STEP = 4


def _swiglu(hb, w1, w3, w2):
    a = jnp.dot(hb, w1, preferred_element_type=F32)
    b = jnp.dot(hb, w3, preferred_element_type=F32)
    return jnp.dot((jax.nn.silu(a) * b).astype(BF16), w2, preferred_element_type=F32)


def _experts_kernel(h_ref, gate_ref, w1_ref, w3_ref, w2_ref, s1_ref, s3_ref, s2_ref, lg_ref, lb_ref,
                    y_ref, hb_ref, acc_ref):
    step = pl.program_id(1)

    @pl.when(step == 0)
    def _():
        hb = h_ref[...].astype(BF16)
        hb_ref[...] = hb
        acc_ref[...] = _swiglu(hb, s1_ref[...], s3_ref[...], s2_ref[...])

    hb = hb_ref[...]
    gate = gate_ref[...]
    lane = lax.broadcasted_iota(jnp.int32, gate.shape, 1)
    for k in range(_EXPERTS_PER_STEP):
        e = step * _EXPERTS_PER_STEP + k
        g_col = jnp.sum(jnp.where(lane == e, gate, 0.0), axis=1, keepdims=True)
        acc_ref[...] += _swiglu(hb, w1_ref[k], w3_ref[k], w2_ref[k]) * g_col

    @pl.when(step == pl.num_programs(1) - 1)
    def _():
        y_ref[...] = _layer_norm(ALPHA * h_ref[...] + acc_ref[...], lg_ref[...], lb_ref[...])


def _experts(h, gate, w1, w3, w2, s1, s3, s2, ln_g, ln_b, tile):
    n = h.shape[0]
    ec = _EXPERTS_PER_STEP
    row = lambda width: pl.BlockSpec((tile, width), lambda i, e: (i, 0))
    full = lambda a: pl.BlockSpec(a.shape, lambda i, e: (0,) * a.ndim)
    wspec = lambda a: pl.BlockSpec((ec,) + a.shape[1:], lambda i, e: (e, 0, 0))
    return pl.pallas_call(
        _experts_kernel,
        out_shape=jax.ShapeDtypeStruct((n, D_MODEL), F32),
        grid=(n // tile, N_EXPERTS // ec),
        in_specs=[row(D_MODEL), row(LANES), wspec(w1), wspec(w3), wspec(w2),
                  full(s1), full(s3), full(s2), full(ln_g), full(ln_b)],
        out_specs=row(D_MODEL),
        scratch_shapes=[pltpu.VMEM((tile, D_MODEL), BF16), pltpu.VMEM((tile, D_MODEL), F32)],
        compiler_params=_cparams("parallel", "arbitrary"),
        name="moe_experts",
    )(h, gate, w1, w3, w2, s1, s3, s2, ln_g, ln_b)


def _rope_freqs():
    inv16 = ROPE_THETA ** (-jnp.arange(0, ROPE_DIM, 2, dtype=F32) / ROPE_DIM)
    inv8 = ROPE_THETA ** (-jnp.arange(0, IDX_ROPE, 2, dtype=F32) / IDX_ROPE)

    def row(inv, fidx):
        return jnp.where(fidx >= 0, inv[np.maximum(fidx, 0)], 0.0)
    lay = _LAYOUT
    freq = jnp.stack([row(inv16, lay["fa"]), row(inv8, lay["fi"]), row(inv16, lay["fb"])])
    sign = jnp.asarray(np.stack([lay["sa"], lay["si"], lay["sb"]]))
    return freq.astype(F32), sign


def _layer(x, positions, w_in, b_gate, g_kv, w_uk, w_uv, w_branch_a, w_branch_b, w_o, ln1_g, ln1_b,
           w_router, router_bias, w1_e, w3_e, w2_e, ws1, ws3, ws2, ln2_g, ln2_b):
    b, s, d = x.shape
    n = b * s
    lay = _LAYOUT
    tile = min(256, n)
    xf = x.reshape(n, d)

    n_proj = _C_GA
    w_perm = (w_in[:, lay["cols"][:n_proj]] * lay["keep"][:n_proj]).astype(BF16)
    w_gate = w_in[:, _OFF_GA:_OFF_GA + 2 * D_MODEL].astype(BF16)
    wuk =jnp.transpose(w_uk, (1, 2, 0)).reshape(A_HEADS * A_NOPE, A_KV_RANK).astype(BF16)
    wuv_t = jnp.zeros((A_HEADS, A_HEADS, HEAD_DIM, A_KV_RANK), F32)
    wuv_t = wuv_t.at[jnp.arange(A_HEADS), jnp.arange(A_HEADS)].set(jnp.transpose(w_uv, (1, 2, 0)))
    wuv_t = wuv_t.reshape(A_HEADS, A_HEADS * HEAD_DIM, A_KV_RANK).astype(BF16)
    masks = {k: jnp.asarray(lay[k], BF16) for k in ("m_ar", "m_an", "m_iq", "m_bq")}
    m_bv = jnp.asarray(lay["m_bv"], F32)

    freq, sign = _rope_freqs()
    tables = _rope_tables(positions.reshape(n, 1), freq, sign, tile)
    qa, ka, ckvt, iq, ikt, iw, *bqkv = _input_projection(xf, w_perm, tables, g_kv.reshape(1, -1), tile)

    a_out = _dsa_mixer(qa, iq, iw, ikt, ka, ckvt, wuk, wuv_t, masks["m_ar"], masks["m_an"], masks["m_iq"], b, s)
    ng = len(B_PATTERNS)
    b_parts = [_dilated_group(bqkv[g], bqkv[ng + g], bqkv[2 * ng + g], masks["m_bq"], m_bv, b, dil)
               for g, (_, dil) in enumerate(B_PATTERNS)]

    h = _merge(xf, a_out, b_parts, w_gate, b_gate.reshape(1, -1), w_branch_a.astype(BF16),
               w_branch_b.astype(BF16), w_o.astype(BF16), ln1_g.reshape(1, -1), ln1_b.reshape(1, -1), tile)

    gate = _router(h, w_router.T, router_bias.reshape(-1, 1), min(1024, n))
    y = _experts(h, gate, w1_e.astype(BF16), w3_e.astype(BF16), w2_e.astype(BF16),
                 ws1.astype(BF16), ws3.astype(BF16), ws2.astype(BF16),
                 ln2_g.reshape(1, -1), ln2_b.reshape(1, -1), min(1024, n))
    return y.reshape(b, s, d)


def kernel(x, positions, w_in, b_gate, g_kv, w_uk, w_uv, w_branch_a, w_branch_b, w_o, ln1_g, ln1_b,
           w_router, router_bias, w1_e, w3_e, w2_e, ws1, ws3, ws2, ln2_g, ln2_b):
    h = x
    for l in range(DEPTH):
        h = _layer(h, positions, w_in[l], b_gate[l], g_kv[l], w_uk[l], w_uv[l], w_branch_a[l],
                   w_branch_b[l], w_o[l], ln1_g[l], ln1_b[l], w_router[l], router_bias[l],
                   w1_e[l], w3_e[l], w2_e[l], ws1[l], ws3[l], ws2[l], ln2_g[l], ln2_b[l])
    return h
```

```python
import functools

import jax
import jax.numpy as jnp
import numpy as np
from jax import lax
from jax.experimental import pallas as pl
from jax.experimental.pallas import tpu as pltpu

F32 = jnp.float32
BF16 = jnp.bfloat16

D_MODEL = 1024
HEAD_DIM = 64
ROPE_DIM = 16
ROPE_THETA = 500000.0
Q_BLOCK = 128
A_HEADS = 8
A_NOPE = HEAD_DIM - ROPE_DIM
A_KV_RANK = 256
A_TOPK_MAX = 256
IDX_HEADS = 8
IDX_DIM = 32
IDX_ROPE = 8
B_PATTERNS = ((128, 1), (512, 4), (2048, 16))
B_GROUP_HEADS = 4
B_HEADS = B_GROUP_HEADS * len(B_PATTERNS)
B_GROUP_W = B_GROUP_HEADS * HEAD_DIM
N_EXPERTS = 64
TOP_K = 8
N_GROUPS = 8
TOPK_GROUPS = 4
D_EXPERT = 256
ROUTED_SCALE = 2.5
DEPTH = 1
ALPHA = (2.0 * DEPTH) ** 0.25
LN_EPS = 1e-5
RMS_EPS = 1e-6

LANES = 128
VMEM_LIMIT = 56 * 1024 * 1024
NEG_BIG = -1e30
INT_MIN = -(2 ** 31)

_OFF_AQ = 0
_OFF_CKV = _OFF_AQ + A_HEADS * HEAD_DIM
_OFF_AKR = _OFF_CKV + A_KV_RANK
_OFF_IQ = _OFF_AKR + ROPE_DIM
_OFF_IK = _OFF_IQ + IDX_HEADS * IDX_DIM
_OFF_IW = _OFF_IK + IDX_DIM
_OFF_BQ = _OFF_IW + IDX_HEADS
_OFF_BK = _OFF_BQ + B_HEADS * HEAD_DIM
_OFF_BV = _OFF_BK + B_HEADS * HEAD_DIM
_OFF_GA = _OFF_BV + B_HEADS * HEAD_DIM
_OFF_GB = _OFF_GA + D_MODEL
_IN_TOTAL = _OFF_GB + D_MODEL

_W_QAR, _W_QAN, _W_CKV, _W_AKR = 128, A_HEADS * A_NOPE, A_KV_RANK, 128
_W_IQ, _W_IK, _W_IW = 256, 256, 128
_W_B = B_HEADS * HEAD_DIM
_C_QAR = 0
_C_QAN = _C_QAR + _W_QAR
_C_CKV = _C_QAN + _W_QAN
_C_AKR = _C_CKV + _W_CKV
_C_IQ = _C_AKR + _W_AKR
_C_IK = _C_IQ + _W_IQ
_C_IW = _C_IK + _W_IK
_C_BQ = _C_IW + _W_IW
_C_BK = _C_BQ + _W_B
_C_BV = _C_BK + _W_B
_C_GA = _C_BV + _W_B
_C_GB = _C_GA + D_MODEL
_P_TOTAL = _C_GB + D_MODEL


def _idx_lane(l):
    if l < 16:
        return l // 4, l % 4
    if l < 64:
        return (l - 16) // 12, 8 + (l - 16) % 12
    if l < 80:
        return (l - 64) // 4, 4 + (l - 64) % 4
    return (l - 80) // 12, 20 + (l - 80) % 12


def _b_lane(l):
    half, r = l // 64, l % 64
    which, rr = r // 32, r % 32
    if rr < 8:
        return which, half * 8 + rr
    return which, 16 + half * 24 + (rr - 8)


def _build_layout():
    cols = np.zeros((_P_TOTAL,), np.int32)
    keep = np.ones((_P_TOTAL,), np.float32)
    for l in range(128):
        half, h, f = l // 64, (l % 64) // 8, l % 8
        cols[_C_QAR + l] = _OFF_AQ + h * HEAD_DIM + half * 8 + f
        cols[_C_AKR + l] = _OFF_AKR + half * 8 + f
    for h in range(A_HEADS):
        for j in range(A_NOPE):
            cols[_C_QAN + h * A_NOPE + j] = _OFF_AQ + h * HEAD_DIM + ROPE_DIM + j
    cols[_C_CKV:_C_CKV + _W_CKV] = _OFF_CKV + np.arange(_W_CKV)
    for sl in range(2):
        for l in range(128):
            hh, d = _idx_lane(l)
            cols[_C_IQ + sl * 128 + l] = _OFF_IQ + (sl * 4 + hh) * IDX_DIM + d
            cols[_C_IK + sl * 128 + l] = _OFF_IK + d
    cols[_C_IW:_C_IW + IDX_HEADS] = _OFF_IW + np.arange(IDX_HEADS)
    keep[_C_IW + IDX_HEADS:_C_IW + _W_IW] = 0.0
    for p in range(B_HEADS // 2):
        for l in range(128):
            which, d = _b_lane(l)
            h = 2 * p + which
            cols[_C_BQ + p * 128 + l] = _OFF_BQ + h * HEAD_DIM + d
            cols[_C_BK + p * 128 + l] = _OFF_BK + h * HEAD_DIM + d
    cols[_C_BV:_C_BV + _W_B] = _OFF_BV + np.arange(_W_B)
    cols[_C_GA:_C_GA + D_MODEL] = _OFF_GA + np.arange(D_MODEL)
    cols[_C_GB:_C_GB + D_MODEL] = _OFF_GB + np.arange(D_MODEL)

    fa = np.array([l % 8 for l in range(128)])
    sa = np.array([-1.0 if l < 64 else 1.0 for l in range(128)], np.float32)
    fi = np.full((128,), -1)
    si = np.zeros((128,), np.float32)
    fb = np.full((128,), -1)
    sb = np.zeros((128,), np.float32)
    for l in range(128):
        if l < 16 or 64 <= l < 80:
            fi[l] = l % 4
            si[l] = -1.0 if l < 64 else 1.0
        if l % 32 < 8:
            fb[l] = l % 32
            sb[l] = -1.0 if l < 64 else 1.0

    m_ar = np.zeros((A_HEADS, 128), np.float32)
    for l in range(128):
        m_ar[(l % 64) // 8, l] = 1.0
    m_an = np.zeros((A_HEADS, _W_QAN), np.float32)
    for h in range(A_HEADS):
        m_an[h, h * A_NOPE:(h + 1) * A_NOPE] = 1.0
    m_iq = np.zeros((IDX_HEADS, _W_IQ), np.float32)
    for sl in range(2):
        for l in range(128):
            m_iq[sl * 4 + _idx_lane(l)[0], sl * 128 + l] = 1.0
    m_bq = np.zeros((B_GROUP_HEADS, B_GROUP_W), np.float32)
    for p in range(2):
        for l in range(128):
            m_bq[2 * p + _b_lane(l)[0], p * 128 + l] = 1.0
    m_bv = np.zeros((B_GROUP_HEADS, B_GROUP_W), np.float32)
    for j in range(B_GROUP_HEADS):
        m_bv[j, j * HEAD_DIM:(j + 1) * HEAD_DIM] = 1.0
    return dict(cols=cols, keep=keep, fa=fa, sa=sa, fi=fi, si=si, fb=fb, sb=sb,
                m_ar=m_ar, m_an=m_an, m_iq=m_iq, m_bq=m_bq, m_bv=m_bv)


_LAYOUT = _build_layout()


def _cparams(*sem):
    return pltpu.CompilerParams(dimension_semantics=sem, vmem_limit_bytes=VMEM_LIMIT)


def _layer_norm(v, g, b):
    mu = jnp.mean(v, axis=-1, keepdims=True)
    var = jnp.mean(jnp.square(v - mu), axis=-1, keepdims=True)
    return (v - mu) * lax.rsqrt(var + LN_EPS) * g + b


def _dot_nt(a, b):
    return lax.dot_general(a, b, (((1,), (1,)), ((), ())), preferred_element_type=F32)


def _rope_table_kernel(pos_ref, freq_ref, sign_ref, out_ref):
    pos = pos_ref[...].astype(F32)
    for k in range(3):
        ang = pos * freq_ref[k:k + 1, :]
        out_ref[:, (2 * k) * LANES:(2 * k + 1) * LANES] = jnp.cos(ang)
        out_ref[:, (2 * k + 1) * LANES:(2 * k + 2) * LANES] = jnp.sin(ang) * sign_ref[k:k + 1, :]


def _rope_tables(pos_col, freq, sign, tile):
    n = pos_col.shape[0]
    return pl.pallas_call(
        _rope_table_kernel,
        out_shape=jax.ShapeDtypeStruct((n, 6 * LANES), F32),
        grid=(n // tile,),
        in_specs=[pl.BlockSpec((tile, 1), lambda i: (i, 0)),
                  pl.BlockSpec((3, LANES), lambda i: (0, 0)),
                  pl.BlockSpec((3, LANES), lambda i: (0, 0))],
        out_specs=pl.BlockSpec((tile, 6 * LANES), lambda i: (i, 0)),
        compiler_params=_cparams("parallel"),
        name="rope_tables",
    )(pos_col, freq, sign)


def _rope_slabs(y, cos, sin):
    outs = []
    for s in range(y.shape[1] // LANES):
        ys = y[:, s * LANES:(s + 1) * LANES]
        outs.append(ys * cos + pltpu.roll(ys, 64, 1) * sin)
    return outs[0] if len(outs) == 1 else jnp.concatenate(outs, axis=1)


def _store_residue_major(out_ref, y, scr_ref, dil):
    if dil == 1:
        out_ref[...] = y.astype(out_ref.dtype)
        return
    rows, width = y.shape[0] // dil, y.shape[1]
    for c in range(width // LANES):
        scr_ref[c] = y[:, c * LANES:(c + 1) * LANES]
    for r in range(dil):
        for c in range(width // LANES):
            lanes = slice(r * width + c * LANES, r * width + (c + 1) * LANES)
            out_ref[:, lanes] = scr_ref[c, pl.ds(r, rows, stride=dil), :].astype(out_ref.dtype)


def _load_token_major(ref, scr_ref, dil):
    if dil == 1:
        return ref[...]
    rows, width = ref.shape[0], ref.shape[1] // dil
    for r in range(dil):
        for c in range(width // LANES):
            lanes = slice(r * width + c * LANES, r * width + (c + 1) * LANES)
            scr_ref[c, pl.ds(r, rows, stride=dil), :] = ref[:, lanes]
    return jnp.concatenate([scr_ref[c] for c in range(width // LANES)], axis=1)


def _proj_kernel(x_ref, w_ref, tab_ref, gkv_ref,
                 qa_ref, ka_ref, ckvt_ref, iq_ref, ik_ref, iw_ref, *rest):
    b_refs, scr_ref = rest[:-1], rest[-1]
    xb = x_ref[...].astype(BF16)

    def proj(c0, width):
        return jnp.dot(xb, w_ref[:, c0:c0 + width], preferred_element_type=F32)

    cos_a, sin_a = tab_ref[:, 0:128], tab_ref[:, 128:256]
    cos_i, sin_i = tab_ref[:, 256:384], tab_ref[:, 384:512]
    cos_b, sin_b = tab_ref[:, 512:640], tab_ref[:, 640:768]

    qa_ref[:, 0:_W_QAR] = _rope_slabs(proj(_C_QAR, _W_QAR), cos_a, sin_a).astype(BF16)
    qa_ref[:, _W_QAR:] = proj(_C_QAN, _W_QAN).astype(BF16)
    ckv = proj(_C_CKV, _W_CKV)
    ckv = ckv * lax.rsqrt(jnp.mean(jnp.square(ckv), axis=-1, keepdims=True) + RMS_EPS) * gkv_ref[...]
    ka_ref[:, 0:_W_CKV] = ckv.astype(BF16)
    ckvt_ref[...] = ckv.T.astype(BF16)
    ka_ref[:, _W_CKV:] = _rope_slabs(proj(_C_AKR, _W_AKR), cos_a, sin_a).astype(BF16)
    iq_ref[...] = _rope_slabs(proj(_C_IQ, _W_IQ), cos_i, sin_i).astype(BF16)
    ik_ref[...] = _rope_slabs(proj(_C_IK, _W_IK), cos_i, sin_i).astype(BF16)
    iw_ref[...] = proj(_C_IW, _W_IW) * ((IDX_HEADS * IDX_DIM) ** -0.5)
    ng = len(B_PATTERNS)
    for kind, c0 in enumerate((_C_BQ, _C_BK, _C_BV)):
        for g, (_, dil) in enumerate(B_PATTERNS):
            y = proj(c0 + g * B_GROUP_W, B_GROUP_W)
            if kind < 2:
                y = _rope_slabs(y, cos_b, sin_b)
            _store_residue_major(b_refs[kind * ng + g], y, scr_ref, dil)


def _input_projection(xf, w_perm, tables, g_kv, tile):
    n = xf.shape[0]
    row = lambda width: pl.BlockSpec((tile, width), lambda i: (i, 0))
    full = lambda a: pl.BlockSpec(a.shape, lambda i: (0,) * a.ndim)
    out_w = [(_W_QAR + _W_QAN, BF16), (_W_CKV + _W_AKR, BF16), None, (_W_IQ, BF16), (_W_IK, BF16),
             (_W_IW, F32)]
    shapes = [jax.ShapeDtypeStruct((A_KV_RANK, n), BF16) if o is None else jax.ShapeDtypeStruct((n, o[0]), o[1])
              for o in out_w]
    specs = [pl.BlockSpec((A_KV_RANK, tile), lambda i: (0, i)) if o is None else row(o[0]) for o in out_w]
    for _ in range(3):
        for _, dil in B_PATTERNS:
            shapes.append(jax.ShapeDtypeStruct((n // dil, dil * B_GROUP_W), BF16))
            specs.append(pl.BlockSpec((tile // dil, dil * B_GROUP_W), lambda i: (i, 0)))
    return pl.pallas_call(
        _proj_kernel,
        out_shape=shapes,
        grid=(n // tile,),
        in_specs=[row(D_MODEL), full(w_perm), row(6 * LANES), full(g_kv)],
        out_specs=specs,
        scratch_shapes=[pltpu.VMEM((B_GROUP_W // LANES, tile, LANES), F32)],
        compiler_params=_cparams("parallel"),
        name="input_projection",
    )(xf, w_perm, tables, g_kv)


def _dilated_kernel(q_ref, kc_ref, kp_ref, vc_ref, vp_ref, mq_ref, mv_ref, o_ref, lse_ref,
                    kwin_ref, vwin_ref, *, tq):
    first = pl.program_id(2) == 0
    kwin_ref[0:Q_BLOCK, :] = kp_ref[...]
    kwin_ref[Q_BLOCK:, :] = kc_ref[...]
    vwin_ref[0:Q_BLOCK, :] = vp_ref[...]
    vwin_ref[Q_BLOCK:, :] = vc_ref[...]
    t = lax.broadcasted_iota(jnp.int32, (Q_BLOCK, 2 * Q_BLOCK), 0)
    c = lax.broadcasted_iota(jnp.int32, (Q_BLOCK, 2 * Q_BLOCK), 1)
    diff = t + Q_BLOCK - c
    band = (diff >= 0) & (diff <= Q_BLOCK)
    scale = HEAD_DIM ** -0.5
    for sb in range(tq // Q_BLOCK):
        valid = band
        if sb == 0:
            valid = band & (c >= jnp.where(first, Q_BLOCK, 0))
        bias = jnp.where(valid, 0.0, NEG_BIG).astype(F32)
        q = q_ref[sb * Q_BLOCK:(sb + 1) * Q_BLOCK, :] * scale
        kw = kwin_ref[sb * Q_BLOCK:(sb + 2) * Q_BLOCK, :]
        vw = vwin_ref[sb * Q_BLOCK:(sb + 2) * Q_BLOCK, :]
        qs = jnp.concatenate([q * mq_ref[j:j + 1, :] for j in range(B_GROUP_HEADS)], axis=0)
        s = _dot_nt(qs, kw)
        o_acc = jnp.zeros((Q_BLOCK, B_GROUP_W), F32)
        lse_acc = jnp.zeros((Q_BLOCK, B_GROUP_W), F32)
        for j in range(B_GROUP_HEADS):
            sj = s[j * Q_BLOCK:(j + 1) * Q_BLOCK, :] + bias
            m = jnp.max(sj, axis=-1, keepdims=True)
            e = jnp.exp(sj - m)
            den = jnp.sum(e, axis=-1, keepdims=True)
            pv = jnp.dot(e.astype(BF16), vw, preferred_element_type=F32)
            mv = mv_ref[j:j + 1, :]
            o_acc = o_acc + (pv / den) * mv
            lse_acc = lse_acc + (m + jnp.log(den)) * mv
        o_ref[sb * Q_BLOCK:(sb + 1) * Q_BLOCK, :] = o_acc
        lse_ref[sb * Q_BLOCK:(sb + 1) * Q_BLOCK, :] = lse_acc


def _dilated_group(bq, bk, bv, mq, mv, b, dil):
    sub = bq.shape[0] // b
    tq = min(512, sub)
    nblk = tq // Q_BLOCK
    view = lambda a: a.reshape(b, sub, dil * B_GROUP_W)
    cur = pl.BlockSpec((None, tq, B_GROUP_W), lambda bi, r, i: (bi, i, r))
    prev = pl.BlockSpec((None, Q_BLOCK, B_GROUP_W),
                        lambda bi, r, i: (bi, jnp.maximum(i * nblk - 1, 0), r))
    const = lambda a: pl.BlockSpec(a.shape, lambda bi, r, i: (0, 0))
    out = cur
    o, lse = pl.pallas_call(
        functools.partial(_dilated_kernel, tq=tq),
        out_shape=[jax.ShapeDtypeStruct((b, sub, dil * B_GROUP_W), F32)] * 2,
        grid=(b, dil, sub // tq),
        in_specs=[cur, cur, prev, cur, prev, const(mq), const(mv)],
        out_specs=[out, out],
        scratch_shapes=[pltpu.VMEM((tq + Q_BLOCK, B_GROUP_W), BF16)] * 2,
        compiler_params=_cparams("parallel", "parallel", "arbitrary"),
        name=f"dilated_attention_d{dil}",
    )(view(bq), view(bk), view(bk), view(bv), view(bv), mq, mv)
    return o.reshape(b * sub, dil * B_GROUP_W), lse.reshape(b * sub, dil * B_GROUP_W)


_TK = 512

def _fold_rows(x, op, slab=8):
    parts = [x[r:r + slab, :] for r in range(0, x.shape[0], slab)]
    while len(parts) > 1:
        parts = [op(parts[k], parts[k + 1]) for k in range(0, len(parts) - 1, 2)] + parts[len(parts) & ~1:]
    return parts[0]


def _skewed_tiles(n_tiles, produce, consume, buf_a, buf_b, carry):
    produce(0, buf_a)

    def pair(t, c):
        j = 2 * t
        produce(j + 1, buf_b)
        c = consume(j, buf_a, c)
        produce(j + 2, buf_a)
        return consume(j + 1, buf_b, c)

    n_pairs = (n_tiles - 1) // 2
    carry = lax.fori_loop(0, n_pairs, pair, carry)
    j = 2 * n_pairs

    def last_two(c):
        produce(j + 1, buf_b)
        return consume(j + 1, buf_b, consume(j, buf_a, c))

    return lax.cond(n_tiles - j == 2, last_two, lambda c: consume(j, buf_a, c), carry)


def _dsa_kernel(qa_ref, iq_ref, iw_ref, ikt_ref, ka_ref, ckvt_ref, wuk_ref, wuv_ref, mar_ref, man_ref, miq_ref,
                out_ref, key_ref, tie_ref, iqs_ref, qcat_ref, acc_ref, sa_ref, sb_ref, ma_ref, mb_ref,
                p_ref, *, n_sel):
    i = pl.program_id(1)
    q0 = i * Q_BLOCK
    n_keys = q0 + Q_BLOCK
    rows = A_HEADS * Q_BLOCK

    n_tiles = (n_keys + _TK - 1) // _TK
    tq_lane = q0 + lax.broadcasted_iota(jnp.int32, (_TK, Q_BLOCK), 1)
    krow = lax.broadcasted_iota(jnp.int32, (_TK, Q_BLOCK), 0)

    iq = iq_ref[...]
    for h in range(IDX_HEADS):
        iqs_ref[:, h * Q_BLOCK:(h + 1) * Q_BLOCK] = (iq * miq_ref[h:h + 1, :]).astype(F32).T.astype(BF16)
    iw_t = iw_ref[...].T

    def score_matmul(j, buf):
        k0 = pl.multiple_of(j * _TK, _TK)
        s = jnp.dot(ikt_ref[pl.ds(k0, _TK), :], iqs_ref[...], preferred_element_type=F32)
        for h in range(IDX_HEADS):
            buf[0][h] = s[:, h * Q_BLOCK:(h + 1) * Q_BLOCK]

    def score_keys(j, buf, carry):
        k0 = pl.multiple_of(j * _TK, _TK)
        sc = jnp.zeros((_TK, Q_BLOCK), F32)
        for h in range(IDX_HEADS):
            sc = sc + jnp.maximum(buf[0][h], 0.0) * iw_t[h:h + 1, :]
        bits = lax.bitcast_convert_type(jnp.where(sc == 0.0, 0.0, sc), jnp.int32)
        okey = bits ^ ((bits >> 31) & jnp.int32(0x7FFFFFFF))
        key_ref[pl.ds(k0, _TK), :] = jnp.where(krow + k0 <= tq_lane, okey, jnp.int32(INT_MIN))
        return carry

    _skewed_tiles(n_tiles, score_matmul, score_keys, (sa_ref, ma_ref), (sb_ref, mb_ref), 0)

    def count_keys(pred):
        def body(j, cnt):
            k0 = pl.multiple_of(j * _TK, _TK)
            hit = jnp.where(pred(key_ref[pl.ds(k0, _TK), :], k0), 1.0, 0.0)
            return cnt + _fold_rows(hit, jnp.add)
        cnt8 = lax.fori_loop(0, n_tiles, body, jnp.zeros((8, Q_BLOCK), F32))
        return jnp.sum(cnt8, axis=0, keepdims=True)

    def bit_step(b, carry):
        prefix, n_ge = carry
        trial = prefix | (jnp.int32(1) << (31 - b))
        t = trial ^ jnp.int32(INT_MIN)
        cnt = count_keys(lambda keys, k0: keys >= t)
        take = cnt >= float(n_sel)
        return jnp.where(take, trial, prefix), jnp.where(take, cnt, n_ge)

    prefix, n_ge = lax.fori_loop(0, 32, bit_step, (jnp.zeros((1, Q_BLOCK), jnp.int32),
                                                   jnp.full((1, Q_BLOCK), float(n_sel), F32)))
    thr = prefix ^ jnp.int32(INT_MIN)

    surplus = n_ge - float(n_sel)
    max_surplus = jnp.max(surplus)

    @pl.when(max_surplus == 1.0)
    def _():
        def last_tie(j, best):
            k0 = pl.multiple_of(j * _TK, _TK)
            pos = jnp.where(key_ref[pl.ds(k0, _TK), :] == thr, krow + k0, -1)
            return jnp.maximum(best, _fold_rows(pos, jnp.maximum))
        best8 = lax.fori_loop(0, n_tiles, last_tie, jnp.full((8, Q_BLOCK), -1, jnp.int32))
        drop_at = jnp.where(surplus > 0.0, jnp.max(best8, axis=0, keepdims=True), -1)

        def demote_one(j, carry):
            k0 = pl.multiple_of(j * _TK, _TK)
            keys = key_ref[pl.ds(k0, _TK), :]
            key_ref[pl.ds(k0, _TK), :] = jnp.where(krow + k0 == drop_at, jnp.int32(INT_MIN), keys)
            return carry

        lax.fori_loop(0, n_tiles, demote_one, 0)

    @pl.when(max_surplus > 1.0)
    def _():
        need = float(n_sel) - count_keys(lambda keys, k0: keys > thr)
        index_bits = (key_ref.shape[0] - 1).bit_length()
        not_tied = jnp.int32(1 << index_bits)

        def tie_positions(j, carry):
            k0 = pl.multiple_of(j * _TK, _TK)
            tied = key_ref[pl.ds(k0, _TK), :] == thr
            tie_ref[pl.ds(k0, _TK), :] = jnp.where(tied, krow + k0, not_tied)
            return carry

        lax.fori_loop(0, n_tiles, tie_positions, 0)

        def count_ties_below(bound):
            def body(j, cnt):
                k0 = pl.multiple_of(j * _TK, _TK)
                hit = jnp.where(tie_ref[pl.ds(k0, _TK), :] < bound, 1.0, 0.0)
                return cnt + _fold_rows(hit, jnp.add)
            cnt8 = lax.fori_loop(0, n_tiles, body, jnp.zeros((8, Q_BLOCK), F32))
            return jnp.sum(cnt8, axis=0, keepdims=True)

        def index_bit(b, bound):
            trial = bound | (jnp.int32(1) << (index_bits - 1 - b))
            return jnp.where(count_ties_below(trial) < need, trial, bound)

        last = lax.fori_loop(0, index_bits, index_bit, jnp.zeros((1, Q_BLOCK), jnp.int32))

        def demote(j, carry):
            k0 = pl.multiple_of(j * _TK, _TK)
            pos = tie_ref[pl.ds(k0, _TK), :]
            drop = (pos > last) & (pos < not_tied)
            key_ref[pl.ds(k0, _TK), :] = jnp.where(drop, jnp.int32(INT_MIN), key_ref[pl.ds(k0, _TK), :])
            return carry

        lax.fori_loop(0, n_tiles, demote, 0)

    q_rope = qa_ref[:, 0:_W_QAR]
    q_nope = qa_ref[:, _W_QAR:]
    scale = HEAD_DIM ** -0.5
    c_rope = A_KV_RANK
    for h in range(A_HEADS):
        q_lat = jnp.dot(q_nope * man_ref[h:h + 1, :], wuk_ref[...], preferred_element_type=F32)
        cols = slice(h * Q_BLOCK, (h + 1) * Q_BLOCK)
        qcat_ref[0:c_rope, cols] = (q_lat.astype(BF16) * scale).astype(F32).T.astype(BF16)
        qcat_ref[c_rope:, cols] = (q_rope * mar_ref[h:h + 1, :] * scale).astype(F32).T.astype(BF16)

    acc_ref[...] = jnp.zeros(acc_ref.shape, F32)

    def logit_matmul(j, buf):
        s_buf, mx_buf = buf
        k0 = pl.multiple_of(j * _TK, _TK)
        sel = (key_ref[pl.ds(k0, _TK), :] >= thr) & (krow + k0 <= tq_lane)
        bias = jnp.where(sel, 0.0, NEG_BIG).astype(F32)
        s = jnp.dot(ka_ref[pl.ds(k0, _TK), :], qcat_ref[...], preferred_element_type=F32)
        for h in range(A_HEADS):
            sh = s[:, h * Q_BLOCK:(h + 1) * Q_BLOCK] + bias
            s_buf[h] = sh
            mx_buf[:, h * Q_BLOCK:(h + 1) * Q_BLOCK] = _fold_rows(sh, jnp.maximum)

    def softmax_pv(j, buf, carry):
        s_buf, mx_buf = buf
        m_old, l_old = carry
        k0 = pl.multiple_of(j * _TK, _TK)
        ckv_t = ckvt_ref[:, pl.ds(k0, _TK)]
        m_parts, l_parts = [], []
        for c in range(A_HEADS // 2):
            a_parts = []
            for h in (2 * c, 2 * c + 1):
                cols = slice(h * Q_BLOCK, (h + 1) * Q_BLOCK)
                m_h = jnp.maximum(m_old[:, cols], jnp.max(mx_buf[:, cols], axis=0, keepdims=True))
                a_h = jnp.exp(m_old[:, cols] - m_h)
                p = jnp.exp(s_buf[h] - m_h)
                p_ref[c, :, (h % 2) * Q_BLOCK:(h % 2 + 1) * Q_BLOCK] = p.astype(BF16)
                l_parts.append(a_h * l_old[:, cols] + jnp.sum(_fold_rows(p, jnp.add), axis=0, keepdims=True))
                m_parts.append(m_h)
                a_parts.append(a_h)
            pv = jnp.dot(ckv_t, p_ref[c], preferred_element_type=F32)
            acc_ref[c] = acc_ref[c] * jnp.concatenate(a_parts, axis=1) + pv
        return jnp.concatenate(m_parts, axis=1), jnp.concatenate(l_parts, axis=1)

    _, l_fin = _skewed_tiles(n_tiles, logit_matmul, softmax_pv, (sa_ref, ma_ref), (sb_ref, mb_ref),
                             (jnp.full((1, rows), NEG_BIG, F32), jnp.zeros((1, rows), F32)))

    inv_l = 1.0 / l_fin
    out_t = jnp.zeros((A_HEADS * HEAD_DIM, Q_BLOCK), F32)
    for h in range(A_HEADS):
        lanes = slice((h % 2) * Q_BLOCK, (h % 2 + 1) * Q_BLOCK)
        o_lat = (acc_ref[h // 2][:, lanes] * inv_l[:, h * Q_BLOCK:(h + 1) * Q_BLOCK]).astype(BF16)
        out_t = out_t + jnp.dot(wuv_ref[h], o_lat, preferred_element_type=F32)
    out_ref[...] = out_t.T.astype(BF16)


def _dsa_mixer(qa, iq, iw, ikt, ka, ckvt, wuk, wuv_t, m_ar, m_an, m_iq, b, s):
    n_sel = min(A_TOPK_MAX, s // 4)
    nq = s // Q_BLOCK
    rows = A_HEADS * Q_BLOCK
    blk = lambda width: pl.BlockSpec((Q_BLOCK, width), lambda bi, i: (bi * nq + i, 0))
    seq = lambda width: pl.BlockSpec((s, width), lambda bi, i: (bi, 0))
    const = lambda a: pl.BlockSpec(a.shape, lambda bi, i: (0,) * a.ndim)
    return pl.pallas_call(
        functools.partial(_dsa_kernel, n_sel=n_sel),
        out_shape=jax.ShapeDtypeStruct((b * s, A_HEADS * HEAD_DIM), BF16),
        grid=(b, nq),
        in_specs=[blk(_W_QAR + _W_QAN), blk(_W_IQ), blk(_W_IW), seq(_W_IK), seq(_W_CKV + _W_AKR),
                  pl.BlockSpec((A_KV_RANK, s), lambda bi, i: (0, bi)),
                  const(wuk), const(wuv_t), const(m_ar), const(m_an), const(m_iq)],
        out_specs=blk(A_HEADS * HEAD_DIM),
        scratch_shapes=[pltpu.VMEM((s, Q_BLOCK), jnp.int32),
                        pltpu.VMEM((s, Q_BLOCK), jnp.int32),
                        pltpu.VMEM((_W_IQ, rows), BF16),
                        pltpu.VMEM((A_KV_RANK + _W_AKR, rows), BF16),
                        pltpu.VMEM((A_HEADS // 2, A_KV_RANK, 2 * Q_BLOCK), F32),
                        pltpu.VMEM((A_HEADS, _TK, Q_BLOCK), F32),
                        pltpu.VMEM((A_HEADS, _TK, Q_BLOCK), F32),
                        pltpu.VMEM((8, rows), F32),
                        pltpu.VMEM((8, rows), F32),
                        pltpu.VMEM((A_HEADS // 2, _TK, 2 * Q_BLOCK), BF16)],
        compiler_params=_cparams("parallel", "arbitrary"),
        name="dsa_attention",
    )(qa, iq, iw, ikt, ka, ckvt, wuk, wuv_t, m_ar, m_an, m_iq)


def _merge_kernel(x_ref, a_ref, o1_ref, o2_ref, o3_ref, l1_ref, l2_ref, l3_ref, wg_ref, bg_ref,
                  wa_ref, wb_ref, wo_ref, lg_ref, lb_ref, h_ref, *scr):
    dils = [dil for _, dil in B_PATTERNS]
    lses = [_load_token_major(r, scr[2 * g], dils[g]) for g, r in enumerate((l1_ref, l2_ref, l3_ref))]
    outs = [_load_token_major(r, scr[2 * g + 1], dils[g]) for g, r in enumerate((o1_ref, o2_ref, o3_ref))]
    mx = jnp.maximum(jnp.maximum(lses[0], lses[1]), lses[2])
    es = [jnp.exp(l - mx) for l in lses]
    den = es[0] + es[1] + es[2]
    b_out = (es[0] / den) * outs[0] + (es[1] / den) * outs[1] + (es[2] / den) * outs[2]
    ya = jnp.dot(a_ref[...], wa_ref[...], preferred_element_type=F32)
    yb = jnp.dot(b_out.astype(BF16), wb_ref[...], preferred_element_type=F32)
    x = x_ref[...]
    gates = jax.nn.sigmoid(jnp.dot(x.astype(BF16), wg_ref[...], preferred_element_type=F32) + bg_ref[...])
    pre = gates[:, 0:D_MODEL] * ya + gates[:, D_MODEL:] * yb
    mix = jnp.dot(pre.astype(BF16), wo_ref[...], preferred_element_type=F32)
    h_ref[...] = _layer_norm(ALPHA * x + mix, lg_ref[...], lb_ref[...])


def _merge(xf, a_out, b_parts, w_gate, b_gate, wa, wb, wo, ln_g, ln_b, tile):
    n = xf.shape[0]
    row = lambda width: pl.BlockSpec((tile, width), lambda i: (i, 0))
    full = lambda a: pl.BlockSpec(a.shape, lambda i: (0,) * a.ndim)
    (o1, l1), (o2, l2), (o3, l3) = b_parts
    grp = [pl.BlockSpec((tile // dil, dil * B_GROUP_W), lambda i: (i, 0)) for _, dil in B_PATTERNS]
    return pl.pallas_call(
        _merge_kernel,
        out_shape=jax.ShapeDtypeStruct((n, D_MODEL), F32),
        grid=(n // tile,),
        in_specs=[row(D_MODEL), row(A_HEADS * HEAD_DIM)] + grp + grp + [full(w_gate), full(b_gate),
                  full(wa), full(wb), full(wo), full(ln_g), full(ln_b)],
        out_specs=row(D_MODEL),
        scratch_shapes=[pltpu.VMEM((B_GROUP_W // LANES, tile, LANES), F32)] * (2 * len(B_PATTERNS)),
        compiler_params=_cparams("parallel"),
        name="merge_output_projection",
    )(xf, a_out, o1, o2, o3, l1, l2, l3, w_gate, b_gate, wa, wb, wo, ln_g, ln_b)


def _first_max(v):
    m = jnp.max(v, axis=0, keepdims=True)
    idx = lax.broadcasted_iota(jnp.int32, v.shape, 0)
    first = jnp.min(jnp.where(v == m, idx, v.shape[0]), axis=0, keepdims=True)
    return m, idx == first


def _router_kernel(h_ref, wr_ref, rb_ref, gate_ref):
    t = h_ref.shape[0]
    gs = N_EXPERTS // N_GROUPS
    logits = lax.dot_general(wr_ref[...], h_ref[...], (((1,), (1,)), ((), ())),
                             precision=lax.Precision.HIGHEST, preferred_element_type=F32)
    scores = jax.nn.sigmoid(logits)
    biased = scores + rb_ref[...]
    gscores = []
    for g in range(N_GROUPS):
        blk = biased[g * gs:(g + 1) * gs, :]
        m1, hit = _first_max(blk)
        m2 = jnp.max(jnp.where(hit, -jnp.inf, blk), axis=0, keepdims=True)
        gscores.append(m1 + m2)
    gscore = jnp.concatenate(gscores, axis=0)
    gsel = jnp.zeros((N_GROUPS, t), F32)
    for _ in range(TOPK_GROUPS):
        _, hit = _first_max(gscore)
        gsel = jnp.where(hit, 1.0, gsel)
        gscore = jnp.where(hit, -jnp.inf, gscore)
    esel = jnp.concatenate([jnp.broadcast_to(gsel[g:g + 1, :], (gs, t)) for g in range(N_GROUPS)], axis=0)
    cand = jnp.where(esel > 0.0, biased, -jnp.inf)
    top_s = jnp.zeros((N_EXPERTS, t), F32)
    for _ in range(TOP_K):
        _, hit = _first_max(cand)
        top_s = jnp.where(hit, scores, top_s)
        cand = jnp.where(hit, -jnp.inf, cand)
    gate = top_s / jnp.sum(top_s, axis=0, keepdims=True) * ROUTED_SCALE
    gate_ref[...] = jnp.concatenate([gate, jnp.zeros((LANES - N_EXPERTS, t), F32)], axis=0).T


def _router(h, wr_t, rb_col, tile):
    n = h.shape[0]
    return pl.pallas_call(
        _router_kernel,
        out_shape=jax.ShapeDtypeStruct((n, LANES), F32),
        grid=(n // tile,),
        in_specs=[pl.BlockSpec((tile, D_MODEL), lambda i: (i, 0)),
                  pl.BlockSpec(wr_t.shape, lambda i: (0, 0)),
                  pl.BlockSpec(rb_col.shape, lambda i: (0, 0))],
        out_specs=pl.BlockSpec((tile, LANES), lambda i: (i, 0)),
        compiler_params=_cparams("parallel"),
        name="moe_router",
    )(h, wr_t, rb_col)


_EXPERTS_PER_STEP = 4


def _swiglu(hb, w1, w3, w2):
    a = jnp.dot(hb, w1, preferred_element_type=F32)
    b = jnp.dot(hb, w3, preferred_element_type=F32)
    return jnp.dot((jax.nn.silu(a) * b).astype(BF16), w2, preferred_element_type=F32)


def _experts_kernel(h_ref, gate_ref, w1_ref, w3_ref, w2_ref, s1_ref, s3_ref, s2_ref, lg_ref, lb_ref,
                    y_ref, hb_ref, acc_ref):
    step = pl.program_id(1)

    @pl.when(step == 0)
    def _():
        hb = h_ref[...].astype(BF16)
        hb_ref[...] = hb
        acc_ref[...] = _swiglu(hb, s1_ref[...], s3_ref[...], s2_ref[...])

    hb = hb_ref[...]
    gate = gate_ref[...]
    lane = lax.broadcasted_iota(jnp.int32, gate.shape, 1)
    for k in range(_EXPERTS_PER_STEP):
        e = step * _EXPERTS_PER_STEP + k
        g_col = jnp.sum(jnp.where(lane == e, gate, 0.0), axis=1, keepdims=True)
        acc_ref[...] += _swiglu(hb, w1_ref[k], w3_ref[k], w2_ref[k]) * g_col

    @pl.when(step == pl.num_programs(1) - 1)
    def _():
        y_ref[...] = _layer_norm(ALPHA * h_ref[...] + acc_ref[...], lg_ref[...], lb_ref[...])


def _experts(h, gate, w1, w3, w2, s1, s3, s2, ln_g, ln_b, tile):
    n = h.shape[0]
    ec = _EXPERTS_PER_STEP
    row = lambda width: pl.BlockSpec((tile, width), lambda i, e: (i, 0))
    full = lambda a: pl.BlockSpec(a.shape, lambda i, e: (0,) * a.ndim)
    wspec = lambda a: pl.BlockSpec((ec,) + a.shape[1:], lambda i, e: (e, 0, 0))
    return pl.pallas_call(
        _experts_kernel,
        out_shape=jax.ShapeDtypeStruct((n, D_MODEL), F32),
        grid=(n // tile, N_EXPERTS // ec),
        in_specs=[row(D_MODEL), row(LANES), wspec(w1), wspec(w3), wspec(w2),
                  full(s1), full(s3), full(s2), full(ln_g), full(ln_b)],
        out_specs=row(D_MODEL),
        scratch_shapes=[pltpu.VMEM((tile, D_MODEL), BF16), pltpu.VMEM((tile, D_MODEL), F32)],
        compiler_params=_cparams("parallel", "arbitrary"),
        name="moe_experts",
    )(h, gate, w1, w3, w2, s1, s3, s2, ln_g, ln_b)


def _rope_freqs():
    inv16 = ROPE_THETA ** (-jnp.arange(0, ROPE_DIM, 2, dtype=F32) / ROPE_DIM)
    inv8 = ROPE_THETA ** (-jnp.arange(0, IDX_ROPE, 2, dtype=F32) / IDX_ROPE)

    def row(inv, fidx):
        return jnp.where(fidx >= 0, inv[np.maximum(fidx, 0)], 0.0)
    lay = _LAYOUT
    freq = jnp.stack([row(inv16, lay["fa"]), row(inv8, lay["fi"]), row(inv16, lay["fb"])])
    sign = jnp.asarray(np.stack([lay["sa"], lay["si"], lay["sb"]]))
    return freq.astype(F32), sign


def _layer(x, positions, w_in, b_gate, g_kv, w_uk, w_uv, w_branch_a, w_branch_b, w_o, ln1_g, ln1_b,
           w_router, router_bias, w1_e, w3_e, w2_e, ws1, ws3, ws2, ln2_g, ln2_b):
    b, s, d = x.shape
    n = b * s
    lay = _LAYOUT
    tile = min(256, n)
    xf = x.reshape(n, d)

    n_proj = _C_GA
    w_perm = (w_in[:, lay["cols"][:n_proj]] * lay["keep"][:n_proj]).astype(BF16)
    w_gate = w_in[:, _OFF_GA:_OFF_GA + 2 * D_MODEL].astype(BF16)
    wuk =jnp.transpose(w_uk, (1, 2, 0)).reshape(A_HEADS * A_NOPE, A_KV_RANK).astype(BF16)
    wuv_t = jnp.zeros((A_HEADS, A_HEADS, HEAD_DIM, A_KV_RANK), F32)
    wuv_t = wuv_t.at[jnp.arange(A_HEADS), jnp.arange(A_HEADS)].set(jnp.transpose(w_uv, (1, 2, 0)))
    wuv_t = wuv_t.reshape(A_HEADS, A_HEADS * HEAD_DIM, A_KV_RANK).astype(BF16)
    masks = {k: jnp.asarray(lay[k], BF16) for k in ("m_ar", "m_an", "m_iq", "m_bq")}
    m_bv = jnp.asarray(lay["m_bv"], F32)

    freq, sign = _rope_freqs()
    tables = _rope_tables(positions.reshape(n, 1), freq, sign, tile)
    qa, ka, ckvt, iq, ikt, iw, *bqkv = _input_projection(xf, w_perm, tables, g_kv.reshape(1, -1), tile)

    a_out = _dsa_mixer(qa, iq, iw, ikt, ka, ckvt, wuk, wuv_t, masks["m_ar"], masks["m_an"], masks["m_iq"], b, s)
    ng = len(B_PATTERNS)
    b_parts = [_dilated_group(bqkv[g], bqkv[ng + g], bqkv[2 * ng + g], masks["m_bq"], m_bv, b, dil)
               for g, (_, dil) in enumerate(B_PATTERNS)]

    h = _merge(xf, a_out, b_parts, w_gate, b_gate.reshape(1, -1), w_branch_a.astype(BF16),
               w_branch_b.astype(BF16), w_o.astype(BF16), ln1_g.reshape(1, -1), ln1_b.reshape(1, -1), tile)

    gate = _router(h, w_router.T, router_bias.reshape(-1, 1), min(1024, n))
    y = _experts(h, gate, w1_e.astype(BF16), w3_e.astype(BF16), w2_e.astype(BF16),
                 ws1.astype(BF16), ws3.astype(BF16), ws2.astype(BF16),
                 ln2_g.reshape(1, -1), ln2_b.reshape(1, -1), min(1024, n))
    return y.reshape(b, s, d)


def kernel(x, positions, w_in, b_gate, g_kv, w_uk, w_uv, w_branch_a, w_branch_b, w_o, ln1_g, ln1_b,
           w_router, router_bias, w1_e, w3_e, w2_e, ws1, ws3, ws2, ln2_g, ln2_b):
    h = x
    for l in range(DEPTH):
        h = _layer(h, positions, w_in[l], b_gate[l], g_kv[l], w_uk[l], w_uv[l], w_branch_a[l],
                   w_branch_b[l], w_o[l], ln1_g[l], ln1_b[l], w_router[l], router_bias[l],
                   w1_e[l], w3_e[l], w2_e[l], ws1[l], ws3[l], ws2[l], ln2_g[l], ln2_b[l])
    return h
```

```python
import functools

import jax
import jax.numpy as jnp
import numpy as np
from jax import lax
from jax.experimental import pallas as pl
from jax.experimental.pallas import tpu as pltpu

F32 = jnp.float32
BF16 = jnp.bfloat16

D_MODEL = 1024
HEAD_DIM = 64
ROPE_DIM = 16
ROPE_THETA = 500000.0
Q_BLOCK = 128
A_HEADS = 8
A_NOPE = HEAD_DIM - ROPE_DIM
A_KV_RANK = 256
A_TOPK_MAX = 256
IDX_HEADS = 8
IDX_DIM = 32
IDX_ROPE = 8
B_PATTERNS = ((128, 1), (512, 4), (2048, 16))
B_GROUP_HEADS = 4
B_HEADS = B_GROUP_HEADS * len(B_PATTERNS)
B_GROUP_W = B_GROUP_HEADS * HEAD_DIM
N_EXPERTS = 64
TOP_K = 8
N_GROUPS = 8
TOPK_GROUPS = 4
D_EXPERT = 256
ROUTED_SCALE = 2.5
DEPTH = 1
ALPHA = (2.0 * DEPTH) ** 0.25
LN_EPS = 1e-5
RMS_EPS = 1e-6

LANES = 128
VMEM_LIMIT = 56 * 1024 * 1024
NEG_BIG = -1e30
INT_MIN = -(2 ** 31)

_OFF_AQ = 0
_OFF_CKV = _OFF_AQ + A_HEADS * HEAD_DIM
_OFF_AKR = _OFF_CKV + A_KV_RANK
_OFF_IQ = _OFF_AKR + ROPE_DIM
_OFF_IK = _OFF_IQ + IDX_HEADS * IDX_DIM
_OFF_IW = _OFF_IK + IDX_DIM
_OFF_BQ = _OFF_IW + IDX_HEADS
_OFF_BK = _OFF_BQ + B_HEADS * HEAD_DIM
_OFF_BV = _OFF_BK + B_HEADS * HEAD_DIM
_OFF_GA = _OFF_BV + B_HEADS * HEAD_DIM
_OFF_GB = _OFF_GA + D_MODEL
_IN_TOTAL = _OFF_GB + D_MODEL

_W_QAR, _W_QAN, _W_CKV, _W_AKR = 128, A_HEADS * A_NOPE, A_KV_RANK, 128
_W_IQ, _W_IK, _W_IW = 256, 256, 128
_W_B = B_HEADS * HEAD_DIM
_C_QAR = 0
_C_QAN = _C_QAR + _W_QAR
_C_CKV = _C_QAN + _W_QAN
_C_AKR = _C_CKV + _W_CKV
_C_IQ = _C_AKR + _W_AKR
_C_IK = _C_IQ + _W_IQ
_C_IW = _C_IK + _W_IK
_C_BQ = _C_IW + _W_IW
_C_BK = _C_BQ + _W_B
_C_BV = _C_BK + _W_B
_C_GA = _C_BV + _W_B
_C_GB = _C_GA + D_MODEL
_P_TOTAL = _C_GB + D_MODEL


def _idx_lane(l):
    if l < 16:
        return l // 4, l % 4
    if l < 64:
        return (l - 16) // 12, 8 + (l - 16) % 12
    if l < 80:
        return (l - 64) // 4, 4 + (l - 64) % 4
    return (l - 80) // 12, 20 + (l - 80) % 12


def _b_lane(l):
    half, r = l // 64, l % 64
    which, rr = r // 32, r % 32
    if rr < 8:
        return which, half * 8 + rr
    return which, 16 + half * 24 + (rr - 8)


def _build_layout():
    cols = np.zeros((_P_TOTAL,), np.int32)
    keep = np.ones((_P_TOTAL,), np.float32)
    for l in range(128):
        half, h, f = l // 64, (l % 64) // 8, l % 8
        cols[_C_QAR + l] = _OFF_AQ + h * HEAD_DIM + half * 8 + f
        cols[_C_AKR + l] = _OFF_AKR + half * 8 + f
    for h in range(A_HEADS):
        for j in range(A_NOPE):
            cols[_C_QAN + h * A_NOPE + j] = _OFF_AQ + h * HEAD_DIM + ROPE_DIM + j
    cols[_C_CKV:_C_CKV + _W_CKV] = _OFF_CKV + np.arange(_W_CKV)
    for sl in range(2):
        for l in range(128):
            hh, d = _idx_lane(l)
            cols[_C_IQ + sl * 128 + l] = _OFF_IQ + (sl * 4 + hh) * IDX_DIM + d
            cols[_C_IK + sl * 128 + l] = _OFF_IK + d
    cols[_C_IW:_C_IW + IDX_HEADS] = _OFF_IW + np.arange(IDX_HEADS)
    keep[_C_IW + IDX_HEADS:_C_IW + _W_IW] = 0.0
    for p in range(B_HEADS // 2):
        for l in range(128):
            which, d = _b_lane(l)
            h = 2 * p + which
            cols[_C_BQ + p * 128 + l] = _OFF_BQ + h * HEAD_DIM + d
            cols[_C_BK + p * 128 + l] = _OFF_BK + h * HEAD_DIM + d
    cols[_C_BV:_C_BV + _W_B] = _OFF_BV + np.arange(_W_B)
    cols[_C_GA:_C_GA + D_MODEL] = _OFF_GA + np.arange(D_MODEL)
    cols[_C_GB:_C_GB + D_MODEL] = _OFF_GB + np.arange(D_MODEL)

    fa = np.array([l % 8 for l in range(128)])
    sa = np.array([-1.0 if l < 64 else 1.0 for l in range(128)], np.float32)
    fi = np.full((128,), -1)
    si = np.zeros((128,), np.float32)
    fb = np.full((128,), -1)
    sb = np.zeros((128,), np.float32)
    for l in range(128):
        if l < 16 or 64 <= l < 80:
            fi[l] = l % 4
            si[l] = -1.0 if l < 64 else 1.0
        if l % 32 < 8:
            fb[l] = l % 32
            sb[l] = -1.0 if l < 64 else 1.0

    m_ar = np.zeros((A_HEADS, 128), np.float32)
    for l in range(128):
        m_ar[(l % 64) // 8, l] = 1.0
    m_an = np.zeros((A_HEADS, _W_QAN), np.float32)
    for h in range(A_HEADS):
        m_an[h, h * A_NOPE:(h + 1) * A_NOPE] = 1.0
    m_iq = np.zeros((IDX_HEADS, _W_IQ), np.float32)
    for sl in range(2):
        for l in range(128):
            m_iq[sl * 4 + _idx_lane(l)[0], sl * 128 + l] = 1.0
    m_bq = np.zeros((B_GROUP_HEADS, B_GROUP_W), np.float32)
    for p in range(2):
        for l in range(128):
            m_bq[2 * p + _b_lane(l)[0], p * 128 + l] = 1.0
    m_bv = np.zeros((B_GROUP_HEADS, B_GROUP_W), np.float32)
    for j in range(B_GROUP_HEADS):
        m_bv[j, j * HEAD_DIM:(j + 1) * HEAD_DIM] = 1.0
    return dict(cols=cols, keep=keep, fa=fa, sa=sa, fi=fi, si=si, fb=fb, sb=sb,
                m_ar=m_ar, m_an=m_an, m_iq=m_iq, m_bq=m_bq, m_bv=m_bv)


_LAYOUT = _build_layout()


def _cparams(*sem):
    return pltpu.CompilerParams(dimension_semantics=sem, vmem_limit_bytes=VMEM_LIMIT)


def _layer_norm(v, g, b):
    mu = jnp.mean(v, axis=-1, keepdims=True)
    var = jnp.mean(jnp.square(v - mu), axis=-1, keepdims=True)
    return (v - mu) * lax.rsqrt(var + LN_EPS) * g + b


def _dot_nt(a, b):
    return lax.dot_general(a, b, (((1,), (1,)), ((), ())), preferred_element_type=F32)


def _rope_table_kernel(pos_ref, freq_ref, place_ref, out_ref):
    ang = pos_ref[...].astype(F32) * freq_ref[...]
    cs = jnp.concatenate([jnp.cos(ang), jnp.sin(ang)], axis=1)
    out_ref[...] = jnp.dot(cs, place_ref[...], precision=lax.Precision.HIGHEST, preferred_element_type=F32)


def _rope_tables(pos_col, freq, place, tile):
    n = pos_col.shape[0]
    return pl.pallas_call(
        _rope_table_kernel,
        out_shape=jax.ShapeDtypeStruct((n, 6 * LANES), F32),
        grid=(n // tile,),
        in_specs=[pl.BlockSpec((tile, 1), lambda i: (i, 0)),
                  pl.BlockSpec(freq.shape, lambda i: (0, 0)),
                  pl.BlockSpec(place.shape, lambda i: (0, 0))],
        out_specs=pl.BlockSpec((tile, 6 * LANES), lambda i: (i, 0)),
        compiler_params=_cparams("parallel"),
        name="rope_tables",
    )(pos_col, freq, place)


def _rope_slabs(y, cos, sin):
    outs = []
    for s in range(y.shape[1] // LANES):
        ys = y[:, s * LANES:(s + 1) * LANES]
        outs.append(ys * cos + pltpu.roll(ys, 64, 1) * sin)
    return outs[0] if len(outs) == 1 else jnp.concatenate(outs, axis=1)


def _store_residue_major(out_ref, y, scr_ref, dil):
    if dil == 1:
        out_ref[...] = y.astype(out_ref.dtype)
        return
    rows, width = y.shape[0] // dil, y.shape[1]
    for c in range(width // LANES):
        scr_ref[c] = y[:, c * LANES:(c + 1) * LANES]
    for r in range(dil):
        for c in range(width // LANES):
            lanes = slice(r * width + c * LANES, r * width + (c + 1) * LANES)
            out_ref[:, lanes] = scr_ref[c, pl.ds(r, rows, stride=dil), :].astype(out_ref.dtype)


def _load_token_major(ref, scr_ref, dil):
    if dil == 1:
        return ref[...]
    rows, width = ref.shape[0], ref.shape[1] // dil
    for r in range(dil):
        for c in range(width // LANES):
            lanes = slice(r * width + c * LANES, r * width + (c + 1) * LANES)
            scr_ref[c, pl.ds(r, rows, stride=dil), :] = ref[:, lanes]
    return jnp.concatenate([scr_ref[c] for c in range(width // LANES)], axis=1)


def _proj_kernel(x_ref, w_ref, tab_ref, gkv_ref,
                 qa_ref, ka_ref, ckvt_ref, iq_ref, ik_ref, iw_ref, *rest):
    b_refs, scr_ref = rest[:-1], rest[-1]
    xb = x_ref[...].astype(BF16)

    def proj(c0, width):
        return jnp.dot(xb, w_ref[:, c0:c0 + width], preferred_element_type=F32)

    cos_a, sin_a = tab_ref[:, 0:128], tab_ref[:, 128:256]
    cos_i, sin_i = tab_ref[:, 256:384], tab_ref[:, 384:512]
    cos_b, sin_b = tab_ref[:, 512:640], tab_ref[:, 640:768]

    qa_ref[:, 0:_W_QAR] = _rope_slabs(proj(_C_QAR, _W_QAR), cos_a, sin_a).astype(BF16)
    qa_ref[:, _W_QAR:] = proj(_C_QAN, _W_QAN).astype(BF16)
    ckv = proj(_C_CKV, _W_CKV)
    ckv = ckv * lax.rsqrt(jnp.mean(jnp.square(ckv), axis=-1, keepdims=True) + RMS_EPS) * gkv_ref[...]
    ka_ref[:, 0:_W_CKV] = ckv.astype(BF16)
    ckvt_ref[...] = ckv.T.astype(BF16)
    ka_ref[:, _W_CKV:] = _rope_slabs(proj(_C_AKR, _W_AKR), cos_a, sin_a).astype(BF16)
    iq_ref[...] = _rope_slabs(proj(_C_IQ, _W_IQ), cos_i, sin_i).astype(BF16)
    ik_ref[...] = _rope_slabs(proj(_C_IK, _W_IK), cos_i, sin_i).astype(BF16)
    iw_ref[...] = proj(_C_IW, _W_IW) * ((IDX_HEADS * IDX_DIM) ** -0.5)
    ng = len(B_PATTERNS)
    for kind, c0 in enumerate((_C_BQ, _C_BK, _C_BV)):
        for g, (_, dil) in enumerate(B_PATTERNS):
            y = proj(c0 + g * B_GROUP_W, B_GROUP_W)
            if kind < 2:
                y = _rope_slabs(y, cos_b, sin_b)
            _store_residue_major(b_refs[kind * ng + g], y, scr_ref, dil)


def _input_projection(xf, w_perm, tables, g_kv, tile):
    n = xf.shape[0]
    row = lambda width: pl.BlockSpec((tile, width), lambda i: (i, 0))
    full = lambda a: pl.BlockSpec(a.shape, lambda i: (0,) * a.ndim)
    out_w = [(_W_QAR + _W_QAN, BF16), (_W_CKV + _W_AKR, BF16), None, (_W_IQ, BF16), (_W_IK, BF16),
             (_W_IW, F32)]
    shapes = [jax.ShapeDtypeStruct((A_KV_RANK, n), BF16) if o is None else jax.ShapeDtypeStruct((n, o[0]), o[1])
              for o in out_w]
    specs = [pl.BlockSpec((A_KV_RANK, tile), lambda i: (0, i)) if o is None else row(o[0]) for o in out_w]
    for _ in range(3):
        for _, dil in B_PATTERNS:
            shapes.append(jax.ShapeDtypeStruct((n // dil, dil * B_GROUP_W), BF16))
            specs.append(pl.BlockSpec((tile // dil, dil * B_GROUP_W), lambda i: (i, 0)))
    return pl.pallas_call(
        _proj_kernel,
        out_shape=shapes,
        grid=(n // tile,),
        in_specs=[row(D_MODEL), full(w_perm), row(6 * LANES), full(g_kv)],
        out_specs=specs,
        scratch_shapes=[pltpu.VMEM((B_GROUP_W // LANES, tile, LANES), F32)],
        compiler_params=_cparams("parallel"),
        name="input_projection",
    )(xf, w_perm, tables, g_kv)


def _dilated_kernel(q_ref, kc_ref, kp_ref, vc_ref, vp_ref, mq_ref, mv_ref, o_ref, lse_ref,
                    kwin_ref, vwin_ref, *, tq):
    first = pl.program_id(2) == 0
    kwin_ref[0:Q_BLOCK, :] = kp_ref[...]
    kwin_ref[Q_BLOCK:, :] = kc_ref[...]
    vwin_ref[0:Q_BLOCK, :] = vp_ref[...]
    vwin_ref[Q_BLOCK:, :] = vc_ref[...]
    t = lax.broadcasted_iota(jnp.int32, (Q_BLOCK, 2 * Q_BLOCK), 0)
    c = lax.broadcasted_iota(jnp.int32, (Q_BLOCK, 2 * Q_BLOCK), 1)
    diff = t + Q_BLOCK - c
    band = (diff >= 0) & (diff <= Q_BLOCK)
    scale = HEAD_DIM ** -0.5
    for sb in range(tq // Q_BLOCK):
        valid = band
        if sb == 0:
            valid = band & (c >= jnp.where(first, Q_BLOCK, 0))
        bias = jnp.where(valid, 0.0, NEG_BIG).astype(F32)
        q = q_ref[sb * Q_BLOCK:(sb + 1) * Q_BLOCK, :] * scale
        kw = kwin_ref[sb * Q_BLOCK:(sb + 2) * Q_BLOCK, :]
        vw = vwin_ref[sb * Q_BLOCK:(sb + 2) * Q_BLOCK, :]
        qs = jnp.concatenate([q * mq_ref[j:j + 1, :] for j in range(B_GROUP_HEADS)], axis=0)
        s = _dot_nt(qs, kw)
        o_acc = jnp.zeros((Q_BLOCK, B_GROUP_W), F32)
        lse_acc = jnp.zeros((Q_BLOCK, B_GROUP_W), F32)
        for j in range(B_GROUP_HEADS):
            sj = s[j * Q_BLOCK:(j + 1) * Q_BLOCK, :] + bias
            m = jnp.max(sj, axis=-1, keepdims=True)
            e = jnp.exp(sj - m)
            den = jnp.sum(e, axis=-1, keepdims=True)
            pv = jnp.dot(e.astype(BF16), vw, preferred_element_type=F32)
            mv = mv_ref[j:j + 1, :]
            o_acc = o_acc + (pv / den) * mv
            lse_acc = lse_acc + (m + jnp.log(den)) * mv
        o_ref[sb * Q_BLOCK:(sb + 1) * Q_BLOCK, :] = o_acc
        lse_ref[sb * Q_BLOCK:(sb + 1) * Q_BLOCK, :] = lse_acc


def _dilated_group(bq, bk, bv, mq, mv, b, dil):
    sub = bq.shape[0] // b
    tq = min(512, sub)
    nblk = tq // Q_BLOCK
    view = lambda a: a.reshape(b, sub, dil * B_GROUP_W)
    cur = pl.BlockSpec((None, tq, B_GROUP_W), lambda bi, r, i: (bi, i, r))
    prev = pl.BlockSpec((None, Q_BLOCK, B_GROUP_W),
                        lambda bi, r, i: (bi, jnp.maximum(i * nblk - 1, 0), r))
    const = lambda a: pl.BlockSpec(a.shape, lambda bi, r, i: (0, 0))
    out = cur
    o, lse = pl.pallas_call(
        functools.partial(_dilated_kernel, tq=tq),
        out_shape=[jax.ShapeDtypeStruct((b, sub, dil * B_GROUP_W), F32)] * 2,
        grid=(b, dil, sub // tq),
        in_specs=[cur, cur, prev, cur, prev, const(mq), const(mv)],
        out_specs=[out, out],
        scratch_shapes=[pltpu.VMEM((tq + Q_BLOCK, B_GROUP_W), BF16)] * 2,
        compiler_params=_cparams("parallel", "parallel", "arbitrary"),
        name=f"dilated_attention_d{dil}",
    )(view(bq), view(bk), view(bk), view(bv), view(bv), mq, mv)
    return o.reshape(b * sub, dil * B_GROUP_W), lse.reshape(b * sub, dil * B_GROUP_W)


_TK = 512

def _fold_rows(x, op, slab=8):
    parts = [x[r:r + slab, :] for r in range(0, x.shape[0], slab)]
    while len(parts) > 1:
        parts = [op(parts[k], parts[k + 1]) for k in range(0, len(parts) - 1, 2)] + parts[len(parts) & ~1:]
    return parts[0]


def _skewed_tiles(n_tiles, produce, consume, buf_a, buf_b, carry):
    produce(0, buf_a)

    def pair(t, c):
        j = 2 * t
        produce(j + 1, buf_b)
        c = consume(j, buf_a, c)
        produce(j + 2, buf_a)
        return consume(j + 1, buf_b, c)

    n_pairs = (n_tiles - 1) // 2
    carry = lax.fori_loop(0, n_pairs, pair, carry)
    j = 2 * n_pairs

    def last_two(c):
        produce(j + 1, buf_b)
        return consume(j + 1, buf_b, consume(j, buf_a, c))

    return lax.cond(n_tiles - j == 2, last_two, lambda c: consume(j, buf_a, c), carry)


def _dsa_kernel(qa_ref, iq_ref, iw_ref, ikt_ref, ka_ref, ckvt_ref, wuk_ref, wuv_ref, mar_ref, man_ref, miq_ref,
                out_ref, key_ref, tie_ref, iqs_ref, qcat_ref, acc_ref, sa_ref, sb_ref, ma_ref, mb_ref,
                p_ref, *, n_sel):
    i = pl.program_id(1)
    q0 = i * Q_BLOCK
    n_keys = q0 + Q_BLOCK
    rows = A_HEADS * Q_BLOCK

    n_tiles = (n_keys + _TK - 1) // _TK
    tq_lane = q0 + lax.broadcasted_iota(jnp.int32, (_TK, Q_BLOCK), 1)
    krow = lax.broadcasted_iota(jnp.int32, (_TK, Q_BLOCK), 0)

    iq = iq_ref[...]
    for h in range(IDX_HEADS):
        iqs_ref[:, h * Q_BLOCK:(h + 1) * Q_BLOCK] = (iq * miq_ref[h:h + 1, :]).astype(F32).T.astype(BF16)
    iw_t = iw_ref[...].T

    def score_matmul(j, buf):
        k0 = pl.multiple_of(j * _TK, _TK)
        s = jnp.dot(ikt_ref[pl.ds(k0, _TK), :], iqs_ref[...], preferred_element_type=F32)
        for h in range(IDX_HEADS):
            buf[0][h] = s[:, h * Q_BLOCK:(h + 1) * Q_BLOCK]

    def score_keys(j, buf, carry):
        k0 = pl.multiple_of(j * _TK, _TK)
        sc = jnp.zeros((_TK, Q_BLOCK), F32)
        for h in range(IDX_HEADS):
            sc = sc + jnp.maximum(buf[0][h], 0.0) * iw_t[h:h + 1, :]
        bits = lax.bitcast_convert_type(jnp.where(sc == 0.0, 0.0, sc), jnp.int32)
        okey = bits ^ ((bits >> 31) & jnp.int32(0x7FFFFFFF))
        key_ref[pl.ds(k0, _TK), :] = jnp.where(krow + k0 <= tq_lane, okey, jnp.int32(INT_MIN))
        return carry

    _skewed_tiles(n_tiles, score_matmul, score_keys, (sa_ref, ma_ref), (sb_ref, mb_ref), 0)

    def count_keys(pred):
        def body(j, cnt):
            k0 = pl.multiple_of(j * _TK, _TK)
            hit = jnp.where(pred(key_ref[pl.ds(k0, _TK), :], k0), 1.0, 0.0)
            return cnt + _fold_rows(hit, jnp.add)
        cnt8 = lax.fori_loop(0, n_tiles, body, jnp.zeros((8, Q_BLOCK), F32))
        return jnp.sum(cnt8, axis=0, keepdims=True)

    def bit_step(b, carry):
        prefix, n_ge = carry
        trial = prefix | (jnp.int32(1) << (31 - b))
        t = trial ^ jnp.int32(INT_MIN)
        cnt = count_keys(lambda keys, k0: keys >= t)
        take = cnt >= float(n_sel)
        return jnp.where(take, trial, prefix), jnp.where(take, cnt, n_ge)

    prefix, n_ge = lax.fori_loop(0, 32, bit_step, (jnp.zeros((1, Q_BLOCK), jnp.int32),
                                                   jnp.full((1, Q_BLOCK), float(n_sel), F32)))
    thr = prefix ^ jnp.int32(INT_MIN)

    surplus = n_ge - float(n_sel)
    max_surplus = jnp.max(surplus)

    @pl.when(max_surplus > 0.0)
    def _():
        need = float(n_sel) - count_keys(lambda keys, k0: keys > thr)
        index_bits = (key_ref.shape[0] - 1).bit_length()
        not_tied = jnp.int32(1 << index_bits)

        def tie_positions(j, carry):
            k0 = pl.multiple_of(j * _TK, _TK)
            tied = key_ref[pl.ds(k0, _TK), :] == thr
            tie_ref[pl.ds(k0, _TK), :] = jnp.where(tied, krow + k0, not_tied)
            return carry

        lax.fori_loop(0, n_tiles, tie_positions, 0)

        def count_ties_below(bound):
            def body(j, cnt):
                k0 = pl.multiple_of(j * _TK, _TK)
                hit = jnp.where(tie_ref[pl.ds(k0, _TK), :] < bound, 1.0, 0.0)
                return cnt + _fold_rows(hit, jnp.add)
            cnt8 = lax.fori_loop(0, n_tiles, body, jnp.zeros((8, Q_BLOCK), F32))
            return jnp.sum(cnt8, axis=0, keepdims=True)

        def index_bit(b, bound):
            trial = bound | (jnp.int32(1) << (index_bits - 1 - b))
            return jnp.where(count_ties_below(trial) < need, trial, bound)

        last = lax.fori_loop(0, index_bits, index_bit, jnp.zeros((1, Q_BLOCK), jnp.int32))

        def demote(j, carry):
            k0 = pl.multiple_of(j * _TK, _TK)
            pos = tie_ref[pl.ds(k0, _TK), :]
            drop = (pos > last) & (pos < not_tied)
            key_ref[pl.ds(k0, _TK), :] = jnp.where(drop, jnp.int32(INT_MIN), key_ref[pl.ds(k0, _TK), :])
            return carry

        lax.fori_loop(0, n_tiles, demote, 0)

    q_rope = qa_ref[:, 0:_W_QAR]
    q_nope = qa_ref[:, _W_QAR:]
    scale = HEAD_DIM ** -0.5
    c_rope = A_KV_RANK
    for h in range(A_HEADS):
        q_lat = jnp.dot(q_nope * man_ref[h:h + 1, :], wuk_ref[...], preferred_element_type=F32)
        cols = slice(h * Q_BLOCK, (h + 1) * Q_BLOCK)
        qcat_ref[0:c_rope, cols] = (q_lat.astype(BF16) * scale).astype(F32).T.astype(BF16)
        qcat_ref[c_rope:, cols] = (q_rope * mar_ref[h:h + 1, :] * scale).astype(F32).T.astype(BF16)

    acc_ref[...] = jnp.zeros(acc_ref.shape, F32)

    def logit_matmul(j, buf):
        s_buf, mx_buf = buf
        k0 = pl.multiple_of(j * _TK, _TK)
        sel = (key_ref[pl.ds(k0, _TK), :] >= thr) & (krow + k0 <= tq_lane)
        bias = jnp.where(sel, 0.0, NEG_BIG).astype(F32)
        s = jnp.dot(ka_ref[pl.ds(k0, _TK), :], qcat_ref[...], preferred_element_type=F32)
        for h in range(A_HEADS):
            sh = s[:, h * Q_BLOCK:(h + 1) * Q_BLOCK] + bias
            s_buf[h] = sh
            mx_buf[:, h * Q_BLOCK:(h + 1) * Q_BLOCK] = _fold_rows(sh, jnp.maximum)

    def softmax_pv(j, buf, carry):
        s_buf, mx_buf = buf
        m_old, l_old = carry
        k0 = pl.multiple_of(j * _TK, _TK)
        ckv_t = ckvt_ref[:, pl.ds(k0, _TK)]
        m_parts, l_parts = [], []
        for c in range(A_HEADS // 2):
            a_parts = []
            for h in (2 * c, 2 * c + 1):
                cols = slice(h * Q_BLOCK, (h + 1) * Q_BLOCK)
                m_h = jnp.maximum(m_old[:, cols], jnp.max(mx_buf[:, cols], axis=0, keepdims=True))
                a_h = jnp.exp(m_old[:, cols] - m_h)
                p = jnp.exp(s_buf[h] - m_h)
                p_ref[c, :, (h % 2) * Q_BLOCK:(h % 2 + 1) * Q_BLOCK] = p.astype(BF16)
                l_parts.append(a_h * l_old[:, cols] + jnp.sum(_fold_rows(p, jnp.add), axis=0, keepdims=True))
                m_parts.append(m_h)
                a_parts.append(a_h)
            pv = jnp.dot(ckv_t, p_ref[c], preferred_element_type=F32)
            acc_ref[c] = acc_ref[c] * jnp.concatenate(a_parts, axis=1) + pv
        return jnp.concatenate(m_parts, axis=1), jnp.concatenate(l_parts, axis=1)

    _, l_fin = _skewed_tiles(n_tiles, logit_matmul, softmax_pv, (sa_ref, ma_ref), (sb_ref, mb_ref),
                             (jnp.full((1, rows), NEG_BIG, F32), jnp.zeros((1, rows), F32)))

    inv_l = 1.0 / l_fin
    out_t = jnp.zeros((A_HEADS * HEAD_DIM, Q_BLOCK), F32)
    for h in range(A_HEADS):
        lanes = slice((h % 2) * Q_BLOCK, (h % 2 + 1) * Q_BLOCK)
        o_lat = (acc_ref[h // 2][:, lanes] * inv_l[:, h * Q_BLOCK:(h + 1) * Q_BLOCK]).astype(BF16)
        out_t = out_t + jnp.dot(wuv_ref[h], o_lat, preferred_element_type=F32)
    out_ref[...] = out_t.T.astype(BF16)


def _dsa_mixer(qa, iq, iw, ikt, ka, ckvt, wuk, wuv_t, m_ar, m_an, m_iq, b, s):
    n_sel = min(A_TOPK_MAX, s // 4)
    nq = s // Q_BLOCK
    rows = A_HEADS * Q_BLOCK
    blk = lambda width: pl.BlockSpec((Q_BLOCK, width), lambda bi, i: (bi * nq + i, 0))
    seq = lambda width: pl.BlockSpec((s, width), lambda bi, i: (bi, 0))
    const = lambda a: pl.BlockSpec(a.shape, lambda bi, i: (0,) * a.ndim)
    return pl.pallas_call(
        functools.partial(_dsa_kernel, n_sel=n_sel),
        out_shape=jax.ShapeDtypeStruct((b * s, A_HEADS * HEAD_DIM), BF16),
        grid=(b, nq),
        in_specs=[blk(_W_QAR + _W_QAN), blk(_W_IQ), blk(_W_IW), seq(_W_IK), seq(_W_CKV + _W_AKR),
                  pl.BlockSpec((A_KV_RANK, s), lambda bi, i: (0, bi)),
                  const(wuk), const(wuv_t), const(m_ar), const(m_an), const(m_iq)],
        out_specs=blk(A_HEADS * HEAD_DIM),
        scratch_shapes=[pltpu.VMEM((s, Q_BLOCK), jnp.int32),
                        pltpu.VMEM((s, Q_BLOCK), jnp.int32),
                        pltpu.VMEM((_W_IQ, rows), BF16),
                        pltpu.VMEM((A_KV_RANK + _W_AKR, rows), BF16),
                        pltpu.VMEM((A_HEADS // 2, A_KV_RANK, 2 * Q_BLOCK), F32),
                        pltpu.VMEM((A_HEADS, _TK, Q_BLOCK), F32),
                        pltpu.VMEM((A_HEADS, _TK, Q_BLOCK), F32),
                        pltpu.VMEM((8, rows), F32),
                        pltpu.VMEM((8, rows), F32),
                        pltpu.VMEM((A_HEADS // 2, _TK, 2 * Q_BLOCK), BF16)],
        compiler_params=_cparams("parallel", "arbitrary"),
        name="dsa_attention",
    )(qa, iq, iw, ikt, ka, ckvt, wuk, wuv_t, m_ar, m_an, m_iq)


def _merge_kernel(x_ref, a_ref, o1_ref, o2_ref, o3_ref, l1_ref, l2_ref, l3_ref, wg_ref, bg_ref,
                  wa_ref, wb_ref, wo_ref, lg_ref, lb_ref, h_ref, *scr):
    dils = [dil for _, dil in B_PATTERNS]
    lses = [_load_token_major(r, scr[2 * g], dils[g]) for g, r in enumerate((l1_ref, l2_ref, l3_ref))]
    outs = [_load_token_major(r, scr[2 * g + 1], dils[g]) for g, r in enumerate((o1_ref, o2_ref, o3_ref))]
    mx = jnp.maximum(jnp.maximum(lses[0], lses[1]), lses[2])
    es = [jnp.exp(l - mx) for l in lses]
    den = es[0] + es[1] + es[2]
    b_out = (es[0] / den) * outs[0] + (es[1] / den) * outs[1] + (es[2] / den) * outs[2]
    ya = jnp.dot(a_ref[...], wa_ref[...], preferred_element_type=F32)
    yb = jnp.dot(b_out.astype(BF16), wb_ref[...], preferred_element_type=F32)
    x = x_ref[...]
    gates = jax.nn.sigmoid(jnp.dot(x.astype(BF16), wg_ref[...], preferred_element_type=F32) + bg_ref[...])
    pre = gates[:, 0:D_MODEL] * ya + gates[:, D_MODEL:] * yb
    mix = jnp.dot(pre.astype(BF16), wo_ref[...], preferred_element_type=F32)
    h_ref[...] = _layer_norm(ALPHA * x + mix, lg_ref[...], lb_ref[...])


def _merge(xf, a_out, b_parts, w_gate, b_gate, wa, wb, wo, ln_g, ln_b, tile):
    n = xf.shape[0]
    row = lambda width: pl.BlockSpec((tile, width), lambda i: (i, 0))
    full = lambda a: pl.BlockSpec(a.shape, lambda i: (0,) * a.ndim)
    (o1, l1), (o2, l2), (o3, l3) = b_parts
    grp = [pl.BlockSpec((tile // dil, dil * B_GROUP_W), lambda i: (i, 0)) for _, dil in B_PATTERNS]
    return pl.pallas_call(
        _merge_kernel,
        out_shape=jax.ShapeDtypeStruct((n, D_MODEL), F32),
        grid=(n // tile,),
        in_specs=[row(D_MODEL), row(A_HEADS * HEAD_DIM)] + grp + grp + [full(w_gate), full(b_gate),
                  full(wa), full(wb), full(wo), full(ln_g), full(ln_b)],
        out_specs=row(D_MODEL),
        scratch_shapes=[pltpu.VMEM((B_GROUP_W // LANES, tile, LANES), F32)] * (2 * len(B_PATTERNS)),
        compiler_params=_cparams("parallel"),
        name="merge_output_projection",
    )(xf, a_out, o1, o2, o3, l1, l2, l3, w_gate, b_gate, wa, wb, wo, ln_g, ln_b)


def _first_max(v):
    m = jnp.max(v, axis=0, keepdims=True)
    idx = lax.broadcasted_iota(jnp.int32, v.shape, 0)
    first = jnp.min(jnp.where(v == m, idx, v.shape[0]), axis=0, keepdims=True)
    return m, idx == first


def _router_kernel(h_ref, wr_ref, rb_ref, gate_ref):
    t = h_ref.shape[0]
    gs = N_EXPERTS // N_GROUPS
    logits = lax.dot_general(wr_ref[...], h_ref[...], (((1,), (1,)), ((), ())),
                             precision=lax.Precision.HIGHEST, preferred_element_type=F32)
    scores = jax.nn.sigmoid(logits)
    biased = scores + rb_ref[...]
    gscores = []
    for g in range(N_GROUPS):
        blk = biased[g * gs:(g + 1) * gs, :]
        m1, hit = _first_max(blk)
        m2 = jnp.max(jnp.where(hit, -jnp.inf, blk), axis=0, keepdims=True)
        gscores.append(m1 + m2)
    gscore = jnp.concatenate(gscores, axis=0)
    gsel = jnp.zeros((N_GROUPS, t), F32)
    for _ in range(TOPK_GROUPS):
        _, hit = _first_max(gscore)
        gsel = jnp.where(hit, 1.0, gsel)
        gscore = jnp.where(hit, -jnp.inf, gscore)
    esel = jnp.concatenate([jnp.broadcast_to(gsel[g:g + 1, :], (gs, t)) for g in range(N_GROUPS)], axis=0)
    cand = jnp.where(esel > 0.0, biased, -jnp.inf)
    top_s = jnp.zeros((N_EXPERTS, t), F32)
    for _ in range(TOP_K):
        _, hit = _first_max(cand)
        top_s = jnp.where(hit, scores, top_s)
        cand = jnp.where(hit, -jnp.inf, cand)
    gate = top_s / jnp.sum(top_s, axis=0, keepdims=True) * ROUTED_SCALE
    gate_ref[...] = jnp.concatenate([gate, jnp.zeros((LANES - N_EXPERTS, t), F32)], axis=0).T


def _router(h, wr_t, rb_col, tile):
    n = h.shape[0]
    return pl.pallas_call(
        _router_kernel,
        out_shape=jax.ShapeDtypeStruct((n, LANES), F32),
        grid=(n // tile,),
        in_specs=[pl.BlockSpec((tile, D_MODEL), lambda i: (i, 0)),
                  pl.BlockSpec(wr_t.shape, lambda i: (0, 0)),
                  pl.BlockSpec(rb_col.shape, lambda i: (0, 0))],
        out_specs=pl.BlockSpec((tile, LANES), lambda i: (i, 0)),
        compiler_params=_cparams("parallel"),
        name="moe_router",
    )(h, wr_t, rb_col)


_EXPERTS_PER_STEP = 4


def _swiglu(hb, w1, w3, w2):
    a = jnp.dot(hb, w1, preferred_element_type=F32)
    b = jnp.dot(hb, w3, preferred_element_type=F32)
    return jnp.dot((jax.nn.silu(a) * b).astype(BF16), w2, preferred_element_type=F32)


def _experts_kernel(h_ref, gate_ref, w1_ref, w3_ref, w2_ref, s1_ref, s3_ref, s2_ref, lg_ref, lb_ref,
                    y_ref, hb_ref, acc_ref):
    step = pl.program_id(1)

    @pl.when(step == 0)
    def _():
        hb = h_ref[...].astype(BF16)
        hb_ref[...] = hb
        acc_ref[...] = _swiglu(hb, s1_ref[...], s3_ref[...], s2_ref[...])

    hb = hb_ref[...]
    gate = gate_ref[...]
    lane = lax.broadcasted_iota(jnp.int32, gate.shape, 1)
    for k in range(_EXPERTS_PER_STEP):
        e = step * _EXPERTS_PER_STEP + k
        g_col = jnp.sum(jnp.where(lane == e, gate, 0.0), axis=1, keepdims=True)
        acc_ref[...] += _swiglu(hb, w1_ref[k], w3_ref[k], w2_ref[k]) * g_col

    @pl.when(step == pl.num_programs(1) - 1)
    def _():
        y_ref[...] = _layer_norm(ALPHA * h_ref[...] + acc_ref[...], lg_ref[...], lb_ref[...])


def _experts(h, gate, w1, w3, w2, s1, s3, s2, ln_g, ln_b, tile):
    n = h.shape[0]
    ec = _EXPERTS_PER_STEP
    row = lambda width: pl.BlockSpec((tile, width), lambda i, e: (i, 0))
    full = lambda a: pl.BlockSpec(a.shape, lambda i, e: (0,) * a.ndim)
    wspec = lambda a: pl.BlockSpec((ec,) + a.shape[1:], lambda i, e: (e, 0, 0))
    return pl.pallas_call(
        _experts_kernel,
        out_shape=jax.ShapeDtypeStruct((n, D_MODEL), F32),
        grid=(n // tile, N_EXPERTS // ec),
        in_specs=[row(D_MODEL), row(LANES), wspec(w1), wspec(w3), wspec(w2),
                  full(s1), full(s3), full(s2), full(ln_g), full(ln_b)],
        out_specs=row(D_MODEL),
        scratch_shapes=[pltpu.VMEM((tile, D_MODEL), BF16), pltpu.VMEM((tile, D_MODEL), F32)],
        compiler_params=_cparams("parallel", "arbitrary"),
        name="moe_experts",
    )(h, gate, w1, w3, w2, s1, s3, s2, ln_g, ln_b)


def _rope_freqs():
    inv16 = ROPE_THETA ** (-jnp.arange(0, ROPE_DIM, 2, dtype=F32) / ROPE_DIM)
    inv8 = ROPE_THETA ** (-jnp.arange(0, IDX_ROPE, 2, dtype=F32) / IDX_ROPE)

    n16, n8 = ROPE_DIM // 2, IDX_ROPE // 2
    freq = jnp.zeros((1, LANES), F32).at[0, 0:n16].set(inv16).at[0, n16:n16 + n8].set(inv8)
    lay = _LAYOUT
    place = np.zeros((2 * LANES, 6 * LANES), np.float32)
    for k, (fidx, sign, base) in enumerate(((lay["fa"], lay["sa"], 0), (lay["fi"], lay["si"], n16),
                                            (lay["fb"], lay["sb"], 0))):
        for l in range(LANES):
            src = base + fidx[l] if fidx[l] >= 0 else n16 + n8
            place[src, 2 * k * LANES + l] = 1.0
            place[LANES + src, (2 * k + 1) * LANES + l] = sign[l]
    return freq, jnp.asarray(place)


def _layer(x, positions, w_in, b_gate, g_kv, w_uk, w_uv, w_branch_a, w_branch_b, w_o, ln1_g, ln1_b,
           w_router, router_bias, w1_e, w3_e, w2_e, ws1, ws3, ws2, ln2_g, ln2_b):
    b, s, d = x.shape
    n = b * s
    lay = _LAYOUT
    tile = min(256, n)
    xf = x.reshape(n, d)

    n_proj = _C_GA
    w_perm = (w_in[:, lay["cols"][:n_proj]] * lay["keep"][:n_proj]).astype(BF16)
    w_gate = w_in[:, _OFF_GA:_OFF_GA + 2 * D_MODEL].astype(BF16)
    wuk =jnp.transpose(w_uk, (1, 2, 0)).reshape(A_HEADS * A_NOPE, A_KV_RANK).astype(BF16)
    wuv_t = jnp.zeros((A_HEADS, A_HEADS, HEAD_DIM, A_KV_RANK), F32)
    wuv_t = wuv_t.at[jnp.arange(A_HEADS), jnp.arange(A_HEADS)].set(jnp.transpose(w_uv, (1, 2, 0)))
    wuv_t = wuv_t.reshape(A_HEADS, A_HEADS * HEAD_DIM, A_KV_RANK).astype(BF16)
    masks = {k: jnp.asarray(lay[k], BF16) for k in ("m_ar", "m_an", "m_iq", "m_bq")}
    m_bv = jnp.asarray(lay["m_bv"], F32)

    freq, place = _rope_freqs()
    tables = _rope_tables(positions.reshape(n, 1), freq, place, tile)
    qa, ka, ckvt, iq, ikt, iw, *bqkv = _input_projection(xf, w_perm, tables, g_kv.reshape(1, -1), tile)

    a_out = _dsa_mixer(qa, iq, iw, ikt, ka, ckvt, wuk, wuv_t, masks["m_ar"], masks["m_an"], masks["m_iq"], b, s)
    ng = len(B_PATTERNS)
    b_parts = [_dilated_group(bqkv[g], bqkv[ng + g], bqkv[2 * ng + g], masks["m_bq"], m_bv, b, dil)
               for g, (_, dil) in enumerate(B_PATTERNS)]

    h = _merge(xf, a_out, b_parts, w_gate, b_gate.reshape(1, -1), w_branch_a.astype(BF16),
               w_branch_b.astype(BF16), w_o.astype(BF16), ln1_g.reshape(1, -1), ln1_b.reshape(1, -1), tile)

    gate = _router(h, w_router.T, router_bias.reshape(-1, 1), min(1024, n))
    y = _experts(h, gate, w1_e.astype(BF16), w3_e.astype(BF16), w2_e.astype(BF16),
                 ws1.astype(BF16), ws3.astype(BF16), ws2.astype(BF16),
                 ln2_g.reshape(1, -1), ln2_b.reshape(1, -1), min(1024, n))
    return y.reshape(b, s, d)


def kernel(x, positions, w_in, b_gate, g_kv, w_uk, w_uv, w_branch_a, w_branch_b, w_o, ln1_g, ln1_b,
           w_router, router_bias, w1_e, w3_e, w2_e, ws1, ws3, ws2, ln2_g, ln2_b):
    h = x
    for l in range(DEPTH):
        h = _layer(h, positions, w_in[l], b_gate[l], g_kv[l], w_uk[l], w_uv[l], w_branch_a[l],
                   w_branch_b[l], w_o[l], ln1_g[l], ln1_b[l], w_router[l], router_bias[l],
                   w1_e[l], w3_e[l], w2_e[l], ws1[l], ws3[l], ws2[l], ln2_g[l], ln2_b[l])
    return h
```

```python
import functools

import jax
import jax.numpy as jnp
import numpy as np
from jax import lax
from jax.experimental import pallas as pl
from jax.experimental.pallas import tpu as pltpu

F32 = jnp.float32
BF16 = jnp.bfloat16

D_MODEL = 1024
HEAD_DIM = 64
ROPE_DIM = 16
ROPE_THETA = 500000.0
Q_BLOCK = 128
A_HEADS = 8
A_NOPE = HEAD_DIM - ROPE_DIM
A_KV_RANK = 256
A_TOPK_MAX = 256
IDX_HEADS = 8
IDX_DIM = 32
IDX_ROPE = 8
B_PATTERNS = ((128, 1), (512, 4), (2048, 16))
B_GROUP_HEADS = 4
B_HEADS = B_GROUP_HEADS * len(B_PATTERNS)
B_GROUP_W = B_GROUP_HEADS * HEAD_DIM
N_EXPERTS = 64
TOP_K = 8
N_GROUPS = 8
TOPK_GROUPS = 4
D_EXPERT = 256
ROUTED_SCALE = 2.5
DEPTH = 1
ALPHA = (2.0 * DEPTH) ** 0.25
LN_EPS = 1e-5
RMS_EPS = 1e-6

LANES = 128
VMEM_LIMIT = 56 * 1024 * 1024
NEG_BIG = -1e30
INT_MIN = -(2 ** 31)

_OFF_AQ = 0
_OFF_CKV = _OFF_AQ + A_HEADS * HEAD_DIM
_OFF_AKR = _OFF_CKV + A_KV_RANK
_OFF_IQ = _OFF_AKR + ROPE_DIM
_OFF_IK = _OFF_IQ + IDX_HEADS * IDX_DIM
_OFF_IW = _OFF_IK + IDX_DIM
_OFF_BQ = _OFF_IW + IDX_HEADS
_OFF_BK = _OFF_BQ + B_HEADS * HEAD_DIM
_OFF_BV = _OFF_BK + B_HEADS * HEAD_DIM
_OFF_GA = _OFF_BV + B_HEADS * HEAD_DIM
_OFF_GB = _OFF_GA + D_MODEL
_IN_TOTAL = _OFF_GB + D_MODEL

_W_QAR, _W_QAN, _W_CKV, _W_AKR = 128, A_HEADS * A_NOPE, A_KV_RANK, 128
_W_IQ, _W_IK, _W_IW = 256, 256, 128
_W_B = B_HEADS * HEAD_DIM
_C_QAR = 0
_C_QAN = _C_QAR + _W_QAR
_C_CKV = _C_QAN + _W_QAN
_C_AKR = _C_CKV + _W_CKV
_C_IQ = _C_AKR + _W_AKR
_C_IK = _C_IQ + _W_IQ
_C_IW = _C_IK + _W_IK
_C_BQ = _C_IW + _W_IW
_C_BK = _C_BQ + _W_B
_C_BV = _C_BK + _W_B
_C_GA = _C_BV + _W_B
_C_GB = _C_GA + D_MODEL
_P_TOTAL = _C_GB + D_MODEL


def _idx_lane(l):
    if l < 16:
        return l // 4, l % 4
    if l < 64:
        return (l - 16) // 12, 8 + (l - 16) % 12
    if l < 80:
        return (l - 64) // 4, 4 + (l - 64) % 4
    return (l - 80) // 12, 20 + (l - 80) % 12


def _b_lane(l):
    half, r = l // 64, l % 64
    which, rr = r // 32, r % 32
    if rr < 8:
        return which, half * 8 + rr
    return which, 16 + half * 24 + (rr - 8)


def _build_layout():
    cols = np.zeros((_P_TOTAL,), np.int32)
    keep = np.ones((_P_TOTAL,), np.float32)
    for l in range(128):
        half, h, f = l // 64, (l % 64) // 8, l % 8
        cols[_C_QAR + l] = _OFF_AQ + h * HEAD_DIM + half * 8 + f
        cols[_C_AKR + l] = _OFF_AKR + half * 8 + f
    for h in range(A_HEADS):
        for j in range(A_NOPE):
            cols[_C_QAN + h * A_NOPE + j] = _OFF_AQ + h * HEAD_DIM + ROPE_DIM + j
    cols[_C_CKV:_C_CKV + _W_CKV] = _OFF_CKV + np.arange(_W_CKV)
    for sl in range(2):
        for l in range(128):
            hh, d = _idx_lane(l)
            cols[_C_IQ + sl * 128 + l] = _OFF_IQ + (sl * 4 + hh) * IDX_DIM + d
            cols[_C_IK + sl * 128 + l] = _OFF_IK + d
    cols[_C_IW:_C_IW + IDX_HEADS] = _OFF_IW + np.arange(IDX_HEADS)
    keep[_C_IW + IDX_HEADS:_C_IW + _W_IW] = 0.0
    for p in range(B_HEADS // 2):
        for l in range(128):
            which, d = _b_lane(l)
            h = 2 * p + which
            cols[_C_BQ + p * 128 + l] = _OFF_BQ + h * HEAD_DIM + d
            cols[_C_BK + p * 128 + l] = _OFF_BK + h * HEAD_DIM + d
    cols[_C_BV:_C_BV + _W_B] = _OFF_BV + np.arange(_W_B)
    cols[_C_GA:_C_GA + D_MODEL] = _OFF_GA + np.arange(D_MODEL)
    cols[_C_GB:_C_GB + D_MODEL] = _OFF_GB + np.arange(D_MODEL)

    fa = np.array([l % 8 for l in range(128)])
    sa = np.array([-1.0 if l < 64 else 1.0 for l in range(128)], np.float32)
    fi = np.full((128,), -1)
    si = np.zeros((128,), np.float32)
    fb = np.full((128,), -1)
    sb = np.zeros((128,), np.float32)
    for l in range(128):
        if l < 16 or 64 <= l < 80:
            fi[l] = l % 4
            si[l] = -1.0 if l < 64 else 1.0
        if l % 32 < 8:
            fb[l] = l % 32
            sb[l] = -1.0 if l < 64 else 1.0

    m_ar = np.zeros((A_HEADS, 128), np.float32)
    for l in range(128):
        m_ar[(l % 64) // 8, l] = 1.0
    m_an = np.zeros((A_HEADS, _W_QAN), np.float32)
    for h in range(A_HEADS):
        m_an[h, h * A_NOPE:(h + 1) * A_NOPE] = 1.0
    m_iq = np.zeros((IDX_HEADS, _W_IQ), np.float32)
    for sl in range(2):
        for l in range(128):
            m_iq[sl * 4 + _idx_lane(l)[0], sl * 128 + l] = 1.0
    m_bq = np.zeros((B_GROUP_HEADS, B_GROUP_W), np.float32)
    for p in range(2):
        for l in range(128):
            m_bq[2 * p + _b_lane(l)[0], p * 128 + l] = 1.0
    m_bv = np.zeros((B_GROUP_HEADS, B_GROUP_W), np.float32)
    for j in range(B_GROUP_HEADS):
        m_bv[j, j * HEAD_DIM:(j + 1) * HEAD_DIM] = 1.0
    return dict(cols=cols, keep=keep, fa=fa, sa=sa, fi=fi, si=si, fb=fb, sb=sb,
                m_ar=m_ar, m_an=m_an, m_iq=m_iq, m_bq=m_bq, m_bv=m_bv)


_LAYOUT = _build_layout()


def _cparams(*sem):
    return pltpu.CompilerParams(dimension_semantics=sem, vmem_limit_bytes=VMEM_LIMIT)


def _layer_norm(v, g, b):
    mu = jnp.mean(v, axis=-1, keepdims=True)
    var = jnp.mean(jnp.square(v - mu), axis=-1, keepdims=True)
    return (v - mu) * lax.rsqrt(var + LN_EPS) * g + b


def _dot_nt(a, b):
    return lax.dot_general(a, b, (((1,), (1,)), ((), ())), preferred_element_type=F32)


def _rope_table_kernel(pos_ref, freq_ref, place_ref, out_ref):
    ang = pos_ref[...].astype(F32) * freq_ref[...]
    cs = jnp.concatenate([jnp.cos(ang), jnp.sin(ang)], axis=1)
    out_ref[...] = jnp.dot(cs, place_ref[...], precision=lax.Precision.HIGHEST, preferred_element_type=F32)


def _rope_tables(pos_col, freq, place, tile):
    n = pos_col.shape[0]
    return pl.pallas_call(
        _rope_table_kernel,
        out_shape=jax.ShapeDtypeStruct((n, 6 * LANES), F32),
        grid=(n // tile,),
        in_specs=[pl.BlockSpec((tile, 1), lambda i: (i, 0)),
                  pl.BlockSpec(freq.shape, lambda i: (0, 0)),
                  pl.BlockSpec(place.shape, lambda i: (0, 0))],
        out_specs=pl.BlockSpec((tile, 6 * LANES), lambda i: (i, 0)),
        compiler_params=_cparams("parallel"),
        name="rope_tables",
    )(pos_col, freq, place)


def _rope_slabs(y, cos, sin):
    outs = []
    for s in range(y.shape[1] // LANES):
        ys = y[:, s * LANES:(s + 1) * LANES]
        outs.append(ys * cos + pltpu.roll(ys, 64, 1) * sin)
    return outs[0] if len(outs) == 1 else jnp.concatenate(outs, axis=1)


def _store_residue_major(out_ref, y, scr_ref, dil):
    if dil == 1:
        out_ref[...] = y.astype(out_ref.dtype)
        return
    rows, width = y.shape[0] // dil, y.shape[1]
    for c in range(width // LANES):
        scr_ref[c] = y[:, c * LANES:(c + 1) * LANES]
    for r in range(dil):
        for c in range(width // LANES):
            lanes = slice(r * width + c * LANES, r * width + (c + 1) * LANES)
            out_ref[:, lanes] = scr_ref[c, pl.ds(r, rows, stride=dil), :].astype(out_ref.dtype)


def _load_token_major(ref, scr_ref, dil):
    if dil == 1:
        return ref[...]
    rows, width = ref.shape[0], ref.shape[1] // dil
    for r in range(dil):
        for c in range(width // LANES):
            lanes = slice(r * width + c * LANES, r * width + (c + 1) * LANES)
            scr_ref[c, pl.ds(r, rows, stride=dil), :] = ref[:, lanes]
    return jnp.concatenate([scr_ref[c] for c in range(width // LANES)], axis=1)


def _proj_kernel(x_ref, w_ref, tab_ref, gkv_ref,
                 qa_ref, ka_ref, ckvt_ref, iq_ref, ik_ref, iw_ref, *rest):
    b_refs, scr_ref = rest[:-1], rest[-1]
    xb = x_ref[...].astype(BF16)

    def proj(c0, width):
        return jnp.dot(xb, w_ref[:, c0:c0 + width], preferred_element_type=F32)

    cos_a, sin_a = tab_ref[:, 0:128], tab_ref[:, 128:256]
    cos_i, sin_i = tab_ref[:, 256:384], tab_ref[:, 384:512]
    cos_b, sin_b = tab_ref[:, 512:640], tab_ref[:, 640:768]

    qa_ref[:, 0:_W_QAR] = _rope_slabs(proj(_C_QAR, _W_QAR), cos_a, sin_a).astype(BF16)
    qa_ref[:, _W_QAR:] = proj(_C_QAN, _W_QAN).astype(BF16)
    ckv = proj(_C_CKV, _W_CKV)
    ckv = ckv * lax.rsqrt(jnp.mean(jnp.square(ckv), axis=-1, keepdims=True) + RMS_EPS) * gkv_ref[...]
    ka_ref[:, 0:_W_CKV] = ckv.astype(BF16)
    ckvt_ref[...] = ckv.T.astype(BF16)
    ka_ref[:, _W_CKV:] = _rope_slabs(proj(_C_AKR, _W_AKR), cos_a, sin_a).astype(BF16)
    iq_ref[...] = _rope_slabs(proj(_C_IQ, _W_IQ), cos_i, sin_i).astype(BF16)
    ik_ref[...] = _rope_slabs(proj(_C_IK, _W_IK), cos_i, sin_i).astype(BF16)
    iw_ref[...] = proj(_C_IW, _W_IW) * ((IDX_HEADS * IDX_DIM) ** -0.5)
    ng = len(B_PATTERNS)
    for kind, c0 in enumerate((_C_BQ, _C_BK, _C_BV)):
        for g, (_, dil) in enumerate(B_PATTERNS):
            y = proj(c0 + g * B_GROUP_W, B_GROUP_W)
            if kind < 2:
                y = _rope_slabs(y, cos_b, sin_b)
            _store_residue_major(b_refs[kind * ng + g], y, scr_ref, dil)


def _input_projection(xf, w_perm, tables, g_kv, tile):
    n = xf.shape[0]
    row = lambda width: pl.BlockSpec((tile, width), lambda i: (i, 0))
    full = lambda a: pl.BlockSpec(a.shape, lambda i: (0,) * a.ndim)
    out_w = [(_W_QAR + _W_QAN, BF16), (_W_CKV + _W_AKR, BF16), None, (_W_IQ, BF16), (_W_IK, BF16),
             (_W_IW, F32)]
    shapes = [jax.ShapeDtypeStruct((A_KV_RANK, n), BF16) if o is None else jax.ShapeDtypeStruct((n, o[0]), o[1])
              for o in out_w]
    specs = [pl.BlockSpec((A_KV_RANK, tile), lambda i: (0, i)) if o is None else row(o[0]) for o in out_w]
    for _ in range(3):
        for _, dil in B_PATTERNS:
            shapes.append(jax.ShapeDtypeStruct((n // dil, dil * B_GROUP_W), BF16))
            specs.append(pl.BlockSpec((tile // dil, dil * B_GROUP_W), lambda i: (i, 0)))
    return pl.pallas_call(
        _proj_kernel,
        out_shape=shapes,
        grid=(n // tile,),
        in_specs=[row(D_MODEL), full(w_perm), row(6 * LANES), full(g_kv)],
        out_specs=specs,
        scratch_shapes=[pltpu.VMEM((B_GROUP_W // LANES, tile, LANES), F32)],
        compiler_params=_cparams("parallel"),
        name="input_projection",
    )(xf, w_perm, tables, g_kv)


def _dilated_kernel(q_ref, kc_ref, kp_ref, vc_ref, vp_ref, mq_ref, mv_ref, o_ref, lse_ref,
                    kwin_ref, vwin_ref, *, tq):
    first = pl.program_id(2) == 0
    kwin_ref[0:Q_BLOCK, :] = kp_ref[...]
    kwin_ref[Q_BLOCK:, :] = kc_ref[...]
    vwin_ref[0:Q_BLOCK, :] = vp_ref[...]
    vwin_ref[Q_BLOCK:, :] = vc_ref[...]
    t = lax.broadcasted_iota(jnp.int32, (Q_BLOCK, 2 * Q_BLOCK), 0)
    c = lax.broadcasted_iota(jnp.int32, (Q_BLOCK, 2 * Q_BLOCK), 1)
    diff = t + Q_BLOCK - c
    band = (diff >= 0) & (diff <= Q_BLOCK)
    scale = HEAD_DIM ** -0.5
    for sb in range(tq // Q_BLOCK):
        valid = band
        if sb == 0:
            valid = band & (c >= jnp.where(first, Q_BLOCK, 0))
        bias = jnp.where(valid, 0.0, NEG_BIG).astype(F32)
        q = q_ref[sb * Q_BLOCK:(sb + 1) * Q_BLOCK, :] * scale
        kw = kwin_ref[sb * Q_BLOCK:(sb + 2) * Q_BLOCK, :]
        vw = vwin_ref[sb * Q_BLOCK:(sb + 2) * Q_BLOCK, :]
        qs = jnp.concatenate([q * mq_ref[j:j + 1, :] for j in range(B_GROUP_HEADS)], axis=0)
        s = _dot_nt(qs, kw)
        o_acc = jnp.zeros((Q_BLOCK, B_GROUP_W), F32)
        lse_acc = jnp.zeros((Q_BLOCK, B_GROUP_W), F32)
        for j in range(B_GROUP_HEADS):
            sj = s[j * Q_BLOCK:(j + 1) * Q_BLOCK, :] + bias
            m = jnp.max(sj, axis=-1, keepdims=True)
            e = jnp.exp(sj - m)
            den = jnp.sum(e, axis=-1, keepdims=True)
            pv = jnp.dot(e.astype(BF16), vw, preferred_element_type=F32)
            mv = mv_ref[j:j + 1, :]
            o_acc = o_acc + (pv / den) * mv
            lse_acc = lse_acc + (m + jnp.log(den)) * mv
        o_ref[sb * Q_BLOCK:(sb + 1) * Q_BLOCK, :] = o_acc
        lse_ref[sb * Q_BLOCK:(sb + 1) * Q_BLOCK, :] = lse_acc


def _dilated_group(bq, bk, bv, mq, mv, b, dil):
    sub = bq.shape[0] // b
    tq = min(512, sub)
    nblk = tq // Q_BLOCK
    view = lambda a: a.reshape(b, sub, dil * B_GROUP_W)
    cur = pl.BlockSpec((None, tq, B_GROUP_W), lambda bi, r, i: (bi, i, r))
    prev = pl.BlockSpec((None, Q_BLOCK, B_GROUP_W),
                        lambda bi, r, i: (bi, jnp.maximum(i * nblk - 1, 0), r))
    const = lambda a: pl.BlockSpec(a.shape, lambda bi, r, i: (0, 0))
    out = cur
    o, lse = pl.pallas_call(
        functools.partial(_dilated_kernel, tq=tq),
        out_shape=[jax.ShapeDtypeStruct((b, sub, dil * B_GROUP_W), F32)] * 2,
        grid=(b, dil, sub // tq),
        in_specs=[cur, cur, prev, cur, prev, const(mq), const(mv)],
        out_specs=[out, out],
        scratch_shapes=[pltpu.VMEM((tq + Q_BLOCK, B_GROUP_W), BF16)] * 2,
        compiler_params=_cparams("parallel", "parallel", "arbitrary"),
        name=f"dilated_attention_d{dil}",
    )(view(bq), view(bk), view(bk), view(bv), view(bv), mq, mv)
    return o.reshape(b * sub, dil * B_GROUP_W), lse.reshape(b * sub, dil * B_GROUP_W)


_TK = 512

def _fold_rows(x, op, slab=8):
    parts = [x[r:r + slab, :] for r in range(0, x.shape[0], slab)]
    while len(parts) > 1:
        parts = [op(parts[k], parts[k + 1]) for k in range(0, len(parts) - 1, 2)] + parts[len(parts) & ~1:]
    return parts[0]


def _skewed_tiles(n_tiles, produce, consume, buf_a, buf_b, carry, finish=None):
    finish = finish or (lambda j, buf, c: c)
    produce(0, buf_a)

    def pair(t, c):
        j = 2 * t
        produce(j + 1, buf_b)
        c = consume(j, buf_a, buf_b, c)
        produce(j + 2, buf_a)
        return consume(j + 1, buf_b, buf_a, c)

    n_pairs = (n_tiles - 1) // 2
    carry = lax.fori_loop(0, n_pairs, pair, carry)
    j = 2 * n_pairs

    def last_two(c):
        produce(j + 1, buf_b)
        return finish(j + 1, buf_b, consume(j + 1, buf_b, buf_a, consume(j, buf_a, buf_b, c)))

    def last_one(c):
        return finish(j, buf_a, consume(j, buf_a, buf_b, c))

    return lax.cond(n_tiles - j == 2, last_two, last_one, carry)


def _dsa_kernel(qa_ref, iq_ref, iw_ref, ikt_ref, ka_ref, ckvt_ref, wuk_ref, wuv_ref, mar_ref, man_ref, miq_ref,
                out_ref, key_ref, tie_ref, iqs_ref, qcat_ref, acc_ref, sa_ref, sb_ref, ma_ref, mb_ref,
                pa_ref, pb_ref, *, n_sel):
    i = pl.program_id(1)
    q0 = i * Q_BLOCK
    n_keys = q0 + Q_BLOCK
    rows = A_HEADS * Q_BLOCK

    n_tiles = (n_keys + _TK - 1) // _TK
    tq_lane = q0 + lax.broadcasted_iota(jnp.int32, (_TK, Q_BLOCK), 1)
    krow = lax.broadcasted_iota(jnp.int32, (_TK, Q_BLOCK), 0)

    iq = iq_ref[...]
    for h in range(IDX_HEADS):
        iqs_ref[:, h * Q_BLOCK:(h + 1) * Q_BLOCK] = (iq * miq_ref[h:h + 1, :]).astype(F32).T.astype(BF16)
    iw_t = iw_ref[...].T

    def score_matmul(j, buf):
        k0 = pl.multiple_of(j * _TK, _TK)
        s = jnp.dot(ikt_ref[pl.ds(k0, _TK), :], iqs_ref[...], preferred_element_type=F32)
        for h in range(IDX_HEADS):
            buf[0][h] = s[:, h * Q_BLOCK:(h + 1) * Q_BLOCK]

    def score_keys(j, buf, _, carry):
        k0 = pl.multiple_of(j * _TK, _TK)
        sc = jnp.zeros((_TK, Q_BLOCK), F32)
        for h in range(IDX_HEADS):
            sc = sc + jnp.maximum(buf[0][h], 0.0) * iw_t[h:h + 1, :]
        bits = lax.bitcast_convert_type(jnp.where(sc == 0.0, 0.0, sc), jnp.int32)
        okey = bits ^ ((bits >> 31) & jnp.int32(0x7FFFFFFF))
        key_ref[pl.ds(k0, _TK), :] = jnp.where(krow + k0 <= tq_lane, okey, jnp.int32(INT_MIN))
        return carry

    _skewed_tiles(n_tiles, score_matmul, score_keys, (sa_ref, ma_ref), (sb_ref, mb_ref), 0)

    def count_keys(pred):
        def body(j, cnt):
            k0 = pl.multiple_of(j * _TK, _TK)
            hit = jnp.where(pred(key_ref[pl.ds(k0, _TK), :], k0), 1.0, 0.0)
            return cnt + _fold_rows(hit, jnp.add)
        cnt8 = lax.fori_loop(0, n_tiles, body, jnp.zeros((8, Q_BLOCK), F32))
        return jnp.sum(cnt8, axis=0, keepdims=True)

    def bit_step(b, carry):
        prefix, n_ge = carry
        trial = prefix | (jnp.int32(1) << (31 - b))
        t = trial ^ jnp.int32(INT_MIN)
        cnt = count_keys(lambda keys, k0: keys >= t)
        take = cnt >= float(n_sel)
        return jnp.where(take, trial, prefix), jnp.where(take, cnt, n_ge)

    prefix, n_ge = lax.fori_loop(0, 32, bit_step, (jnp.zeros((1, Q_BLOCK), jnp.int32),
                                                   jnp.full((1, Q_BLOCK), float(n_sel), F32)))
    thr = prefix ^ jnp.int32(INT_MIN)

    surplus = n_ge - float(n_sel)
    max_surplus = jnp.max(surplus)

    @pl.when(max_surplus > 0.0)
    def _():
        need = float(n_sel) - count_keys(lambda keys, k0: keys > thr)
        index_bits = (key_ref.shape[0] - 1).bit_length()
        not_tied = jnp.int32(1 << index_bits)

        def tie_positions(j, carry):
            k0 = pl.multiple_of(j * _TK, _TK)
            tied = key_ref[pl.ds(k0, _TK), :] == thr
            tie_ref[pl.ds(k0, _TK), :] = jnp.where(tied, krow + k0, not_tied)
            return carry

        lax.fori_loop(0, n_tiles, tie_positions, 0)

        def count_ties_below(bound):
            def body(j, cnt):
                k0 = pl.multiple_of(j * _TK, _TK)
                hit = jnp.where(tie_ref[pl.ds(k0, _TK), :] < bound, 1.0, 0.0)
                return cnt + _fold_rows(hit, jnp.add)
            cnt8 = lax.fori_loop(0, n_tiles, body, jnp.zeros((8, Q_BLOCK), F32))
            return jnp.sum(cnt8, axis=0, keepdims=True)

        def index_bit(b, bound):
            trial = bound | (jnp.int32(1) << (index_bits - 1 - b))
            return jnp.where(count_ties_below(trial) < need, trial, bound)

        last = lax.fori_loop(0, index_bits, index_bit, jnp.zeros((1, Q_BLOCK), jnp.int32))

        def demote(j, carry):
            k0 = pl.multiple_of(j * _TK, _TK)
            pos = tie_ref[pl.ds(k0, _TK), :]
            drop = (pos > last) & (pos < not_tied)
            key_ref[pl.ds(k0, _TK), :] = jnp.where(drop, jnp.int32(INT_MIN), key_ref[pl.ds(k0, _TK), :])
            return carry

        lax.fori_loop(0, n_tiles, demote, 0)

    q_rope = qa_ref[:, 0:_W_QAR]
    q_nope = qa_ref[:, _W_QAR:]
    scale = HEAD_DIM ** -0.5
    c_rope = A_KV_RANK
    for h in range(A_HEADS):
        q_lat = jnp.dot(q_nope * man_ref[h:h + 1, :], wuk_ref[...], preferred_element_type=F32)
        cols = slice(h * Q_BLOCK, (h + 1) * Q_BLOCK)
        qcat_ref[0:c_rope, cols] = (q_lat.astype(BF16) * scale).astype(F32).T.astype(BF16)
        qcat_ref[c_rope:, cols] = (q_rope * mar_ref[h:h + 1, :] * scale).astype(F32).T.astype(BF16)

    acc_ref[...] = jnp.zeros(acc_ref.shape, F32)

    def logit_matmul(j, buf):
        s_buf, mx_buf = buf[0], buf[1]
        k0 = pl.multiple_of(j * _TK, _TK)
        sel =(key_ref[pl.ds(k0, _TK), :] >= thr) & (krow + k0 <= tq_lane)
        bias = jnp.where(sel, 0.0, NEG_BIG).astype(F32)
        s = jnp.dot(ka_ref[pl.ds(k0, _TK), :], qcat_ref[...], preferred_element_type=F32)
        for h in range(A_HEADS):
            sh = s[:, h * Q_BLOCK:(h + 1) * Q_BLOCK] + bias
            s_buf[h] = sh
            mx_buf[:, h * Q_BLOCK:(h + 1) * Q_BLOCK] = _fold_rows(sh, jnp.maximum)

    def accumulate(j, p_buf, alpha):
        k0 = pl.multiple_of(j * _TK, _TK)
        ckv_t = ckvt_ref[:, pl.ds(k0, _TK)]
        for c in range(A_HEADS // 2):
            pv = jnp.dot(ckv_t, p_buf[c], preferred_element_type=F32)
            acc_ref[c] = acc_ref[c] * alpha[:, c * 2 * Q_BLOCK:(c + 1) * 2 * Q_BLOCK] + pv

    def softmax_pv(j, buf, other, carry):
        s_buf, mx_buf, p_buf = buf
        m_old, l_old, a_prev = carry
        m_parts, l_parts, a_parts = [], [], []
        for h in range(A_HEADS):
            cols = slice(h * Q_BLOCK, (h + 1) * Q_BLOCK)
            m_h = jnp.maximum(m_old[:, cols], jnp.max(mx_buf[:, cols], axis=0, keepdims=True))
            a_h = jnp.exp(m_old[:, cols] - m_h)
            p = jnp.exp(s_buf[h] - m_h)
            p_buf[h // 2, :, (h % 2) * Q_BLOCK:(h % 2 + 1) * Q_BLOCK] = p.astype(BF16)
            l_parts.append(a_h * l_old[:, cols] + jnp.sum(_fold_rows(p, jnp.add), axis=0, keepdims=True))
            m_parts.append(m_h)
            a_parts.append(a_h)
        accumulate(jnp.maximum(j - 1, 0), other[2], a_prev)
        return (jnp.concatenate(m_parts, axis=1), jnp.concatenate(l_parts, axis=1),
                jnp.concatenate(a_parts, axis=1))

    def last_accumulate(j, buf, carry):
        accumulate(j, buf[2], carry[2])
        return carry

    pb_ref[...] = jnp.zeros(pb_ref.shape, BF16)
    _, l_fin, _ = _skewed_tiles(n_tiles, logit_matmul, softmax_pv, (sa_ref, ma_ref, pa_ref), (sb_ref, mb_ref, pb_ref),
                                (jnp.full((1, rows), NEG_BIG, F32), jnp.zeros((1, rows), F32),
                                 jnp.ones((1, rows), F32)), finish=last_accumulate)

    inv_l = 1.0 / l_fin
    out_t = jnp.zeros((A_HEADS * HEAD_DIM, Q_BLOCK), F32)
    for h in range(A_HEADS):
        lanes = slice((h % 2) * Q_BLOCK, (h % 2 + 1) * Q_BLOCK)
        o_lat = (acc_ref[h // 2][:, lanes] * inv_l[:, h * Q_BLOCK:(h + 1) * Q_BLOCK]).astype(BF16)
        out_t = out_t + jnp.dot(wuv_ref[h], o_lat, preferred_element_type=F32)
    out_ref[...] = out_t.T.astype(BF16)


def _dsa_mixer(qa, iq, iw, ikt, ka, ckvt, wuk, wuv_t, m_ar, m_an, m_iq, b, s):
    n_sel = min(A_TOPK_MAX, s // 4)
    nq = s // Q_BLOCK
    rows = A_HEADS * Q_BLOCK
    blk = lambda width: pl.BlockSpec((Q_BLOCK, width), lambda bi, i: (bi * nq + i, 0))
    seq = lambda width: pl.BlockSpec((s, width), lambda bi, i: (bi, 0))
    const = lambda a: pl.BlockSpec(a.shape, lambda bi, i: (0,) * a.ndim)
    return pl.pallas_call(
        functools.partial(_dsa_kernel, n_sel=n_sel),
        out_shape=jax.ShapeDtypeStruct((b * s, A_HEADS * HEAD_DIM), BF16),
        grid=(b, nq),
        in_specs=[blk(_W_QAR + _W_QAN), blk(_W_IQ), blk(_W_IW), seq(_W_IK), seq(_W_CKV + _W_AKR),
                  pl.BlockSpec((A_KV_RANK, s), lambda bi, i: (0, bi)),
                  const(wuk), const(wuv_t), const(m_ar), const(m_an), const(m_iq)],
        out_specs=blk(A_HEADS * HEAD_DIM),
        scratch_shapes=[pltpu.VMEM((s, Q_BLOCK), jnp.int32),
                        pltpu.VMEM((s, Q_BLOCK), jnp.int32),
                        pltpu.VMEM((_W_IQ, rows), BF16),
                        pltpu.VMEM((A_KV_RANK + _W_AKR, rows), BF16),
                        pltpu.VMEM((A_HEADS // 2, A_KV_RANK, 2 * Q_BLOCK), F32),
                        pltpu.VMEM((A_HEADS, _TK, Q_BLOCK), F32),
                        pltpu.VMEM((A_HEADS, _TK, Q_BLOCK), F32),
                        pltpu.VMEM((8, rows), F32),
                        pltpu.VMEM((8, rows), F32),
                        pltpu.VMEM((A_HEADS // 2, _TK, 2 * Q_BLOCK), BF16),
                        pltpu.VMEM((A_HEADS // 2, _TK, 2 * Q_BLOCK), BF16)],
        compiler_params=_cparams("parallel", "arbitrary"),
        name="dsa_attention",
    )(qa, iq, iw, ikt, ka, ckvt, wuk, wuv_t, m_ar, m_an, m_iq)


def _merge_kernel(x_ref, a_ref, o1_ref, o2_ref, o3_ref, l1_ref, l2_ref, l3_ref, wg_ref, bg_ref,
                  wa_ref, wb_ref, wo_ref, lg_ref, lb_ref, h_ref, *scr):
    dils = [dil for _, dil in B_PATTERNS]
    lses = [_load_token_major(r, scr[2 * g], dils[g]) for g, r in enumerate((l1_ref, l2_ref, l3_ref))]
    outs = [_load_token_major(r, scr[2 * g + 1], dils[g]) for g, r in enumerate((o1_ref, o2_ref, o3_ref))]
    mx = jnp.maximum(jnp.maximum(lses[0], lses[1]), lses[2])
    es = [jnp.exp(l - mx) for l in lses]
    den = es[0] + es[1] + es[2]
    b_out = (es[0] / den) * outs[0] + (es[1] / den) * outs[1] + (es[2] / den) * outs[2]
    ya = jnp.dot(a_ref[...], wa_ref[...], preferred_element_type=F32)
    yb = jnp.dot(b_out.astype(BF16), wb_ref[...], preferred_element_type=F32)
    x = x_ref[...]
    gates = jax.nn.sigmoid(jnp.dot(x.astype(BF16), wg_ref[...], preferred_element_type=F32) + bg_ref[...])
    pre = gates[:, 0:D_MODEL] * ya + gates[:, D_MODEL:] * yb
    mix = jnp.dot(pre.astype(BF16), wo_ref[...], preferred_element_type=F32)
    h_ref[...] = _layer_norm(ALPHA * x + mix, lg_ref[...], lb_ref[...])


def _merge(xf, a_out, b_parts, w_gate, b_gate, wa, wb, wo, ln_g, ln_b, tile):
    n = xf.shape[0]
    row = lambda width: pl.BlockSpec((tile, width), lambda i: (i, 0))
    full = lambda a: pl.BlockSpec(a.shape, lambda i: (0,) * a.ndim)
    (o1, l1), (o2, l2), (o3, l3) = b_parts
    grp = [pl.BlockSpec((tile // dil, dil * B_GROUP_W), lambda i: (i, 0)) for _, dil in B_PATTERNS]
    return pl.pallas_call(
        _merge_kernel,
        out_shape=jax.ShapeDtypeStruct((n, D_MODEL), F32),
        grid=(n // tile,),
        in_specs=[row(D_MODEL), row(A_HEADS * HEAD_DIM)] + grp + grp + [full(w_gate), full(b_gate),
                  full(wa), full(wb), full(wo), full(ln_g), full(ln_b)],
        out_specs=row(D_MODEL),
        scratch_shapes=[pltpu.VMEM((B_GROUP_W // LANES, tile, LANES), F32)] * (2 * len(B_PATTERNS)),
        compiler_params=_cparams("parallel"),
        name="merge_output_projection",
    )(xf, a_out, o1, o2, o3, l1, l2, l3, w_gate, b_gate, wa, wb, wo, ln_g, ln_b)


def _first_max(v):
    m = jnp.max(v, axis=0, keepdims=True)
    idx = lax.broadcasted_iota(jnp.int32, v.shape, 0)
    first = jnp.min(jnp.where(v == m, idx, v.shape[0]), axis=0, keepdims=True)
    return m, idx == first


def _router_kernel(h_ref, wr_ref, rb_ref, gate_ref):
    t = h_ref.shape[0]
    gs = N_EXPERTS // N_GROUPS
    logits = lax.dot_general(wr_ref[...], h_ref[...], (((1,), (1,)), ((), ())),
                             precision=lax.Precision.HIGHEST, preferred_element_type=F32)
    scores = jax.nn.sigmoid(logits)
    biased = scores + rb_ref[...]
    gscores = []
    for g in range(N_GROUPS):
        blk = biased[g * gs:(g + 1) * gs, :]
        m1, hit = _first_max(blk)
        m2 = jnp.max(jnp.where(hit, -jnp.inf, blk), axis=0, keepdims=True)
        gscores.append(m1 + m2)
    gscore = jnp.concatenate(gscores, axis=0)
    gsel = jnp.zeros((N_GROUPS, t), F32)
    for _ in range(TOPK_GROUPS):
        _, hit = _first_max(gscore)
        gsel = jnp.where(hit, 1.0, gsel)
        gscore = jnp.where(hit, -jnp.inf, gscore)
    esel = jnp.concatenate([jnp.broadcast_to(gsel[g:g + 1, :], (gs, t)) for g in range(N_GROUPS)], axis=0)
    cand = jnp.where(esel > 0.0, biased, -jnp.inf)
    top_s = jnp.zeros((N_EXPERTS, t), F32)
    for _ in range(TOP_K):
        _, hit = _first_max(cand)
        top_s = jnp.where(hit, scores, top_s)
        cand = jnp.where(hit, -jnp.inf, cand)
    gate = top_s / jnp.sum(top_s, axis=0, keepdims=True) * ROUTED_SCALE
    gate_ref[...] = jnp.concatenate([gate, jnp.zeros((LANES - N_EXPERTS, t), F32)], axis=0).T


def _router(h, wr_t, rb_col, tile):
    n = h.shape[0]
    return pl.pallas_call(
        _router_kernel,
        out_shape=jax.ShapeDtypeStruct((n, LANES), F32),
        grid=(n // tile,),
        in_specs=[pl.BlockSpec((tile, D_MODEL), lambda i: (i, 0)),
                  pl.BlockSpec(wr_t.shape, lambda i: (0, 0)),
                  pl.BlockSpec(rb_col.shape, lambda i: (0, 0))],
        out_specs=pl.BlockSpec((tile, LANES), lambda i: (i, 0)),
        compiler_params=_cparams("parallel"),
        name="moe_router",
    )(h, wr_t, rb_col)


_EXPERTS_PER_STEP = 4


def _swiglu(hb, w1, w3, w2):
    a = jnp.dot(hb, w1, preferred_element_type=F32)
    b = jnp.dot(hb, w3, preferred_element_type=F32)
    return jnp.dot((jax.nn.silu(a) * b).astype(BF16), w2, preferred_element_type=F32)


def _experts_kernel(h_ref, gate_ref, w1_ref, w3_ref, w2_ref, s1_ref, s3_ref, s2_ref, lg_ref, lb_ref,
                    y_ref, hb_ref, acc_ref):
    step = pl.program_id(1)

    @pl.when(step == 0)
    def _():
        hb = h_ref[...].astype(BF16)
        hb_ref[...] = hb
        acc_ref[...] = _swiglu(hb, s1_ref[...], s3_ref[...], s2_ref[...])

    hb = hb_ref[...]
    gate = gate_ref[...]
    lane = lax.broadcasted_iota(jnp.int32, gate.shape, 1)
    for k in range(_EXPERTS_PER_STEP):
        e = step * _EXPERTS_PER_STEP + k
        g_col = jnp.sum(jnp.where(lane == e, gate, 0.0), axis=1, keepdims=True)
        acc_ref[...] += _swiglu(hb, w1_ref[k], w3_ref[k], w2_ref[k]) * g_col

    @pl.when(step == pl.num_programs(1) - 1)
    def _():
        y_ref[...] = _layer_norm(ALPHA * h_ref[...] + acc_ref[...], lg_ref[...], lb_ref[...])


def _experts(h, gate, w1, w3, w2, s1, s3, s2, ln_g, ln_b, tile):
    n = h.shape[0]
    ec = _EXPERTS_PER_STEP
    row = lambda width: pl.BlockSpec((tile, width), lambda i, e: (i, 0))
    full = lambda a: pl.BlockSpec(a.shape, lambda i, e: (0,) * a.ndim)
    wspec = lambda a: pl.BlockSpec((ec,) + a.shape[1:], lambda i, e: (e, 0, 0))
    return pl.pallas_call(
        _experts_kernel,
        out_shape=jax.ShapeDtypeStruct((n, D_MODEL), F32),
        grid=(n // tile, N_EXPERTS // ec),
        in_specs=[row(D_MODEL), row(LANES), wspec(w1), wspec(w3), wspec(w2),
                  full(s1), full(s3), full(s2), full(ln_g), full(ln_b)],
        out_specs=row(D_MODEL),
        scratch_shapes=[pltpu.VMEM((tile, D_MODEL), BF16), pltpu.VMEM((tile, D_MODEL), F32)],
        compiler_params=_cparams("parallel", "arbitrary"),
        name="moe_experts",
    )(h, gate, w1, w3, w2, s1, s3, s2, ln_g, ln_b)


def _rope_freqs():
    inv16 = ROPE_THETA ** (-jnp.arange(0, ROPE_DIM, 2, dtype=F32) / ROPE_DIM)
    inv8 = ROPE_THETA ** (-jnp.arange(0, IDX_ROPE, 2, dtype=F32) / IDX_ROPE)

    n16, n8 = ROPE_DIM // 2, IDX_ROPE // 2
    freq = jnp.zeros((1, LANES), F32).at[0, 0:n16].set(inv16).at[0, n16:n16 + n8].set(inv8)
    lay = _LAYOUT
    place = np.zeros((2 * LANES, 6 * LANES), np.float32)
    for k, (fidx, sign, base) in enumerate(((lay["fa"], lay["sa"], 0), (lay["fi"], lay["si"], n16),
                                            (lay["fb"], lay["sb"], 0))):
        for l in range(LANES):
            src = base + fidx[l] if fidx[l] >= 0 else n16 + n8
            place[src, 2 * k * LANES + l] = 1.0
            place[LANES + src, (2 * k + 1) * LANES + l] = sign[l]
    return freq, jnp.asarray(place)


def _layer(x, positions, w_in, b_gate, g_kv, w_uk, w_uv, w_branch_a, w_branch_b, w_o, ln1_g, ln1_b,
           w_router, router_bias, w1_e, w3_e, w2_e, ws1, ws3, ws2, ln2_g, ln2_b):
    b, s, d = x.shape
    n = b * s
    lay = _LAYOUT
    tile = min(256, n)
    xf = x.reshape(n, d)

    n_proj = _C_GA
    w_perm = (w_in[:, lay["cols"][:n_proj]] * lay["keep"][:n_proj]).astype(BF16)
    w_gate = w_in[:, _OFF_GA:_OFF_GA + 2 * D_MODEL].astype(BF16)
    wuk =jnp.transpose(w_uk, (1, 2, 0)).reshape(A_HEADS * A_NOPE, A_KV_RANK).astype(BF16)
    wuv_t = jnp.zeros((A_HEADS, A_HEADS, HEAD_DIM, A_KV_RANK), F32)
    wuv_t = wuv_t.at[jnp.arange(A_HEADS), jnp.arange(A_HEADS)].set(jnp.transpose(w_uv, (1, 2, 0)))
    wuv_t = wuv_t.reshape(A_HEADS, A_HEADS * HEAD_DIM, A_KV_RANK).astype(BF16)
    masks = {k: jnp.asarray(lay[k], BF16) for k in ("m_ar", "m_an", "m_iq", "m_bq")}
    m_bv = jnp.asarray(lay["m_bv"], F32)

    freq, place = _rope_freqs()
    tables = _rope_tables(positions.reshape(n, 1), freq, place, tile)
    qa, ka, ckvt, iq, ikt, iw, *bqkv = _input_projection(xf, w_perm, tables, g_kv.reshape(1, -1), tile)

    a_out = _dsa_mixer(qa, iq, iw, ikt, ka, ckvt, wuk, wuv_t, masks["m_ar"], masks["m_an"], masks["m_iq"], b, s)
    ng = len(B_PATTERNS)
    b_parts = [_dilated_group(bqkv[g], bqkv[ng + g], bqkv[2 * ng + g], masks["m_bq"], m_bv, b, dil)
               for g, (_, dil) in enumerate(B_PATTERNS)]

    h = _merge(xf, a_out, b_parts, w_gate, b_gate.reshape(1, -1), w_branch_a.astype(BF16),
               w_branch_b.astype(BF16), w_o.astype(BF16), ln1_g.reshape(1, -1), ln1_b.reshape(1, -1), tile)

    gate = _router(h, w_router.T, router_bias.reshape(-1, 1), min(1024, n))
    y = _experts(h, gate, w1_e.astype(BF16), w3_e.astype(BF16), w2_e.astype(BF16),
                 ws1.astype(BF16), ws3.astype(BF16), ws2.astype(BF16),
                 ln2_g.reshape(1, -1), ln2_b.reshape(1, -1), min(1024, n))
    return y.reshape(b, s, d)


def kernel(x, positions, w_in, b_gate, g_kv, w_uk, w_uv, w_branch_a, w_branch_b, w_o, ln1_g, ln1_b,
           w_router, router_bias, w1_e, w3_e, w2_e, ws1, ws3, ws2, ln2_g, ln2_b):
    h = x
    for l in range(DEPTH):
        h = _layer(h, positions, w_in[l], b_gate[l], g_kv[l], w_uk[l], w_uv[l], w_branch_a[l],
                   w_branch_b[l], w_o[l], ln1_g[l], ln1_b[l], w_router[l], router_bias[l],
                   w1_e[l], w3_e[l], w2_e[l], ws1[l], ws3[l], ws2[l], ln2_g[l], ln2_b[l])
    return h
```

```python
import functools

import jax
import jax.numpy as jnp
import numpy as np
from jax import lax
from jax.experimental import pallas as pl
from jax.experimental.pallas import tpu as pltpu

F32 = jnp.float32
BF16 = jnp.bfloat16

D_MODEL = 1024
HEAD_DIM = 64
ROPE_DIM = 16
ROPE_THETA = 500000.0
Q_BLOCK = 128
A_HEADS = 8
A_NOPE = HEAD_DIM - ROPE_DIM
A_KV_RANK = 256
A_TOPK_MAX = 256
IDX_HEADS = 8
IDX_DIM = 32
IDX_ROPE = 8
B_PATTERNS = ((128, 1), (512, 4), (2048, 16))
B_GROUP_HEADS = 4
B_HEADS = B_GROUP_HEADS * len(B_PATTERNS)
B_GROUP_W = B_GROUP_HEADS * HEAD_DIM
N_EXPERTS = 64
TOP_K = 8
N_GROUPS = 8
TOPK_GROUPS = 4
D_EXPERT = 256
ROUTED_SCALE = 2.5
DEPTH = 1
ALPHA = (2.0 * DEPTH) ** 0.25
LN_EPS = 1e-5
RMS_EPS = 1e-6

LANES = 128
VMEM_LIMIT = 56 * 1024 * 1024
NEG_BIG = -1e30
INT_MIN = -(2 ** 31)

_OFF_AQ = 0
_OFF_CKV = _OFF_AQ + A_HEADS * HEAD_DIM
_OFF_AKR = _OFF_CKV + A_KV_RANK
_OFF_IQ = _OFF_AKR + ROPE_DIM
_OFF_IK = _OFF_IQ + IDX_HEADS * IDX_DIM
_OFF_IW = _OFF_IK + IDX_DIM
_OFF_BQ = _OFF_IW + IDX_HEADS
_OFF_BK = _OFF_BQ + B_HEADS * HEAD_DIM
_OFF_BV = _OFF_BK + B_HEADS * HEAD_DIM
_OFF_GA = _OFF_BV + B_HEADS * HEAD_DIM
_OFF_GB = _OFF_GA + D_MODEL
_IN_TOTAL = _OFF_GB + D_MODEL

_W_QAR, _W_QAN, _W_CKV, _W_AKR = 128, A_HEADS * A_NOPE, A_KV_RANK, 128
_W_IQ, _W_IK, _W_IW = 256, 256, 128
_W_B = B_HEADS * HEAD_DIM
_C_QAR = 0
_C_QAN = _C_QAR + _W_QAR
_C_CKV = _C_QAN + _W_QAN
_C_AKR = _C_CKV + _W_CKV
_C_IQ = _C_AKR + _W_AKR
_C_IK = _C_IQ + _W_IQ
_C_IW = _C_IK + _W_IK
_C_BQ = _C_IW + _W_IW
_C_BK = _C_BQ + _W_B
_C_BV = _C_BK + _W_B
_C_GA = _C_BV + _W_B
_C_GB = _C_GA + D_MODEL
_P_TOTAL = _C_GB + D_MODEL


def _idx_lane(l):
    if l < 16:
        return l // 4, l % 4
    if l < 64:
        return (l - 16) // 12, 8 + (l - 16) % 12
    if l < 80:
        return (l - 64) // 4, 4 + (l - 64) % 4
    return (l - 80) // 12, 20 + (l - 80) % 12


def _b_lane(l):
    half, r = l // 64, l % 64
    which, rr = r // 32, r % 32
    if rr < 8:
        return which, half * 8 + rr
    return which, 16 + half * 24 + (rr - 8)


def _build_layout():
    cols = np.zeros((_P_TOTAL,), np.int32)
    keep = np.ones((_P_TOTAL,), np.float32)
    for l in range(128):
        half, h, f = l // 64, (l % 64) // 8, l % 8
        cols[_C_QAR + l] = _OFF_AQ + h * HEAD_DIM + half * 8 + f
        cols[_C_AKR + l] = _OFF_AKR + half * 8 + f
    for h in range(A_HEADS):
        for j in range(A_NOPE):
            cols[_C_QAN + h * A_NOPE + j] = _OFF_AQ + h * HEAD_DIM + ROPE_DIM + j
    cols[_C_CKV:_C_CKV + _W_CKV] = _OFF_CKV + np.arange(_W_CKV)
    for sl in range(2):
        for l in range(128):
            hh, d = _idx_lane(l)
            cols[_C_IQ + sl * 128 + l] = _OFF_IQ + (sl * 4 + hh) * IDX_DIM + d
            cols[_C_IK + sl * 128 + l] = _OFF_IK + d
    cols[_C_IW:_C_IW + IDX_HEADS] = _OFF_IW + np.arange(IDX_HEADS)
    keep[_C_IW + IDX_HEADS:_C_IW + _W_IW] = 0.0
    for p in range(B_HEADS // 2):
        for l in range(128):
            which, d = _b_lane(l)
            h = 2 * p + which
            cols[_C_BQ + p * 128 + l] = _OFF_BQ + h * HEAD_DIM + d
            cols[_C_BK + p * 128 + l] = _OFF_BK + h * HEAD_DIM + d
    cols[_C_BV:_C_BV + _W_B] = _OFF_BV + np.arange(_W_B)
    cols[_C_GA:_C_GA + D_MODEL] = _OFF_GA + np.arange(D_MODEL)
    cols[_C_GB:_C_GB + D_MODEL] = _OFF_GB + np.arange(D_MODEL)

    fa = np.array([l % 8 for l in range(128)])
    sa = np.array([-1.0 if l < 64 else 1.0 for l in range(128)], np.float32)
    fi = np.full((128,), -1)
    si = np.zeros((128,), np.float32)
    fb = np.full((128,), -1)
    sb = np.zeros((128,), np.float32)
    for l in range(128):
        if l < 16 or 64 <= l < 80:
            fi[l] = l % 4
            si[l] = -1.0 if l < 64 else 1.0
        if l % 32 < 8:
            fb[l] = l % 32
            sb[l] = -1.0 if l < 64 else 1.0

    m_ar = np.zeros((A_HEADS, 128), np.float32)
    for l in range(128):
        m_ar[(l % 64) // 8, l] = 1.0
    m_an = np.zeros((A_HEADS, _W_QAN), np.float32)
    for h in range(A_HEADS):
        m_an[h, h * A_NOPE:(h + 1) * A_NOPE] = 1.0
    m_iq = np.zeros((IDX_HEADS, _W_IQ), np.float32)
    for sl in range(2):
        for l in range(128):
            m_iq[sl * 4 + _idx_lane(l)[0], sl * 128 + l] = 1.0
    m_bq = np.zeros((B_GROUP_HEADS, B_GROUP_W), np.float32)
    for p in range(2):
        for l in range(128):
            m_bq[2 * p + _b_lane(l)[0], p * 128 + l] = 1.0
    m_bv = np.zeros((B_GROUP_HEADS, B_GROUP_W), np.float32)
    for j in range(B_GROUP_HEADS):
        m_bv[j, j * HEAD_DIM:(j + 1) * HEAD_DIM] = 1.0
    return dict(cols=cols, keep=keep, fa=fa, sa=sa, fi=fi, si=si, fb=fb, sb=sb,
                m_ar=m_ar, m_an=m_an, m_iq=m_iq, m_bq=m_bq, m_bv=m_bv)


_LAYOUT = _build_layout()


def _cparams(*sem):
    return pltpu.CompilerParams(dimension_semantics=sem, vmem_limit_bytes=VMEM_LIMIT)


def _layer_norm(v, g, b):
    mu = jnp.mean(v, axis=-1, keepdims=True)
    var = jnp.mean(jnp.square(v - mu), axis=-1, keepdims=True)
    return (v - mu) * lax.rsqrt(var + LN_EPS) * g + b


def _dot_nt(a, b):
    return lax.dot_general(a, b, (((1,), (1,)), ((), ())), preferred_element_type=F32)


def _rope_table_kernel(pos_ref, freq_ref, place_ref, out_ref):
    ang = pos_ref[...].astype(F32) * freq_ref[...]
    cs = jnp.concatenate([jnp.cos(ang), jnp.sin(ang)], axis=1)
    out_ref[...] = jnp.dot(cs, place_ref[...], precision=lax.Precision.HIGHEST, preferred_element_type=F32)


def _rope_tables(pos_col, freq, place, tile):
    n = pos_col.shape[0]
    return pl.pallas_call(
        _rope_table_kernel,
        out_shape=jax.ShapeDtypeStruct((n, 6 * LANES), F32),
        grid=(n // tile,),
        in_specs=[pl.BlockSpec((tile, 1), lambda i: (i, 0)),
                  pl.BlockSpec(freq.shape, lambda i: (0, 0)),
                  pl.BlockSpec(place.shape, lambda i: (0, 0))],
        out_specs=pl.BlockSpec((tile, 6 * LANES), lambda i: (i, 0)),
        compiler_params=_cparams("parallel"),
        name="rope_tables",
    )(pos_col, freq, place)


def _rope_slabs(y, cos, sin):
    outs = []
    for s in range(y.shape[1] // LANES):
        ys = y[:, s * LANES:(s + 1) * LANES]
        outs.append(ys * cos + pltpu.roll(ys, 64, 1) * sin)
    return outs[0] if len(outs) == 1 else jnp.concatenate(outs, axis=1)


def _store_residue_major(out_ref, y, scr_ref, dil):
    if dil == 1:
        out_ref[...] = y.astype(out_ref.dtype)
        return
    rows, width = y.shape[0] // dil, y.shape[1]
    for c in range(width // LANES):
        scr_ref[c] = y[:, c * LANES:(c + 1) * LANES]
    for r in range(dil):
        for c in range(width // LANES):
            lanes = slice(r * width + c * LANES, r * width + (c + 1) * LANES)
            out_ref[:, lanes] = scr_ref[c, pl.ds(r, rows, stride=dil), :].astype(out_ref.dtype)


def _load_token_major(ref, scr_ref, dil):
    if dil == 1:
        return ref[...]
    rows, width = ref.shape[0], ref.shape[1] // dil
    for r in range(dil):
        for c in range(width // LANES):
            lanes = slice(r * width + c * LANES, r * width + (c + 1) * LANES)
            scr_ref[c, pl.ds(r, rows, stride=dil), :] = ref[:, lanes]
    return jnp.concatenate([scr_ref[c] for c in range(width // LANES)], axis=1)


def _proj_kernel(x_ref, w_ref, tab_ref, gkv_ref,
                 qa_ref, ka_ref, ckvt_ref, iq_ref, ik_ref, iw_ref, *rest):
    b_refs, scr_ref = rest[:-1], rest[-1]
    xb = x_ref[...].astype(BF16)

    def proj(c0, width):
        return jnp.dot(xb, w_ref[:, c0:c0 + width], preferred_element_type=F32)

    cos_a, sin_a = tab_ref[:, 0:128], tab_ref[:, 128:256]
    cos_i, sin_i = tab_ref[:, 256:384], tab_ref[:, 384:512]
    cos_b, sin_b = tab_ref[:, 512:640], tab_ref[:, 640:768]

    qa_ref[:, 0:_W_QAR] = _rope_slabs(proj(_C_QAR, _W_QAR), cos_a, sin_a).astype(BF16)
    qa_ref[:, _W_QAR:] = proj(_C_QAN, _W_QAN).astype(BF16)
    ckv = proj(_C_CKV, _W_CKV)
    ckv = ckv * lax.rsqrt(jnp.mean(jnp.square(ckv), axis=-1, keepdims=True) + RMS_EPS) * gkv_ref[...]
    ka_ref[:, 0:_W_CKV] = ckv.astype(BF16)
    ckvt_ref[...] = ckv.T.astype(BF16)
    ka_ref[:, _W_CKV:] = _rope_slabs(proj(_C_AKR, _W_AKR), cos_a, sin_a).astype(BF16)
    iq_ref[...] = _rope_slabs(proj(_C_IQ, _W_IQ), cos_i, sin_i).astype(BF16)
    ik_ref[...] = _rope_slabs(proj(_C_IK, _W_IK), cos_i, sin_i).astype(BF16)
    iw_ref[...] = proj(_C_IW, _W_IW) * ((IDX_HEADS * IDX_DIM) ** -0.5)
    ng = len(B_PATTERNS)
    for kind, c0 in enumerate((_C_BQ, _C_BK, _C_BV)):
        for g, (_, dil) in enumerate(B_PATTERNS):
            y = proj(c0 + g * B_GROUP_W, B_GROUP_W)
            if kind < 2:
                y = _rope_slabs(y, cos_b, sin_b)
            _store_residue_major(b_refs[kind * ng + g], y, scr_ref, dil)


def _input_projection(xf, w_perm, tables, g_kv, tile):
    n = xf.shape[0]
    row = lambda width: pl.BlockSpec((tile, width), lambda i: (i, 0))
    full = lambda a: pl.BlockSpec(a.shape, lambda i: (0,) * a.ndim)
    out_w = [(_W_QAR + _W_QAN, BF16), (_W_CKV + _W_AKR, BF16), None, (_W_IQ, BF16), (_W_IK, BF16),
             (_W_IW, F32)]
    shapes = [jax.ShapeDtypeStruct((A_KV_RANK, n), BF16) if o is None else jax.ShapeDtypeStruct((n, o[0]), o[1])
              for o in out_w]
    specs = [pl.BlockSpec((A_KV_RANK, tile), lambda i: (0, i)) if o is None else row(o[0]) for o in out_w]
    for _ in range(3):
        for _, dil in B_PATTERNS:
            shapes.append(jax.ShapeDtypeStruct((n // dil, dil * B_GROUP_W), BF16))
            specs.append(pl.BlockSpec((tile // dil, dil * B_GROUP_W), lambda i: (i, 0)))
    return pl.pallas_call(
        _proj_kernel,
        out_shape=shapes,
        grid=(n // tile,),
        in_specs=[row(D_MODEL), full(w_perm), row(6 * LANES), full(g_kv)],
        out_specs=specs,
        scratch_shapes=[pltpu.VMEM((B_GROUP_W // LANES, tile, LANES), F32)],
        compiler_params=_cparams("parallel"),
        name="input_projection",
    )(xf, w_perm, tables, g_kv)


def _dilated_kernel(q_ref, kc_ref, kp_ref, vc_ref, vp_ref, mq_ref, mv_ref, o_ref, lse_ref,
                    kwin_ref, vwin_ref, *, tq):
    first = pl.program_id(2) == 0
    kwin_ref[0:Q_BLOCK, :] = kp_ref[...]
    kwin_ref[Q_BLOCK:, :] = kc_ref[...]
    vwin_ref[0:Q_BLOCK, :] = vp_ref[...]
    vwin_ref[Q_BLOCK:, :] = vc_ref[...]
    t = lax.broadcasted_iota(jnp.int32, (Q_BLOCK, 2 * Q_BLOCK), 0)
    c = lax.broadcasted_iota(jnp.int32, (Q_BLOCK, 2 * Q_BLOCK), 1)
    diff = t + Q_BLOCK - c
    band = (diff >= 0) & (diff <= Q_BLOCK)
    scale = HEAD_DIM ** -0.5
    for sb in range(tq // Q_BLOCK):
        valid = band
        if sb == 0:
            valid = band & (c >= jnp.where(first, Q_BLOCK, 0))
        bias = jnp.where(valid, 0.0, NEG_BIG).astype(F32)
        q = q_ref[sb * Q_BLOCK:(sb + 1) * Q_BLOCK, :] * scale
        kw = kwin_ref[sb * Q_BLOCK:(sb + 2) * Q_BLOCK, :]
        vw = vwin_ref[sb * Q_BLOCK:(sb + 2) * Q_BLOCK, :]
        qs = jnp.concatenate([q * mq_ref[j:j + 1, :] for j in range(B_GROUP_HEADS)], axis=0)
        s = _dot_nt(qs, kw)
        o_acc = jnp.zeros((Q_BLOCK, B_GROUP_W), F32)
        lse_acc = jnp.zeros((Q_BLOCK, B_GROUP_W), F32)
        for j in range(B_GROUP_HEADS):
            sj = s[j * Q_BLOCK:(j + 1) * Q_BLOCK, :] + bias
            m = jnp.max(sj, axis=-1, keepdims=True)
            e = jnp.exp(sj - m)
            den = jnp.sum(e, axis=-1, keepdims=True)
            pv = jnp.dot(e.astype(BF16), vw, preferred_element_type=F32)
            mv = mv_ref[j:j + 1, :]
            o_acc = o_acc + (pv / den) * mv
            lse_acc = lse_acc + (m + jnp.log(den)) * mv
        o_ref[sb * Q_BLOCK:(sb + 1) * Q_BLOCK, :] = o_acc
        lse_ref[sb * Q_BLOCK:(sb + 1) * Q_BLOCK, :] = lse_acc


def _dilated_group(bq, bk, bv, mq, mv, b, dil):
    sub = bq.shape[0] // b
    tq = min(512, sub)
    nblk = tq // Q_BLOCK
    view = lambda a: a.reshape(b, sub, dil * B_GROUP_W)
    cur = pl.BlockSpec((None, tq, B_GROUP_W), lambda bi, r, i: (bi, i, r))
    prev = pl.BlockSpec((None, Q_BLOCK, B_GROUP_W),
                        lambda bi, r, i: (bi, jnp.maximum(i * nblk - 1, 0), r))
    const = lambda a: pl.BlockSpec(a.shape, lambda bi, r, i: (0, 0))
    out = cur
    o, lse = pl.pallas_call(
        functools.partial(_dilated_kernel, tq=tq),
        out_shape=[jax.ShapeDtypeStruct((b, sub, dil * B_GROUP_W), F32)] * 2,
        grid=(b, dil, sub // tq),
        in_specs=[cur, cur, prev, cur, prev, const(mq), const(mv)],
        out_specs=[out, out],
        scratch_shapes=[pltpu.VMEM((tq + Q_BLOCK, B_GROUP_W), BF16)] * 2,
        compiler_params=_cparams("parallel", "parallel", "arbitrary"),
        name=f"dilated_attention_d{dil}",
    )(view(bq), view(bk), view(bk), view(bv), view(bv), mq, mv)
    return o.reshape(b * sub, dil * B_GROUP_W), lse.reshape(b * sub, dil * B_GROUP_W)


_TK = 512

def _fold_rows(x, op, slab=8):
    parts = [x[r:r + slab, :] for r in range(0, x.shape[0], slab)]
    while len(parts) > 1:
        parts = [op(parts[k], parts[k + 1]) for k in range(0, len(parts) - 1, 2)] + parts[len(parts) & ~1:]
    return parts[0]


def _skewed_tiles(n_tiles, produce, consume, buf_a, buf_b, carry, finish=None):
    finish = finish or (lambda j, buf, c: c)
    produce(0, buf_a)

    def pair(t, c):
        j = 2 * t
        produce(j + 1, buf_b)
        c = consume(j, buf_a, buf_b, c)
        produce(j + 2, buf_a)
        return consume(j + 1, buf_b, buf_a, c)

    n_pairs = (n_tiles - 1) // 2
    carry = lax.fori_loop(0, n_pairs, pair, carry)
    j = 2 * n_pairs

    def last_two(c):
        produce(j + 1, buf_b)
        return finish(j + 1, buf_b, consume(j + 1, buf_b, buf_a, consume(j, buf_a, buf_b, c)))

    def last_one(c):
        return finish(j, buf_a, consume(j, buf_a, buf_b, c))

    return lax.cond(n_tiles - j == 2, last_two, last_one, carry)


def _dsa_kernel(qa_ref, iq_ref, iw_ref, ikt_ref, ka_ref, ckvt_ref, wuk_ref, wuv_ref, mar_ref, man_ref, miq_ref,
                out_ref, key_ref, tie_ref, iqs_ref, qcat_ref, acc_ref, sa_ref, sb_ref, ma_ref, mb_ref,
                pa_ref, pb_ref, *, n_sel):
    i = pl.program_id(1)
    q0 = i * Q_BLOCK
    n_keys = q0 + Q_BLOCK
    rows = A_HEADS * Q_BLOCK

    n_tiles = (n_keys + _TK - 1) // _TK
    tq_lane = q0 + lax.broadcasted_iota(jnp.int32, (_TK, Q_BLOCK), 1)
    krow = lax.broadcasted_iota(jnp.int32, (_TK, Q_BLOCK), 0)

    iq = iq_ref[...]
    for h in range(IDX_HEADS):
        iqs_ref[:, h * Q_BLOCK:(h + 1) * Q_BLOCK] = (iq * miq_ref[h:h + 1, :]).astype(F32).T.astype(BF16)
    iw_t = iw_ref[...].T

    def score_matmul(j, buf):
        k0 = pl.multiple_of(j * _TK, _TK)
        s = jnp.dot(ikt_ref[pl.ds(k0, _TK), :], iqs_ref[...], preferred_element_type=F32)
        for h in range(IDX_HEADS):
            buf[0][h] = s[:, h * Q_BLOCK:(h + 1) * Q_BLOCK]

    def score_keys(j, buf, _, carry):
        k0 = pl.multiple_of(j * _TK, _TK)
        sc = jnp.zeros((_TK, Q_BLOCK), F32)
        for h in range(IDX_HEADS):
            sc = sc + jnp.maximum(buf[0][h], 0.0) * iw_t[h:h + 1, :]
        bits = lax.bitcast_convert_type(jnp.where(sc == 0.0, 0.0, sc), jnp.int32)
        okey = bits ^ ((bits >> 31) & jnp.int32(0x7FFFFFFF))
        key_ref[pl.ds(k0, _TK), :] = jnp.where(krow + k0 <= tq_lane, okey, jnp.int32(INT_MIN))
        return carry

    _skewed_tiles(n_tiles, score_matmul, score_keys, (sa_ref, ma_ref), (sb_ref, mb_ref), 0)

    def count_keys(pred):
        def body(j, cnt):
            k0 = pl.multiple_of(j * _TK, _TK)
            hit = jnp.where(pred(key_ref[pl.ds(k0, _TK), :], k0), 1.0, 0.0)
            return cnt + _fold_rows(hit, jnp.add)
        cnt8 = lax.fori_loop(0, n_tiles, body, jnp.zeros((8, Q_BLOCK), F32))
        return jnp.sum(cnt8, axis=0, keepdims=True)

    def bit_step(b, carry):
        prefix, n_ge = carry
        trial = prefix | (jnp.int32(1) << (31 - b))
        t = trial ^ jnp.int32(INT_MIN)
        cnt = count_keys(lambda keys, k0: keys >= t)
        take = cnt >= float(n_sel)
        return jnp.where(take, trial, prefix), jnp.where(take, cnt, n_ge)

    state = (jnp.zeros((1, Q_BLOCK), jnp.int32), jnp.full((1, Q_BLOCK), float(n_sel), F32))
    state = lax.fori_loop(0, 28, bit_step, state)
    for lo, hi in ((28, 30), (30, 32)):
        settled = jnp.max(state[1]) == float(n_sel)
        state = lax.cond(settled, lambda st: st, functools.partial(lax.fori_loop, lo, hi, bit_step), state)
    prefix, n_ge = state
    thr = prefix ^ jnp.int32(INT_MIN)

    surplus = n_ge - float(n_sel)
    max_surplus = jnp.max(surplus)

    @pl.when(max_surplus > 0.0)
    def _():
        need = float(n_sel) - count_keys(lambda keys, k0: keys > thr)
        index_bits = (key_ref.shape[0] - 1).bit_length()
        not_tied = jnp.int32(1 << index_bits)

        def tie_positions(j, carry):
            k0 = pl.multiple_of(j * _TK, _TK)
            tied = key_ref[pl.ds(k0, _TK), :] == thr
            tie_ref[pl.ds(k0, _TK), :] = jnp.where(tied, krow + k0, not_tied)
            return carry

        lax.fori_loop(0, n_tiles, tie_positions, 0)

        def count_ties_below(bound):
            def body(j, cnt):
                k0 = pl.multiple_of(j * _TK, _TK)
                hit = jnp.where(tie_ref[pl.ds(k0, _TK), :] < bound, 1.0, 0.0)
                return cnt + _fold_rows(hit, jnp.add)
            cnt8 = lax.fori_loop(0, n_tiles, body, jnp.zeros((8, Q_BLOCK), F32))
            return jnp.sum(cnt8, axis=0, keepdims=True)

        def index_bit(b, bound):
            trial = bound | (jnp.int32(1) << (index_bits - 1 - b))
            return jnp.where(count_ties_below(trial) < need, trial, bound)

        last = lax.fori_loop(0, index_bits, index_bit, jnp.zeros((1, Q_BLOCK), jnp.int32))

        def demote(j, carry):
            k0 = pl.multiple_of(j * _TK, _TK)
            pos = tie_ref[pl.ds(k0, _TK), :]
            drop = (pos > last) & (pos < not_tied)
            key_ref[pl.ds(k0, _TK), :] = jnp.where(drop, jnp.int32(INT_MIN), key_ref[pl.ds(k0, _TK), :])
            return carry

        lax.fori_loop(0, n_tiles, demote, 0)

    q_rope = qa_ref[:, 0:_W_QAR]
    q_nope = qa_ref[:, _W_QAR:]
    scale = HEAD_DIM ** -0.5
    c_rope = A_KV_RANK
    for h in range(A_HEADS):
        q_lat = jnp.dot(q_nope * man_ref[h:h + 1, :], wuk_ref[...], preferred_element_type=F32)
        cols = slice(h * Q_BLOCK, (h + 1) * Q_BLOCK)
        qcat_ref[0:c_rope, cols] = (q_lat.astype(BF16) * scale).astype(F32).T.astype(BF16)
        qcat_ref[c_rope:, cols] = (q_rope * mar_ref[h:h + 1, :] * scale).astype(F32).T.astype(BF16)

    acc_ref[...] = jnp.zeros(acc_ref.shape, F32)

    def logit_matmul(j, buf):
        s_buf, mx_buf = buf[0], buf[1]
        k0 = pl.multiple_of(j * _TK, _TK)
        sel =(key_ref[pl.ds(k0, _TK), :] >= thr) & (krow + k0 <= tq_lane)
        bias = jnp.where(sel, 0.0, NEG_BIG).astype(F32)
        s = jnp.dot(ka_ref[pl.ds(k0, _TK), :], qcat_ref[...], preferred_element_type=F32)
        for h in range(A_HEADS):
            sh = s[:, h * Q_BLOCK:(h + 1) * Q_BLOCK] + bias
            s_buf[h] = sh
            mx_buf[:, h * Q_BLOCK:(h + 1) * Q_BLOCK] = _fold_rows(sh, jnp.maximum)

    def accumulate(j, p_buf, alpha):
        k0 = pl.multiple_of(j * _TK, _TK)
        ckv_t = ckvt_ref[:, pl.ds(k0, _TK)]
        for c in range(A_HEADS // 2):
            pv = jnp.dot(ckv_t, p_buf[c], preferred_element_type=F32)
            acc_ref[c] = acc_ref[c] * alpha[:, c * 2 * Q_BLOCK:(c + 1) * 2 * Q_BLOCK] + pv

    def softmax_pv(j, buf, other, carry):
        s_buf, mx_buf, p_buf = buf
        m_old, l_old, a_prev = carry
        m_parts, l_parts, a_parts = [], [], []
        for h in range(A_HEADS):
            cols = slice(h * Q_BLOCK, (h + 1) * Q_BLOCK)
            m_h = jnp.maximum(m_old[:, cols], jnp.max(mx_buf[:, cols], axis=0, keepdims=True))
            a_h = jnp.exp(m_old[:, cols] - m_h)
            p = jnp.exp(s_buf[h] - m_h)
            p_buf[h // 2, :, (h % 2) * Q_BLOCK:(h % 2 + 1) * Q_BLOCK] = p.astype(BF16)
            l_parts.append(a_h * l_old[:, cols] + jnp.sum(_fold_rows(p, jnp.add), axis=0, keepdims=True))
            m_parts.append(m_h)
            a_parts.append(a_h)
        accumulate(jnp.maximum(j - 1, 0), other[2], a_prev)
        return (jnp.concatenate(m_parts, axis=1), jnp.concatenate(l_parts, axis=1),
                jnp.concatenate(a_parts, axis=1))

    def last_accumulate(j, buf, carry):
        accumulate(j, buf[2], carry[2])
        return carry

    pb_ref[...] = jnp.zeros(pb_ref.shape, BF16)
    _, l_fin, _ = _skewed_tiles(n_tiles, logit_matmul, softmax_pv, (sa_ref, ma_ref, pa_ref), (sb_ref, mb_ref, pb_ref),
                                (jnp.full((1, rows), NEG_BIG, F32), jnp.zeros((1, rows), F32),
                                 jnp.ones((1, rows), F32)), finish=last_accumulate)

    inv_l = 1.0 / l_fin
    out_t = jnp.zeros((A_HEADS * HEAD_DIM, Q_BLOCK), F32)
    for h in range(A_HEADS):
        lanes = slice((h % 2) * Q_BLOCK, (h % 2 + 1) * Q_BLOCK)
        o_lat = (acc_ref[h // 2][:, lanes] * inv_l[:, h * Q_BLOCK:(h + 1) * Q_BLOCK]).astype(BF16)
        out_t = out_t + jnp.dot(wuv_ref[h], o_lat, preferred_element_type=F32)
    out_ref[...] = out_t.T.astype(BF16)


def _dsa_mixer(qa, iq, iw, ikt, ka, ckvt, wuk, wuv_t, m_ar, m_an, m_iq, b, s):
    n_sel = min(A_TOPK_MAX, s // 4)
    nq = s // Q_BLOCK
    rows = A_HEADS * Q_BLOCK
    blk = lambda width: pl.BlockSpec((Q_BLOCK, width), lambda bi, i: (bi * nq + i, 0))
    seq = lambda width: pl.BlockSpec((s, width), lambda bi, i: (bi, 0))
    const = lambda a: pl.BlockSpec(a.shape, lambda bi, i: (0,) * a.ndim)
    return pl.pallas_call(
        functools.partial(_dsa_kernel, n_sel=n_sel),
        out_shape=jax.ShapeDtypeStruct((b * s, A_HEADS * HEAD_DIM), BF16),
        grid=(b, nq),
        in_specs=[blk(_W_QAR + _W_QAN), blk(_W_IQ), blk(_W_IW), seq(_W_IK), seq(_W_CKV + _W_AKR),
                  pl.BlockSpec((A_KV_RANK, s), lambda bi, i: (0, bi)),
                  const(wuk), const(wuv_t), const(m_ar), const(m_an), const(m_iq)],
        out_specs=blk(A_HEADS * HEAD_DIM),
        scratch_shapes=[pltpu.VMEM((s, Q_BLOCK), jnp.int32),
                        pltpu.VMEM((s, Q_BLOCK), jnp.int32),
                        pltpu.VMEM((_W_IQ, rows), BF16),
                        pltpu.VMEM((A_KV_RANK + _W_AKR, rows), BF16),
                        pltpu.VMEM((A_HEADS // 2, A_KV_RANK, 2 * Q_BLOCK), F32),
                        pltpu.VMEM((A_HEADS, _TK, Q_BLOCK), F32),
                        pltpu.VMEM((A_HEADS, _TK, Q_BLOCK), F32),
                        pltpu.VMEM((8, rows), F32),
                        pltpu.VMEM((8, rows), F32),
                        pltpu.VMEM((A_HEADS // 2, _TK, 2 * Q_BLOCK), BF16),
                        pltpu.VMEM((A_HEADS // 2, _TK, 2 * Q_BLOCK), BF16)],
        compiler_params=_cparams("parallel", "arbitrary"),
        name="dsa_attention",
    )(qa, iq, iw, ikt, ka, ckvt, wuk, wuv_t, m_ar, m_an, m_iq)


def _merge_kernel(x_ref, a_ref, o1_ref, o2_ref, o3_ref, l1_ref, l2_ref, l3_ref, wg_ref, bg_ref,
                  wa_ref, wb_ref, wo_ref, lg_ref, lb_ref, h_ref, *scr):
    dils = [dil for _, dil in B_PATTERNS]
    lses = [_load_token_major(r, scr[2 * g], dils[g]) for g, r in enumerate((l1_ref, l2_ref, l3_ref))]
    outs = [_load_token_major(r, scr[2 * g + 1], dils[g]) for g, r in enumerate((o1_ref, o2_ref, o3_ref))]
    mx = jnp.maximum(jnp.maximum(lses[0], lses[1]), lses[2])
    es = [jnp.exp(l - mx) for l in lses]
    den = es[0] + es[1] + es[2]
    b_out = (es[0] / den) * outs[0] + (es[1] / den) * outs[1] + (es[2] / den) * outs[2]
    ya = jnp.dot(a_ref[...], wa_ref[...], preferred_element_type=F32)
    yb = jnp.dot(b_out.astype(BF16), wb_ref[...], preferred_element_type=F32)
    x = x_ref[...]
    gates = jax.nn.sigmoid(jnp.dot(x.astype(BF16), wg_ref[...], preferred_element_type=F32) + bg_ref[...])
    pre = gates[:, 0:D_MODEL] * ya + gates[:, D_MODEL:] * yb
    mix = jnp.dot(pre.astype(BF16), wo_ref[...], preferred_element_type=F32)
    h_ref[...] = _layer_norm(ALPHA * x + mix, lg_ref[...], lb_ref[...])


def _merge(xf, a_out, b_parts, w_gate, b_gate, wa, wb, wo, ln_g, ln_b, tile):
    n = xf.shape[0]
    row = lambda width: pl.BlockSpec((tile, width), lambda i: (i, 0))
    full = lambda a: pl.BlockSpec(a.shape, lambda i: (0,) * a.ndim)
    (o1, l1), (o2, l2), (o3, l3) = b_parts
    grp = [pl.BlockSpec((tile // dil, dil * B_GROUP_W), lambda i: (i, 0)) for _, dil in B_PATTERNS]
    return pl.pallas_call(
        _merge_kernel,
        out_shape=jax.ShapeDtypeStruct((n, D_MODEL), F32),
        grid=(n // tile,),
        in_specs=[row(D_MODEL), row(A_HEADS * HEAD_DIM)] + grp + grp + [full(w_gate), full(b_gate),
                  full(wa), full(wb), full(wo), full(ln_g), full(ln_b)],
        out_specs=row(D_MODEL),
        scratch_shapes=[pltpu.VMEM((B_GROUP_W // LANES, tile, LANES), F32)] * (2 * len(B_PATTERNS)),
        compiler_params=_cparams("parallel"),
        name="merge_output_projection",
    )(xf, a_out, o1, o2, o3, l1, l2, l3, w_gate, b_gate, wa, wb, wo, ln_g, ln_b)


def _first_max(v):
    m = jnp.max(v, axis=0, keepdims=True)
    idx = lax.broadcasted_iota(jnp.int32, v.shape, 0)
    first = jnp.min(jnp.where(v == m, idx, v.shape[0]), axis=0, keepdims=True)
    return m, idx == first


def _router_kernel(h_ref, wr_ref, rb_ref, gate_ref):
    t = h_ref.shape[0]
    gs = N_EXPERTS // N_GROUPS
    logits = lax.dot_general(wr_ref[...], h_ref[...], (((1,), (1,)), ((), ())),
                             precision=lax.Precision.HIGHEST, preferred_element_type=F32)
    scores = jax.nn.sigmoid(logits)
    biased = scores + rb_ref[...]
    gscores = []
    for g in range(N_GROUPS):
        blk = biased[g * gs:(g + 1) * gs, :]
        m1, hit = _first_max(blk)
        m2 = jnp.max(jnp.where(hit, -jnp.inf, blk), axis=0, keepdims=True)
        gscores.append(m1 + m2)
    gscore = jnp.concatenate(gscores, axis=0)
    gsel = jnp.zeros((N_GROUPS, t), F32)
    for _ in range(TOPK_GROUPS):
        _, hit = _first_max(gscore)
        gsel = jnp.where(hit, 1.0, gsel)
        gscore = jnp.where(hit, -jnp.inf, gscore)
    esel = jnp.concatenate([jnp.broadcast_to(gsel[g:g + 1, :], (gs, t)) for g in range(N_GROUPS)], axis=0)
    cand = jnp.where(esel > 0.0, biased, -jnp.inf)
    top_s = jnp.zeros((N_EXPERTS, t), F32)
    for _ in range(TOP_K):
        _, hit = _first_max(cand)
        top_s = jnp.where(hit, scores, top_s)
        cand = jnp.where(hit, -jnp.inf, cand)
    gate = top_s / jnp.sum(top_s, axis=0, keepdims=True) * ROUTED_SCALE
    gate_ref[...] = jnp.concatenate([gate, jnp.zeros((LANES - N_EXPERTS, t), F32)], axis=0).T


def _router(h, wr_t, rb_col, tile):
    n = h.shape[0]
    return pl.pallas_call(
        _router_kernel,
        out_shape=jax.ShapeDtypeStruct((n, LANES), F32),
        grid=(n // tile,),
        in_specs=[pl.BlockSpec((tile, D_MODEL), lambda i: (i, 0)),
                  pl.BlockSpec(wr_t.shape, lambda i: (0, 0)),
                  pl.BlockSpec(rb_col.shape, lambda i: (0, 0))],
        out_specs=pl.BlockSpec((tile, LANES), lambda i: (i, 0)),
        compiler_params=_cparams("parallel"),
        name="moe_router",
    )(h, wr_t, rb_col)


_EXPERTS_PER_STEP = 4


def _swiglu(hb, w1, w3, w2):
    a = jnp.dot(hb, w1, preferred_element_type=F32)
    b = jnp.dot(hb, w3, preferred_element_type=F32)
    return jnp.dot((jax.nn.silu(a) * b).astype(BF16), w2, preferred_element_type=F32)


def _experts_kernel(h_ref, gate_ref, w1_ref, w3_ref, w2_ref, s1_ref, s3_ref, s2_ref, lg_ref, lb_ref,
                    y_ref, hb_ref, acc_ref):
    step = pl.program_id(1)

    @pl.when(step == 0)
    def _():
        hb = h_ref[...].astype(BF16)
        hb_ref[...] = hb
        acc_ref[...] = _swiglu(hb, s1_ref[...], s3_ref[...], s2_ref[...])

    hb = hb_ref[...]
    gate = gate_ref[...]
    lane = lax.broadcasted_iota(jnp.int32, gate.shape, 1)
    for k in range(_EXPERTS_PER_STEP):
        e = step * _EXPERTS_PER_STEP + k
        g_col = jnp.sum(jnp.where(lane == e, gate, 0.0), axis=1, keepdims=True)
        acc_ref[...] += _swiglu(hb, w1_ref[k], w3_ref[k], w2_ref[k]) * g_col

    @pl.when(step == pl.num_programs(1) - 1)
    def _():
        y_ref[...] = _layer_norm(ALPHA * h_ref[...] + acc_ref[...], lg_ref[...], lb_ref[...])


def _experts(h, gate, w1, w3, w2, s1, s3, s2, ln_g, ln_b, tile):
    n = h.shape[0]
    ec = _EXPERTS_PER_STEP
    row = lambda width: pl.BlockSpec((tile, width), lambda i, e: (i, 0))
    full = lambda a: pl.BlockSpec(a.shape, lambda i, e: (0,) * a.ndim)
    wspec = lambda a: pl.BlockSpec((ec,) + a.shape[1:], lambda i, e: (e, 0, 0))
    return pl.pallas_call(
        _experts_kernel,
        out_shape=jax.ShapeDtypeStruct((n, D_MODEL), F32),
        grid=(n // tile, N_EXPERTS // ec),
        in_specs=[row(D_MODEL), row(LANES), wspec(w1), wspec(w3), wspec(w2),
                  full(s1), full(s3), full(s2), full(ln_g), full(ln_b)],
        out_specs=row(D_MODEL),
        scratch_shapes=[pltpu.VMEM((tile, D_MODEL), BF16), pltpu.VMEM((tile, D_MODEL), F32)],
        compiler_params=_cparams("parallel", "arbitrary"),
        name="moe_experts",
    )(h, gate, w1, w3, w2, s1, s3, s2, ln_g, ln_b)


def _rope_freqs():
    inv16 = ROPE_THETA ** (-jnp.arange(0, ROPE_DIM, 2, dtype=F32) / ROPE_DIM)
    inv8 = ROPE_THETA ** (-jnp.arange(0, IDX_ROPE, 2, dtype=F32) / IDX_ROPE)

    n16, n8 = ROPE_DIM // 2, IDX_ROPE // 2
    freq = jnp.zeros((1, LANES), F32).at[0, 0:n16].set(inv16).at[0, n16:n16 + n8].set(inv8)
    lay = _LAYOUT
    place = np.zeros((2 * LANES, 6 * LANES), np.float32)
    for k, (fidx, sign, base) in enumerate(((lay["fa"], lay["sa"], 0), (lay["fi"], lay["si"], n16),
                                            (lay["fb"], lay["sb"], 0))):
        for l in range(LANES):
            src = base + fidx[l] if fidx[l] >= 0 else n16 + n8
            place[src, 2 * k * LANES + l] = 1.0
            place[LANES + src, (2 * k + 1) * LANES + l] = sign[l]
    return freq, jnp.asarray(place)


def _layer(x, positions, w_in, b_gate, g_kv, w_uk, w_uv, w_branch_a, w_branch_b, w_o, ln1_g, ln1_b,
           w_router, router_bias, w1_e, w3_e, w2_e, ws1, ws3, ws2, ln2_g, ln2_b):
    b, s, d = x.shape
    n = b * s
    lay = _LAYOUT
    tile = min(256, n)
    xf = x.reshape(n, d)

    n_proj = _C_GA
    w_perm = (w_in[:, lay["cols"][:n_proj]] * lay["keep"][:n_proj]).astype(BF16)
    w_gate = w_in[:, _OFF_GA:_OFF_GA + 2 * D_MODEL].astype(BF16)
    wuk =jnp.transpose(w_uk, (1, 2, 0)).reshape(A_HEADS * A_NOPE, A_KV_RANK).astype(BF16)
    wuv_t = jnp.zeros((A_HEADS, A_HEADS, HEAD_DIM, A_KV_RANK), F32)
    wuv_t = wuv_t.at[jnp.arange(A_HEADS), jnp.arange(A_HEADS)].set(jnp.transpose(w_uv, (1, 2, 0)))
    wuv_t = wuv_t.reshape(A_HEADS, A_HEADS * HEAD_DIM, A_KV_RANK).astype(BF16)
    masks = {k: jnp.asarray(lay[k], BF16) for k in ("m_ar", "m_an", "m_iq", "m_bq")}
    m_bv = jnp.asarray(lay["m_bv"], F32)

    freq, place = _rope_freqs()
    tables = _rope_tables(positions.reshape(n, 1), freq, place, tile)
    qa, ka, ckvt, iq, ikt, iw, *bqkv = _input_projection(xf, w_perm, tables, g_kv.reshape(1, -1), tile)

    a_out = _dsa_mixer(qa, iq, iw, ikt, ka, ckvt, wuk, wuv_t, masks["m_ar"], masks["m_an"], masks["m_iq"], b, s)
    ng = len(B_PATTERNS)
    b_parts = [_dilated_group(bqkv[g], bqkv[ng + g], bqkv[2 * ng + g], masks["m_bq"], m_bv, b, dil)
               for g, (_, dil) in enumerate(B_PATTERNS)]

    h = _merge(xf, a_out, b_parts, w_gate, b_gate.reshape(1, -1), w_branch_a.astype(BF16),
               w_branch_b.astype(BF16), w_o.astype(BF16), ln1_g.reshape(1, -1), ln1_b.reshape(1, -1), tile)

    gate = _router(h, w_router.T, router_bias.reshape(-1, 1), min(1024, n))
    y = _experts(h, gate, w1_e.astype(BF16), w3_e.astype(BF16), w2_e.astype(BF16),
                 ws1.astype(BF16), ws3.astype(BF16), ws2.astype(BF16),
                 ln2_g.reshape(1, -1), ln2_b.reshape(1, -1), min(1024, n))
    return y.reshape(b, s, d)


def kernel(x, positions, w_in, b_gate, g_kv, w_uk, w_uv, w_branch_a, w_branch_b, w_o, ln1_g, ln1_b,
           w_router, router_bias, w1_e, w3_e, w2_e, ws1, ws3, ws2, ln2_g, ln2_b):
    h = x
    for l in range(DEPTH):
        h = _layer(h, positions, w_in[l], b_gate[l], g_kv[l], w_uk[l], w_uv[l], w_branch_a[l],
                   w_branch_b[l], w_o[l], ln1_g[l], ln1_b[l], w_router[l], router_bias[l],
                   w1_e[l], w3_e[l], w2_e[l], ws1[l], ws3[l], ws2[l], ln2_g[l], ln2_b[l])
    return h
```

```python
import functools

import jax
import jax.numpy as jnp
import numpy as np
from jax import lax
from jax.experimental import pallas as pl
from jax.experimental.pallas import tpu as pltpu

F32 = jnp.float32
BF16 = jnp.bfloat16

D_MODEL = 1024
HEAD_DIM = 64
ROPE_DIM = 16
ROPE_THETA = 500000.0
Q_BLOCK = 128
A_HEADS = 8
A_NOPE = HEAD_DIM - ROPE_DIM
A_KV_RANK = 256
A_TOPK_MAX = 256
IDX_HEADS = 8
IDX_DIM = 32
IDX_ROPE = 8
B_PATTERNS = ((128, 1), (512, 4), (2048, 16))
B_GROUP_HEADS = 4
B_HEADS = B_GROUP_HEADS * len(B_PATTERNS)
B_GROUP_W = B_GROUP_HEADS * HEAD_DIM
N_EXPERTS = 64
TOP_K = 8
N_GROUPS = 8
TOPK_GROUPS = 4
D_EXPERT = 256
ROUTED_SCALE = 2.5
DEPTH = 1
ALPHA = (2.0 * DEPTH) ** 0.25
LN_EPS = 1e-5
RMS_EPS = 1e-6

LANES = 128
VMEM_LIMIT = 56 * 1024 * 1024
NEG_BIG = -1e30
INT_MIN = -(2 ** 31)

_OFF_AQ = 0
_OFF_CKV = _OFF_AQ + A_HEADS * HEAD_DIM
_OFF_AKR = _OFF_CKV + A_KV_RANK
_OFF_IQ = _OFF_AKR + ROPE_DIM
_OFF_IK = _OFF_IQ + IDX_HEADS * IDX_DIM
_OFF_IW = _OFF_IK + IDX_DIM
_OFF_BQ = _OFF_IW + IDX_HEADS
_OFF_BK = _OFF_BQ + B_HEADS * HEAD_DIM
_OFF_BV = _OFF_BK + B_HEADS * HEAD_DIM
_OFF_GA = _OFF_BV + B_HEADS * HEAD_DIM
_OFF_GB = _OFF_GA + D_MODEL
_IN_TOTAL = _OFF_GB + D_MODEL

_W_QAR, _W_QAN, _W_CKV, _W_AKR = 128, A_HEADS * A_NOPE, A_KV_RANK, 128
_W_IQ, _W_IK, _W_IW = 256, 256, 128
_W_B = B_HEADS * HEAD_DIM
_C_QAR = 0
_C_QAN = _C_QAR + _W_QAR
_C_CKV = _C_QAN + _W_QAN
_C_AKR = _C_CKV + _W_CKV
_C_IQ = _C_AKR + _W_AKR
_C_IK = _C_IQ + _W_IQ
_C_IW = _C_IK + _W_IK
_C_BQ = _C_IW + _W_IW
_C_BK = _C_BQ + _W_B
_C_BV = _C_BK + _W_B
_C_GA = _C_BV + _W_B
_C_GB = _C_GA + D_MODEL
_P_TOTAL = _C_GB + D_MODEL


def _idx_lane(l):
    if l < 16:
        return l // 4, l % 4
    if l < 64:
        return (l - 16) // 12, 8 + (l - 16) % 12
    if l < 80:
        return (l - 64) // 4, 4 + (l - 64) % 4
    return (l - 80) // 12, 20 + (l - 80) % 12


def _b_lane(l):
    half, r = l // 64, l % 64
    which, rr = r // 32, r % 32
    if rr < 8:
        return which, half * 8 + rr
    return which, 16 + half * 24 + (rr - 8)


def _build_layout():
    cols = np.zeros((_P_TOTAL,), np.int32)
    keep = np.ones((_P_TOTAL,), np.float32)
    for l in range(128):
        half, h, f = l // 64, (l % 64) // 8, l % 8
        cols[_C_QAR + l] = _OFF_AQ + h * HEAD_DIM + half * 8 + f
        cols[_C_AKR + l] = _OFF_AKR + half * 8 + f
    for h in range(A_HEADS):
        for j in range(A_NOPE):
            cols[_C_QAN + h * A_NOPE + j] = _OFF_AQ + h * HEAD_DIM + ROPE_DIM + j
    cols[_C_CKV:_C_CKV + _W_CKV] = _OFF_CKV + np.arange(_W_CKV)
    for sl in range(2):
        for l in range(128):
            hh, d = _idx_lane(l)
            cols[_C_IQ + sl * 128 + l] = _OFF_IQ + (sl * 4 + hh) * IDX_DIM + d
            cols[_C_IK + sl * 128 + l] = _OFF_IK + d
    cols[_C_IW:_C_IW + IDX_HEADS] = _OFF_IW + np.arange(IDX_HEADS)
    keep[_C_IW + IDX_HEADS:_C_IW + _W_IW] = 0.0
    for p in range(B_HEADS // 2):
        for l in range(128):
            which, d = _b_lane(l)
            h = 2 * p + which
            cols[_C_BQ + p * 128 + l] = _OFF_BQ + h * HEAD_DIM + d
            cols[_C_BK + p * 128 + l] = _OFF_BK + h * HEAD_DIM + d
    cols[_C_BV:_C_BV + _W_B] = _OFF_BV + np.arange(_W_B)
    cols[_C_GA:_C_GA + D_MODEL] = _OFF_GA + np.arange(D_MODEL)
    cols[_C_GB:_C_GB + D_MODEL] = _OFF_GB + np.arange(D_MODEL)

    fa = np.array([l % 8 for l in range(128)])
    sa = np.array([-1.0 if l < 64 else 1.0 for l in range(128)], np.float32)
    fi = np.full((128,), -1)
    si = np.zeros((128,), np.float32)
    fb = np.full((128,), -1)
    sb = np.zeros((128,), np.float32)
    for l in range(128):
        if l < 16 or 64 <= l < 80:
            fi[l] = l % 4
            si[l] = -1.0 if l < 64 else 1.0
        if l % 32 < 8:
            fb[l] = l % 32
            sb[l] = -1.0 if l < 64 else 1.0

    m_ar = np.zeros((A_HEADS, 128), np.float32)
    for l in range(128):
        m_ar[(l % 64) // 8, l] = 1.0
    m_iq = np.zeros((IDX_HEADS, _W_IQ), np.float32)
    for sl in range(2):
        for l in range(128):
            m_iq[sl * 4 + _idx_lane(l)[0], sl * 128 + l] = 1.0
    m_bq = np.zeros((B_GROUP_HEADS, B_GROUP_W), np.float32)
    for p in range(2):
        for l in range(128):
            m_bq[2 * p + _b_lane(l)[0], p * 128 + l] = 1.0
    m_bv = np.zeros((B_GROUP_HEADS, B_GROUP_W), np.float32)
    for j in range(B_GROUP_HEADS):
        m_bv[j, j * HEAD_DIM:(j + 1) * HEAD_DIM] = 1.0
    return dict(cols=cols, keep=keep, fa=fa, sa=sa, fi=fi, si=si, fb=fb, sb=sb,
                m_ar=m_ar, m_iq=m_iq, m_bq=m_bq, m_bv=m_bv)


_LAYOUT = _build_layout()


def _cparams(*sem):
    return pltpu.CompilerParams(dimension_semantics=sem, vmem_limit_bytes=VMEM_LIMIT)


def _layer_norm(v, g, b):
    mu = jnp.mean(v, axis=-1, keepdims=True)
    var = jnp.mean(jnp.square(v - mu), axis=-1, keepdims=True)
    return (v - mu) * lax.rsqrt(var + LN_EPS) * g + b


def _dot_nt(a, b):
    return lax.dot_general(a, b, (((1,), (1,)), ((), ())), preferred_element_type=F32)


def _rope_table_kernel(pos_ref, freq_ref, place_ref, out_ref):
    ang = pos_ref[...].astype(F32) * freq_ref[...]
    cs = jnp.concatenate([jnp.cos(ang), jnp.sin(ang)], axis=1)
    out_ref[...] = jnp.dot(cs, place_ref[...], precision=lax.Precision.HIGHEST, preferred_element_type=F32)


def _rope_tables(pos_col, freq, place, tile):
    n = pos_col.shape[0]
    return pl.pallas_call(
        _rope_table_kernel,
        out_shape=jax.ShapeDtypeStruct((n, 6 * LANES), F32),
        grid=(n // tile,),
        in_specs=[pl.BlockSpec((tile, 1), lambda i: (i, 0)),
                  pl.BlockSpec(freq.shape, lambda i: (0, 0)),
                  pl.BlockSpec(place.shape, lambda i: (0, 0))],
        out_specs=pl.BlockSpec((tile, 6 * LANES), lambda i: (i, 0)),
        compiler_params=_cparams("parallel"),
        name="rope_tables",
    )(pos_col, freq, place)


def _rope_slabs(y, cos, sin):
    outs = []
    for s in range(y.shape[1] // LANES):
        ys = y[:, s * LANES:(s + 1) * LANES]
        outs.append(ys * cos + pltpu.roll(ys, 64, 1) * sin)
    return outs[0] if len(outs) == 1 else jnp.concatenate(outs, axis=1)


def _store_residue_major(out_ref, y, scr_ref, dil):
    if dil == 1:
        out_ref[...] = y.astype(out_ref.dtype)
        return
    rows, width = y.shape[0] // dil, y.shape[1]
    for c in range(width // LANES):
        scr_ref[c] = y[:, c * LANES:(c + 1) * LANES]
    for r in range(dil):
        for c in range(width // LANES):
            lanes = slice(r * width + c * LANES, r * width + (c + 1) * LANES)
            out_ref[:, lanes] = scr_ref[c, pl.ds(r, rows, stride=dil), :].astype(out_ref.dtype)


def _load_token_major(ref, scr_ref, dil):
    if dil == 1:
        return ref[...]
    rows, width = ref.shape[0], ref.shape[1] // dil
    for r in range(dil):
        for c in range(width // LANES):
            lanes = slice(r * width + c * LANES, r * width + (c + 1) * LANES)
            scr_ref[c, pl.ds(r, rows, stride=dil), :] = ref[:, lanes]
    return jnp.concatenate([scr_ref[c] for c in range(width // LANES)], axis=1)


def _proj_kernel(x_ref, w_ref, tab_ref, gkv_ref,
                 qa_ref, ka_ref, ckvt_ref, iq_ref, ik_ref, iw_ref, *rest):
    b_refs, scr_ref = rest[:-1], rest[-1]
    xb = x_ref[...].astype(BF16)

    def proj(c0, width):
        return jnp.dot(xb, w_ref[:, c0:c0 + width], preferred_element_type=F32)

    cos_a, sin_a = tab_ref[:, 0:128], tab_ref[:, 128:256]
    cos_i, sin_i = tab_ref[:, 256:384], tab_ref[:, 384:512]
    cos_b, sin_b = tab_ref[:, 512:640], tab_ref[:, 640:768]

    qa_ref[:, 0:_W_QAR] = _rope_slabs(proj(_C_QAR, _W_QAR), cos_a, sin_a).astype(BF16)
    qa_ref[:, _W_QAR:] = proj(_C_QAN, _W_QAN).astype(BF16)
    ckv = proj(_C_CKV, _W_CKV)
    ckv = ckv * lax.rsqrt(jnp.mean(jnp.square(ckv), axis=-1, keepdims=True) + RMS_EPS) * gkv_ref[...]
    ka_ref[:, 0:_W_CKV] = ckv.astype(BF16)
    ckvt_ref[...] = ckv.T.astype(BF16)
    ka_ref[:, _W_CKV:] = _rope_slabs(proj(_C_AKR, _W_AKR), cos_a, sin_a).astype(BF16)
    iq_ref[...] = _rope_slabs(proj(_C_IQ, _W_IQ), cos_i, sin_i).astype(BF16)
    ik_ref[...] = _rope_slabs(proj(_C_IK, _W_IK), cos_i, sin_i).astype(BF16)
    iw_ref[...] = proj(_C_IW, _W_IW) * ((IDX_HEADS * IDX_DIM) ** -0.5)
    ng = len(B_PATTERNS)
    for kind, c0 in enumerate((_C_BQ, _C_BK, _C_BV)):
        for g, (_, dil) in enumerate(B_PATTERNS):
            y = proj(c0 + g * B_GROUP_W, B_GROUP_W)
            if kind < 2:
                y = _rope_slabs(y, cos_b, sin_b)
            _store_residue_major(b_refs[kind * ng + g], y, scr_ref, dil)


def _input_projection(xf, w_perm, tables, g_kv, tile):
    n = xf.shape[0]
    row = lambda width: pl.BlockSpec((tile, width), lambda i: (i, 0))
    full = lambda a: pl.BlockSpec(a.shape, lambda i: (0,) * a.ndim)
    out_w = [(_W_QAR + _W_QAN, BF16), (_W_CKV + _W_AKR, BF16), None, (_W_IQ, BF16), (_W_IK, BF16),
             (_W_IW, F32)]
    shapes = [jax.ShapeDtypeStruct((A_KV_RANK, n), BF16) if o is None else jax.ShapeDtypeStruct((n, o[0]), o[1])
              for o in out_w]
    specs = [pl.BlockSpec((A_KV_RANK, tile), lambda i: (0, i)) if o is None else row(o[0]) for o in out_w]
    for _ in range(3):
        for _, dil in B_PATTERNS:
            shapes.append(jax.ShapeDtypeStruct((n // dil, dil * B_GROUP_W), BF16))
            specs.append(pl.BlockSpec((tile // dil, dil * B_GROUP_W), lambda i: (i, 0)))
    return pl.pallas_call(
        _proj_kernel,
        out_shape=shapes,
        grid=(n // tile,),
        in_specs=[row(D_MODEL), full(w_perm), row(6 * LANES), full(g_kv)],
        out_specs=specs,
        scratch_shapes=[pltpu.VMEM((B_GROUP_W // LANES, tile, LANES), F32)],
        compiler_params=_cparams("parallel"),
        name="input_projection",
    )(xf, w_perm, tables, g_kv)


def _dilated_kernel(q_ref, kc_ref, kp_ref, vc_ref, vp_ref, mq_ref, mv_ref, o_ref, lse_ref,
                    kwin_ref, vwin_ref, *, tq):
    first = pl.program_id(2) == 0
    kwin_ref[0:Q_BLOCK, :] = kp_ref[...]
    kwin_ref[Q_BLOCK:, :] = kc_ref[...]
    vwin_ref[0:Q_BLOCK, :] = vp_ref[...]
    vwin_ref[Q_BLOCK:, :] = vc_ref[...]
    t = lax.broadcasted_iota(jnp.int32, (Q_BLOCK, 2 * Q_BLOCK), 0)
    c = lax.broadcasted_iota(jnp.int32, (Q_BLOCK, 2 * Q_BLOCK), 1)
    diff = t + Q_BLOCK - c
    band = (diff >= 0) & (diff <= Q_BLOCK)
    scale = HEAD_DIM ** -0.5
    for sb in range(tq // Q_BLOCK):
        valid = band
        if sb == 0:
            valid = band & (c >= jnp.where(first, Q_BLOCK, 0))
        bias = jnp.where(valid, 0.0, NEG_BIG).astype(F32)
        q = q_ref[sb * Q_BLOCK:(sb + 1) * Q_BLOCK, :] * scale
        kw = kwin_ref[sb * Q_BLOCK:(sb + 2) * Q_BLOCK, :]
        vw = vwin_ref[sb * Q_BLOCK:(sb + 2) * Q_BLOCK, :]
        qs = jnp.concatenate([q * mq_ref[j:j + 1, :] for j in range(B_GROUP_HEADS)], axis=0)
        s = _dot_nt(qs, kw)
        o_acc = jnp.zeros((Q_BLOCK, B_GROUP_W), F32)
        lse_acc = jnp.zeros((Q_BLOCK, B_GROUP_W), F32)
        for j in range(B_GROUP_HEADS):
            sj = s[j * Q_BLOCK:(j + 1) * Q_BLOCK, :] + bias
            m = jnp.max(sj, axis=-1, keepdims=True)
            e = jnp.exp(sj - m)
            den = jnp.sum(e, axis=-1, keepdims=True)
            pv = jnp.dot(e.astype(BF16), vw, preferred_element_type=F32)
            mv = mv_ref[j:j + 1, :]
            o_acc = o_acc + (pv / den) * mv
            lse_acc = lse_acc + (m + jnp.log(den)) * mv
        o_ref[sb * Q_BLOCK:(sb + 1) * Q_BLOCK, :] = o_acc
        lse_ref[sb * Q_BLOCK:(sb + 1) * Q_BLOCK, :] = lse_acc


def _dilated_group(bq, bk, bv, mq, mv, b, dil):
    sub = bq.shape[0] // b
    tq = min(512, sub)
    nblk = tq // Q_BLOCK
    view = lambda a: a.reshape(b, sub, dil * B_GROUP_W)
    cur = pl.BlockSpec((None, tq, B_GROUP_W), lambda bi, r, i: (bi, i, r))
    prev = pl.BlockSpec((None, Q_BLOCK, B_GROUP_W),
                        lambda bi, r, i: (bi, jnp.maximum(i * nblk - 1, 0), r))
    const = lambda a: pl.BlockSpec(a.shape, lambda bi, r, i: (0, 0))
    out = cur
    o, lse = pl.pallas_call(
        functools.partial(_dilated_kernel, tq=tq),
        out_shape=[jax.ShapeDtypeStruct((b, sub, dil * B_GROUP_W), F32)] * 2,
        grid=(b, dil, sub // tq),
        in_specs=[cur, cur, prev, cur, prev, const(mq), const(mv)],
        out_specs=[out, out],
        scratch_shapes=[pltpu.VMEM((tq + Q_BLOCK, B_GROUP_W), BF16)] * 2,
        compiler_params=_cparams("parallel", "parallel", "arbitrary"),
        name=f"dilated_attention_d{dil}",
    )(view(bq), view(bk), view(bk), view(bv), view(bv), mq, mv)
    return o.reshape(b * sub, dil * B_GROUP_W), lse.reshape(b * sub, dil * B_GROUP_W)


_TK = 512

def _fold_rows(x, op, slab=8):
    parts = [x[r:r + slab, :] for r in range(0, x.shape[0], slab)]
    while len(parts) > 1:
        parts = [op(parts[k], parts[k + 1]) for k in range(0, len(parts) - 1, 2)] + parts[len(parts) & ~1:]
    return parts[0]


def _skewed_tiles(n_tiles, produce, consume, buf_a, buf_b, carry, finish=None):
    finish = finish or (lambda j, buf, c: c)
    produce(0, buf_a)

    def pair(t, c):
        j = 2 * t
        produce(j + 1, buf_b)
        c = consume(j, buf_a, buf_b, c)
        produce(j + 2, buf_a)
        return consume(j + 1, buf_b, buf_a, c)

    n_pairs = (n_tiles - 1) // 2
    carry = lax.fori_loop(0, n_pairs, pair, carry)
    j = 2 * n_pairs

    def last_two(c):
        produce(j + 1, buf_b)
        return finish(j + 1, buf_b, consume(j + 1, buf_b, buf_a, consume(j, buf_a, buf_b, c)))

    def last_one(c):
        return finish(j, buf_a, consume(j, buf_a, buf_b, c))

    return lax.cond(n_tiles - j == 2, last_two, last_one, carry)


def _dsa_kernel(qa_ref, iq_ref, iw_ref, ikt_ref, ka_ref, ckvt_ref, wuk_ref, wuv_ref, mar_ref, miq_ref,
                out_ref, key_ref, tie_ref, iqs_ref, qcat_ref, acc_ref, sa_ref, sb_ref, ma_ref, mb_ref,
                pa_ref, pb_ref, *, n_sel):
    i = pl.program_id(1)
    q0 = i * Q_BLOCK
    n_keys = q0 + Q_BLOCK
    rows = A_HEADS * Q_BLOCK

    n_tiles = (n_keys + _TK - 1) // _TK
    tq_lane = q0 + lax.broadcasted_iota(jnp.int32, (_TK, Q_BLOCK), 1)
    krow = lax.broadcasted_iota(jnp.int32, (_TK, Q_BLOCK), 0)

    iq_t = iq_ref[...].astype(F32).T
    for h in range(IDX_HEADS):
        iqs_ref[:, h * Q_BLOCK:(h + 1) * Q_BLOCK] = (iq_t * miq_ref[:, h:h + 1].astype(F32)).astype(BF16)
    iw_t = iw_ref[...].T

    def score_matmul(j, buf):
        k0 = pl.multiple_of(j * _TK, _TK)
        s = jnp.dot(ikt_ref[pl.ds(k0, _TK), :], iqs_ref[...], preferred_element_type=F32)
        for h in range(IDX_HEADS):
            buf[0][h] = s[:, h * Q_BLOCK:(h + 1) * Q_BLOCK]

    def score_keys(j, buf, _, carry):
        k0 = pl.multiple_of(j * _TK, _TK)
        sc = jnp.zeros((_TK, Q_BLOCK), F32)
        for h in range(IDX_HEADS):
            sc = sc + jnp.maximum(buf[0][h], 0.0) * iw_t[h:h + 1, :]
        bits = lax.bitcast_convert_type(jnp.where(sc == 0.0, 0.0, sc), jnp.int32)
        okey = bits ^ ((bits >> 31) & jnp.int32(0x7FFFFFFF))
        key_ref[pl.ds(k0, _TK), :] = jnp.where(krow + k0 <= tq_lane, okey, jnp.int32(INT_MIN))
        return carry

    _skewed_tiles(n_tiles, score_matmul, score_keys, (sa_ref, ma_ref), (sb_ref, mb_ref), 0)

    def count_keys(pred):
        def body(j, cnt):
            k0 = pl.multiple_of(j * _TK, _TK)
            hit = jnp.where(pred(key_ref[pl.ds(k0, _TK), :], k0), 1.0, 0.0)
            return cnt + _fold_rows(hit, jnp.add)
        cnt8 = lax.fori_loop(0, n_tiles, body, jnp.zeros((8, Q_BLOCK), F32))
        return jnp.sum(cnt8, axis=0, keepdims=True)

    def bit_step(b, carry):
        prefix, n_ge = carry
        trial = prefix | (jnp.int32(1) << (31 - b))
        t = trial ^ jnp.int32(INT_MIN)
        cnt = count_keys(lambda keys, k0: keys >= t)
        take = cnt >= float(n_sel)
        return jnp.where(take, trial, prefix), jnp.where(take, cnt, n_ge)

    state = (jnp.zeros((1, Q_BLOCK), jnp.int32), jnp.full((1, Q_BLOCK), float(n_sel), F32))
    state = lax.fori_loop(0, 28, bit_step, state)
    for lo, hi in ((28, 30), (30, 32)):
        settled = jnp.max(state[1]) == float(n_sel)
        state = lax.cond(settled, lambda st: st, functools.partial(lax.fori_loop, lo, hi, bit_step), state)
    prefix, n_ge = state
    thr = prefix ^ jnp.int32(INT_MIN)

    surplus = n_ge - float(n_sel)
    max_surplus = jnp.max(surplus)

    @pl.when(max_surplus > 0.0)
    def _():
        need = float(n_sel) - count_keys(lambda keys, k0: keys > thr)
        index_bits = (key_ref.shape[0] - 1).bit_length()
        not_tied = jnp.int32(1 << index_bits)

        def tie_positions(j, carry):
            k0 = pl.multiple_of(j * _TK, _TK)
            tied = key_ref[pl.ds(k0, _TK), :] == thr
            tie_ref[pl.ds(k0, _TK), :] = jnp.where(tied, krow + k0, not_tied)
            return carry

        lax.fori_loop(0, n_tiles, tie_positions, 0)

        def count_ties_below(bound):
            def body(j, cnt):
                k0 = pl.multiple_of(j * _TK, _TK)
                hit = jnp.where(tie_ref[pl.ds(k0, _TK), :] < bound, 1.0, 0.0)
                return cnt + _fold_rows(hit, jnp.add)
            cnt8 = lax.fori_loop(0, n_tiles, body, jnp.zeros((8, Q_BLOCK), F32))
            return jnp.sum(cnt8, axis=0, keepdims=True)

        def index_bit(b, bound):
            trial = bound | (jnp.int32(1) << (index_bits - 1 - b))
            return jnp.where(count_ties_below(trial) < need, trial, bound)

        last = lax.fori_loop(0, index_bits, index_bit, jnp.zeros((1, Q_BLOCK), jnp.int32))

        def demote(j, carry):
            k0 = pl.multiple_of(j * _TK, _TK)
            pos = tie_ref[pl.ds(k0, _TK), :]
            drop = (pos > last) & (pos < not_tied)
            key_ref[pl.ds(k0, _TK), :] = jnp.where(drop, jnp.int32(INT_MIN), key_ref[pl.ds(k0, _TK), :])
            return carry

        lax.fori_loop(0, n_tiles, demote, 0)

    q_rope_t = qa_ref[:, 0:_W_QAR].astype(F32).T.astype(BF16)
    q_nope_t = qa_ref[:, _W_QAR:].astype(F32).T.astype(BF16)
    scale = HEAD_DIM ** -0.5
    c_rope = A_KV_RANK
    for h in range(A_HEADS):
        q_lat_t = jnp.dot(wuk_ref[h], q_nope_t[h * A_NOPE:(h + 1) * A_NOPE, :], preferred_element_type=F32)
        cols = slice(h * Q_BLOCK, (h + 1) * Q_BLOCK)
        qcat_ref[0:c_rope, cols] = q_lat_t.astype(BF16) * scale
        qcat_ref[c_rope:, cols] = q_rope_t * mar_ref[:, h:h + 1] * scale

    acc_ref[...] = jnp.zeros(acc_ref.shape, F32)

    def logit_matmul(j, buf):
        s_buf, mx_buf = buf[0], buf[1]
        k0 = pl.multiple_of(j * _TK, _TK)
        sel =(key_ref[pl.ds(k0, _TK), :] >= thr) & (krow + k0 <= tq_lane)
        bias = jnp.where(sel, 0.0, NEG_BIG).astype(F32)
        s = jnp.dot(ka_ref[pl.ds(k0, _TK), :], qcat_ref[...], preferred_element_type=F32)
        for h in range(A_HEADS):
            sh = s[:, h * Q_BLOCK:(h + 1) * Q_BLOCK] + bias
            s_buf[h] = sh
            mx_buf[:, h * Q_BLOCK:(h + 1) * Q_BLOCK] = _fold_rows(sh, jnp.maximum)

    def accumulate(j, p_buf, alpha):
        k0 = pl.multiple_of(j * _TK, _TK)
        ckv_t = ckvt_ref[:, pl.ds(k0, _TK)]
        for c in range(A_HEADS // 2):
            pv = jnp.dot(ckv_t, p_buf[c], preferred_element_type=F32)
            acc_ref[c] = acc_ref[c] * alpha[:, c * 2 * Q_BLOCK:(c + 1) * 2 * Q_BLOCK] + pv

    def softmax_pv(j, buf, other, carry):
        s_buf, mx_buf, p_buf = buf
        m_old, l_old, a_prev = carry
        m_parts, l_parts, a_parts = [], [], []
        for h in range(A_HEADS):
            cols = slice(h * Q_BLOCK, (h + 1) * Q_BLOCK)
            m_h = jnp.maximum(m_old[:, cols], jnp.max(mx_buf[:, cols], axis=0, keepdims=True))
            a_h = jnp.exp(m_old[:, cols] - m_h)
            p = jnp.exp(s_buf[h] - m_h)
            p_buf[h // 2, :, (h % 2) * Q_BLOCK:(h % 2 + 1) * Q_BLOCK] = p.astype(BF16)
            l_parts.append(a_h * l_old[:, cols] + jnp.sum(_fold_rows(p, jnp.add), axis=0, keepdims=True))
            m_parts.append(m_h)
            a_parts.append(a_h)
        accumulate(jnp.maximum(j - 1, 0), other[2], a_prev)
        return (jnp.concatenate(m_parts, axis=1), jnp.concatenate(l_parts, axis=1),
                jnp.concatenate(a_parts, axis=1))

    def last_accumulate(j, buf, carry):
        accumulate(j, buf[2], carry[2])
        return carry

    pb_ref[...] = jnp.zeros(pb_ref.shape, BF16)
    _, l_fin, _ = _skewed_tiles(n_tiles, logit_matmul, softmax_pv, (sa_ref, ma_ref, pa_ref), (sb_ref, mb_ref, pb_ref),
                                (jnp.full((1, rows), NEG_BIG, F32), jnp.zeros((1, rows), F32),
                                 jnp.ones((1, rows), F32)), finish=last_accumulate)

    inv_l = 1.0 / l_fin
    out_t = []
    for h in range(A_HEADS):
        lanes = slice((h % 2) * Q_BLOCK, (h % 2 + 1) * Q_BLOCK)
        o_lat = (acc_ref[h // 2][:, lanes] * inv_l[:, h * Q_BLOCK:(h + 1) * Q_BLOCK]).astype(BF16)
        out_t.append(jnp.dot(wuv_ref[h], o_lat, preferred_element_type=F32))
    out_ref[...] = jnp.concatenate(out_t, axis=0).T.astype(BF16)


def _dsa_mixer(qa, iq, iw, ikt, ka, ckvt, wuk, wuv_t, m_ar, m_iq, b, s):
    n_sel = min(A_TOPK_MAX, s // 4)
    nq = s // Q_BLOCK
    rows = A_HEADS * Q_BLOCK
    blk = lambda width: pl.BlockSpec((Q_BLOCK, width), lambda bi, i: (bi * nq + i, 0))
    seq = lambda width: pl.BlockSpec((s, width), lambda bi, i: (bi, 0))
    const = lambda a: pl.BlockSpec(a.shape, lambda bi, i: (0,) * a.ndim)
    return pl.pallas_call(
        functools.partial(_dsa_kernel, n_sel=n_sel),
        out_shape=jax.ShapeDtypeStruct((b * s, A_HEADS * HEAD_DIM), BF16),
        grid=(b, nq),
        in_specs=[blk(_W_QAR + _W_QAN), blk(_W_IQ), blk(_W_IW), seq(_W_IK), seq(_W_CKV + _W_AKR),
                  pl.BlockSpec((A_KV_RANK, s), lambda bi, i: (0, bi)),
                  const(wuk), const(wuv_t), const(m_ar), const(m_iq)],
        out_specs=blk(A_HEADS * HEAD_DIM),
        scratch_shapes=[pltpu.VMEM((s, Q_BLOCK), jnp.int32),
                        pltpu.VMEM((s, Q_BLOCK), jnp.int32),
                        pltpu.VMEM((_W_IQ, rows), BF16),
                        pltpu.VMEM((A_KV_RANK + _W_AKR, rows), BF16),
                        pltpu.VMEM((A_HEADS // 2, A_KV_RANK, 2 * Q_BLOCK), F32),
                        pltpu.VMEM((A_HEADS, _TK, Q_BLOCK), F32),
                        pltpu.VMEM((A_HEADS, _TK, Q_BLOCK), F32),
                        pltpu.VMEM((8, rows), F32),
                        pltpu.VMEM((8, rows), F32),
                        pltpu.VMEM((A_HEADS // 2, _TK, 2 * Q_BLOCK), BF16),
                        pltpu.VMEM((A_HEADS // 2, _TK, 2 * Q_BLOCK), BF16)],
        compiler_params=_cparams("parallel", "arbitrary"),
        name="dsa_attention",
    )(qa, iq, iw, ikt, ka, ckvt, wuk, wuv_t, m_ar, m_iq)


def _merge_kernel(x_ref, a_ref, o1_ref, o2_ref, o3_ref, l1_ref, l2_ref, l3_ref, wg_ref, bg_ref,
                  wa_ref, wb_ref, wo_ref, lg_ref, lb_ref, h_ref, *scr):
    dils = [dil for _, dil in B_PATTERNS]
    lses = [_load_token_major(r, scr[2 * g], dils[g]) for g, r in enumerate((l1_ref, l2_ref, l3_ref))]
    outs = [_load_token_major(r, scr[2 * g + 1], dils[g]) for g, r in enumerate((o1_ref, o2_ref, o3_ref))]
    mx = jnp.maximum(jnp.maximum(lses[0], lses[1]), lses[2])
    es = [jnp.exp(l - mx) for l in lses]
    den = es[0] + es[1] + es[2]
    b_out = (es[0] / den) * outs[0] + (es[1] / den) * outs[1] + (es[2] / den) * outs[2]
    ya = jnp.dot(a_ref[...], wa_ref[...], preferred_element_type=F32)
    yb = jnp.dot(b_out.astype(BF16), wb_ref[...], preferred_element_type=F32)
    x = x_ref[...]
    gates = jax.nn.sigmoid(jnp.dot(x.astype(BF16), wg_ref[...], preferred_element_type=F32) + bg_ref[...])
    pre = gates[:, 0:D_MODEL] * ya + gates[:, D_MODEL:] * yb
    mix = jnp.dot(pre.astype(BF16), wo_ref[...], preferred_element_type=F32)
    h_ref[...] = _layer_norm(ALPHA * x + mix, lg_ref[...], lb_ref[...])


def _merge(xf, a_out, b_parts, w_gate, b_gate, wa, wb, wo, ln_g, ln_b, tile):
    n = xf.shape[0]
    row = lambda width: pl.BlockSpec((tile, width), lambda i: (i, 0))
    full = lambda a: pl.BlockSpec(a.shape, lambda i: (0,) * a.ndim)
    (o1, l1), (o2, l2), (o3, l3) = b_parts
    grp = [pl.BlockSpec((tile // dil, dil * B_GROUP_W), lambda i: (i, 0)) for _, dil in B_PATTERNS]
    return pl.pallas_call(
        _merge_kernel,
        out_shape=jax.ShapeDtypeStruct((n, D_MODEL), F32),
        grid=(n // tile,),
        in_specs=[row(D_MODEL), row(A_HEADS * HEAD_DIM)] + grp + grp + [full(w_gate), full(b_gate),
                  full(wa), full(wb), full(wo), full(ln_g), full(ln_b)],
        out_specs=row(D_MODEL),
        scratch_shapes=[pltpu.VMEM((B_GROUP_W // LANES, tile, LANES), F32)] * (2 * len(B_PATTERNS)),
        compiler_params=_cparams("parallel"),
        name="merge_output_projection",
    )(xf, a_out, o1, o2, o3, l1, l2, l3, w_gate, b_gate, wa, wb, wo, ln_g, ln_b)


def _first_max(v):
    m = jnp.max(v, axis=0, keepdims=True)
    idx = lax.broadcasted_iota(jnp.int32, v.shape, 0)
    first = jnp.min(jnp.where(v == m, idx, v.shape[0]), axis=0, keepdims=True)
    return m, idx == first


def _router_kernel(h_ref, wr_ref, rb_ref, gate_ref):
    t = h_ref.shape[0]
    gs = N_EXPERTS // N_GROUPS
    logits = lax.dot_general(wr_ref[...], h_ref[...], (((1,), (1,)), ((), ())),
                             precision=lax.Precision.HIGHEST, preferred_element_type=F32)
    scores = jax.nn.sigmoid(logits)
    biased = scores + rb_ref[...]
    gscores = []
    for g in range(N_GROUPS):
        blk = biased[g * gs:(g + 1) * gs, :]
        m1, hit = _first_max(blk)
        m2 = jnp.max(jnp.where(hit, -jnp.inf, blk), axis=0, keepdims=True)
        gscores.append(m1 + m2)
    gscore = jnp.concatenate(gscores, axis=0)
    gsel = jnp.zeros((N_GROUPS, t), F32)
    for _ in range(TOPK_GROUPS):
        _, hit = _first_max(gscore)
        gsel = jnp.where(hit, 1.0, gsel)
        gscore = jnp.where(hit, -jnp.inf, gscore)
    esel = jnp.concatenate([jnp.broadcast_to(gsel[g:g + 1, :], (gs, t)) for g in range(N_GROUPS)], axis=0)
    cand = jnp.where(esel > 0.0, biased, -jnp.inf)
    top_s = jnp.zeros((N_EXPERTS, t), F32)
    for _ in range(TOP_K):
        _, hit = _first_max(cand)
        top_s = jnp.where(hit, scores, top_s)
        cand = jnp.where(hit, -jnp.inf, cand)
    gate = top_s / jnp.sum(top_s, axis=0, keepdims=True) * ROUTED_SCALE
    gate_ref[...] = jnp.concatenate([gate, jnp.zeros((LANES - N_EXPERTS, t), F32)], axis=0).T


def _router(h, wr_t, rb_col, tile):
    n = h.shape[0]
    return pl.pallas_call(
        _router_kernel,
        out_shape=jax.ShapeDtypeStruct((n, LANES), F32),
        grid=(n // tile,),
        in_specs=[pl.BlockSpec((tile, D_MODEL), lambda i: (i, 0)),
                  pl.BlockSpec(wr_t.shape, lambda i: (0, 0)),
                  pl.BlockSpec(rb_col.shape, lambda i: (0, 0))],
        out_specs=pl.BlockSpec((tile, LANES), lambda i: (i, 0)),
        compiler_params=_cparams("parallel"),
        name="moe_router",
    )(h, wr_t, rb_col)


_EXPERTS_PER_STEP = 4


def _swiglu(hb, w1, w3, w2):
    a = jnp.dot(hb, w1, preferred_element_type=F32)
    b = jnp.dot(hb, w3, preferred_element_type=F32)
    return jnp.dot((jax.nn.silu(a) * b).astype(BF16), w2, preferred_element_type=F32)


def _experts_kernel(h_ref, gate_ref, w1_ref, w3_ref, w2_ref, s1_ref, s3_ref, s2_ref, lg_ref, lb_ref,
                    y_ref, hb_ref, acc_ref):
    step = pl.program_id(1)

    @pl.when(step == 0)
    def _():
        hb = h_ref[...].astype(BF16)
        hb_ref[...] = hb
        acc_ref[...] = _swiglu(hb, s1_ref[...], s3_ref[...], s2_ref[...])

    hb = hb_ref[...]
    gate = gate_ref[...]
    lane = lax.broadcasted_iota(jnp.int32, gate.shape, 1)
    for k in range(_EXPERTS_PER_STEP):
        e = step * _EXPERTS_PER_STEP + k
        g_col = jnp.sum(jnp.where(lane == e, gate, 0.0), axis=1, keepdims=True)
        acc_ref[...] += _swiglu(hb, w1_ref[k], w3_ref[k], w2_ref[k]) * g_col

    @pl.when(step == pl.num_programs(1) - 1)
    def _():
        y_ref[...] = _layer_norm(ALPHA * h_ref[...] + acc_ref[...], lg_ref[...], lb_ref[...])


def _experts(h, gate, w1, w3, w2, s1, s3, s2, ln_g, ln_b, tile):
    n = h.shape[0]
    ec = _EXPERTS_PER_STEP
    row = lambda width: pl.BlockSpec((tile, width), lambda i, e: (i, 0))
    full = lambda a: pl.BlockSpec(a.shape, lambda i, e: (0,) * a.ndim)
    wspec = lambda a: pl.BlockSpec((ec,) + a.shape[1:], lambda i, e: (e, 0, 0))
    return pl.pallas_call(
        _experts_kernel,
        out_shape=jax.ShapeDtypeStruct((n, D_MODEL), F32),
        grid=(n // tile, N_EXPERTS // ec),
        in_specs=[row(D_MODEL), row(LANES), wspec(w1), wspec(w3), wspec(w2),
                  full(s1), full(s3), full(s2), full(ln_g), full(ln_b)],
        out_specs=row(D_MODEL),
        scratch_shapes=[pltpu.VMEM((tile, D_MODEL), BF16), pltpu.VMEM((tile, D_MODEL), F32)],
        compiler_params=_cparams("parallel", "arbitrary"),
        name="moe_experts",
    )(h, gate, w1, w3, w2, s1, s3, s2, ln_g, ln_b)


def _rope_freqs():
    inv16 = ROPE_THETA ** (-jnp.arange(0, ROPE_DIM, 2, dtype=F32) / ROPE_DIM)
    inv8 = ROPE_THETA ** (-jnp.arange(0, IDX_ROPE, 2, dtype=F32) / IDX_ROPE)

    n16, n8 = ROPE_DIM // 2, IDX_ROPE // 2
    freq = jnp.zeros((1, LANES), F32).at[0, 0:n16].set(inv16).at[0, n16:n16 + n8].set(inv8)
    lay = _LAYOUT
    place = np.zeros((2 * LANES, 6 * LANES), np.float32)
    for k, (fidx, sign, base) in enumerate(((lay["fa"], lay["sa"], 0), (lay["fi"], lay["si"], n16),
                                            (lay["fb"], lay["sb"], 0))):
        for l in range(LANES):
            src = base + fidx[l] if fidx[l] >= 0 else n16 + n8
            place[src, 2 * k * LANES + l] = 1.0
            place[LANES + src, (2 * k + 1) * LANES + l] = sign[l]
    return freq, jnp.asarray(place)


def _layer(x, positions, w_in, b_gate, g_kv, w_uk, w_uv, w_branch_a, w_branch_b, w_o, ln1_g, ln1_b,
           w_router, router_bias, w1_e, w3_e, w2_e, ws1, ws3, ws2, ln2_g, ln2_b):
    b, s, d = x.shape
    n = b * s
    lay = _LAYOUT
    tile = min(256, n)
    xf = x.reshape(n, d)

    n_proj = _C_GA
    w_perm = (w_in[:, lay["cols"][:n_proj]] * lay["keep"][:n_proj]).astype(BF16)
    w_gate = w_in[:, _OFF_GA:_OFF_GA + 2 * D_MODEL].astype(BF16)
    wuk = jnp.transpose(w_uk, (1, 0, 2)).astype(BF16)
    wuv_t = jnp.transpose(w_uv, (1, 2, 0)).astype(BF16)
    masks = {"m_bq": jnp.asarray(lay["m_bq"], BF16), "m_ar": jnp.asarray(lay["m_ar"].T, BF16),
             "m_iq": jnp.asarray(lay["m_iq"].T, BF16)}
    m_bv = jnp.asarray(lay["m_bv"], F32)

    freq, place = _rope_freqs()
    tables = _rope_tables(positions.reshape(n, 1), freq, place, tile)
    qa, ka, ckvt, iq, ikt, iw, *bqkv = _input_projection(xf, w_perm, tables, g_kv.reshape(1, -1), tile)

    a_out = _dsa_mixer(qa, iq, iw, ikt, ka, ckvt, wuk, wuv_t, masks["m_ar"], masks["m_iq"], b, s)
    ng = len(B_PATTERNS)
    b_parts = [_dilated_group(bqkv[g], bqkv[ng + g], bqkv[2 * ng + g], masks["m_bq"], m_bv, b, dil)
               for g, (_, dil) in enumerate(B_PATTERNS)]

    h = _merge(xf, a_out, b_parts, w_gate, b_gate.reshape(1, -1), w_branch_a.astype(BF16),
               w_branch_b.astype(BF16), w_o.astype(BF16), ln1_g.reshape(1, -1), ln1_b.reshape(1, -1), tile)

    gate = _router(h, w_router.T, router_bias.reshape(-1, 1), min(1024, n))
    y = _experts(h, gate, w1_e.astype(BF16), w3_e.astype(BF16), w2_e.astype(BF16),
                 ws1.astype(BF16), ws3.astype(BF16), ws2.astype(BF16),
                 ln2_g.reshape(1, -1), ln2_b.reshape(1, -1), min(1024, n))
    return y.reshape(b, s, d)


def kernel(x, positions, w_in, b_gate, g_kv, w_uk, w_uv, w_branch_a, w_branch_b, w_o, ln1_g, ln1_b,
           w_router, router_bias, w1_e, w3_e, w2_e, ws1, ws3, ws2, ln2_g, ln2_b):
    h = x
    for l in range(DEPTH):
        h = _layer(h, positions, w_in[l], b_gate[l], g_kv[l], w_uk[l], w_uv[l], w_branch_a[l],
                   w_branch_b[l], w_o[l], ln1_g[l], ln1_b[l], w_router[l], router_bias[l],
                   w1_e[l], w3_e[l], w2_e[l], ws1[l], ws3[l], ws2[l], ln2_g[l], ln2_b[l])
    return h
```

```python
import functools

import jax
import jax.numpy as jnp
import numpy as np
from jax import lax
from jax.experimental import pallas as pl
from jax.experimental.pallas import tpu as pltpu

F32 = jnp.float32
BF16 = jnp.bfloat16

D_MODEL = 1024
HEAD_DIM = 64
ROPE_DIM = 16
ROPE_THETA = 500000.0
Q_BLOCK = 128
A_HEADS = 8
A_NOPE = HEAD_DIM - ROPE_DIM
A_KV_RANK = 256
A_TOPK_MAX = 256
IDX_HEADS = 8
IDX_DIM = 32
IDX_ROPE = 8
B_PATTERNS = ((128, 1), (512, 4), (2048, 16))
B_GROUP_HEADS = 4
B_HEADS = B_GROUP_HEADS * len(B_PATTERNS)
B_GROUP_W = B_GROUP_HEADS * HEAD_DIM
N_EXPERTS = 64
TOP_K = 8
N_GROUPS = 8
TOPK_GROUPS = 4
D_EXPERT = 256
ROUTED_SCALE = 2.5
DEPTH = 1
ALPHA = (2.0 * DEPTH) ** 0.25
LN_EPS = 1e-5
RMS_EPS = 1e-6

LANES = 128
VMEM_LIMIT = 56 * 1024 * 1024
_TOKEN_TILE = 256
_MOE_TILE = 1024
NEG_BIG = -1e30
INT_MIN = -(2 ** 31)

_OFF_AQ = 0
_OFF_CKV = _OFF_AQ + A_HEADS * HEAD_DIM
_OFF_AKR = _OFF_CKV + A_KV_RANK
_OFF_IQ = _OFF_AKR + ROPE_DIM
_OFF_IK = _OFF_IQ + IDX_HEADS * IDX_DIM
_OFF_IW = _OFF_IK + IDX_DIM
_OFF_BQ = _OFF_IW + IDX_HEADS
_OFF_BK = _OFF_BQ + B_HEADS * HEAD_DIM
_OFF_BV = _OFF_BK + B_HEADS * HEAD_DIM
_OFF_GA = _OFF_BV + B_HEADS * HEAD_DIM
_OFF_GB = _OFF_GA + D_MODEL
_IN_TOTAL = _OFF_GB + D_MODEL

_W_QAR, _W_QAN, _W_CKV, _W_AKR = 128, A_HEADS * A_NOPE, A_KV_RANK, 128
_W_IQ, _W_IK, _W_IW = 256, 256, 128
_W_B = B_HEADS * HEAD_DIM
_C_QAR = 0
_C_QAN = _C_QAR + _W_QAR
_C_CKV = _C_QAN + _W_QAN
_C_AKR = _C_CKV + _W_CKV
_C_IQ = _C_AKR + _W_AKR
_C_IK = _C_IQ + _W_IQ
_C_IW = _C_IK + _W_IK
_C_BQ = _C_IW + _W_IW
_C_BK = _C_BQ + _W_B
_C_BV = _C_BK + _W_B
_C_GA = _C_BV + _W_B
_C_GB = _C_GA + D_MODEL
_P_TOTAL = _C_GB + D_MODEL


def _idx_lane(l):
    if l < 16:
        return l // 4, l % 4
    if l < 64:
        return (l - 16) // 12, 8 + (l - 16) % 12
    if l < 80:
        return (l - 64) // 4, 4 + (l - 64) % 4
    return (l - 80) // 12, 20 + (l - 80) % 12


def _b_lane(l):
    half, r = l // 64, l % 64
    which, rr = r // 32, r % 32
    if rr < 8:
        return which, half * 8 + rr
    return which, 16 + half * 24 + (rr - 8)


def _build_layout():
    cols = np.zeros((_P_TOTAL,), np.int32)
    keep = np.ones((_P_TOTAL,), np.float32)
    for l in range(128):
        half, h, f = l // 64, (l % 64) // 8, l % 8
        cols[_C_QAR + l] = _OFF_AQ + h * HEAD_DIM + half * 8 + f
        cols[_C_AKR + l] = _OFF_AKR + half * 8 + f
    for h in range(A_HEADS):
        for j in range(A_NOPE):
            cols[_C_QAN + h * A_NOPE + j] = _OFF_AQ + h * HEAD_DIM + ROPE_DIM + j
    cols[_C_CKV:_C_CKV + _W_CKV] = _OFF_CKV + np.arange(_W_CKV)
    for sl in range(2):
        for l in range(128):
            hh, d = _idx_lane(l)
            cols[_C_IQ + sl * 128 + l] = _OFF_IQ + (sl * 4 + hh) * IDX_DIM + d
            cols[_C_IK + sl * 128 + l] = _OFF_IK + d
    cols[_C_IW:_C_IW + IDX_HEADS] = _OFF_IW + np.arange(IDX_HEADS)
    keep[_C_IW + IDX_HEADS:_C_IW + _W_IW] = 0.0
    for p in range(B_HEADS // 2):
        for l in range(128):
            which, d = _b_lane(l)
            h = 2 * p + which
            cols[_C_BQ + p * 128 + l] = _OFF_BQ + h * HEAD_DIM + d
            cols[_C_BK + p * 128 + l] = _OFF_BK + h * HEAD_DIM + d
    cols[_C_BV:_C_BV + _W_B] = _OFF_BV + np.arange(_W_B)
    cols[_C_GA:_C_GA + D_MODEL] = _OFF_GA + np.arange(D_MODEL)
    cols[_C_GB:_C_GB + D_MODEL] = _OFF_GB + np.arange(D_MODEL)

    fa = np.array([l % 8 for l in range(128)])
    sa = np.array([-1.0 if l < 64 else 1.0 for l in range(128)], np.float32)
    fi = np.full((128,), -1)
    si = np.zeros((128,), np.float32)
    fb = np.full((128,), -1)
    sb = np.zeros((128,), np.float32)
    for l in range(128):
        if l < 16 or 64 <= l < 80:
            fi[l] = l % 4
            si[l] = -1.0 if l < 64 else 1.0
        if l % 32 < 8:
            fb[l] = l % 32
            sb[l] = -1.0 if l < 64 else 1.0

    m_ar = np.zeros((A_HEADS, 128), np.float32)
    for l in range(128):
        m_ar[(l % 64) // 8, l] = 1.0
    m_iq = np.zeros((IDX_HEADS, _W_IQ), np.float32)
    for sl in range(2):
        for l in range(128):
            m_iq[sl * 4 + _idx_lane(l)[0], sl * 128 + l] = 1.0
    m_bq = np.zeros((B_GROUP_HEADS, B_GROUP_W), np.float32)
    for p in range(2):
        for l in range(128):
            m_bq[2 * p + _b_lane(l)[0], p * 128 + l] = 1.0
    m_bv = np.zeros((B_GROUP_HEADS, B_GROUP_W), np.float32)
    for j in range(B_GROUP_HEADS):
        m_bv[j, j * HEAD_DIM:(j + 1) * HEAD_DIM] = 1.0
    return dict(cols=cols, keep=keep, fa=fa, sa=sa, fi=fi, si=si, fb=fb, sb=sb,
                m_ar=m_ar, m_iq=m_iq, m_bq=m_bq, m_bv=m_bv)


_LAYOUT = _build_layout()


def _cparams(*sem):
    return pltpu.CompilerParams(dimension_semantics=sem, vmem_limit_bytes=VMEM_LIMIT)


def _layer_norm(v, g, b):
    mu = jnp.mean(v, axis=-1, keepdims=True)
    var = jnp.mean(jnp.square(v - mu), axis=-1, keepdims=True)
    return (v - mu) * lax.rsqrt(var + LN_EPS) * g + b


def _dot_nt(a, b):
    return lax.dot_general(a, b, (((1,), (1,)), ((), ())), preferred_element_type=F32)


def _rope_table_kernel(pos_ref, freq_ref, place_ref, out_ref):
    ang = pos_ref[...].astype(F32) * freq_ref[...]
    cs = jnp.concatenate([jnp.cos(ang), jnp.sin(ang)], axis=1)
    out_ref[...] = jnp.dot(cs, place_ref[...], precision=lax.Precision.HIGHEST, preferred_element_type=F32)


def _rope_tables(pos_col, freq, place, tile):
    n = pos_col.shape[0]
    return pl.pallas_call(
        _rope_table_kernel,
        out_shape=jax.ShapeDtypeStruct((n, 6 * LANES), F32),
        grid=(n // tile,),
        in_specs=[pl.BlockSpec((tile, 1), lambda i: (i, 0)),
                  pl.BlockSpec(freq.shape, lambda i: (0, 0)),
                  pl.BlockSpec(place.shape, lambda i: (0, 0))],
        out_specs=pl.BlockSpec((tile, 6 * LANES), lambda i: (i, 0)),
        compiler_params=_cparams("parallel"),
        name="rope_tables",
    )(pos_col, freq, place)


def _rope_slabs(y, cos, sin):
    outs = []
    for s in range(y.shape[1] // LANES):
        ys = y[:, s * LANES:(s + 1) * LANES]
        outs.append(ys * cos + pltpu.roll(ys, 64, 1) * sin)
    return outs[0] if len(outs) == 1 else jnp.concatenate(outs, axis=1)


def _store_residue_major(out_ref, y, scr_ref, dil):
    if dil == 1:
        out_ref[...] = y.astype(out_ref.dtype)
        return
    rows, width = y.shape[0] // dil, y.shape[1]
    for c in range(width // LANES):
        scr_ref[c] = y[:, c * LANES:(c + 1) * LANES]
    for r in range(dil):
        for c in range(width // LANES):
            lanes = slice(r * width + c * LANES, r * width + (c + 1) * LANES)
            out_ref[:, lanes] = scr_ref[c, pl.ds(r, rows, stride=dil), :].astype(out_ref.dtype)


def _load_token_major(ref, scr_ref, dil):
    if dil == 1:
        return ref[...]
    rows, width = ref.shape[0], ref.shape[1] // dil
    for r in range(dil):
        for c in range(width // LANES):
            lanes = slice(r * width + c * LANES, r * width + (c + 1) * LANES)
            scr_ref[c, pl.ds(r, rows, stride=dil), :] = ref[:, lanes]
    return jnp.concatenate([scr_ref[c] for c in range(width // LANES)], axis=1)


def _proj_kernel(x_ref, w_ref, tab_ref, gkv_ref,
                 qa_ref, ka_ref, ckvt_ref, iq_ref, ik_ref, iw_ref, *rest):
    b_refs, scr_ref = rest[:-1], rest[-1]
    xb = x_ref[...].astype(BF16)

    def proj(c0, width):
        return jnp.dot(xb, w_ref[:, c0:c0 + width], preferred_element_type=F32)

    cos_a, sin_a = tab_ref[:, 0:128], tab_ref[:, 128:256]
    cos_i, sin_i = tab_ref[:, 256:384], tab_ref[:, 384:512]
    cos_b, sin_b = tab_ref[:, 512:640], tab_ref[:, 640:768]

    qa_ref[:, 0:_W_QAR] = _rope_slabs(proj(_C_QAR, _W_QAR), cos_a, sin_a).astype(BF16)
    qa_ref[:, _W_QAR:] = proj(_C_QAN, _W_QAN).astype(BF16)
    ckv = proj(_C_CKV, _W_CKV)
    ckv = ckv * lax.rsqrt(jnp.mean(jnp.square(ckv), axis=-1, keepdims=True) + RMS_EPS) * gkv_ref[...]
    ka_ref[:, 0:_W_CKV] = ckv.astype(BF16)
    ckvt_ref[...] = ckv.T.astype(BF16)
    ka_ref[:, _W_CKV:] = _rope_slabs(proj(_C_AKR, _W_AKR), cos_a, sin_a).astype(BF16)
    iq_ref[...] = _rope_slabs(proj(_C_IQ, _W_IQ), cos_i, sin_i).astype(BF16)
    ik_ref[...] = _rope_slabs(proj(_C_IK, _W_IK), cos_i, sin_i).astype(BF16)
    iw_ref[...] = proj(_C_IW, _W_IW) * ((IDX_HEADS * IDX_DIM) ** -0.5)
    ng = len(B_PATTERNS)
    for kind, c0 in enumerate((_C_BQ, _C_BK, _C_BV)):
        for g, (_, dil) in enumerate(B_PATTERNS):
            y = proj(c0 + g * B_GROUP_W, B_GROUP_W)
            if kind < 2:
                y = _rope_slabs(y, cos_b, sin_b)
            _store_residue_major(b_refs[kind * ng + g], y, scr_ref, dil)


def _input_projection(xf, w_perm, tables, g_kv, tile):
    n = xf.shape[0]
    row = lambda width: pl.BlockSpec((tile, width), lambda i: (i, 0))
    full = lambda a: pl.BlockSpec(a.shape, lambda i: (0,) * a.ndim)
    out_w = [(_W_QAR + _W_QAN, BF16), (_W_CKV + _W_AKR, BF16), None, (_W_IQ, BF16), (_W_IK, BF16),
             (_W_IW, F32)]
    shapes = [jax.ShapeDtypeStruct((A_KV_RANK, n), BF16) if o is None else jax.ShapeDtypeStruct((n, o[0]), o[1])
              for o in out_w]
    specs = [pl.BlockSpec((A_KV_RANK, tile), lambda i: (0, i)) if o is None else row(o[0]) for o in out_w]
    for _ in range(3):
        for _, dil in B_PATTERNS:
            shapes.append(jax.ShapeDtypeStruct((n // dil, dil * B_GROUP_W), BF16))
            specs.append(pl.BlockSpec((tile // dil, dil * B_GROUP_W), lambda i: (i, 0)))
    return pl.pallas_call(
        _proj_kernel,
        out_shape=shapes,
        grid=(n // tile,),
        in_specs=[row(D_MODEL), full(w_perm), row(6 * LANES), full(g_kv)],
        out_specs=specs,
        scratch_shapes=[pltpu.VMEM((B_GROUP_W // LANES, tile, LANES), F32)],
        compiler_params=_cparams("parallel"),
        name="input_projection",
    )(xf, w_perm, tables, g_kv)


def _dilated_kernel(q_ref, kc_ref, kp_ref, vc_ref, vp_ref, mq_ref, mv_ref, o_ref, lse_ref,
                    kwin_ref, vwin_ref, *, tq):
    first = pl.program_id(2) == 0
    kwin_ref[0:Q_BLOCK, :] = kp_ref[...]
    kwin_ref[Q_BLOCK:, :] = kc_ref[...]
    vwin_ref[0:Q_BLOCK, :] = vp_ref[...]
    vwin_ref[Q_BLOCK:, :] = vc_ref[...]
    t = lax.broadcasted_iota(jnp.int32, (Q_BLOCK, 2 * Q_BLOCK), 0)
    c = lax.broadcasted_iota(jnp.int32, (Q_BLOCK, 2 * Q_BLOCK), 1)
    diff = t + Q_BLOCK - c
    band = (diff >= 0) & (diff <= Q_BLOCK)
    scale = HEAD_DIM ** -0.5
    for sb in range(tq // Q_BLOCK):
        valid = band
        if sb == 0:
            valid = band & (c >= jnp.where(first, Q_BLOCK, 0))
        bias = jnp.where(valid, 0.0, NEG_BIG).astype(F32)
        q = q_ref[sb * Q_BLOCK:(sb + 1) * Q_BLOCK, :] * scale
        kw = kwin_ref[sb * Q_BLOCK:(sb + 2) * Q_BLOCK, :]
        vw = vwin_ref[sb * Q_BLOCK:(sb + 2) * Q_BLOCK, :]
        qs = jnp.concatenate([q * mq_ref[j:j + 1, :] for j in range(B_GROUP_HEADS)], axis=0)
        s = _dot_nt(qs, kw)
        o_acc = jnp.zeros((Q_BLOCK, B_GROUP_W), F32)
        lse_acc = jnp.zeros((Q_BLOCK, B_GROUP_W), F32)
        for j in range(B_GROUP_HEADS):
            sj = s[j * Q_BLOCK:(j + 1) * Q_BLOCK, :] + bias
            m = jnp.max(sj, axis=-1, keepdims=True)
            e = jnp.exp(sj - m)
            den = jnp.sum(e, axis=-1, keepdims=True)
            pv = jnp.dot(e.astype(BF16), vw, preferred_element_type=F32)
            mv = mv_ref[j:j + 1, :]
            o_acc = o_acc + (pv / den) * mv
            lse_acc = lse_acc + (m + jnp.log(den)) * mv
        o_ref[sb * Q_BLOCK:(sb + 1) * Q_BLOCK, :] = o_acc
        lse_ref[sb * Q_BLOCK:(sb + 1) * Q_BLOCK, :] = lse_acc


def _dilated_group(bq, bk, bv, mq, mv, b, dil):
    sub = bq.shape[0] // b
    tq = min(512, sub)
    nblk = tq // Q_BLOCK
    view = lambda a: a.reshape(b, sub, dil * B_GROUP_W)
    cur = pl.BlockSpec((None, tq, B_GROUP_W), lambda bi, r, i: (bi, i, r))
    prev = pl.BlockSpec((None, Q_BLOCK, B_GROUP_W),
                        lambda bi, r, i: (bi, jnp.maximum(i * nblk - 1, 0), r))
    const = lambda a: pl.BlockSpec(a.shape, lambda bi, r, i: (0, 0))
    out = cur
    o, lse = pl.pallas_call(
        functools.partial(_dilated_kernel, tq=tq),
        out_shape=[jax.ShapeDtypeStruct((b, sub, dil * B_GROUP_W), F32)] * 2,
        grid=(b, dil, sub // tq),
        in_specs=[cur, cur, prev, cur, prev, const(mq), const(mv)],
        out_specs=[out, out],
        scratch_shapes=[pltpu.VMEM((tq + Q_BLOCK, B_GROUP_W), BF16)] * 2,
        compiler_params=_cparams("parallel", "parallel", "arbitrary"),
        name=f"dilated_attention_d{dil}",
    )(view(bq), view(bk), view(bk), view(bv), view(bv), mq, mv)
    return o.reshape(b * sub, dil * B_GROUP_W), lse.reshape(b * sub, dil * B_GROUP_W)


_TK = 512

def _fold_rows(x, op, slab=8):
    parts = [x[r:r + slab, :] for r in range(0, x.shape[0], slab)]
    while len(parts) > 1:
        parts = [op(parts[k], parts[k + 1]) for k in range(0, len(parts) - 1, 2)] + parts[len(parts) & ~1:]
    return parts[0]


def _skewed_tiles(n_tiles, produce, consume, buf_a, buf_b, carry, finish=None):
    finish = finish or (lambda j, buf, c: c)
    produce(0, buf_a)

    def pair(t, c):
        j = 2 * t
        produce(j + 1, buf_b)
        c = consume(j, buf_a, buf_b, c)
        produce(j + 2, buf_a)
        return consume(j + 1, buf_b, buf_a, c)

    n_pairs = (n_tiles - 1) // 2
    carry = lax.fori_loop(0, n_pairs, pair, carry)
    j = 2 * n_pairs

    def last_two(c):
        produce(j + 1, buf_b)
        return finish(j + 1, buf_b, consume(j + 1, buf_b, buf_a, consume(j, buf_a, buf_b, c)))

    def last_one(c):
        return finish(j, buf_a, consume(j, buf_a, buf_b, c))

    return lax.cond(n_tiles - j == 2, last_two, last_one, carry)


def _dsa_kernel(qa_ref, iq_ref, iw_ref, ikt_ref, ka_ref, ckvt_ref, wuk_ref, wuv_ref, mar_ref, miq_ref,
                out_ref, key_ref, tie_ref, iqs_ref, qcat_ref, acc_ref, sa_ref, sb_ref, ma_ref, mb_ref,
                pa_ref, pb_ref, *, n_sel):
    i = pl.program_id(1)
    q0 = i * Q_BLOCK
    n_keys = q0 + Q_BLOCK
    rows = A_HEADS * Q_BLOCK

    n_tiles = (n_keys + _TK - 1) // _TK
    tq_lane = q0 + lax.broadcasted_iota(jnp.int32, (_TK, Q_BLOCK), 1)
    krow = lax.broadcasted_iota(jnp.int32, (_TK, Q_BLOCK), 0)

    iq_t = iq_ref[...].astype(F32).T
    for h in range(IDX_HEADS):
        iqs_ref[:, h * Q_BLOCK:(h + 1) * Q_BLOCK] = (iq_t * miq_ref[:, h:h + 1].astype(F32)).astype(BF16)
    iw_t = iw_ref[...].T

    def score_matmul(j, buf):
        k0 = pl.multiple_of(j * _TK, _TK)
        s = jnp.dot(ikt_ref[pl.ds(k0, _TK), :], iqs_ref[...], preferred_element_type=F32)
        for h in range(IDX_HEADS):
            buf[0][h] = s[:, h * Q_BLOCK:(h + 1) * Q_BLOCK]

    def score_keys(j, buf, _, carry):
        k0 = pl.multiple_of(j * _TK, _TK)
        sc = jnp.zeros((_TK, Q_BLOCK), F32)
        for h in range(IDX_HEADS):
            sc = sc + jnp.maximum(buf[0][h], 0.0) * iw_t[h:h + 1, :]
        bits = lax.bitcast_convert_type(jnp.where(sc == 0.0, 0.0, sc), jnp.int32)
        okey = bits ^ ((bits >> 31) & jnp.int32(0x7FFFFFFF))
        key_ref[pl.ds(k0, _TK), :] = jnp.where(krow + k0 <= tq_lane, okey, jnp.int32(INT_MIN))
        return carry

    _skewed_tiles(n_tiles, score_matmul, score_keys, (sa_ref, ma_ref), (sb_ref, mb_ref), 0)

    def count_keys(pred):
        def tile_hits(j):
            k0 = pl.multiple_of(j * _TK, _TK)
            hit = jnp.where(pred(key_ref[pl.ds(k0, _TK), :], k0), 1.0, 0.0)
            return _fold_rows(hit, jnp.add)

        def two_tiles(t, cnt):
            return cnt + (tile_hits(2 * t) + tile_hits(2 * t + 1))
        cnt8 = lax.fori_loop(0, n_tiles // 2, two_tiles, jnp.zeros((8, Q_BLOCK), F32))
        cnt8 = lax.cond(n_tiles % 2 == 1, lambda c: c + tile_hits(n_tiles - 1), lambda c: c, cnt8)
        return jnp.sum(cnt8, axis=0, keepdims=True)

    def bit_step(b, carry):
        prefix, n_ge = carry
        trial = prefix | (jnp.int32(1) << (31 - b))
        t = trial ^ jnp.int32(INT_MIN)
        cnt = count_keys(lambda keys, k0: keys >= t)
        take = cnt >= float(n_sel)
        return jnp.where(take, trial, prefix), jnp.where(take, cnt, n_ge)

    state = (jnp.zeros((1, Q_BLOCK), jnp.int32), jnp.full((1, Q_BLOCK), float(n_sel), F32))
    state = lax.fori_loop(0, 28, bit_step, state)
    for lo, hi in ((28, 30), (30, 32)):
        settled = jnp.max(state[1]) == float(n_sel)
        state = lax.cond(settled, lambda st: st, functools.partial(lax.fori_loop, lo, hi, bit_step), state)
    prefix, n_ge = state
    thr = prefix ^ jnp.int32(INT_MIN)

    surplus = n_ge - float(n_sel)
    max_surplus = jnp.max(surplus)

    @pl.when(max_surplus > 0.0)
    def _():
        need = float(n_sel) - count_keys(lambda keys, k0: keys > thr)
        index_bits = (key_ref.shape[0] - 1).bit_length()
        not_tied = jnp.int32(1 << index_bits)

        def tie_positions(j, carry):
            k0 = pl.multiple_of(j * _TK, _TK)
            tied = key_ref[pl.ds(k0, _TK), :] == thr
            tie_ref[pl.ds(k0, _TK), :] = jnp.where(tied, krow + k0, not_tied)
            return carry

        lax.fori_loop(0, n_tiles, tie_positions, 0)

        def count_ties_below(bound):
            def body(j, cnt):
                k0 = pl.multiple_of(j * _TK, _TK)
                hit = jnp.where(tie_ref[pl.ds(k0, _TK), :] < bound, 1.0, 0.0)
                return cnt + _fold_rows(hit, jnp.add)
            cnt8 = lax.fori_loop(0, n_tiles, body, jnp.zeros((8, Q_BLOCK), F32))
            return jnp.sum(cnt8, axis=0, keepdims=True)

        def index_bit(b, bound):
            trial = bound | (jnp.int32(1) << (index_bits - 1 - b))
            return jnp.where(count_ties_below(trial) < need, trial, bound)

        last = lax.fori_loop(0, index_bits, index_bit, jnp.zeros((1, Q_BLOCK), jnp.int32))

        def demote(j, carry):
            k0 = pl.multiple_of(j * _TK, _TK)
            pos = tie_ref[pl.ds(k0, _TK), :]
            drop = (pos > last) & (pos < not_tied)
            key_ref[pl.ds(k0, _TK), :] = jnp.where(drop, jnp.int32(INT_MIN), key_ref[pl.ds(k0, _TK), :])
            return carry

        lax.fori_loop(0, n_tiles, demote, 0)

    q_rope_t = qa_ref[:, 0:_W_QAR].astype(F32).T.astype(BF16)
    q_nope_t = qa_ref[:, _W_QAR:].astype(F32).T.astype(BF16)
    scale = HEAD_DIM ** -0.5
    c_rope = A_KV_RANK
    for h in range(A_HEADS):
        q_lat_t = jnp.dot(wuk_ref[h], q_nope_t[h * A_NOPE:(h + 1) * A_NOPE, :], preferred_element_type=F32)
        cols = slice(h * Q_BLOCK, (h + 1) * Q_BLOCK)
        qcat_ref[0:c_rope, cols] = q_lat_t.astype(BF16) * scale
        qcat_ref[c_rope:, cols] = q_rope_t * mar_ref[:, h:h + 1] * scale

    acc_ref[...] = jnp.zeros(acc_ref.shape, F32)

    def logit_matmul(j, buf):
        s_buf, mx_buf = buf[0], buf[1]
        k0 = pl.multiple_of(j * _TK, _TK)
        sel =(key_ref[pl.ds(k0, _TK), :] >= thr) & (krow + k0 <= tq_lane)
        bias = jnp.where(sel, 0.0, NEG_BIG).astype(F32)
        s = jnp.dot(ka_ref[pl.ds(k0, _TK), :], qcat_ref[...], preferred_element_type=F32)
        for h in range(A_HEADS):
            sh = s[:, h * Q_BLOCK:(h + 1) * Q_BLOCK] + bias
            s_buf[h] = sh
            mx_buf[:, h * Q_BLOCK:(h + 1) * Q_BLOCK] = _fold_rows(sh, jnp.maximum)

    def accumulate(j, p_buf, alpha):
        k0 = pl.multiple_of(j * _TK, _TK)
        ckv_t = ckvt_ref[:, pl.ds(k0, _TK)]
        for c in range(A_HEADS // 2):
            pv = jnp.dot(ckv_t, p_buf[c], preferred_element_type=F32)
            acc_ref[c] = acc_ref[c] * alpha[:, c * 2 * Q_BLOCK:(c + 1) * 2 * Q_BLOCK] + pv

    def softmax_pv(j, buf, other, carry):
        s_buf, mx_buf, p_buf = buf
        m_old, l_old, a_prev = carry
        m_parts, l_parts, a_parts = [], [], []
        for h in range(A_HEADS):
            cols = slice(h * Q_BLOCK, (h + 1) * Q_BLOCK)
            m_h = jnp.maximum(m_old[:, cols], jnp.max(mx_buf[:, cols], axis=0, keepdims=True))
            a_h = jnp.exp(m_old[:, cols] - m_h)
            p = jnp.exp(s_buf[h] - m_h)
            p_buf[h // 2, :, (h % 2) * Q_BLOCK:(h % 2 + 1) * Q_BLOCK] = p.astype(BF16)
            l_parts.append(a_h * l_old[:, cols] + jnp.sum(_fold_rows(p, jnp.add), axis=0, keepdims=True))
            m_parts.append(m_h)
            a_parts.append(a_h)
        accumulate(jnp.maximum(j - 1, 0), other[2], a_prev)
        return (jnp.concatenate(m_parts, axis=1), jnp.concatenate(l_parts, axis=1),
                jnp.concatenate(a_parts, axis=1))

    def last_accumulate(j, buf, carry):
        accumulate(j, buf[2], carry[2])
        return carry

    pb_ref[...] = jnp.zeros(pb_ref.shape, BF16)
    _, l_fin, _ = _skewed_tiles(n_tiles, logit_matmul, softmax_pv, (sa_ref, ma_ref, pa_ref), (sb_ref, mb_ref, pb_ref),
                                (jnp.full((1, rows), NEG_BIG, F32), jnp.zeros((1, rows), F32),
                                 jnp.ones((1, rows), F32)), finish=last_accumulate)

    inv_l = 1.0 / l_fin
    out_t = []
    for h in range(A_HEADS):
        lanes = slice((h % 2) * Q_BLOCK, (h % 2 + 1) * Q_BLOCK)
        o_lat = (acc_ref[h // 2][:, lanes] * inv_l[:, h * Q_BLOCK:(h + 1) * Q_BLOCK]).astype(BF16)
        out_t.append(jnp.dot(wuv_ref[h], o_lat, preferred_element_type=F32))
    out_ref[...] = jnp.concatenate(out_t, axis=0).T.astype(BF16)


def _dsa_mixer(qa, iq, iw, ikt, ka, ckvt, wuk, wuv_t, m_ar, m_iq, b, s):
    n_sel = min(A_TOPK_MAX, s // 4)
    nq = s // Q_BLOCK
    rows = A_HEADS * Q_BLOCK
    blk = lambda width: pl.BlockSpec((Q_BLOCK, width), lambda bi, i: (bi * nq + i, 0))
    seq = lambda width: pl.BlockSpec((s, width), lambda bi, i: (bi, 0))
    const = lambda a: pl.BlockSpec(a.shape, lambda bi, i: (0,) * a.ndim)
    return pl.pallas_call(
        functools.partial(_dsa_kernel, n_sel=n_sel),
        out_shape=jax.ShapeDtypeStruct((b * s, A_HEADS * HEAD_DIM), BF16),
        grid=(b, nq),
        in_specs=[blk(_W_QAR + _W_QAN), blk(_W_IQ), blk(_W_IW), seq(_W_IK), seq(_W_CKV + _W_AKR),
                  pl.BlockSpec((A_KV_RANK, s), lambda bi, i: (0, bi)),
                  const(wuk), const(wuv_t), const(m_ar), const(m_iq)],
        out_specs=blk(A_HEADS * HEAD_DIM),
        scratch_shapes=[pltpu.VMEM((s, Q_BLOCK), jnp.int32),
                        pltpu.VMEM((s, Q_BLOCK), jnp.int32),
                        pltpu.VMEM((_W_IQ, rows), BF16),
                        pltpu.VMEM((A_KV_RANK + _W_AKR, rows), BF16),
                        pltpu.VMEM((A_HEADS // 2, A_KV_RANK, 2 * Q_BLOCK), F32),
                        pltpu.VMEM((A_HEADS, _TK, Q_BLOCK), F32),
                        pltpu.VMEM((A_HEADS, _TK, Q_BLOCK), F32),
                        pltpu.VMEM((8, rows), F32),
                        pltpu.VMEM((8, rows), F32),
                        pltpu.VMEM((A_HEADS // 2, _TK, 2 * Q_BLOCK), BF16),
                        pltpu.VMEM((A_HEADS // 2, _TK, 2 * Q_BLOCK), BF16)],
        compiler_params=_cparams("parallel", "arbitrary"),
        name="dsa_attention",
    )(qa, iq, iw, ikt, ka, ckvt, wuk, wuv_t, m_ar, m_iq)


def _merge_kernel(x_ref, a_ref, o1_ref, o2_ref, o3_ref, l1_ref, l2_ref, l3_ref, wg_ref, bg_ref,
                  wa_ref, wb_ref, wo_ref, lg_ref, lb_ref, h_ref, *scr):
    dils = [dil for _, dil in B_PATTERNS]
    lses = [_load_token_major(r, scr[2 * g], dils[g]) for g, r in enumerate((l1_ref, l2_ref, l3_ref))]
    outs = [_load_token_major(r, scr[2 * g + 1], dils[g]) for g, r in enumerate((o1_ref, o2_ref, o3_ref))]
    mx = jnp.maximum(jnp.maximum(lses[0], lses[1]), lses[2])
    es = [jnp.exp(l - mx) for l in lses]
    den = es[0] + es[1] + es[2]
    b_out = (es[0] / den) * outs[0] + (es[1] / den) * outs[1] + (es[2] / den) * outs[2]
    ya = jnp.dot(a_ref[...], wa_ref[...], preferred_element_type=F32)
    yb = jnp.dot(b_out.astype(BF16), wb_ref[...], preferred_element_type=F32)
    x = x_ref[...]
    gates = jax.nn.sigmoid(jnp.dot(x.astype(BF16), wg_ref[...], preferred_element_type=F32) + bg_ref[...])
    pre = gates[:, 0:D_MODEL] * ya + gates[:, D_MODEL:] * yb
    mix = jnp.dot(pre.astype(BF16), wo_ref[...], preferred_element_type=F32)
    h_ref[...] = _layer_norm(ALPHA * x + mix, lg_ref[...], lb_ref[...])


def _merge(xf, a_out, b_parts, w_gate, b_gate, wa, wb, wo, ln_g, ln_b, tile):
    n = xf.shape[0]
    row = lambda width: pl.BlockSpec((tile, width), lambda i: (i, 0))
    full = lambda a: pl.BlockSpec(a.shape, lambda i: (0,) * a.ndim)
    (o1, l1), (o2, l2), (o3, l3) = b_parts
    grp = [pl.BlockSpec((tile // dil, dil * B_GROUP_W), lambda i: (i, 0)) for _, dil in B_PATTERNS]
    return pl.pallas_call(
        _merge_kernel,
        out_shape=jax.ShapeDtypeStruct((n, D_MODEL), F32),
        grid=(n // tile,),
        in_specs=[row(D_MODEL), row(A_HEADS * HEAD_DIM)] + grp + grp + [full(w_gate), full(b_gate),
                  full(wa), full(wb), full(wo), full(ln_g), full(ln_b)],
        out_specs=row(D_MODEL),
        scratch_shapes=[pltpu.VMEM((B_GROUP_W // LANES, tile, LANES), F32)] * (2 * len(B_PATTERNS)),
        compiler_params=_cparams("parallel"),
        name="merge_output_projection",
    )(xf, a_out, o1, o2, o3, l1, l2, l3, w_gate, b_gate, wa, wb, wo, ln_g, ln_b)


def _first_max(v):
    m = jnp.max(v, axis=0, keepdims=True)
    idx = lax.broadcasted_iota(jnp.int32, v.shape, 0)
    first = jnp.min(jnp.where(v == m, idx, v.shape[0]), axis=0, keepdims=True)
    return m, idx == first


def _router_kernel(h_ref, wr_ref, rb_ref, gate_ref):
    t = h_ref.shape[0]
    gs = N_EXPERTS // N_GROUPS
    logits = lax.dot_general(wr_ref[...], h_ref[...], (((1,), (1,)), ((), ())),
                             precision=lax.Precision.HIGHEST, preferred_element_type=F32)
    scores = jax.nn.sigmoid(logits)
    biased = scores + rb_ref[...]
    gscores = []
    for g in range(N_GROUPS):
        blk = biased[g * gs:(g + 1) * gs, :]
        m1, hit = _first_max(blk)
        m2 = jnp.max(jnp.where(hit, -jnp.inf, blk), axis=0, keepdims=True)
        gscores.append(m1 + m2)
    gscore = jnp.concatenate(gscores, axis=0)
    gsel = jnp.zeros((N_GROUPS, t), F32)
    for _ in range(TOPK_GROUPS):
        _, hit = _first_max(gscore)
        gsel = jnp.where(hit, 1.0, gsel)
        gscore = jnp.where(hit, -jnp.inf, gscore)
    esel = jnp.concatenate([jnp.broadcast_to(gsel[g:g + 1, :], (gs, t)) for g in range(N_GROUPS)], axis=0)
    cand = jnp.where(esel > 0.0, biased, -jnp.inf)
    top_s = jnp.zeros((N_EXPERTS, t), F32)
    for _ in range(TOP_K):
        _, hit = _first_max(cand)
        top_s = jnp.where(hit, scores, top_s)
        cand = jnp.where(hit, -jnp.inf, cand)
    gate = top_s / jnp.sum(top_s, axis=0, keepdims=True) * ROUTED_SCALE
    gate_ref[...] = jnp.concatenate([gate, jnp.zeros((LANES - N_EXPERTS, t), F32)], axis=0).T


def _router(h, wr_t, rb_col, tile):
    n = h.shape[0]
    return pl.pallas_call(
        _router_kernel,
        out_shape=jax.ShapeDtypeStruct((n, LANES), F32),
        grid=(n // tile,),
        in_specs=[pl.BlockSpec((tile, D_MODEL), lambda i: (i, 0)),
                  pl.BlockSpec(wr_t.shape, lambda i: (0, 0)),
                  pl.BlockSpec(rb_col.shape, lambda i: (0, 0))],
        out_specs=pl.BlockSpec((tile, LANES), lambda i: (i, 0)),
        compiler_params=_cparams("parallel"),
        name="moe_router",
    )(h, wr_t, rb_col)


_EXPERTS_PER_STEP = 4


def _swiglu(hb, w1, w3, w2):
    a = jnp.dot(hb, w1, preferred_element_type=F32)
    b = jnp.dot(hb, w3, preferred_element_type=F32)
    return jnp.dot((jax.nn.silu(a) * b).astype(BF16), w2, preferred_element_type=F32)


def _experts_kernel(h_ref, gate_ref, w1_ref, w3_ref, w2_ref, s1_ref, s3_ref, s2_ref, lg_ref, lb_ref,
                    y_ref, hb_ref, acc_ref):
    step = pl.program_id(1)

    @pl.when(step == 0)
    def _():
        hb = h_ref[...].astype(BF16)
        hb_ref[...] = hb
        acc_ref[...] = _swiglu(hb, s1_ref[...], s3_ref[...], s2_ref[...])

    hb = hb_ref[...]
    gate = gate_ref[...]
    lane = lax.broadcasted_iota(jnp.int32, gate.shape, 1)
    for k in range(_EXPERTS_PER_STEP):
        e = step * _EXPERTS_PER_STEP + k
        g_col = jnp.sum(jnp.where(lane == e, gate, 0.0), axis=1, keepdims=True)
        acc_ref[...] += _swiglu(hb, w1_ref[k], w3_ref[k], w2_ref[k]) * g_col

    @pl.when(step == pl.num_programs(1) - 1)
    def _():
        y_ref[...] = _layer_norm(ALPHA * h_ref[...] + acc_ref[...], lg_ref[...], lb_ref[...])


def _experts(h, gate, w1, w3, w2, s1, s3, s2, ln_g, ln_b, tile):
    n = h.shape[0]
    ec = _EXPERTS_PER_STEP
    row = lambda width: pl.BlockSpec((tile, width), lambda i, e: (i, 0))
    full = lambda a: pl.BlockSpec(a.shape, lambda i, e: (0,) * a.ndim)
    wspec = lambda a: pl.BlockSpec((ec,) + a.shape[1:], lambda i, e: (e, 0, 0))
    return pl.pallas_call(
        _experts_kernel,
        out_shape=jax.ShapeDtypeStruct((n, D_MODEL), F32),
        grid=(n // tile, N_EXPERTS // ec),
        in_specs=[row(D_MODEL), row(LANES), wspec(w1), wspec(w3), wspec(w2),
                  full(s1), full(s3), full(s2), full(ln_g), full(ln_b)],
        out_specs=row(D_MODEL),
        scratch_shapes=[pltpu.VMEM((tile, D_MODEL), BF16), pltpu.VMEM((tile, D_MODEL), F32)],
        compiler_params=_cparams("parallel", "arbitrary"),
        name="moe_experts",
    )(h, gate, w1, w3, w2, s1, s3, s2, ln_g, ln_b)


def _rope_freqs():
    inv16 = ROPE_THETA ** (-jnp.arange(0, ROPE_DIM, 2, dtype=F32) / ROPE_DIM)
    inv8 = ROPE_THETA ** (-jnp.arange(0, IDX_ROPE, 2, dtype=F32) / IDX_ROPE)

    n16, n8 = ROPE_DIM // 2, IDX_ROPE // 2
    freq = jnp.zeros((1, LANES), F32).at[0, 0:n16].set(inv16).at[0, n16:n16 + n8].set(inv8)
    lay = _LAYOUT
    place = np.zeros((2 * LANES, 6 * LANES), np.float32)
    for k, (fidx, sign, base) in enumerate(((lay["fa"], lay["sa"], 0), (lay["fi"], lay["si"], n16),
                                            (lay["fb"], lay["sb"], 0))):
        for l in range(LANES):
            src = base + fidx[l] if fidx[l] >= 0 else n16 + n8
            place[src, 2 * k * LANES + l] = 1.0
            place[LANES + src, (2 * k + 1) * LANES + l] = sign[l]
    return freq, jnp.asarray(place)


def _layer(x, positions, w_in, b_gate, g_kv, w_uk, w_uv, w_branch_a, w_branch_b, w_o, ln1_g, ln1_b,
           w_router, router_bias, w1_e, w3_e, w2_e, ws1, ws3, ws2, ln2_g, ln2_b):
    b, s, d = x.shape
    n = b * s
    lay = _LAYOUT
    tile, moe_tile = min(_TOKEN_TILE, n), min(_MOE_TILE, n)
    xf = x.reshape(n, d)

    n_proj = _C_GA
    w_perm = (w_in[:, lay["cols"][:n_proj]] * lay["keep"][:n_proj]).astype(BF16)
    w_gate = w_in[:, _OFF_GA:_OFF_GA + 2 * D_MODEL].astype(BF16)
    wuk = jnp.transpose(w_uk, (1, 0, 2)).astype(BF16)
    wuv_t = jnp.transpose(w_uv, (1, 2, 0)).astype(BF16)
    masks = {"m_bq": jnp.asarray(lay["m_bq"], BF16), "m_ar": jnp.asarray(lay["m_ar"].T, BF16),
             "m_iq": jnp.asarray(lay["m_iq"].T, BF16)}
    m_bv = jnp.asarray(lay["m_bv"], F32)

    freq, place = _rope_freqs()
    tables = _rope_tables(positions.reshape(n, 1), freq, place, tile)
    qa, ka, ckvt, iq, ikt, iw, *bqkv = _input_projection(xf, w_perm, tables, g_kv.reshape(1, -1), tile)

    a_out = _dsa_mixer(qa, iq, iw, ikt, ka, ckvt, wuk, wuv_t, masks["m_ar"], masks["m_iq"], b, s)
    ng = len(B_PATTERNS)
    b_parts = [_dilated_group(bqkv[g], bqkv[ng + g], bqkv[2 * ng + g], masks["m_bq"], m_bv, b, dil)
               for g, (_, dil) in enumerate(B_PATTERNS)]

    h = _merge(xf, a_out, b_parts, w_gate, b_gate.reshape(1, -1), w_branch_a.astype(BF16),
               w_branch_b.astype(BF16), w_o.astype(BF16), ln1_g.reshape(1, -1), ln1_b.reshape(1, -1), tile)

    gate = _router(h, w_router.T, router_bias.reshape(-1, 1), moe_tile)
    y = _experts(h, gate, w1_e.astype(BF16), w3_e.astype(BF16), w2_e.astype(BF16),
                 ws1.astype(BF16), ws3.astype(BF16), ws2.astype(BF16),
                 ln2_g.reshape(1, -1), ln2_b.reshape(1, -1), moe_tile)
    return y.reshape(b, s, d)


def kernel(x, positions, w_in, b_gate, g_kv, w_uk, w_uv, w_branch_a, w_branch_b, w_o, ln1_g, ln1_b,
           w_router, router_bias, w1_e, w3_e, w2_e, ws1, ws3, ws2, ln2_g, ln2_b):
    h = x
    for l in range(DEPTH):
        h = _layer(h, positions, w_in[l], b_gate[l], g_kv[l], w_uk[l], w_uv[l], w_branch_a[l],
                   w_branch_b[l], w_o[l], ln1_g[l], ln1_b[l], w_router[l], router_bias[l],
                   w1_e[l], w3_e[l], w2_e[l], ws1[l], ws3[l], ws2[l], ln2_g[l], ln2_b[l])
    return h
```

```python
import functools

import jax
import jax.numpy as jnp
import numpy as np
from jax import lax
from jax.experimental import pallas as pl
from jax.experimental.pallas import tpu as pltpu

F32 = jnp.float32
BF16 = jnp.bfloat16

D_MODEL = 1024
HEAD_DIM = 64
ROPE_DIM = 16
ROPE_THETA = 500000.0
Q_BLOCK = 128
A_HEADS = 8
A_NOPE = HEAD_DIM - ROPE_DIM
A_KV_RANK = 256
A_TOPK_MAX = 256
IDX_HEADS = 8
IDX_DIM = 32
IDX_ROPE = 8
B_PATTERNS = ((128, 1), (512, 4), (2048, 16))
B_GROUP_HEADS = 4
B_HEADS = B_GROUP_HEADS * len(B_PATTERNS)
B_GROUP_W = B_GROUP_HEADS * HEAD_DIM
N_EXPERTS = 64
TOP_K = 8
N_GROUPS = 8
TOPK_GROUPS = 4
D_EXPERT = 256
ROUTED_SCALE = 2.5
DEPTH = 1
ALPHA = (2.0 * DEPTH) ** 0.25
LN_EPS = 1e-5
RMS_EPS = 1e-6

LANES = 128
VMEM_LIMIT = 56 * 1024 * 1024
_TOKEN_TILE = 256
_MOE_TILE = 1024
NEG_BIG = -1e30
INT_MIN = -(2 ** 31)

_OFF_AQ = 0
_OFF_CKV = _OFF_AQ + A_HEADS * HEAD_DIM
_OFF_AKR = _OFF_CKV + A_KV_RANK
_OFF_IQ = _OFF_AKR + ROPE_DIM
_OFF_IK = _OFF_IQ + IDX_HEADS * IDX_DIM
_OFF_IW = _OFF_IK + IDX_DIM
_OFF_BQ = _OFF_IW + IDX_HEADS
_OFF_BK = _OFF_BQ + B_HEADS * HEAD_DIM
_OFF_BV = _OFF_BK + B_HEADS * HEAD_DIM
_OFF_GA = _OFF_BV + B_HEADS * HEAD_DIM
_OFF_GB = _OFF_GA + D_MODEL
_IN_TOTAL = _OFF_GB + D_MODEL

_W_QAR, _W_QAN, _W_CKV, _W_AKR = 128, A_HEADS * A_NOPE, A_KV_RANK, 128
_W_IQ, _W_IK, _W_IW = 256, 256, 128
_W_B = B_HEADS * HEAD_DIM
_C_QAR = 0
_C_QAN = _C_QAR + _W_QAR
_C_CKV = _C_QAN + _W_QAN
_C_AKR = _C_CKV + _W_CKV
_C_IQ = _C_AKR + _W_AKR
_C_IK = _C_IQ + _W_IQ
_C_IW = _C_IK + _W_IK
_C_BQ = _C_IW + _W_IW
_C_BK = _C_BQ + _W_B
_C_BV = _C_BK + _W_B
_C_GA = _C_BV + _W_B
_C_GB = _C_GA + D_MODEL
_P_TOTAL = _C_GB + D_MODEL


def _idx_lane(l):
    if l < 16:
        return l // 4, l % 4
    if l < 64:
        return (l - 16) // 12, 8 + (l - 16) % 12
    if l < 80:
        return (l - 64) // 4, 4 + (l - 64) % 4
    return (l - 80) // 12, 20 + (l - 80) % 12


def _b_lane(l):
    half, r = l // 64, l % 64
    which, rr = r // 32, r % 32
    if rr < 8:
        return which, half * 8 + rr
    return which, 16 + half * 24 + (rr - 8)


def _build_layout():
    cols = np.zeros((_P_TOTAL,), np.int32)
    keep = np.ones((_P_TOTAL,), np.float32)
    for l in range(128):
        half, h, f = l // 64, (l % 64) // 8, l % 8
        cols[_C_QAR + l] = _OFF_AQ + h * HEAD_DIM + half * 8 + f
        cols[_C_AKR + l] = _OFF_AKR + half * 8 + f
    for h in range(A_HEADS):
        for j in range(A_NOPE):
            cols[_C_QAN + h * A_NOPE + j] = _OFF_AQ + h * HEAD_DIM + ROPE_DIM + j
    cols[_C_CKV:_C_CKV + _W_CKV] = _OFF_CKV + np.arange(_W_CKV)
    for sl in range(2):
        for l in range(128):
            hh, d = _idx_lane(l)
            cols[_C_IQ + sl * 128 + l] = _OFF_IQ + (sl * 4 + hh) * IDX_DIM + d
            cols[_C_IK + sl * 128 + l] = _OFF_IK + d
    cols[_C_IW:_C_IW + IDX_HEADS] = _OFF_IW + np.arange(IDX_HEADS)
    keep[_C_IW + IDX_HEADS:_C_IW + _W_IW] = 0.0
    for p in range(B_HEADS // 2):
        for l in range(128):
            which, d = _b_lane(l)
            h = 2 * p + which
            cols[_C_BQ + p * 128 + l] = _OFF_BQ + h * HEAD_DIM + d
            cols[_C_BK + p * 128 + l] = _OFF_BK + h * HEAD_DIM + d
    cols[_C_BV:_C_BV + _W_B] = _OFF_BV + np.arange(_W_B)
    cols[_C_GA:_C_GA + D_MODEL] = _OFF_GA + np.arange(D_MODEL)
    cols[_C_GB:_C_GB + D_MODEL] = _OFF_GB + np.arange(D_MODEL)

    fa = np.array([l % 8 for l in range(128)])
    sa = np.array([-1.0 if l < 64 else 1.0 for l in range(128)], np.float32)
    fi = np.full((128,), -1)
    si = np.zeros((128,), np.float32)
    fb = np.full((128,), -1)
    sb = np.zeros((128,), np.float32)
    for l in range(128):
        if l < 16 or 64 <= l < 80:
            fi[l] = l % 4
            si[l] = -1.0 if l < 64 else 1.0
        if l % 32 < 8:
            fb[l] = l % 32
            sb[l] = -1.0 if l < 64 else 1.0

    m_ar = np.zeros((A_HEADS, 128), np.float32)
    for l in range(128):
        m_ar[(l % 64) // 8, l] = 1.0
    m_iq = np.zeros((IDX_HEADS, _W_IQ), np.float32)
    for sl in range(2):
        for l in range(128):
            m_iq[sl * 4 + _idx_lane(l)[0], sl * 128 + l] = 1.0
    m_bq = np.zeros((B_GROUP_HEADS, B_GROUP_W), np.float32)
    for p in range(2):
        for l in range(128):
            m_bq[2 * p + _b_lane(l)[0], p * 128 + l] = 1.0
    m_bv = np.zeros((B_GROUP_HEADS, B_GROUP_W), np.float32)
    for j in range(B_GROUP_HEADS):
        m_bv[j, j * HEAD_DIM:(j + 1) * HEAD_DIM] = 1.0
    return dict(cols=cols, keep=keep, fa=fa, sa=sa, fi=fi, si=si, fb=fb, sb=sb,
                m_ar=m_ar, m_iq=m_iq, m_bq=m_bq, m_bv=m_bv)


_LAYOUT = _build_layout()


def _cparams(*sem):
    return pltpu.CompilerParams(dimension_semantics=sem, vmem_limit_bytes=VMEM_LIMIT)


def _layer_norm(v, g, b):
    mu = jnp.mean(v, axis=-1, keepdims=True)
    var = jnp.mean(jnp.square(v - mu), axis=-1, keepdims=True)
    return (v - mu) * lax.rsqrt(var + LN_EPS) * g + b


def _dot_nt(a, b):
    return lax.dot_general(a, b, (((1,), (1,)), ((), ())), preferred_element_type=F32)


def _rope_table_kernel(pos_ref, freq_ref, place_ref, out_ref):
    t = pos_ref.shape[1]
    ang = freq_ref[...] * pos_ref[...].astype(F32)
    cs_t = jnp.concatenate([jnp.cos(ang), jnp.sin(ang), jnp.zeros((LANES - 2 * ang.shape[0], t), F32)], axis=0)
    out_ref[...] = jnp.dot(cs_t.T, place_ref[...], precision=lax.Precision.HIGHEST, preferred_element_type=F32)


def _rope_tables(pos_row, freq, place, tile):
    n = pos_row.shape[1]
    return pl.pallas_call(
        _rope_table_kernel,
        out_shape=jax.ShapeDtypeStruct((n, 6 * LANES), F32),
        grid=(n // tile,),
        in_specs=[pl.BlockSpec((1, tile), lambda i: (0, i)),
                  pl.BlockSpec(freq.shape, lambda i: (0, 0)),
                  pl.BlockSpec(place.shape, lambda i: (0, 0))],
        out_specs=pl.BlockSpec((tile, 6 * LANES), lambda i: (i, 0)),
        compiler_params=_cparams("parallel"),
        name="rope_tables",
    )(pos_row, freq, place)


def _rope_slabs(y, cos, sin):
    outs = []
    for s in range(y.shape[1] // LANES):
        ys = y[:, s * LANES:(s + 1) * LANES]
        outs.append(ys * cos + pltpu.roll(ys, 64, 1) * sin)
    return outs[0] if len(outs) == 1 else jnp.concatenate(outs, axis=1)


def _store_residue_major(out_ref, y, scr_ref, dil):
    if dil == 1:
        out_ref[...] = y.astype(out_ref.dtype)
        return
    rows, width = y.shape[0] // dil, y.shape[1]
    for c in range(width // LANES):
        scr_ref[c] = y[:, c * LANES:(c + 1) * LANES]
    for r in range(dil):
        for c in range(width // LANES):
            lanes = slice(r * width + c * LANES, r * width + (c + 1) * LANES)
            out_ref[:, lanes] = scr_ref[c, pl.ds(r, rows, stride=dil), :].astype(out_ref.dtype)


def _load_token_major(ref, scr_ref, dil):
    if dil == 1:
        return ref[...]
    rows, width = ref.shape[0], ref.shape[1] // dil
    for r in range(dil):
        for c in range(width // LANES):
            lanes = slice(r * width + c * LANES, r * width + (c + 1) * LANES)
            scr_ref[c, pl.ds(r, rows, stride=dil), :] = ref[:, lanes]
    return jnp.concatenate([scr_ref[c] for c in range(width // LANES)], axis=1)


def _proj_kernel(x_ref, w_ref, tab_ref, gkv_ref,
                 qa_ref, ka_ref, ckvt_ref, iq_ref, ik_ref, iw_ref, *rest):
    b_refs, scr_ref = rest[:-1], rest[-1]
    xb = x_ref[...].astype(BF16)

    def proj(c0, width):
        return jnp.dot(xb, w_ref[:, c0:c0 + width], preferred_element_type=F32)

    cos_a, sin_a = tab_ref[:, 0:128], tab_ref[:, 128:256]
    cos_i, sin_i = tab_ref[:, 256:384], tab_ref[:, 384:512]
    cos_b, sin_b = tab_ref[:, 512:640], tab_ref[:, 640:768]

    qa_ref[:, 0:_W_QAR] = _rope_slabs(proj(_C_QAR, _W_QAR), cos_a, sin_a).astype(BF16)
    qa_ref[:, _W_QAR:] = proj(_C_QAN, _W_QAN).astype(BF16)
    ckv = proj(_C_CKV, _W_CKV)
    ckv = ckv * lax.rsqrt(jnp.mean(jnp.square(ckv), axis=-1, keepdims=True) + RMS_EPS) * gkv_ref[...]
    ka_ref[:, 0:_W_CKV] = ckv.astype(BF16)
    ckvt_ref[...] = ckv.T.astype(BF16)
    ka_ref[:, _W_CKV:] = _rope_slabs(proj(_C_AKR, _W_AKR), cos_a, sin_a).astype(BF16)
    iq_ref[...] = _rope_slabs(proj(_C_IQ, _W_IQ), cos_i, sin_i).astype(BF16)
    ik_ref[...] = _rope_slabs(proj(_C_IK, _W_IK), cos_i, sin_i).astype(BF16)
    iw_ref[...] = proj(_C_IW, _W_IW) * ((IDX_HEADS * IDX_DIM) ** -0.5)
    ng = len(B_PATTERNS)
    for kind, c0 in enumerate((_C_BQ, _C_BK, _C_BV)):
        for g, (_, dil) in enumerate(B_PATTERNS):
            y = proj(c0 + g * B_GROUP_W, B_GROUP_W)
            if kind < 2:
                y = _rope_slabs(y, cos_b, sin_b)
            _store_residue_major(b_refs[kind * ng + g], y, scr_ref, dil)


def _input_projection(xf, w_perm, tables, g_kv, tile):
    n = xf.shape[0]
    row = lambda width: pl.BlockSpec((tile, width), lambda i: (i, 0))
    full = lambda a: pl.BlockSpec(a.shape, lambda i: (0,) * a.ndim)
    out_w = [(_W_QAR + _W_QAN, BF16), (_W_CKV + _W_AKR, BF16), None, (_W_IQ, BF16), (_W_IK, BF16),
             (_W_IW, F32)]
    shapes = [jax.ShapeDtypeStruct((A_KV_RANK, n), BF16) if o is None else jax.ShapeDtypeStruct((n, o[0]), o[1])
              for o in out_w]
    specs = [pl.BlockSpec((A_KV_RANK, tile), lambda i: (0, i)) if o is None else row(o[0]) for o in out_w]
    for _ in range(3):
        for _, dil in B_PATTERNS:
            shapes.append(jax.ShapeDtypeStruct((n // dil, dil * B_GROUP_W), BF16))
            specs.append(pl.BlockSpec((tile // dil, dil * B_GROUP_W), lambda i: (i, 0)))
    return pl.pallas_call(
        _proj_kernel,
        out_shape=shapes,
        grid=(n // tile,),
        in_specs=[row(D_MODEL), full(w_perm), row(6 * LANES), full(g_kv)],
        out_specs=specs,
        scratch_shapes=[pltpu.VMEM((B_GROUP_W // LANES, tile, LANES), F32)],
        compiler_params=_cparams("parallel"),
        name="input_projection",
    )(xf, w_perm, tables, g_kv)


def _dilated_kernel(q_ref, kc_ref, kp_ref, vc_ref, vp_ref, mq_ref, mv_ref, o_ref, lse_ref,
                    kwin_ref, vwin_ref, *, tq):
    first = pl.program_id(2) == 0
    kwin_ref[0:Q_BLOCK, :] = kp_ref[...]
    kwin_ref[Q_BLOCK:, :] = kc_ref[...]
    vwin_ref[0:Q_BLOCK, :] = vp_ref[...]
    vwin_ref[Q_BLOCK:, :] = vc_ref[...]
    t = lax.broadcasted_iota(jnp.int32, (Q_BLOCK, 2 * Q_BLOCK), 0)
    c = lax.broadcasted_iota(jnp.int32, (Q_BLOCK, 2 * Q_BLOCK), 1)
    diff = t + Q_BLOCK - c
    band = (diff >= 0) & (diff <= Q_BLOCK)
    scale = HEAD_DIM ** -0.5
    for sb in range(tq // Q_BLOCK):
        valid = band
        if sb == 0:
            valid = band & (c >= jnp.where(first, Q_BLOCK, 0))
        bias = jnp.where(valid, 0.0, NEG_BIG).astype(F32)
        q = q_ref[sb * Q_BLOCK:(sb + 1) * Q_BLOCK, :] * scale
        kw = kwin_ref[sb * Q_BLOCK:(sb + 2) * Q_BLOCK, :]
        vw = vwin_ref[sb * Q_BLOCK:(sb + 2) * Q_BLOCK, :]
        qs = jnp.concatenate([q * mq_ref[j:j + 1, :] for j in range(B_GROUP_HEADS)], axis=0)
        s = _dot_nt(qs, kw)
        o_acc = jnp.zeros((Q_BLOCK, B_GROUP_W), F32)
        lse_acc = jnp.zeros((Q_BLOCK, B_GROUP_W), F32)
        for j in range(B_GROUP_HEADS):
            sj = s[j * Q_BLOCK:(j + 1) * Q_BLOCK, :] + bias
            m = jnp.max(sj, axis=-1, keepdims=True)
            e = jnp.exp(sj - m)
            den = jnp.sum(e, axis=-1, keepdims=True)
            pv = jnp.dot(e.astype(BF16), vw, preferred_element_type=F32)
            mv = mv_ref[j:j + 1, :]
            o_acc = o_acc + (pv / den) * mv
            lse_acc = lse_acc + (m + jnp.log(den)) * mv
        o_ref[sb * Q_BLOCK:(sb + 1) * Q_BLOCK, :] = o_acc
        lse_ref[sb * Q_BLOCK:(sb + 1) * Q_BLOCK, :] = lse_acc


def _dilated_group(bq, bk, bv, mq, mv, b, dil):
    sub = bq.shape[0] // b
    tq = min(512, sub)
    nblk = tq // Q_BLOCK
    view = lambda a: a.reshape(b, sub, dil * B_GROUP_W)
    cur = pl.BlockSpec((None, tq, B_GROUP_W), lambda bi, r, i: (bi, i, r))
    prev = pl.BlockSpec((None, Q_BLOCK, B_GROUP_W),
                        lambda bi, r, i: (bi, jnp.maximum(i * nblk - 1, 0), r))
    const = lambda a: pl.BlockSpec(a.shape, lambda bi, r, i: (0, 0))
    out = cur
    o, lse = pl.pallas_call(
        functools.partial(_dilated_kernel, tq=tq),
        out_shape=[jax.ShapeDtypeStruct((b, sub, dil * B_GROUP_W), F32)] * 2,
        grid=(b, dil, sub // tq),
        in_specs=[cur, cur, prev, cur, prev, const(mq), const(mv)],
        out_specs=[out, out],
        scratch_shapes=[pltpu.VMEM((tq + Q_BLOCK, B_GROUP_W), BF16)] * 2,
        compiler_params=_cparams("parallel", "parallel", "arbitrary"),
        name=f"dilated_attention_d{dil}",
    )(view(bq), view(bk), view(bk), view(bv), view(bv), mq, mv)
    return o.reshape(b * sub, dil * B_GROUP_W), lse.reshape(b * sub, dil * B_GROUP_W)


_TK = 512

def _fold_rows(x, op, slab=8):
    parts = [x[r:r + slab, :] for r in range(0, x.shape[0], slab)]
    while len(parts) > 1:
        parts = [op(parts[k], parts[k + 1]) for k in range(0, len(parts) - 1, 2)] + parts[len(parts) & ~1:]
    return parts[0]


def _skewed_tiles(n_tiles, produce, consume, buf_a, buf_b, carry, finish=None):
    finish = finish or (lambda j, buf, c: c)
    produce(0, buf_a)

    def pair(t, c):
        j = 2 * t
        produce(j + 1, buf_b)
        c = consume(j, buf_a, buf_b, c)
        produce(j + 2, buf_a)
        return consume(j + 1, buf_b, buf_a, c)

    n_pairs = (n_tiles - 1) // 2
    carry = lax.fori_loop(0, n_pairs, pair, carry)
    j = 2 * n_pairs

    def last_two(c):
        produce(j + 1, buf_b)
        return finish(j + 1, buf_b, consume(j + 1, buf_b, buf_a, consume(j, buf_a, buf_b, c)))

    def last_one(c):
        return finish(j, buf_a, consume(j, buf_a, buf_b, c))

    return lax.cond(n_tiles - j == 2, last_two, last_one, carry)


def _dsa_kernel(qa_ref, iq_ref, iw_ref, ikt_ref, ka_ref, ckvt_ref, wuk_ref, wuv_ref, mar_ref, miq_ref,
                out_ref, key_ref, tie_ref, iqs_ref, qcat_ref, acc_ref, sa_ref, sb_ref, ma_ref, mb_ref,
                pa_ref, pb_ref, *, n_sel):
    i = pl.program_id(1)
    q0 = i * Q_BLOCK
    n_keys = q0 + Q_BLOCK
    rows = A_HEADS * Q_BLOCK

    n_tiles = (n_keys + _TK - 1) // _TK
    tq_lane = q0 + lax.broadcasted_iota(jnp.int32, (_TK, Q_BLOCK), 1)
    krow = lax.broadcasted_iota(jnp.int32, (_TK, Q_BLOCK), 0)

    iq_t = iq_ref[...].astype(F32).T
    for h in range(IDX_HEADS):
        iqs_ref[:, h * Q_BLOCK:(h + 1) * Q_BLOCK] = (iq_t * miq_ref[:, h:h + 1].astype(F32)).astype(BF16)
    iw_t = iw_ref[...].T

    def score_matmul(j, buf):
        k0 = pl.multiple_of(j * _TK, _TK)
        s = jnp.dot(ikt_ref[pl.ds(k0, _TK), :], iqs_ref[...], preferred_element_type=F32)
        for h in range(IDX_HEADS):
            buf[0][h] = s[:, h * Q_BLOCK:(h + 1) * Q_BLOCK]

    def score_keys(j, buf, _, carry):
        k0 = pl.multiple_of(j * _TK, _TK)
        sc = jnp.zeros((_TK, Q_BLOCK), F32)
        for h in range(IDX_HEADS):
            sc = sc + jnp.maximum(buf[0][h], 0.0) * iw_t[h:h + 1, :]
        bits = lax.bitcast_convert_type(jnp.where(sc == 0.0, 0.0, sc), jnp.int32)
        okey = bits ^ ((bits >> 31) & jnp.int32(0x7FFFFFFF))
        key_ref[pl.ds(k0, _TK), :] = jnp.where(krow + k0 <= tq_lane, okey, jnp.int32(INT_MIN))
        return carry

    _skewed_tiles(n_tiles, score_matmul, score_keys, (sa_ref, ma_ref), (sb_ref, mb_ref), 0)

    def count_keys(pred):
        def tile_hits(j):
            k0 = pl.multiple_of(j * _TK, _TK)
            hit = jnp.where(pred(key_ref[pl.ds(k0, _TK), :], k0), 1.0, 0.0)
            return _fold_rows(hit, jnp.add)

        def two_tiles(t, cnt):
            return cnt + (tile_hits(2 * t) + tile_hits(2 * t + 1))
        cnt8 = lax.fori_loop(0, n_tiles // 2, two_tiles, jnp.zeros((8, Q_BLOCK), F32))
        cnt8 = lax.cond(n_tiles % 2 == 1, lambda c: c + tile_hits(n_tiles - 1), lambda c: c, cnt8)
        return jnp.sum(cnt8, axis=0, keepdims=True)

    def bit_step(b, carry):
        prefix, n_ge = carry
        trial = prefix | (jnp.int32(1) << (31 - b))
        t = trial ^ jnp.int32(INT_MIN)
        cnt = count_keys(lambda keys, k0: keys >= t)
        take = cnt >= float(n_sel)
        return jnp.where(take, trial, prefix), jnp.where(take, cnt, n_ge)

    state = (jnp.zeros((1, Q_BLOCK), jnp.int32), jnp.full((1, Q_BLOCK), float(n_sel), F32))
    state = lax.fori_loop(0, 26, bit_step, state)
    for lo, hi in ((26, 28), (28, 30), (30, 32)):
        settled = jnp.max(state[1]) == float(n_sel)
        state = lax.cond(settled, lambda st: st, functools.partial(lax.fori_loop, lo, hi, bit_step), state)
    prefix, n_ge = state
    thr = prefix ^ jnp.int32(INT_MIN)

    surplus = n_ge - float(n_sel)
    max_surplus = jnp.max(surplus)

    @pl.when(max_surplus > 0.0)
    def _():
        need = float(n_sel) - count_keys(lambda keys, k0: keys > thr)
        index_bits = (key_ref.shape[0] - 1).bit_length()
        not_tied = jnp.int32(1 << index_bits)

        def tie_positions(j, carry):
            k0 = pl.multiple_of(j * _TK, _TK)
            tied = key_ref[pl.ds(k0, _TK), :] == thr
            tie_ref[pl.ds(k0, _TK), :] = jnp.where(tied, krow + k0, not_tied)
            return carry

        lax.fori_loop(0, n_tiles, tie_positions, 0)

        def count_ties_below(bound):
            def body(j, cnt):
                k0 = pl.multiple_of(j * _TK, _TK)
                hit = jnp.where(tie_ref[pl.ds(k0, _TK), :] < bound, 1.0, 0.0)
                return cnt + _fold_rows(hit, jnp.add)
            cnt8 = lax.fori_loop(0, n_tiles, body, jnp.zeros((8, Q_BLOCK), F32))
            return jnp.sum(cnt8, axis=0, keepdims=True)

        def index_bit(b, bound):
            trial = bound | (jnp.int32(1) << (index_bits - 1 - b))
            return jnp.where(count_ties_below(trial) < need, trial, bound)

        last = lax.fori_loop(0, index_bits, index_bit, jnp.zeros((1, Q_BLOCK), jnp.int32))

        def demote(j, carry):
            k0 = pl.multiple_of(j * _TK, _TK)
            pos = tie_ref[pl.ds(k0, _TK), :]
            drop = (pos > last) & (pos < not_tied)
            key_ref[pl.ds(k0, _TK), :] = jnp.where(drop, jnp.int32(INT_MIN), key_ref[pl.ds(k0, _TK), :])
            return carry

        lax.fori_loop(0, n_tiles, demote, 0)

    q_rope_t = qa_ref[:, 0:_W_QAR].astype(F32).T.astype(BF16)
    q_nope_t = qa_ref[:, _W_QAR:].astype(F32).T.astype(BF16)
    scale = HEAD_DIM ** -0.5
    c_rope = A_KV_RANK
    for h in range(A_HEADS):
        q_lat_t = jnp.dot(wuk_ref[h], q_nope_t[h * A_NOPE:(h + 1) * A_NOPE, :], preferred_element_type=F32)
        cols = slice(h * Q_BLOCK, (h + 1) * Q_BLOCK)
        qcat_ref[0:c_rope, cols] = q_lat_t.astype(BF16) * scale
        qcat_ref[c_rope:, cols] = q_rope_t * mar_ref[:, h:h + 1] * scale

    acc_ref[...] = jnp.zeros(acc_ref.shape, F32)
    thr_sel = jnp.maximum(thr, jnp.int32(INT_MIN + 1))

    def logit_matmul(j, buf):
        s_buf, mx_buf = buf[0], buf[1]
        k0 = pl.multiple_of(j * _TK, _TK)
        sel = key_ref[pl.ds(k0, _TK), :] >= thr_sel
        bias = jnp.where(sel, 0.0, NEG_BIG).astype(F32)
        s = jnp.dot(ka_ref[pl.ds(k0, _TK), :], qcat_ref[...], preferred_element_type=F32)
        for h in range(A_HEADS):
            sh = s[:, h * Q_BLOCK:(h + 1) * Q_BLOCK] + bias
            s_buf[h] = sh
            mx_buf[:, h * Q_BLOCK:(h + 1) * Q_BLOCK] = _fold_rows(sh, jnp.maximum)

    def accumulate(j, p_buf, alpha):
        k0 = pl.multiple_of(j * _TK, _TK)
        ckv_t = ckvt_ref[:, pl.ds(k0, _TK)]
        for c in range(A_HEADS // 2):
            pv = jnp.dot(ckv_t, p_buf[c], preferred_element_type=F32)
            acc_ref[c] = acc_ref[c] * alpha[:, c * 2 * Q_BLOCK:(c + 1) * 2 * Q_BLOCK] + pv

    def softmax_pv(j, buf, other, carry):
        s_buf, mx_buf, p_buf = buf
        m_old, l_old, a_prev = carry
        m_parts, l_parts, a_parts = [], [], []
        for h in range(A_HEADS):
            cols = slice(h * Q_BLOCK, (h + 1) * Q_BLOCK)
            m_h = jnp.maximum(m_old[:, cols], jnp.max(mx_buf[:, cols], axis=0, keepdims=True))
            a_h = jnp.exp(m_old[:, cols] - m_h)
            p = jnp.exp(s_buf[h] - m_h)
            p_buf[h // 2, :, (h % 2) * Q_BLOCK:(h % 2 + 1) * Q_BLOCK] = p.astype(BF16)
            l_parts.append(a_h * l_old[:, cols] + jnp.sum(_fold_rows(p, jnp.add), axis=0, keepdims=True))
            m_parts.append(m_h)
            a_parts.append(a_h)
        accumulate(jnp.maximum(j - 1, 0), other[2], a_prev)
        return (jnp.concatenate(m_parts, axis=1), jnp.concatenate(l_parts, axis=1),
                jnp.concatenate(a_parts, axis=1))

    def last_accumulate(j, buf, carry):
        accumulate(j, buf[2], carry[2])
        return carry

    pb_ref[...] = jnp.zeros(pb_ref.shape, BF16)
    _, l_fin, _ = _skewed_tiles(n_tiles, logit_matmul, softmax_pv, (sa_ref, ma_ref, pa_ref), (sb_ref, mb_ref, pb_ref),
                                (jnp.full((1, rows), NEG_BIG, F32), jnp.zeros((1, rows), F32),
                                 jnp.ones((1, rows), F32)), finish=last_accumulate)

    inv_l = 1.0 / l_fin
    out_t = []
    for h in range(A_HEADS):
        lanes = slice((h % 2) * Q_BLOCK, (h % 2 + 1) * Q_BLOCK)
        o_lat = (acc_ref[h // 2][:, lanes] * inv_l[:, h * Q_BLOCK:(h + 1) * Q_BLOCK]).astype(BF16)
        out_t.append(jnp.dot(wuv_ref[h], o_lat, preferred_element_type=F32))
    out_ref[...] = jnp.concatenate(out_t, axis=0).T.astype(BF16)


def _dsa_mixer(qa, iq, iw, ikt, ka, ckvt, wuk, wuv_t, m_ar, m_iq, b, s):
    n_sel = min(A_TOPK_MAX, s // 4)
    nq = s // Q_BLOCK
    rows = A_HEADS * Q_BLOCK
    blk = lambda width: pl.BlockSpec((Q_BLOCK, width), lambda bi, i: (bi * nq + i, 0))
    seq = lambda width: pl.BlockSpec((s, width), lambda bi, i: (bi, 0))
    const = lambda a: pl.BlockSpec(a.shape, lambda bi, i: (0,) * a.ndim)
    return pl.pallas_call(
        functools.partial(_dsa_kernel, n_sel=n_sel),
        out_shape=jax.ShapeDtypeStruct((b * s, A_HEADS * HEAD_DIM), BF16),
        grid=(b, nq),
        in_specs=[blk(_W_QAR + _W_QAN), blk(_W_IQ), blk(_W_IW), seq(_W_IK), seq(_W_CKV + _W_AKR),
                  pl.BlockSpec((A_KV_RANK, s), lambda bi, i: (0, bi)),
                  const(wuk), const(wuv_t), const(m_ar), const(m_iq)],
        out_specs=blk(A_HEADS * HEAD_DIM),
        scratch_shapes=[pltpu.VMEM((s, Q_BLOCK), jnp.int32),
                        pltpu.VMEM((s, Q_BLOCK), jnp.int32),
                        pltpu.VMEM((_W_IQ, rows), BF16),
                        pltpu.VMEM((A_KV_RANK + _W_AKR, rows), BF16),
                        pltpu.VMEM((A_HEADS // 2, A_KV_RANK, 2 * Q_BLOCK), F32),
                        pltpu.VMEM((A_HEADS, _TK, Q_BLOCK), F32),
                        pltpu.VMEM((A_HEADS, _TK, Q_BLOCK), F32),
                        pltpu.VMEM((8, rows), F32),
                        pltpu.VMEM((8, rows), F32),
                        pltpu.VMEM((A_HEADS // 2, _TK, 2 * Q_BLOCK), BF16),
                        pltpu.VMEM((A_HEADS // 2, _TK, 2 * Q_BLOCK), BF16)],
        compiler_params=_cparams("parallel", "arbitrary"),
        name="dsa_attention",
    )(qa, iq, iw, ikt, ka, ckvt, wuk, wuv_t, m_ar, m_iq)


def _merge_kernel(x_ref, a_ref, o1_ref, o2_ref, o3_ref, l1_ref, l2_ref, l3_ref, wg_ref, bg_ref,
                  wa_ref, wb_ref, wo_ref, lg_ref, lb_ref, h_ref, *scr):
    dils = [dil for _, dil in B_PATTERNS]
    lses = [_load_token_major(r, scr[2 * g], dils[g]) for g, r in enumerate((l1_ref, l2_ref, l3_ref))]
    outs = [_load_token_major(r, scr[2 * g + 1], dils[g]) for g, r in enumerate((o1_ref, o2_ref, o3_ref))]
    mx = jnp.maximum(jnp.maximum(lses[0], lses[1]), lses[2])
    es = [jnp.exp(l - mx) for l in lses]
    den = es[0] + es[1] + es[2]
    b_out = (es[0] / den) * outs[0] + (es[1] / den) * outs[1] + (es[2] / den) * outs[2]
    ya = jnp.dot(a_ref[...], wa_ref[...], preferred_element_type=F32)
    yb = jnp.dot(b_out.astype(BF16), wb_ref[...], preferred_element_type=F32)
    x = x_ref[...]
    gates = jax.nn.sigmoid(jnp.dot(x.astype(BF16), wg_ref[...], preferred_element_type=F32) + bg_ref[...])
    pre = gates[:, 0:D_MODEL] * ya + gates[:, D_MODEL:] * yb
    mix = jnp.dot(pre.astype(BF16), wo_ref[...], preferred_element_type=F32)
    h_ref[...] = _layer_norm(ALPHA * x + mix, lg_ref[...], lb_ref[...])


def _merge(xf, a_out, b_parts, w_gate, b_gate, wa, wb, wo, ln_g, ln_b, tile):
    n = xf.shape[0]
    row = lambda width: pl.BlockSpec((tile, width), lambda i: (i, 0))
    full = lambda a: pl.BlockSpec(a.shape, lambda i: (0,) * a.ndim)
    (o1, l1), (o2, l2), (o3, l3) = b_parts
    grp = [pl.BlockSpec((tile // dil, dil * B_GROUP_W), lambda i: (i, 0)) for _, dil in B_PATTERNS]
    return pl.pallas_call(
        _merge_kernel,
        out_shape=jax.ShapeDtypeStruct((n, D_MODEL), F32),
        grid=(n // tile,),
        in_specs=[row(D_MODEL), row(A_HEADS * HEAD_DIM)] + grp + grp + [full(w_gate), full(b_gate),
                  full(wa), full(wb), full(wo), full(ln_g), full(ln_b)],
        out_specs=row(D_MODEL),
        scratch_shapes=[pltpu.VMEM((B_GROUP_W // LANES, tile, LANES), F32)] * (2 * len(B_PATTERNS)),
        compiler_params=_cparams("parallel"),
        name="merge_output_projection",
    )(xf, a_out, o1, o2, o3, l1, l2, l3, w_gate, b_gate, wa, wb, wo, ln_g, ln_b)


def _first_max(v):
    m = jnp.max(v, axis=0, keepdims=True)
    idx = lax.broadcasted_iota(jnp.int32, v.shape, 0)
    first = jnp.min(jnp.where(v == m, idx, v.shape[0]), axis=0, keepdims=True)
    return m, idx == first


def _router_kernel(h_ref, wr_ref, rb_ref, gate_ref):
    t = h_ref.shape[0]
    gs = N_EXPERTS // N_GROUPS
    logits = jnp.dot(h_ref[...], wr_ref[...], precision=lax.Precision.HIGHEST,
                     preferred_element_type=F32).T[0:N_EXPERTS, :]
    scores = jax.nn.sigmoid(logits)
    biased = scores + rb_ref[...]
    gscores = []
    for g in range(N_GROUPS):
        blk = biased[g * gs:(g + 1) * gs, :]
        m1, hit = _first_max(blk)
        m2 = jnp.max(jnp.where(hit, -jnp.inf, blk), axis=0, keepdims=True)
        gscores.append(m1 + m2)
    gscore = jnp.concatenate(gscores, axis=0)
    gsel = jnp.zeros((N_GROUPS, t), F32)
    for _ in range(TOPK_GROUPS):
        _, hit = _first_max(gscore)
        gsel = jnp.where(hit, 1.0, gsel)
        gscore = jnp.where(hit, -jnp.inf, gscore)
    esel = jnp.concatenate([jnp.broadcast_to(gsel[g:g + 1, :], (gs, t)) for g in range(N_GROUPS)], axis=0)
    cand = jnp.where(esel > 0.0, biased, -jnp.inf)
    top_s = jnp.zeros((N_EXPERTS, t), F32)
    for _ in range(TOP_K):
        _, hit = _first_max(cand)
        top_s = jnp.where(hit, scores, top_s)
        cand = jnp.where(hit, -jnp.inf, cand)
    gate = top_s / jnp.sum(top_s, axis=0, keepdims=True) * ROUTED_SCALE
    gate_ref[...] = jnp.concatenate([gate, jnp.zeros((LANES - N_EXPERTS, t), F32)], axis=0).T


def _router(h, wr_t, rb_col, tile):
    n = h.shape[0]
    return pl.pallas_call(
        _router_kernel,
        out_shape=jax.ShapeDtypeStruct((n, LANES), F32),
        grid=(n // tile,),
        in_specs=[pl.BlockSpec((tile, D_MODEL), lambda i: (i, 0)),
                  pl.BlockSpec(wr_t.shape, lambda i: (0, 0)),
                  pl.BlockSpec(rb_col.shape, lambda i: (0, 0))],
        out_specs=pl.BlockSpec((tile, LANES), lambda i: (i, 0)),
        compiler_params=_cparams("parallel"),
        name="moe_router",
    )(h, wr_t, rb_col)


_EXPERTS_PER_STEP = 4


def _swiglu(hb, w1, w3, w2):
    a = jnp.dot(hb, w1, preferred_element_type=F32)
    b = jnp.dot(hb, w3, preferred_element_type=F32)
    return jnp.dot((jax.nn.silu(a) * b).astype(BF16), w2, preferred_element_type=F32)


def _experts_kernel(h_ref, gate_ref, w1_ref, w3_ref, w2_ref, s1_ref, s3_ref, s2_ref, lg_ref, lb_ref,
                    y_ref, hb_ref, acc_ref):
    step = pl.program_id(1)

    @pl.when(step == 0)
    def _():
        hb = h_ref[...].astype(BF16)
        hb_ref[...] = hb
        acc_ref[...] = _swiglu(hb, s1_ref[...], s3_ref[...], s2_ref[...])

    hb = hb_ref[...]
    gate = gate_ref[...]
    lane = lax.broadcasted_iota(jnp.int32, gate.shape, 1)
    for k in range(_EXPERTS_PER_STEP):
        e = step * _EXPERTS_PER_STEP + k
        g_col = jnp.sum(jnp.where(lane == e, gate, 0.0), axis=1, keepdims=True)
        acc_ref[...] += _swiglu(hb, w1_ref[k], w3_ref[k], w2_ref[k]) * g_col

    @pl.when(step == pl.num_programs(1) - 1)
    def _():
        y_ref[...] = _layer_norm(ALPHA * h_ref[...] + acc_ref[...], lg_ref[...], lb_ref[...])


def _experts(h, gate, w1, w3, w2, s1, s3, s2, ln_g, ln_b, tile):
    n = h.shape[0]
    ec = _EXPERTS_PER_STEP
    row = lambda width: pl.BlockSpec((tile, width), lambda i, e: (i, 0))
    full = lambda a: pl.BlockSpec(a.shape, lambda i, e: (0,) * a.ndim)
    wspec = lambda a: pl.BlockSpec((ec,) + a.shape[1:], lambda i, e: (e, 0, 0))
    return pl.pallas_call(
        _experts_kernel,
        out_shape=jax.ShapeDtypeStruct((n, D_MODEL), F32),
        grid=(n // tile, N_EXPERTS // ec),
        in_specs=[row(D_MODEL), row(LANES), wspec(w1), wspec(w3), wspec(w2),
                  full(s1), full(s3), full(s2), full(ln_g), full(ln_b)],
        out_specs=row(D_MODEL),
        scratch_shapes=[pltpu.VMEM((tile, D_MODEL), BF16), pltpu.VMEM((tile, D_MODEL), F32)],
        compiler_params=_cparams("parallel", "arbitrary"),
        name="moe_experts",
    )(h, gate, w1, w3, w2, s1, s3, s2, ln_g, ln_b)


def _rope_freqs():
    inv16 = ROPE_THETA ** (-jnp.arange(0, ROPE_DIM, 2, dtype=F32) / ROPE_DIM)
    inv8 = ROPE_THETA ** (-jnp.arange(0, IDX_ROPE, 2, dtype=F32) / IDX_ROPE)

    n16, n8, slots = ROPE_DIM // 2, IDX_ROPE // 2, 16
    freq = jnp.zeros((slots, 1), F32).at[0:n16, 0].set(inv16).at[n16:n16 + n8, 0].set(inv8)
    lay = _LAYOUT
    place = np.zeros((LANES, 6 * LANES), np.float32)
    for k, (fidx, sign, base) in enumerate(((lay["fa"], lay["sa"], 0), (lay["fi"], lay["si"], n16),
                                            (lay["fb"], lay["sb"], 0))):
        for l in range(LANES):
            src = base + fidx[l] if fidx[l] >= 0 else n16 + n8
            place[src, 2 * k * LANES + l] = 1.0
            place[slots + src, (2 * k + 1) * LANES + l] = sign[l]
    return freq, jnp.asarray(place)


def _layer(x, positions, w_in, b_gate, g_kv, w_uk, w_uv, w_branch_a, w_branch_b, w_o, ln1_g, ln1_b,
           w_router, router_bias, w1_e, w3_e, w2_e, ws1, ws3, ws2, ln2_g, ln2_b):
    b, s, d = x.shape
    n = b * s
    lay = _LAYOUT
    tile, moe_tile = min(_TOKEN_TILE, n), min(_MOE_TILE, n)
    xf = x.reshape(n, d)

    n_proj = _C_GA
    w_perm = (w_in[:, lay["cols"][:n_proj]] * lay["keep"][:n_proj]).astype(BF16)
    w_gate = w_in[:, _OFF_GA:_OFF_GA + 2 * D_MODEL].astype(BF16)
    wuk = jnp.transpose(w_uk, (1, 0, 2)).astype(BF16)
    wuv_t = jnp.transpose(w_uv, (1, 2, 0)).astype(BF16)
    masks = {"m_bq": jnp.asarray(lay["m_bq"], BF16), "m_ar": jnp.asarray(lay["m_ar"].T, BF16),
             "m_iq": jnp.asarray(lay["m_iq"].T, BF16)}
    m_bv = jnp.asarray(lay["m_bv"], F32)

    freq, place = _rope_freqs()
    tables = _rope_tables(positions.reshape(1, n), freq, place, tile)
    qa, ka, ckvt, iq, ikt, iw, *bqkv = _input_projection(xf, w_perm, tables, g_kv.reshape(1, -1), tile)

    a_out = _dsa_mixer(qa, iq, iw, ikt, ka, ckvt, wuk, wuv_t, masks["m_ar"], masks["m_iq"], b, s)
    ng = len(B_PATTERNS)
    b_parts = [_dilated_group(bqkv[g], bqkv[ng + g], bqkv[2 * ng + g], masks["m_bq"], m_bv, b, dil)
               for g, (_, dil) in enumerate(B_PATTERNS)]

    h = _merge(xf, a_out, b_parts, w_gate, b_gate.reshape(1, -1), w_branch_a.astype(BF16),
               w_branch_b.astype(BF16), w_o.astype(BF16), ln1_g.reshape(1, -1), ln1_b.reshape(1, -1), tile)

    w_route = jnp.pad(w_router, ((0, 0), (0, LANES - N_EXPERTS)))
    gate = _router(h, w_route, router_bias.reshape(-1, 1), moe_tile)
    y = _experts(h, gate, w1_e.astype(BF16), w3_e.astype(BF16), w2_e.astype(BF16),
                 ws1.astype(BF16), ws3.astype(BF16), ws2.astype(BF16),
                 ln2_g.reshape(1, -1), ln2_b.reshape(1, -1), moe_tile)
    return y.reshape(b, s, d)


def kernel(x, positions, w_in, b_gate, g_kv, w_uk, w_uv, w_branch_a, w_branch_b, w_o, ln1_g, ln1_b,
           w_router, router_bias, w1_e, w3_e, w2_e, ws1, ws3, ws2, ln2_g, ln2_b):
    h = x
    for l in range(DEPTH):
        h = _layer(h, positions, w_in[l], b_gate[l], g_kv[l], w_uk[l], w_uv[l], w_branch_a[l],
                   w_branch_b[l], w_o[l], ln1_g[l], ln1_b[l], w_router[l], router_bias[l],
                   w1_e[l], w3_e[l], w2_e[l], ws1[l], ws3[l], ws2[l], ln2_g[l], ln2_b[l])
    return h
```
